```python
import math
import jax
import jax.numpy as jnp
from jax import lax
import numpy as np

D_MODEL = 1024
BATCH = 8
SEQ = 2048
DEPTH = 4

HEAD_DIM = 64
A_HEADS = 6
A_PATTERNS = ((128, 1), (512, 4), (2048, 16))
A_WIDTH = A_HEADS * HEAD_DIM
B_HEADS = 4
B_NOPE = 64
B_ROPE = 32
B_V = 64
B_Q_RANK = 256
B_KV_RANK = 128
B_WIDTH = B_HEADS * B_V
C_HEADS = 6
C_KEY = 32
C_VAL = 64
C_WIDTH = C_HEADS * C_VAL
C_CHUNK = 128
MIX_WIDTH = A_WIDTH + B_WIDTH + C_WIDTH
IN_SIZES = (A_WIDTH, A_WIDTH, A_WIDTH, B_Q_RANK, B_KV_RANK, B_ROPE, C_HEADS * C_KEY, C_HEADS * C_KEY, C_WIDTH, C_WIDTH)
IN_WIDTH = sum(IN_SIZES)
FFN_HIDDEN = -(-8 * D_MODEL // (3 * 256)) * 256
N_BUCKETS = 32
MAX_DISTANCE = 2048
ROPE_BASE = 10000.0
EPS = 1e-6
Q_BLOCK = 128

kernel_name = 'hybrid_dilated_mla_retention_trunk'


def rmsnorm(x, g=None):
    xf = x.astype(jnp.float32)
    y = xf * lax.rsqrt(jnp.mean(xf * xf, axis=-1, keepdims=True) + EPS)
    if g is not None:
        y = y * g.astype(jnp.float32)
    return y.astype(x.dtype)


def group_norm_heads(x):
    xf = x.astype(jnp.float32)
    mu = jnp.mean(xf, axis=-1, keepdims=True)
    var = jnp.mean(jnp.square(xf - mu), axis=-1, keepdims=True)
    return ((xf - mu) * lax.rsqrt(var + EPS)).astype(x.dtype)


def rope(x, pos):
    half = x.shape[-1] // 2
    inv_freq = (1.0 / (ROPE_BASE ** (np.arange(half, dtype=np.float32) / half))).astype(np.float32)
    ang = pos[:, None, :, None].astype(jnp.float32) * inv_freq
    cos, sin = jnp.cos(ang), jnp.sin(ang)
    x1 = x[..., :half].astype(jnp.float32)
    x2 = x[..., half:].astype(jnp.float32)
    return jnp.concatenate([x1 * cos - x2 * sin, x1 * sin + x2 * cos], axis=-1).astype(x.dtype)


def t5_bucket(dist):
    max_exact = N_BUCKETS // 2
    safe = np.maximum(dist, 1).astype(np.float32)
    large = max_exact + (np.log(safe / max_exact) / np.log(MAX_DISTANCE / max_exact) * (N_BUCKETS - max_exact)).astype(np.int32)
    large = np.minimum(large, N_BUCKETS - 1)
    return np.where(dist < max_exact, dist, large).astype(np.int32)


def to_heads(t, h):
    b, s, _ = t.shape
    return t.reshape(b, s, h, -1).transpose(0, 2, 1, 3)


def from_heads(t):
    b, h, s, d = t.shape
    return t.transpose(0, 2, 1, 3).reshape(b, s, h * d)


def dilated_attention(q, k, v, rel_bias):
    b, h, s, dh = q.shape
    scale = dh ** -0.5
    patterns = []
    for (w, d) in A_PATTERNS:
        dist = np.arange(w // d + 1, dtype=np.int32) * d
        bias = jnp.transpose(rel_bias[t5_bucket(dist)]).astype(jnp.float32)
        patterns.append((jnp.asarray(dist), bias))

    def block(i):
        start = i * Q_BLOCK
        t = start + jnp.arange(Q_BLOCK, dtype=jnp.int32)
        q_blk = lax.dynamic_slice_in_dim(q, start, Q_BLOCK, axis=2).astype(jnp.float32) * scale
        ms, ls, outs = [], [], []
        for dist, bias in patterns:
            idx = t[:, None] - dist[None, :]
            valid = idx >= 0
            idx = jnp.maximum(idx, 0)
            k_g = k[:, :, idx].astype(jnp.float32)
            v_g = v[:, :, idx].astype(jnp.float32)
            sc = jnp.einsum('bhqd,bhqnd->bhqn', q_blk, k_g) + bias[None, :, None, :]
            sc = jnp.where(valid, sc, -jnp.inf)
            m = jnp.max(sc, axis=-1)
            p = jnp.exp(sc - m[..., None])
            l = jnp.sum(p, axis=-1)
            outs.append(jnp.einsum('bhqn,bhqnd->bhqd', p, v_g) / l[..., None])
            ms.append(m)
            ls.append(l)
        m_all = jnp.stack(ms)
        l_all = jnp.stack(ls)
        o_all = jnp.stack(outs)
        wts = l_all * jnp.exp(m_all - jnp.max(m_all, axis=0))
        out = jnp.sum(wts[..., None] * o_all, axis=0) / jnp.sum(wts, axis=0)[..., None]
        return out.astype(q.dtype)

    out = lax.map(block, jnp.arange(s // Q_BLOCK))
    return jnp.transpose(out, (1, 2, 0, 3, 4)).reshape(b, h, s, dh)


def mla(q_lat, kv_lat, k_rope, pos, q_norm_g, kv_norm_g, w_uq, w_ukv):
    b, s, _ = q_lat.shape
    q = to_heads(rmsnorm(q_lat, q_norm_g) @ w_uq, B_HEADS)
    q_nope = q[..., :B_NOPE]
    q_pe = rope(q[..., B_NOPE:], pos)
    kv = to_heads(rmsnorm(kv_lat, kv_norm_g) @ w_ukv, B_HEADS)
    k_nope = kv[..., :B_NOPE]
    v = kv[..., B_NOPE:]
    k_pe = rope(k_rope[:, None], pos)[:, 0]
    scale = (B_NOPE + B_ROPE) ** -0.5
    key_idx = jnp.arange(s, dtype=jnp.int32)

    def block(i):
        start = i * Q_BLOCK
        t = start + jnp.arange(Q_BLOCK, dtype=jnp.int32)
        qn = lax.dynamic_slice_in_dim(q_nope, start, Q_BLOCK, axis=2)
        qp = lax.dynamic_slice_in_dim(q_pe, start, Q_BLOCK, axis=2)
        sc = (jnp.einsum('bhqd,bhkd->bhqk', qn, k_nope) + jnp.einsum('bhqd,bkd->bhqk', qp, k_pe)).astype(jnp.float32) * scale
        sc = jnp.where(key_idx[None, :] <= t[:, None], sc, -jnp.inf)
        p = jax.nn.softmax(sc, axis=-1)
        return jnp.einsum('bhqk,bhkd->bhqd', p.astype(v.dtype), v)

    out = lax.map(block, jnp.arange(s // Q_BLOCK))
    return jnp.transpose(out, (1, 2, 0, 3, 4)).reshape(b, B_HEADS, s, B_V)


def retention(q, k, v, pos):
    b, h, s, dk = q.shape
    dv = v.shape[-1]
    n = s // C_CHUNK
    qf = rope(q, pos).astype(jnp.float32)
    kf = rope(k, pos).astype(jnp.float32) * (dk ** -0.5)
    vf = v.astype(jnp.float32)
    log_g = np.log(1.0 - 2.0 ** (-5.0 - np.arange(h))).astype(np.float32)
    i = np.arange(C_CHUNK, dtype=np.float32)
    rel = i[:, None] - i[None, :]
    decay_intra = jnp.asarray(np.exp(np.maximum(rel, 0.0)[None] * log_g[:, None, None]) * (rel >= 0)[None], dtype=jnp.float32)
    xi = jnp.asarray(np.exp((i + 1.0)[None, :] * log_g[:, None]), dtype=jnp.float32)
    zeta = jnp.asarray(np.exp((C_CHUNK - 1.0 - i)[None, :] * log_g[:, None]), dtype=jnp.float32)
    chunk_decay = jnp.asarray(np.exp(C_CHUNK * log_g), dtype=jnp.float32)

    def chunks(t):
        return t.reshape(b, h, n, C_CHUNK, t.shape[-1]).transpose(2, 0, 1, 3, 4)

    def step(state, qkv):
        qc, kc, vc = qkv
        sc = jnp.einsum('bhid,bhjd->bhij', qc, kc) * decay_intra
        o = jnp.einsum('bhij,bhjv->bhiv', sc, vc) + jnp.einsum('bhid,bhdv->bhiv', qc, state) * xi[:, :, None]
        state = state * chunk_decay[:, None, None] + jnp.einsum('bhjd,bhjv->bhdv', kc * zeta[:, :, None], vc)
        return state, o

    state0 = jnp.zeros((b, h, dk, dv), jnp.float32)
    _, o = lax.scan(step, state0, (chunks(qf), chunks(kf), chunks(vf)))
    return jnp.transpose(o, (1, 2, 0, 3, 4)).reshape(b, h, s, dv).astype(v.dtype)


def setup_inputs(seed: int = 0) -> dict:
    key = jax.random.key(seed)
    ks = jax.random.split(key, 20)

    def nrm(k, shape, sd):
        return jax.random.normal(k, shape, jnp.float32) * sd

    return {
        'x': nrm(ks[0], (BATCH, SEQ, D_MODEL), 1.0),
        'c': nrm(ks[1], (BATCH, D_MODEL), 1.0),
        'positions': jnp.arange(SEQ, dtype=jnp.int32)[None, :] + jax.random.randint(ks[2], (BATCH, 1), 0, 4096, dtype=jnp.int32),
        'rel_bias': nrm(ks[3], (N_BUCKETS, A_HEADS), 0.5),
        'ada_w': nrm(ks[4], (DEPTH, D_MODEL, 6 * D_MODEL), 0.5 * D_MODEL ** -0.5),
        'ada_b': nrm(ks[5], (DEPTH, 6 * D_MODEL), 0.02),
        'norm1_g': 1.0 + nrm(ks[6], (DEPTH, D_MODEL), 0.05),
        'w_in': nrm(ks[7], (DEPTH, D_MODEL, IN_WIDTH), D_MODEL ** -0.5),
        'mla_q_norm': 1.0 + nrm(ks[8], (DEPTH, B_Q_RANK), 0.05),
        'mla_kv_norm': 1.0 + nrm(ks[9], (DEPTH, B_KV_RANK), 0.05),
        'mla_w_uq': nrm(ks[10], (DEPTH, B_Q_RANK, B_HEADS * (B_NOPE + B_ROPE)), B_Q_RANK ** -0.5),
        'mla_w_ukv': nrm(ks[11], (DEPTH, B_KV_RANK, B_HEADS * (B_NOPE + B_V)), B_KV_RANK ** -0.5),
        'mix_gain': 1.0 + nrm(ks[12], (DEPTH, MIX_WIDTH), 0.05),
        'w_out': nrm(ks[13], (DEPTH, MIX_WIDTH, D_MODEL), MIX_WIDTH ** -0.5),
        'norm2_g': 1.0 + nrm(ks[14], (DEPTH, D_MODEL), 0.05),
        'ffn_w_gate': nrm(ks[15], (DEPTH, D_MODEL, FFN_HIDDEN), D_MODEL ** -0.5),
        'ffn_w_up': nrm(ks[16], (DEPTH, D_MODEL, FFN_HIDDEN), D_MODEL ** -0.5),
        'ffn_w_down': nrm(ks[17], (DEPTH, FFN_HIDDEN, D_MODEL), FFN_HIDDEN ** -0.5),
        'final_norm': 1.0 + nrm(ks[18], (D_MODEL,), 0.05),
    }


def reference(x, c, positions, rel_bias, ada_w, ada_b, norm1_g, w_in, mla_q_norm, mla_kv_norm, mla_w_uq, mla_w_ukv, mix_gain, w_out, norm2_g, ffn_w_gate, ffn_w_up, ffn_w_down, final_norm):
    split_at = [int(v) for v in np.cumsum(IN_SIZES)[:-1]]
    c_act = jax.nn.silu(c)
    for l in range(DEPTH):
        mod = (c_act @ ada_w[l] + ada_b[l])[:, None, :]
        sh1, sc1, g1, sh2, sc2, g2 = jnp.split(mod, 6, axis=-1)
        h = rmsnorm(x, norm1_g[l]) * (1.0 + sc1) + sh1
        proj = h @ w_in[l]
        aq, ak, av, bq_lat, bkv_lat, bk_rope, cq, ck, cv, cg = jnp.split(proj, split_at, axis=-1)
        oa = dilated_attention(to_heads(aq, A_HEADS), to_heads(ak, A_HEADS), to_heads(av, A_HEADS), rel_bias)
        ob = mla(bq_lat, bkv_lat, bk_rope, positions, mla_q_norm[l], mla_kv_norm[l], mla_w_uq[l], mla_w_ukv[l])
        oc = retention(to_heads(cq, C_HEADS), to_heads(ck, C_HEADS), to_heads(cv, C_HEADS), positions)
        mix = jnp.concatenate([
            from_heads(rmsnorm(oa)),
            from_heads(rmsnorm(ob)),
            from_heads(group_norm_heads(oc)) * jax.nn.silu(cg),
        ], axis=-1) * mix_gain[l]
        x = x + g1 * (mix @ w_out[l])
        h = rmsnorm(x, norm2_g[l]) * (1.0 + sc2) + sh2
        ffn = (jax.nn.silu(h @ ffn_w_gate[l]) * (h @ ffn_w_up[l])) @ ffn_w_down[l]
        x = x + g2 * ffn
    return rmsnorm(x, final_norm)
```

```python
import functools

import numpy as np
import jax
import jax.numpy as jnp
from jax import lax
from jax.experimental import pallas as pl
from jax.experimental.pallas import tpu as pltpu

F32 = jnp.float32
BF16 = jnp.bfloat16

HEAD_DIM = 64
A_HEADS = 6
A_PATTERNS = ((128, 1), (512, 4), (2048, 16))
A_WIDTH = A_HEADS * HEAD_DIM
B_HEADS = 4
B_NOPE = 64
B_ROPE = 32
B_V = 64
B_Q_RANK = 256
B_KV_RANK = 128
B_WIDTH = B_HEADS * B_V
C_HEADS = 6
C_KEY = 32
C_VAL = 64
C_WIDTH = C_HEADS * C_VAL
C_QK = C_HEADS * C_KEY
C_CHUNK = 128
N_BUCKETS = 32
MAX_DISTANCE = 2048
ROPE_BASE = 10000.0
EPS = 1e-6

LANES = 128
WIN = 128
NEG = -1e30
VMEM_LIMIT = 56 * 1024 * 1024

OFF_A = 0
OFF_BQ = 3 * A_WIDTH
OFF_BKV = OFF_BQ + B_Q_RANK
OFF_C = OFF_BKV + B_KV_RANK
OFF_ROPE = OFF_C + 2 * C_QK + 2 * C_WIDTH
IN_COLS = OFF_ROPE + LANES


def _const_spec(shape, index_map):
    return pl.BlockSpec(shape, index_map, pipeline_mode=pl.Buffered(1))


def _silu(x):
    return x / (1.0 + jnp.exp(-x))


def _rms(x):
    return x * lax.rsqrt(jnp.mean(x * x, axis=-1, keepdims=True) + EPS)


def _mods_kernel(c_ref, w_ref, b_ref, o_ref):
    c = c_ref[...]
    o_ref[...] = jnp.dot(_silu(c), w_ref[...], preferred_element_type=F32,
                         precision=lax.Precision.HIGHEST) + b_ref[...]


def _mods(c, ada_w, ada_b):
    depth, d, n = ada_w.shape
    b = c.shape[0]
    tn = 1536
    return pl.pallas_call(
        _mods_kernel,
        grid=(depth, n // tn),
        in_specs=[
            pl.BlockSpec((b, d), lambda l, j: (0, 0)),
            pl.BlockSpec((None, d, tn), lambda l, j: (l, 0, j)),
            pl.BlockSpec((None, 1, tn), lambda l, j: (l, 0, j)),
        ],
        out_specs=pl.BlockSpec((None, b, tn), lambda l, j: (l, 0, j)),
        out_shape=jax.ShapeDtypeStruct((depth, b, n), F32),
        compiler_params=pltpu.CompilerParams(vmem_limit_bytes=VMEM_LIMIT),
        name="adaln_mods",
    )(c, ada_w, ada_b.reshape(depth, 1, n))


def _rope_kernel(pos_ref, f_ref, s_ref, cb_ref, sb_ref, cc_ref, sc_ref):
    p = pos_ref[...].astype(F32)
    ang_b = p * f_ref[0:1, :]
    ang_c = p * f_ref[1:2, :]
    cb_ref[...] = jnp.cos(ang_b)
    sb_ref[...] = jnp.sin(ang_b) * s_ref[0:1, :]
    cc_ref[...] = jnp.cos(ang_c)
    sc_ref[...] = jnp.sin(ang_c) * s_ref[1:2, :]


def _rope_tables(positions):
    t = positions.size
    half = B_ROPE // 2
    inv_freq = (1.0 / (ROPE_BASE ** (np.arange(half, dtype=np.float32) / half))).astype(np.float32)
    zeros = np.zeros(half, np.float32)
    ones = np.ones(half, np.float32)
    f_b = np.concatenate([np.zeros(B_NOPE, np.float32), inv_freq, inv_freq, np.zeros(32, np.float32)])
    s_b = np.concatenate([np.zeros(B_NOPE, np.float32), -ones, ones, np.zeros(32, np.float32)])
    f_c = np.tile(np.concatenate([inv_freq, inv_freq]), LANES // C_KEY)
    s_c = np.tile(np.concatenate([-ones, ones]), LANES // C_KEY)
    del zeros
    freqs = jnp.asarray(np.stack([f_b, f_c]))
    signs = jnp.asarray(np.stack([s_b, s_c]))
    tr = 2048
    out = jax.ShapeDtypeStruct((t, LANES), F32)
    row = pl.BlockSpec((tr, LANES), lambda i: (i, 0))
    return pl.pallas_call(
        _rope_kernel,
        grid=(t // tr,),
        in_specs=[
            pl.BlockSpec((tr, 1), lambda i: (i, 0)),
            pl.BlockSpec((2, LANES), lambda i: (0, 0)),
            pl.BlockSpec((2, LANES), lambda i: (0, 0)),
        ],
        out_specs=[row, row, row, row],
        out_shape=[out, out, out, out],
        name="rope_tables",
    )(positions.reshape(t, 1), freqs, signs)


def _t5_bucket(dist):
    max_exact = N_BUCKETS // 2
    safe = np.maximum(dist, 1).astype(np.float32)
    large = max_exact + (np.log(safe / max_exact) / np.log(MAX_DISTANCE / max_exact)
                         * (N_BUCKETS - max_exact)).astype(np.int32)
    large = np.minimum(large, N_BUCKETS - 1)
    return np.where(dist < max_exact, dist, large).astype(np.int32)


def _bias_kernel(bmap_ref, rb_ref, o_ref):
    h = pl.program_id(1)
    bm = bmap_ref[...]
    t = jnp.full(bm.shape, NEG, F32)
    for b in range(N_BUCKETS):
        t = jnp.where(bm == b, rb_ref[b, h], t)
    o_ref[...] = t


def _bias_tables(rel_bias):
    qi = np.arange(WIN)[:, None]
    c = np.arange(2 * WIN)[None, :]
    j = qi - c + WIN
    maps = []
    for (w, d) in A_PATTERNS:
        assert w // d == WIN
        bucket = _t5_bucket(np.arange(WIN + 1, dtype=np.int32) * d)
        maps.append(np.where((j >= 0) & (j <= WIN), bucket[np.clip(j, 0, WIN)], -1).astype(np.int32))
    bmap = jnp.asarray(np.stack(maps))
    npat = len(A_PATTERNS)
    return pl.pallas_call(
        _bias_kernel,
        grid=(npat, A_HEADS),
        in_specs=[
            pl.BlockSpec((None, WIN, 2 * WIN), lambda p, h: (p, 0, 0)),
            pl.BlockSpec(memory_space=pltpu.SMEM),
        ],
        out_specs=pl.BlockSpec((None, WIN, 2 * WIN), lambda p, h: (p, h, 0)),
        out_shape=jax.ShapeDtypeStruct((npat, A_HEADS * WIN, 2 * WIN), F32),
        name="t5_bias_tables",
    )(bmap, rel_bias)


def _rot_half(x, x1_mask):
    w = x.shape[-1]
    half = B_ROPE // 2
    return jnp.where(x1_mask, pltpu.roll(x, w - half, 1), pltpu.roll(x, half, 1))


def _inproj_kernel(x_ref, mod_ref, g_ref, w_ref, qg_ref, kvg_ref, wuq_ref, wuk_ref, wuv_ref,
                   cb_ref, sb_ref, cc_ref, sc_ref, ks_ref,
                   a_ref, bq_ref, bk_ref, bv_ref, cqk_ref, cv_ref, cg_ref):
    x = x_ref[...]
    tm = x.shape[0]
    h = (_rms(x) * g_ref[...] * (1.0 + mod_ref[1:2, :]) + mod_ref[0:1, :]).astype(BF16)

    def proj(lo, hi):
        return jnp.dot(h, w_ref[:, lo:hi], preferred_element_type=F32)

    a_ref[:, 0:A_WIDTH] = (proj(OFF_A, OFF_A + A_WIDTH) * (HEAD_DIM ** -0.5)).astype(BF16)
    a_ref[:, A_WIDTH:3 * A_WIDTH] = proj(OFF_A + A_WIDTH, OFF_BQ).astype(BF16)

    lane = lax.broadcasted_iota(jnp.int32, (tm, LANES), 1)
    b_x1 = (lane >= B_NOPE) & (lane < B_NOPE + B_ROPE // 2)
    c_x1 = (lane & (C_KEY // 2)) == 0
    cos_b = cb_ref[...]
    sin_b = sb_ref[...]

    q_lat = proj(OFF_BQ, OFF_BKV)
    qn = (_rms(q_lat) * qg_ref[...]).astype(BF16)
    q = jnp.dot(qn, wuq_ref[...], preferred_element_type=F32)
    kv_lat = proj(OFF_BKV, OFF_C)
    kvn = (_rms(kv_lat) * kvg_ref[...]).astype(BF16)
    k_nope = jnp.dot(kvn, wuk_ref[...], preferred_element_type=F32)
    bv_ref[...] = jnp.dot(kvn, wuv_ref[...], preferred_element_type=F32).astype(BF16)
    k_rope = proj(OFF_ROPE, IN_COLS)
    k_pe = k_rope * cos_b + _rot_half(k_rope, b_x1) * sin_b
    b_scale = (B_NOPE + B_ROPE) ** -0.5
    for hh in range(B_HEADS):
        sl = slice(hh * LANES, (hh + 1) * LANES)
        qh = q[:, sl]
        bq_ref[:, sl] = ((qh * cos_b + _rot_half(qh, b_x1) * sin_b) * b_scale).astype(BF16)
        bk_ref[:, sl] = (k_nope[:, sl] + k_pe).astype(BF16)

    cos_c = cc_ref[...]
    sin_c = sc_ref[...]
    qk = proj(OFF_C, OFF_C + 2 * C_QK)
    for t in range(2 * C_QK // LANES):
        sl = slice(t * LANES, (t + 1) * LANES)
        xt = qk[:, sl]
        cqk_ref[:, sl] = ((xt * cos_c + _rot_half(xt, c_x1) * sin_c) * ks_ref[:, sl]).astype(BF16)
    cv_ref[...] = proj(OFF_C + 2 * C_QK, OFF_C + 2 * C_QK + C_WIDTH).astype(BF16)
    cg_ref[...] = _silu(proj(OFF_C + 2 * C_QK + C_WIDTH, OFF_ROPE)).astype(BF16)


def _inproj(x, mods, layer, prep, tables, batch, seq, tm):
    t, d = x.shape
    nt = seq // tm
    row = lambda w: pl.BlockSpec((tm, w), lambda b, i: (b * nt + i, 0))
    wl = lambda a: _const_spec((None,) + a.shape[1:], lambda b, i: (layer,) + (0,) * (a.ndim - 1))
    cos_b, sin_b, cos_c, sin_c = tables
    kscale = jnp.asarray(np.concatenate([np.ones(C_QK, np.float32),
                                         np.full(C_QK, C_KEY ** -0.5, np.float32)])[None, :])
    outs = [(3 * A_WIDTH, BF16), (B_HEADS * LANES, BF16), (B_HEADS * LANES, BF16), (B_WIDTH, BF16),
            (2 * C_QK, BF16), (C_WIDTH, BF16), (C_WIDTH, BF16)]
    return pl.pallas_call(
        _inproj_kernel,
        grid=(batch, nt),
        in_specs=[
            row(d),
            pl.BlockSpec((None, None, 6, d), lambda b, i: (layer, b, 0, 0)),
            wl(prep["norm1_g"]), wl(prep["w_in"]), wl(prep["q_norm"]), wl(prep["kv_norm"]),
            wl(prep["w_uq"]), wl(prep["w_uk"]), wl(prep["w_uv"]),
            row(LANES), row(LANES), row(LANES), row(LANES),
            _const_spec((1, 2 * C_QK), lambda b, i: (0, 0)),
        ],
        out_specs=[row(w) for w, _ in outs],
        out_shape=[jax.ShapeDtypeStruct((t, w), dt) for w, dt in outs],
        compiler_params=pltpu.CompilerParams(
            dimension_semantics=("arbitrary", "arbitrary"), vmem_limit_bytes=VMEM_LIMIT),
        name="inproj",
    )(x, mods, prep["norm1_g"], prep["w_in"], prep["q_norm"], prep["kv_norm"],
      prep["w_uq"], prep["w_uk"], prep["w_uv"], cos_b, sin_b, cos_c, sin_c, kscale)


def _dilated_kernel(q_ref, k_ref, v_ref, bias_ref, gain_ref, o_ref,
                    qf, kf, vf, acc2, m2, l2, acc3, m3, l3, *, seq):
    qf[...] = q_ref[...].astype(F32)
    kf[...] = k_ref[...].astype(F32)
    vf[...] = v_ref[...].astype(F32)
    lane = lax.broadcasted_iota(jnp.int32, (WIN, LANES), 1)
    lo = lane < HEAD_DIM

    def rows(start, d):
        return pl.ds(start, WIN) if d == 1 else pl.ds(start, WIN, stride=d)

    def block(p, start, d, first):
        q = qf[rows(start, d), :]
        q2 = jnp.concatenate([jnp.where(lo, q, 0.0), jnp.where(lo, 0.0, q)], axis=0).astype(BF16)
        if first:
            kcat = kf[rows(start, d), :].astype(BF16)
            vcat = vf[rows(start, d), :].astype(BF16)
            bias = bias_ref[p, :, WIN:2 * WIN]
        else:
            prev = start - d * WIN
            kcat = jnp.concatenate([kf[rows(prev, d), :], kf[rows(start, d), :]], axis=0).astype(BF16)
            vcat = jnp.concatenate([vf[rows(prev, d), :], vf[rows(start, d), :]], axis=0).astype(BF16)
            bias = bias_ref[p]
        s = lax.dot_general(q2, kcat, (((1,), (1,)), ((), ())), preferred_element_type=F32) + bias
        m = jnp.max(s, axis=1, keepdims=True)
        e = jnp.exp(s - m)
        l = jnp.sum(e, axis=1, keepdims=True)
        acc = jnp.dot(e.astype(BF16), vcat, preferred_element_type=F32)
        return (jnp.where(lo, acc[:WIN], acc[WIN:]),
                jnp.where(lo, m[:WIN], m[WIN:]),
                jnp.where(lo, l[:WIN], l[WIN:]))

    def run_dilated(p, d, acc_ref, m_ref, l_ref):
        nb = seq // d // WIN

        def store(start, res):
            acc, m, l = res
            acc_ref[rows(start, d), :] = acc
            m_ref[rows(start, d), :] = m
            l_ref[rows(start, d), :] = l

        def per_class(r, carry):
            store(r, block(p, r, d, True))
            if nb > 1:
                def per_block(mb, c):
                    start = r + d * WIN * mb
                    store(start, block(p, start, d, False))
                    return c
                lax.fori_loop(1, nb, per_block, 0)
            return carry

        lax.fori_loop(0, d, per_class, 0)

    run_dilated(1, A_PATTERNS[1][1], acc2, m2, l2)
    run_dilated(2, A_PATTERNS[2][1], acc3, m3, l3)

    gain = gain_ref[...]

    def finish(start, res):
        acc1, m1, l1 = res
        sl = pl.ds(start, WIN)
        mb, mc = m2[sl, :], m3[sl, :]
        mx = jnp.maximum(jnp.maximum(m1, mb), mc)
        w1, w2, w3 = jnp.exp(m1 - mx), jnp.exp(mb - mx), jnp.exp(mc - mx)
        num = w1 * acc1 + w2 * acc2[sl, :] + w3 * acc3[sl, :]
        den = w1 * l1 + w2 * l2[sl, :] + w3 * l3[sl, :]
        o = num / den
        sq = o * o
        ms_lo = jnp.sum(jnp.where(lo, sq, 0.0), axis=1, keepdims=True)
        ms_hi = jnp.sum(jnp.where(lo, 0.0, sq), axis=1, keepdims=True)
        inv = lax.rsqrt(jnp.where(lo, ms_lo, ms_hi) * (1.0 / HEAD_DIM) + EPS)
        o_ref[sl, :] = (o * inv * gain).astype(o_ref.dtype)

    finish(0, block(0, 0, 1, True))

    def per_block(mb, c):
        start = pl.multiple_of(mb * WIN, WIN)
        finish(start, block(0, start, 1, False))
        return c

    lax.fori_loop(1, seq // WIN, per_block, 0)


def _dilated(a_qkv, bias, gain_a, batch, seq):
    t = a_qkv.shape[0]
    pairs = A_WIDTH // LANES
    assert A_PATTERNS[0][1] == 1 and all(seq % (d * WIN) == 0 for _, d in A_PATTERNS)
    npat = bias.shape[0]
    col = lambda off: pl.BlockSpec((seq, LANES), lambda b, g: (b, off + g))
    scratch = [pltpu.VMEM((seq, LANES), F32) for _ in range(9)]
    return pl.pallas_call(
        functools.partial(_dilated_kernel, seq=seq),
        grid=(batch, pairs),
        in_specs=[
            col(0), col(pairs), col(2 * pairs),
            pl.BlockSpec((npat, 2 * WIN, 2 * WIN), lambda b, g: (0, g, 0)),
            pl.BlockSpec((1, LANES), lambda b, g: (0, g)),
        ],
        out_specs=pl.BlockSpec((seq, LANES), lambda b, g: (b, g)),
        out_shape=jax.ShapeDtypeStruct((t, A_WIDTH), BF16),
        scratch_shapes=scratch,
        compiler_params=pltpu.CompilerParams(
            dimension_semantics=("arbitrary", "arbitrary"), vmem_limit_bytes=VMEM_LIMIT),
        name="dilated_attention",
    )(a_qkv, a_qkv, a_qkv, bias, gain_a)


def _mla_kernel(q_ref, k_ref, v_ref, gain_ref, o_ref, *, seq, tq):
    lane = lax.broadcasted_iota(jnp.int32, (tq, LANES), 1)
    lo = lane < B_V
    row = lax.broadcasted_iota(jnp.int32, (tq, tq), 0)
    colm = lax.broadcasted_iota(jnp.int32, (tq, tq), 1)
    causal = colm <= row
    gain = gain_ref[...]

    def head(hh, q0, i):
        hs = slice(hh * LANES, (hh + 1) * LANES)
        q = q_ref[pl.ds(q0, tq), hs]

        def scores(k0):
            k = k_ref[pl.ds(k0, tq), hs]
            return lax.dot_general(q, k, (((1,), (1,)), ((), ())), preferred_element_type=F32)

        s = jnp.where(causal, scores(q0), NEG)
        m = jnp.max(s, axis=1, keepdims=True)
        e = jnp.exp(s - m)
        l = jnp.sum(e, axis=1, keepdims=True)
        acc = jnp.dot(e.astype(BF16), v_ref[pl.ds(q0, tq), :], preferred_element_type=F32)

        def kv_step(j, carry):
            m, l, acc = carry
            k0 = pl.multiple_of(j * tq, tq)
            s = scores(k0)
            m_new = jnp.maximum(m, jnp.max(s, axis=1, keepdims=True))
            alpha = jnp.exp(m - m_new)
            e = jnp.exp(s - m_new)
            l = alpha * l + jnp.sum(e, axis=1, keepdims=True)
            acc = alpha * acc + jnp.dot(e.astype(BF16), v_ref[pl.ds(k0, tq), :],
                                        preferred_element_type=F32)
            return m_new, l, acc

        m, l, acc = lax.fori_loop(0, i, kv_step, (m, l, acc))
        return acc / l

    def q_block(i, c):
        q0 = pl.multiple_of(i * tq, tq)
        o = jnp.where(lo, head(0, q0, i), head(1, q0, i))
        sq = o * o
        ms_lo = jnp.sum(jnp.where(lo, sq, 0.0), axis=1, keepdims=True)
        ms_hi = jnp.sum(jnp.where(lo, 0.0, sq), axis=1, keepdims=True)
        inv = lax.rsqrt(jnp.where(lo, ms_lo, ms_hi) * (1.0 / B_V) + EPS)
        o_ref[pl.ds(q0, tq), :] = (o * inv * gain).astype(o_ref.dtype)
        return c

    lax.fori_loop(0, seq // tq, q_block, 0)


def _mla(bq, bk, bv, gain_b, batch, seq):
    t = bq.shape[0]
    pairs = B_HEADS // 2
    tq = 256
    return pl.pallas_call(
        functools.partial(_mla_kernel, seq=seq, tq=tq),
        grid=(batch, pairs),
        in_specs=[
            pl.BlockSpec((seq, 2 * LANES), lambda b, g: (b, g)),
            pl.BlockSpec((seq, 2 * LANES), lambda b, g: (b, g)),
            pl.BlockSpec((seq, LANES), lambda b, g: (b, g)),
            pl.BlockSpec((1, LANES), lambda b, g: (0, g)),
        ],
        out_specs=pl.BlockSpec((seq, LANES), lambda b, g: (b, g)),
        out_shape=jax.ShapeDtypeStruct((t, B_WIDTH), BF16),
        compiler_params=pltpu.CompilerParams(
            dimension_semantics=("arbitrary", "arbitrary"), vmem_limit_bytes=VMEM_LIMIT),
        name="latent_attention",
    )(bq, bk, bv, gain_b)


def _retention_consts():
    h = C_HEADS
    log_g = np.log(1.0 - 2.0 ** (-5.0 - np.arange(h))).astype(np.float32)
    i = np.arange(C_CHUNK, dtype=np.float32)
    rel = i[:, None] - i[None, :]
    decay = (np.exp(np.maximum(rel, 0.0)[None] * log_g[:, None, None]) * (rel >= 0)[None]).astype(np.float32)
    xi = np.exp((i + 1.0)[None, :] * log_g[:, None]).astype(np.float32)
    zeta = np.exp((C_CHUNK - 1.0 - i)[None, :] * log_g[:, None]).astype(np.float32)
    chunk_decay = np.exp(C_CHUNK * log_g).astype(np.float32)
    w = 2 * C_QK
    xi_mat = np.repeat(xi.T, C_VAL, axis=1)
    zeta_mat = np.zeros((C_CHUNK, w), np.float32)
    zeta_mat[:, :C_QK] = np.repeat(zeta.T, C_KEY, axis=1)
    cd = np.repeat(chunk_decay, C_VAL)[None, :]
    bd = np.zeros((w, C_WIDTH), np.float32)
    for hh in range(h):
        bd[hh * C_KEY:(hh + 1) * C_KEY, hh * C_VAL:(hh + 1) * C_VAL] = 1.0
    return tuple(jnp.asarray(a) for a in (decay, xi_mat, zeta_mat, cd, bd))


def _retention_kernel(qk_ref, v_ref, g_ref, decay_ref, xi_ref, zeta_ref, cd_ref, bd_ref, gain_ref,
                      o_ref, state_ref, *, seq):
    w = 2 * C_QK
    state_ref[...] = jnp.zeros(state_ref.shape, F32)
    lane_qk = lax.broadcasted_iota(jnp.int32, (C_CHUNK, w), 1)
    lane_v = lax.broadcasted_iota(jnp.int32, (C_CHUNK, C_WIDTH), 1)
    lane = lax.broadcasted_iota(jnp.int32, (C_CHUNK, LANES), 1)
    lo = lane < C_VAL
    gain = gain_ref[...]

    def chunk(n, c):
        r0 = pl.multiple_of(n * C_CHUNK, C_CHUNK)
        rows = pl.ds(r0, C_CHUNK)
        qk = qk_ref[rows, :]
        qk32 = qk.astype(F32)
        kk32 = pltpu.roll(qk32, C_QK, 1)
        kk = kk32.astype(BF16)
        v = v_ref[rows, :]
        state = state_ref[...]
        o = jnp.dot(qk, state.astype(BF16), preferred_element_type=F32) * xi_ref[...]
        for hh in range(C_HEADS):
            q_h = jnp.where((lane_qk >= hh * C_KEY) & (lane_qk < (hh + 1) * C_KEY), qk32, 0.0).astype(BF16)
            s = lax.dot_general(q_h, kk, (((1,), (1,)), ((), ())), preferred_element_type=F32)
            s = (s * decay_ref[hh]).astype(BF16)
            o_h = jnp.dot(s, v, preferred_element_type=F32)
            o = o + jnp.where((lane_v >= hh * C_VAL) & (lane_v < (hh + 1) * C_VAL), o_h, 0.0)
        kz = (kk32 * zeta_ref[...]).astype(BF16)
        upd = lax.dot_general(kz, v, (((0,), (0,)), ((), ())), preferred_element_type=F32)
        state_ref[...] = state * cd_ref[...] + upd * bd_ref[...]
        for t in range(C_WIDTH // LANES):
            sl = slice(t * LANES, (t + 1) * LANES)
            x = o[:, sl]
            mu_lo = jnp.sum(jnp.where(lo, x, 0.0), axis=1, keepdims=True)
            mu_hi = jnp.sum(jnp.where(lo, 0.0, x), axis=1, keepdims=True)
            dlt = x - jnp.where(lo, mu_lo, mu_hi) * (1.0 / C_VAL)
            sq = dlt * dlt
            var_lo = jnp.sum(jnp.where(lo, sq, 0.0), axis=1, keepdims=True)
            var_hi = jnp.sum(jnp.where(lo, 0.0, sq), axis=1, keepdims=True)
            y = dlt * lax.rsqrt(jnp.where(lo, var_lo, var_hi) * (1.0 / C_VAL) + EPS)
            o_ref[rows, sl] = (y * g_ref[rows, sl].astype(F32) * gain[:, sl]).astype(o_ref.dtype)
        return c

    lax.fori_loop(0, seq // C_CHUNK, chunk, 0)


def _retention(cqk, cv, cg, gain_c, consts, batch, seq):
    t = cqk.shape[0]
    w = 2 * C_QK
    decay, xi_mat, zeta_mat, cd, bd = consts
    tok = lambda width: pl.BlockSpec((seq, width), lambda b: (b, 0))
    full = lambda a: _const_spec(a.shape, lambda b: (0,) * a.ndim)
    return pl.pallas_call(
        functools.partial(_retention_kernel, seq=seq),
        grid=(batch,),
        in_specs=[tok(w), tok(C_WIDTH), tok(C_WIDTH),
                  full(decay), full(xi_mat), full(zeta_mat), full(cd), full(bd), full(gain_c)],
        out_specs=tok(C_WIDTH),
        out_shape=jax.ShapeDtypeStruct((t, C_WIDTH), BF16),
        scratch_shapes=[pltpu.VMEM((w, C_WIDTH), F32)],
        compiler_params=pltpu.CompilerParams(
            dimension_semantics=("arbitrary",), vmem_limit_bytes=VMEM_LIMIT),
        name="retention",
    )(cqk, cv, cg, decay, xi_mat, zeta_mat, cd, bd, gain_c)


def _outffn_kernel(x_ref, ma_ref, mb_ref, mc_ref, mod_ref, wo_ref, g_ref, wg_ref, wu_ref, wd_ref, fg_ref,
                   o_ref, mix_ref, hid_ref, *, final, fchunk):
    mix_ref[:, 0:A_WIDTH] = ma_ref[...]
    mix_ref[:, A_WIDTH:A_WIDTH + B_WIDTH] = mb_ref[...]
    mix_ref[:, A_WIDTH + B_WIDTH:] = mc_ref[...]
    att = jnp.dot(mix_ref[...], wo_ref[...], preferred_element_type=F32)
    x = x_ref[...] + mod_ref[2:3, :] * att
    h = (_rms(x) * g_ref[...] * (1.0 + mod_ref[4:5, :]) + mod_ref[3:4, :]).astype(BF16)
    hidden = wg_ref.shape[1]
    for j in range(hidden // fchunk):
        sl = slice(j * fchunk, (j + 1) * fchunk)
        gate = jnp.dot(h, wg_ref[:, sl], preferred_element_type=F32)
        up = jnp.dot(h, wu_ref[:, sl], preferred_element_type=F32)
        hid_ref[:, sl] = (_silu(gate) * up).astype(BF16)
    ffn = jnp.dot(hid_ref[...], wd_ref[...], preferred_element_type=F32)
    y = x + mod_ref[5:6, :] * ffn
    if final:
        y = _rms(y) * fg_ref[...]
    o_ref[...] = y


def _outffn(x, mix_a, mix_b, mix_c, mods, layer, prep, final_g, batch, seq, tm, final):
    t, d = x.shape
    nt = seq // tm
    hidden = prep["w_gate"].shape[2]
    row = lambda w: pl.BlockSpec((tm, w), lambda b, i: (b * nt + i, 0))
    wl = lambda a: _const_spec((None,) + a.shape[1:], lambda b, i: (layer,) + (0,) * (a.ndim - 1))
    return pl.pallas_call(
        functools.partial(_outffn_kernel, final=final, fchunk=256),
        grid=(batch, nt),
        in_specs=[
            row(d), row(A_WIDTH), row(B_WIDTH), row(C_WIDTH),
            pl.BlockSpec((None, None, 6, d), lambda b, i: (layer, b, 0, 0)),
            wl(prep["w_out"]), wl(prep["norm2_g"]), wl(prep["w_gate"]), wl(prep["w_up"]), wl(prep["w_down"]),
            _const_spec((1, d), lambda b, i: (0, 0)),
        ],
        out_specs=row(d),
        out_shape=jax.ShapeDtypeStruct((t, d), F32),
        scratch_shapes=[pltpu.VMEM((tm, d), BF16), pltpu.VMEM((tm, hidden), BF16)],
        compiler_params=pltpu.CompilerParams(
            dimension_semantics=("arbitrary", "arbitrary"), vmem_limit_bytes=VMEM_LIMIT),
        name="outproj_ffn",
    )(x, mix_a, mix_b, mix_c, mods, prep["w_out"], prep["norm2_g"], prep["w_gate"], prep["w_up"],
      prep["w_down"], final_g)


def _prepare(norm1_g, w_in, mla_q_norm, mla_kv_norm, mla_w_uq, mla_w_ukv, w_out, norm2_g,
             ffn_w_gate, ffn_w_up, ffn_w_down):
    depth, d, _ = w_in.shape
    b_lat0 = 3 * A_WIDTH
    rope0 = b_lat0 + B_Q_RANK + B_KV_RANK
    c0 = rope0 + B_ROPE
    pad = lambda n: jnp.zeros((depth, d, n), w_in.dtype)
    w_in_p = jnp.concatenate([
        w_in[:, :, :rope0],
        w_in[:, :, c0:],
        pad(B_NOPE), w_in[:, :, rope0:c0], pad(LANES - B_NOPE - B_ROPE),
    ], axis=2).astype(BF16)
    assert w_in_p.shape[2] == IN_COLS
    uq = mla_w_uq.reshape(depth, B_Q_RANK, B_HEADS, B_NOPE + B_ROPE)
    uq = jnp.pad(uq, ((0, 0), (0, 0), (0, 0), (0, LANES - B_NOPE - B_ROPE)))
    ukv = mla_w_ukv.reshape(depth, B_KV_RANK, B_HEADS, B_NOPE + B_V)
    uk = jnp.pad(ukv[..., :B_NOPE], ((0, 0), (0, 0), (0, 0), (0, LANES - B_NOPE)))
    uv = ukv[..., B_NOPE:]
    return {
        "norm1_g": norm1_g[:, None, :],
        "w_in": w_in_p,
        "q_norm": mla_q_norm[:, None, :],
        "kv_norm": mla_kv_norm[:, None, :],
        "w_uq": uq.reshape(depth, B_Q_RANK, B_HEADS * LANES).astype(BF16),
        "w_uk": uk.reshape(depth, B_KV_RANK, B_HEADS * LANES).astype(BF16),
        "w_uv": uv.reshape(depth, B_KV_RANK, B_WIDTH).astype(BF16),
        "w_out": w_out.astype(BF16),
        "norm2_g": norm2_g[:, None, :],
        "w_gate": ffn_w_gate.astype(BF16),
        "w_up": ffn_w_up.astype(BF16),
        "w_down": ffn_w_down.astype(BF16),
    }


def kernel(x, c, positions, rel_bias, ada_w, ada_b, norm1_g, w_in, mla_q_norm, mla_kv_norm, mla_w_uq,
           mla_w_ukv, mix_gain, w_out, norm2_g, ffn_w_gate, ffn_w_up, ffn_w_down, final_norm):
    batch, seq, d = x.shape
    depth = w_in.shape[0]
    tm = 512
    assert seq % tm == 0 and seq % C_CHUNK == 0

    prep = _prepare(norm1_g, w_in, mla_q_norm, mla_kv_norm, mla_w_uq, mla_w_ukv, w_out, norm2_g,
                    ffn_w_gate, ffn_w_up, ffn_w_down)
    mods = _mods(c, ada_w, ada_b).reshape(depth, batch, 6, d)
    tables = _rope_tables(positions)
    bias = _bias_tables(rel_bias)
    ret_consts = _retention_consts()
    final_g = final_norm[None, :]

    xf = x.reshape(batch * seq, d)
    for l in range(depth):
        a_qkv, bq, bk, bv, cqk, cv, cg = _inproj(xf, mods, l, prep, tables, batch, seq, tm)
        gain = mix_gain[l][None, :]
        mix_a = _dilated(a_qkv, bias, gain[:, :A_WIDTH], batch, seq)
        mix_b = _mla(bq, bk, bv, gain[:, A_WIDTH:A_WIDTH + B_WIDTH], batch, seq)
        mix_c = _retention(cqk, cv, cg, gain[:, A_WIDTH + B_WIDTH:], ret_consts, batch, seq)
        xf = _outffn(xf, mix_a, mix_b, mix_c, mods, l, prep, final_g, batch, seq, tm, final=(l == depth - 1))
    return xf.reshape(batch, seq, d)
```

```python
import functools

import numpy as np
import jax
import jax.numpy as jnp
from jax import lax
from jax.experimental import pallas as pl
from jax.experimental.pallas import tpu as pltpu

F32 = jnp.float32
BF16 = jnp.bfloat16

HEAD_DIM = 64
A_HEADS = 6
A_PATTERNS = ((128, 1), (512, 4), (2048, 16))
A_WIDTH = A_HEADS * HEAD_DIM
B_HEADS = 4
B_NOPE = 64
B_ROPE = 32
B_V = 64
B_Q_RANK = 256
B_KV_RANK = 128
B_WIDTH = B_HEADS * B_V
C_HEADS = 6
C_KEY = 32
C_VAL = 64
C_WIDTH = C_HEADS * C_VAL
C_QK = C_HEADS * C_KEY
C_CHUNK = 128
N_BUCKETS = 32
MAX_DISTANCE = 2048
ROPE_BASE = 10000.0
EPS = 1e-6

LANES = 128
WIN = 128
NEG = -1e30
LOG2E = 1.4426950408889634
VMEM_LIMIT = 56 * 1024 * 1024

OFF_A = 0
OFF_BQ = 3 * A_WIDTH
OFF_BKV = OFF_BQ + B_Q_RANK
OFF_C = OFF_BKV + B_KV_RANK
OFF_ROPE = OFF_C + 2 * C_QK + 2 * C_WIDTH
IN_COLS = OFF_ROPE + LANES


def _const_spec(shape, index_map):
    return pl.BlockSpec(shape, index_map, pipeline_mode=pl.Buffered(1))


def _silu(x):
    return x / (1.0 + jnp.exp(-x))


def _rms(x):
    return x * lax.rsqrt(jnp.mean(x * x, axis=-1, keepdims=True) + EPS)


def _mods_kernel(c_ref, w_ref, b_ref, o_ref):
    c = c_ref[...]
    o_ref[...] = jnp.dot(_silu(c), w_ref[...], preferred_element_type=F32,
                         precision=lax.Precision.HIGHEST) + b_ref[...]


def _mods(c, ada_w, ada_b):
    depth, d, n = ada_w.shape
    b = c.shape[0]
    tn = 1536
    return pl.pallas_call(
        _mods_kernel,
        grid=(depth, n // tn),
        in_specs=[
            pl.BlockSpec((b, d), lambda l, j: (0, 0)),
            pl.BlockSpec((None, d, tn), lambda l, j: (l, 0, j)),
            pl.BlockSpec((None, 1, tn), lambda l, j: (l, 0, j)),
        ],
        out_specs=pl.BlockSpec((None, b, tn), lambda l, j: (l, 0, j)),
        out_shape=jax.ShapeDtypeStruct((depth, b, n), F32),
        compiler_params=pltpu.CompilerParams(vmem_limit_bytes=VMEM_LIMIT),
        name="adaln_mods",
    )(c, ada_w, ada_b.reshape(depth, 1, n))


def _rope_kernel(pos_ref, f_ref, s_ref, cb_ref, sb_ref, cc_ref, sc_ref):
    p = pos_ref[...].astype(F32)
    ang_b = p * f_ref[0:1, :]
    ang_c = p * f_ref[1:2, :]
    cb_ref[...] = jnp.cos(ang_b)
    sb_ref[...] = jnp.sin(ang_b) * s_ref[0:1, :]
    cc_ref[...] = jnp.cos(ang_c)
    sc_ref[...] = jnp.sin(ang_c) * s_ref[1:2, :]


def _rope_tables(positions):
    t = positions.size
    half = B_ROPE // 2
    inv_freq = (1.0 / (ROPE_BASE ** (np.arange(half, dtype=np.float32) / half))).astype(np.float32)
    zeros = np.zeros(half, np.float32)
    ones = np.ones(half, np.float32)
    f_b = np.concatenate([np.zeros(B_NOPE, np.float32), inv_freq, inv_freq, np.zeros(32, np.float32)])
    s_b = np.concatenate([np.zeros(B_NOPE, np.float32), -ones, ones, np.zeros(32, np.float32)])
    f_c = np.tile(np.concatenate([inv_freq, inv_freq]), LANES // C_KEY)
    s_c = np.tile(np.concatenate([-ones, ones]), LANES // C_KEY)
    del zeros
    freqs = jnp.asarray(np.stack([f_b, f_c]))
    signs = jnp.asarray(np.stack([s_b, s_c]))
    tr = 2048
    out = jax.ShapeDtypeStruct((t, LANES), F32)
    row = pl.BlockSpec((tr, LANES), lambda i: (i, 0))
    return pl.pallas_call(
        _rope_kernel,
        grid=(t // tr,),
        in_specs=[
            pl.BlockSpec((tr, 1), lambda i: (i, 0)),
            pl.BlockSpec((2, LANES), lambda i: (0, 0)),
            pl.BlockSpec((2, LANES), lambda i: (0, 0)),
        ],
        out_specs=[row, row, row, row],
        out_shape=[out, out, out, out],
        name="rope_tables",
    )(positions.reshape(t, 1), freqs, signs)


def _t5_bucket(dist):
    max_exact = N_BUCKETS // 2
    safe = np.maximum(dist, 1).astype(np.float32)
    large = max_exact + (np.log(safe / max_exact) / np.log(MAX_DISTANCE / max_exact)
                         * (N_BUCKETS - max_exact)).astype(np.int32)
    large = np.minimum(large, N_BUCKETS - 1)
    return np.where(dist < max_exact, dist, large).astype(np.int32)


def _bias_kernel(bmap_ref, rb_ref, o_ref):
    h = pl.program_id(1)
    bm = bmap_ref[...]
    t = jnp.full(bm.shape, NEG, F32)
    for b in range(N_BUCKETS):
        t = jnp.where(bm == b, rb_ref[b, h], t)
    o_ref[...] = t


def _bias_tables(rel_bias):
    qi = np.arange(WIN)[:, None]
    c = np.arange(2 * WIN)[None, :]
    j = qi - c + WIN
    maps = []
    for (w, d) in A_PATTERNS:
        assert w // d == WIN
        bucket = _t5_bucket(np.arange(WIN + 1, dtype=np.int32) * d)
        maps.append(np.where((j >= 0) & (j <= WIN), bucket[np.clip(j, 0, WIN)], -1).astype(np.int32))
    bmap = jnp.asarray(np.stack(maps))
    npat = len(A_PATTERNS)
    return pl.pallas_call(
        _bias_kernel,
        grid=(npat, A_HEADS),
        in_specs=[
            pl.BlockSpec((None, WIN, 2 * WIN), lambda p, h: (p, 0, 0)),
            pl.BlockSpec(memory_space=pltpu.SMEM),
        ],
        out_specs=pl.BlockSpec((None, WIN, 2 * WIN), lambda p, h: (p, h, 0)),
        out_shape=jax.ShapeDtypeStruct((npat, A_HEADS * WIN, 2 * WIN), F32),
        name="t5_bias_tables",
    )(bmap, rel_bias)


def _rot_half(x, x1_mask):
    w = x.shape[-1]
    half = B_ROPE // 2
    return jnp.where(x1_mask, pltpu.roll(x, w - half, 1), pltpu.roll(x, half, 1))


def _inproj_kernel(x_ref, mod_ref, g_ref, w_ref, qg_ref, kvg_ref, wuq_ref, wuk_ref, wuv_ref,
                   cb_ref, sb_ref, cc_ref, sc_ref, ks_ref,
                   a_ref, bq_ref, bk_ref, bv_ref, cqk_ref, cv_ref, cg_ref):
    x = x_ref[...]
    tm = x.shape[0]
    h = (_rms(x) * g_ref[...] * (1.0 + mod_ref[1:2, :]) + mod_ref[0:1, :]).astype(BF16)

    def proj(lo, hi):
        return jnp.dot(h, w_ref[:, lo:hi], preferred_element_type=F32)

    a_ref[:, 0:A_WIDTH] = (proj(OFF_A, OFF_A + A_WIDTH) * (HEAD_DIM ** -0.5)).astype(BF16)
    a_ref[:, A_WIDTH:3 * A_WIDTH] = proj(OFF_A + A_WIDTH, OFF_BQ).astype(BF16)

    lane = lax.broadcasted_iota(jnp.int32, (tm, LANES), 1)
    b_x1 = (lane >= B_NOPE) & (lane < B_NOPE + B_ROPE // 2)
    c_x1 = (lane & (C_KEY // 2)) == 0
    cos_b = cb_ref[...]
    sin_b = sb_ref[...]

    q_lat = proj(OFF_BQ, OFF_BKV)
    qn = (_rms(q_lat) * qg_ref[...]).astype(BF16)
    q = jnp.dot(qn, wuq_ref[...], preferred_element_type=F32)
    kv_lat = proj(OFF_BKV, OFF_C)
    kvn = (_rms(kv_lat) * kvg_ref[...]).astype(BF16)
    k_nope = jnp.dot(kvn, wuk_ref[...], preferred_element_type=F32)
    bv_ref[...] = jnp.dot(kvn, wuv_ref[...], preferred_element_type=F32).astype(BF16)
    k_rope = proj(OFF_ROPE, IN_COLS)
    k_pe = k_rope * cos_b + _rot_half(k_rope, b_x1) * sin_b
    b_scale = (B_NOPE + B_ROPE) ** -0.5 * LOG2E
    for hh in range(B_HEADS):
        sl = slice(hh * LANES, (hh + 1) * LANES)
        qh = q[:, sl]
        bq_ref[:, sl] = ((qh * cos_b + _rot_half(qh, b_x1) * sin_b) * b_scale).astype(BF16)
        bk_ref[:, sl] = (k_nope[:, sl] + k_pe).astype(BF16)

    cos_c = cc_ref[...]
    sin_c = sc_ref[...]
    qk = proj(OFF_C, OFF_C + 2 * C_QK)
    for t in range(2 * C_QK // LANES):
        sl = slice(t * LANES, (t + 1) * LANES)
        xt = qk[:, sl]
        cqk_ref[:, sl] = ((xt * cos_c + _rot_half(xt, c_x1) * sin_c) * ks_ref[:, sl]).astype(BF16)
    cv_ref[...] = proj(OFF_C + 2 * C_QK, OFF_C + 2 * C_QK + C_WIDTH).astype(BF16)
    cg_ref[...] = _silu(proj(OFF_C + 2 * C_QK + C_WIDTH, OFF_ROPE)).astype(BF16)


def _inproj(x, mods, layer, prep, tables, batch, seq, tm):
    t, d = x.shape
    nt = seq // tm
    row = lambda w: pl.BlockSpec((tm, w), lambda b, i: (b * nt + i, 0))
    wl = lambda a: _const_spec((None,) + a.shape[1:], lambda b, i: (layer,) + (0,) * (a.ndim - 1))
    cos_b, sin_b, cos_c, sin_c = tables
    kscale = jnp.asarray(np.concatenate([np.ones(C_QK, np.float32),
                                         np.full(C_QK, C_KEY ** -0.5, np.float32)])[None, :])
    outs = [(3 * A_WIDTH, BF16), (B_HEADS * LANES, BF16), (B_HEADS * LANES, BF16), (B_WIDTH, BF16),
            (2 * C_QK, BF16), (C_WIDTH, BF16), (C_WIDTH, BF16)]
    return pl.pallas_call(
        _inproj_kernel,
        grid=(batch, nt),
        in_specs=[
            row(d),
            pl.BlockSpec((None, None, 6, d), lambda b, i: (layer, b, 0, 0)),
            wl(prep["norm1_g"]), wl(prep["w_in"]), wl(prep["q_norm"]), wl(prep["kv_norm"]),
            wl(prep["w_uq"]), wl(prep["w_uk"]), wl(prep["w_uv"]),
            row(LANES), row(LANES), row(LANES), row(LANES),
            _const_spec((1, 2 * C_QK), lambda b, i: (0, 0)),
        ],
        out_specs=[row(w) for w, _ in outs],
        out_shape=[jax.ShapeDtypeStruct((t, w), dt) for w, dt in outs],
        compiler_params=pltpu.CompilerParams(
            dimension_semantics=("arbitrary", "arbitrary"), vmem_limit_bytes=VMEM_LIMIT),
        name="inproj",
    )(x, mods, prep["norm1_g"], prep["w_in"], prep["q_norm"], prep["kv_norm"],
      prep["w_uq"], prep["w_uk"], prep["w_uv"], cos_b, sin_b, cos_c, sin_c, kscale)


def _dilated_kernel(q_ref, k_ref, v_ref, bias_ref, gain_ref, o_ref,
                    qf, kf, vf, acc2, m2, l2, acc3, m3, l3, *, seq):
    qf[...] = q_ref[...].astype(F32)
    kf[...] = k_ref[...].astype(F32)
    vf[...] = v_ref[...].astype(F32)
    lane = lax.broadcasted_iota(jnp.int32, (WIN, LANES), 1)
    lo = lane < HEAD_DIM

    def rows(start, d):
        return pl.ds(start, WIN) if d == 1 else pl.ds(start, WIN, stride=d)

    def block(p, start, d, first):
        q = qf[rows(start, d), :]
        q2 = jnp.concatenate([jnp.where(lo, q, 0.0), jnp.where(lo, 0.0, q)], axis=0).astype(BF16)
        if first:
            kcat = kf[rows(start, d), :].astype(BF16)
            vcat = vf[rows(start, d), :].astype(BF16)
            bias = bias_ref[p, :, WIN:2 * WIN]
        else:
            prev = start - d * WIN
            kcat = jnp.concatenate([kf[rows(prev, d), :], kf[rows(start, d), :]], axis=0).astype(BF16)
            vcat = jnp.concatenate([vf[rows(prev, d), :], vf[rows(start, d), :]], axis=0).astype(BF16)
            bias = bias_ref[p]
        s = lax.dot_general(q2, kcat, (((1,), (1,)), ((), ())), preferred_element_type=F32) + bias
        m = jnp.max(s, axis=1, keepdims=True)
        e = jnp.exp(s - m)
        l = jnp.sum(e, axis=1, keepdims=True)
        acc = jnp.dot(e.astype(BF16), vcat, preferred_element_type=F32)
        return (jnp.where(lo, acc[:WIN], acc[WIN:]),
                jnp.where(lo, m[:WIN], m[WIN:]),
                jnp.where(lo, l[:WIN], l[WIN:]))

    def run_dilated(p, d, acc_ref, m_ref, l_ref):
        nb = seq // d // WIN

        def store(start, res):
            acc, m, l = res
            acc_ref[rows(start, d), :] = acc
            m_ref[rows(start, d), :] = m
            l_ref[rows(start, d), :] = l

        def per_class(r, carry):
            store(r, block(p, r, d, True))
            if nb > 1:
                def per_block(mb, c):
                    start = r + d * WIN * mb
                    store(start, block(p, start, d, False))
                    return c
                lax.fori_loop(1, nb, per_block, 0)
            return carry

        lax.fori_loop(0, d, per_class, 0)

    run_dilated(1, A_PATTERNS[1][1], acc2, m2, l2)
    run_dilated(2, A_PATTERNS[2][1], acc3, m3, l3)

    gain = gain_ref[...]

    def finish(start, res):
        acc1, m1, l1 = res
        sl = pl.ds(start, WIN)
        mb, mc = m2[sl, :], m3[sl, :]
        mx = jnp.maximum(jnp.maximum(m1, mb), mc)
        w1, w2, w3 = jnp.exp(m1 - mx), jnp.exp(mb - mx), jnp.exp(mc - mx)
        num = w1 * acc1 + w2 * acc2[sl, :] + w3 * acc3[sl, :]
        den = w1 * l1 + w2 * l2[sl, :] + w3 * l3[sl, :]
        o = num / den
        sq = o * o
        ms_lo = jnp.sum(jnp.where(lo, sq, 0.0), axis=1, keepdims=True)
        ms_hi = jnp.sum(jnp.where(lo, 0.0, sq), axis=1, keepdims=True)
        inv = lax.rsqrt(jnp.where(lo, ms_lo, ms_hi) * (1.0 / HEAD_DIM) + EPS)
        o_ref[sl, :] = (o * inv * gain).astype(o_ref.dtype)

    finish(0, block(0, 0, 1, True))

    def per_block(mb, c):
        start = pl.multiple_of(mb * WIN, WIN)
        finish(start, block(0, start, 1, False))
        return c

    lax.fori_loop(1, seq // WIN, per_block, 0)


def _dilated(a_qkv, bias, gain_a, batch, seq):
    t = a_qkv.shape[0]
    pairs = A_WIDTH // LANES
    assert A_PATTERNS[0][1] == 1 and all(seq % (d * WIN) == 0 for _, d in A_PATTERNS)
    npat = bias.shape[0]
    col = lambda off: pl.BlockSpec((seq, LANES), lambda b, g: (b, off + g))
    scratch = [pltpu.VMEM((seq, LANES), F32) for _ in range(9)]
    return pl.pallas_call(
        functools.partial(_dilated_kernel, seq=seq),
        grid=(batch, pairs),
        in_specs=[
            col(0), col(pairs), col(2 * pairs),
            pl.BlockSpec((npat, 2 * WIN, 2 * WIN), lambda b, g: (0, g, 0)),
            pl.BlockSpec((1, LANES), lambda b, g: (0, g)),
        ],
        out_specs=pl.BlockSpec((seq, LANES), lambda b, g: (b, g)),
        out_shape=jax.ShapeDtypeStruct((t, A_WIDTH), BF16),
        scratch_shapes=scratch,
        compiler_params=pltpu.CompilerParams(
            dimension_semantics=("arbitrary", "arbitrary"), vmem_limit_bytes=VMEM_LIMIT),
        name="dilated_attention",
    )(a_qkv, a_qkv, a_qkv, bias, gain_a)


def _mla_kernel(q_ref, k_ref, v_ref, gain_ref, o_ref, *, seq, tq):
    key_idx = lax.broadcasted_iota(jnp.int32, (tq, tq), 0)
    qry_idx = lax.broadcasted_iota(jnp.int32, (tq, tq), 1)
    causal = key_idx <= qry_idx
    vrow = lax.broadcasted_iota(jnp.int32, (LANES, tq), 0)
    lo = vrow < B_V
    gain = gain_ref[...]

    def head(hh, i):
        hs = slice(hh * LANES, (hh + 1) * LANES)
        q = q_ref[i * tq:(i + 1) * tq, hs]
        m = l = acc = None
        for j in range(i + 1):
            ks = slice(j * tq, (j + 1) * tq)
            s = lax.dot_general(k_ref[ks, hs], q, (((1,), (1,)), ((), ())), preferred_element_type=F32)
            if j == i:
                s = jnp.where(causal, s, NEG)
            bm = jnp.max(s, axis=0, keepdims=True)
            m_new = bm if m is None else jnp.maximum(m, bm)
            e = jnp.exp2(s - m_new)
            ls = jnp.sum(e, axis=0, keepdims=True)
            pv = lax.dot_general(v_ref[ks, :], e.astype(BF16), (((0,), (0,)), ((), ())),
                                 preferred_element_type=F32)
            if m is None:
                l, acc = ls, pv
            else:
                alpha = jnp.exp2(m - m_new)
                l = alpha * l + ls
                acc = alpha * acc + pv
            m = m_new
        return acc / l

    for i in range(seq // tq):
        o = jnp.where(lo, head(0, i), head(1, i))
        sq = o * o
        ms_lo = jnp.sum(jnp.where(lo, sq, 0.0), axis=0, keepdims=True)
        ms_hi = jnp.sum(jnp.where(lo, 0.0, sq), axis=0, keepdims=True)
        inv = lax.rsqrt(jnp.where(lo, ms_lo, ms_hi) * (1.0 / B_V) + EPS)
        o_ref[i * tq:(i + 1) * tq, :] = ((o * inv).T * gain).astype(o_ref.dtype)


def _mla(bq, bk, bv, gain_b, batch, seq):
    t = bq.shape[0]
    pairs = B_HEADS // 2
    tq = 512
    assert seq % tq == 0
    return pl.pallas_call(
        functools.partial(_mla_kernel, seq=seq, tq=tq),
        grid=(batch, pairs),
        in_specs=[
            pl.BlockSpec((seq, 2 * LANES), lambda b, g: (b, g)),
            pl.BlockSpec((seq, 2 * LANES), lambda b, g: (b, g)),
            pl.BlockSpec((seq, LANES), lambda b, g: (b, g)),
            pl.BlockSpec((1, LANES), lambda b, g: (0, g)),
        ],
        out_specs=pl.BlockSpec((seq, LANES), lambda b, g: (b, g)),
        out_shape=jax.ShapeDtypeStruct((t, B_WIDTH), BF16),
        compiler_params=pltpu.CompilerParams(
            dimension_semantics=("arbitrary", "arbitrary"), vmem_limit_bytes=VMEM_LIMIT),
        name="latent_attention",
    )(bq, bk, bv, gain_b)


def _retention_consts():
    h = C_HEADS
    log_g = np.log(1.0 - 2.0 ** (-5.0 - np.arange(h))).astype(np.float32)
    i = np.arange(C_CHUNK, dtype=np.float32)
    rel = i[:, None] - i[None, :]
    decay = (np.exp(np.maximum(rel, 0.0)[None] * log_g[:, None, None]) * (rel >= 0)[None]).astype(np.float32)
    xi = np.exp((i + 1.0)[None, :] * log_g[:, None]).astype(np.float32)
    zeta = np.exp((C_CHUNK - 1.0 - i)[None, :] * log_g[:, None]).astype(np.float32)
    chunk_decay = np.exp(C_CHUNK * log_g).astype(np.float32)
    w = 2 * C_QK
    xi_mat = np.repeat(xi.T, C_VAL, axis=1)
    zeta_mat = np.zeros((C_CHUNK, w), np.float32)
    zeta_mat[:, :C_QK] = np.repeat(zeta.T, C_KEY, axis=1)
    cd = np.repeat(chunk_decay, C_VAL)[None, :]
    bd = np.zeros((w, C_WIDTH), np.float32)
    for hh in range(h):
        bd[hh * C_KEY:(hh + 1) * C_KEY, hh * C_VAL:(hh + 1) * C_VAL] = 1.0
    return tuple(jnp.asarray(a) for a in (decay, xi_mat, zeta_mat, cd, bd))


def _retention_kernel(qk_ref, v_ref, g_ref, decay_ref, xi_ref, zeta_ref, cd_ref, bd_ref, gain_ref,
                      o_ref, state_ref, *, seq):
    w = 2 * C_QK
    state_ref[...] = jnp.zeros(state_ref.shape, F32)
    lane_qk = lax.broadcasted_iota(jnp.int32, (C_CHUNK, w), 1)
    lane_v = lax.broadcasted_iota(jnp.int32, (C_CHUNK, C_WIDTH), 1)
    lane = lax.broadcasted_iota(jnp.int32, (C_CHUNK, LANES), 1)
    lo = lane < C_VAL
    gain = gain_ref[...]

    def chunk(n, c):
        r0 = pl.multiple_of(n * C_CHUNK, C_CHUNK)
        rows = pl.ds(r0, C_CHUNK)
        qk = qk_ref[rows, :]
        qk32 = qk.astype(F32)
        kk32 = pltpu.roll(qk32, C_QK, 1)
        kk = kk32.astype(BF16)
        v = v_ref[rows, :]
        state = state_ref[...]
        o = jnp.dot(qk, state.astype(BF16), preferred_element_type=F32) * xi_ref[...]
        for hh in range(C_HEADS):
            q_h = jnp.where((lane_qk >= hh * C_KEY) & (lane_qk < (hh + 1) * C_KEY), qk32, 0.0).astype(BF16)
            s = lax.dot_general(q_h, kk, (((1,), (1,)), ((), ())), preferred_element_type=F32)
            s = (s * decay_ref[hh]).astype(BF16)
            o_h = jnp.dot(s, v, preferred_element_type=F32)
            o = o + jnp.where((lane_v >= hh * C_VAL) & (lane_v < (hh + 1) * C_VAL), o_h, 0.0)
        kz = (kk32 * zeta_ref[...]).astype(BF16)
        upd = lax.dot_general(kz, v, (((0,), (0,)), ((), ())), preferred_element_type=F32)
        state_ref[...] = state * cd_ref[...] + upd * bd_ref[...]
        for t in range(C_WIDTH // LANES):
            sl = slice(t * LANES, (t + 1) * LANES)
            x = o[:, sl]
            mu_lo = jnp.sum(jnp.where(lo, x, 0.0), axis=1, keepdims=True)
            mu_hi = jnp.sum(jnp.where(lo, 0.0, x), axis=1, keepdims=True)
            dlt = x - jnp.where(lo, mu_lo, mu_hi) * (1.0 / C_VAL)
            sq = dlt * dlt
            var_lo = jnp.sum(jnp.where(lo, sq, 0.0), axis=1, keepdims=True)
            var_hi = jnp.sum(jnp.where(lo, 0.0, sq), axis=1, keepdims=True)
            y = dlt * lax.rsqrt(jnp.where(lo, var_lo, var_hi) * (1.0 / C_VAL) + EPS)
            o_ref[rows, sl] = (y * g_ref[rows, sl].astype(F32) * gain[:, sl]).astype(o_ref.dtype)
        return c

    lax.fori_loop(0, seq // C_CHUNK, chunk, 0)


def _retention(cqk, cv, cg, gain_c, consts, batch, seq):
    t = cqk.shape[0]
    w = 2 * C_QK
    decay, xi_mat, zeta_mat, cd, bd = consts
    tok = lambda width: pl.BlockSpec((seq, width), lambda b: (b, 0))
    full = lambda a: _const_spec(a.shape, lambda b: (0,) * a.ndim)
    return pl.pallas_call(
        functools.partial(_retention_kernel, seq=seq),
        grid=(batch,),
        in_specs=[tok(w), tok(C_WIDTH), tok(C_WIDTH),
                  full(decay), full(xi_mat), full(zeta_mat), full(cd), full(bd), full(gain_c)],
        out_specs=tok(C_WIDTH),
        out_shape=jax.ShapeDtypeStruct((t, C_WIDTH), BF16),
        scratch_shapes=[pltpu.VMEM((w, C_WIDTH), F32)],
        compiler_params=pltpu.CompilerParams(
            dimension_semantics=("arbitrary",), vmem_limit_bytes=VMEM_LIMIT),
        name="retention",
    )(cqk, cv, cg, decay, xi_mat, zeta_mat, cd, bd, gain_c)


def _outffn_kernel(x_ref, ma_ref, mb_ref, mc_ref, mod_ref, wo_ref, g_ref, wg_ref, wu_ref, wd_ref, fg_ref,
                   o_ref, mix_ref, hid_ref, *, final, fchunk):
    mix_ref[:, 0:A_WIDTH] = ma_ref[...]
    mix_ref[:, A_WIDTH:A_WIDTH + B_WIDTH] = mb_ref[...]
    mix_ref[:, A_WIDTH + B_WIDTH:] = mc_ref[...]
    att = jnp.dot(mix_ref[...], wo_ref[...], preferred_element_type=F32)
    x = x_ref[...] + mod_ref[2:3, :] * att
    h = (_rms(x) * g_ref[...] * (1.0 + mod_ref[4:5, :]) + mod_ref[3:4, :]).astype(BF16)
    hidden = wg_ref.shape[1]
    for j in range(hidden // fchunk):
        sl = slice(j * fchunk, (j + 1) * fchunk)
        gate = jnp.dot(h, wg_ref[:, sl], preferred_element_type=F32)
        up = jnp.dot(h, wu_ref[:, sl], preferred_element_type=F32)
        hid_ref[:, sl] = (_silu(gate) * up).astype(BF16)
    ffn = jnp.dot(hid_ref[...], wd_ref[...], preferred_element_type=F32)
    y = x + mod_ref[5:6, :] * ffn
    if final:
        y = _rms(y) * fg_ref[...]
    o_ref[...] = y


def _outffn(x, mix_a, mix_b, mix_c, mods, layer, prep, final_g, batch, seq, tm, final):
    t, d = x.shape
    nt = seq // tm
    hidden = prep["w_gate"].shape[2]
    row = lambda w: pl.BlockSpec((tm, w), lambda b, i: (b * nt + i, 0))
    wl = lambda a: _const_spec((None,) + a.shape[1:], lambda b, i: (layer,) + (0,) * (a.ndim - 1))
    return pl.pallas_call(
        functools.partial(_outffn_kernel, final=final, fchunk=256),
        grid=(batch, nt),
        in_specs=[
            row(d), row(A_WIDTH), row(B_WIDTH), row(C_WIDTH),
            pl.BlockSpec((None, None, 6, d), lambda b, i: (layer, b, 0, 0)),
            wl(prep["w_out"]), wl(prep["norm2_g"]), wl(prep["w_gate"]), wl(prep["w_up"]), wl(prep["w_down"]),
            _const_spec((1, d), lambda b, i: (0, 0)),
        ],
        out_specs=row(d),
        out_shape=jax.ShapeDtypeStruct((t, d), F32),
        scratch_shapes=[pltpu.VMEM((tm, d), BF16), pltpu.VMEM((tm, hidden), BF16)],
        compiler_params=pltpu.CompilerParams(
            dimension_semantics=("arbitrary", "arbitrary"), vmem_limit_bytes=VMEM_LIMIT),
        name="outproj_ffn",
    )(x, mix_a, mix_b, mix_c, mods, prep["w_out"], prep["norm2_g"], prep["w_gate"], prep["w_up"],
      prep["w_down"], final_g)


def _prepare(norm1_g, w_in, mla_q_norm, mla_kv_norm, mla_w_uq, mla_w_ukv, w_out, norm2_g,
             ffn_w_gate, ffn_w_up, ffn_w_down):
    depth, d, _ = w_in.shape
    b_lat0 = 3 * A_WIDTH
    rope0 = b_lat0 + B_Q_RANK + B_KV_RANK
    c0 = rope0 + B_ROPE
    pad = lambda n: jnp.zeros((depth, d, n), w_in.dtype)
    w_in_p = jnp.concatenate([
        w_in[:, :, :rope0],
        w_in[:, :, c0:],
        pad(B_NOPE), w_in[:, :, rope0:c0], pad(LANES - B_NOPE - B_ROPE),
    ], axis=2).astype(BF16)
    assert w_in_p.shape[2] == IN_COLS
    uq = mla_w_uq.reshape(depth, B_Q_RANK, B_HEADS, B_NOPE + B_ROPE)
    uq = jnp.pad(uq, ((0, 0), (0, 0), (0, 0), (0, LANES - B_NOPE - B_ROPE)))
    ukv = mla_w_ukv.reshape(depth, B_KV_RANK, B_HEADS, B_NOPE + B_V)
    uk = jnp.pad(ukv[..., :B_NOPE], ((0, 0), (0, 0), (0, 0), (0, LANES - B_NOPE)))
    uv = ukv[..., B_NOPE:]
    return {
        "norm1_g": norm1_g[:, None, :],
        "w_in": w_in_p,
        "q_norm": mla_q_norm[:, None, :],
        "kv_norm": mla_kv_norm[:, None, :],
        "w_uq": uq.reshape(depth, B_Q_RANK, B_HEADS * LANES).astype(BF16),
        "w_uk": uk.reshape(depth, B_KV_RANK, B_HEADS * LANES).astype(BF16),
        "w_uv": uv.reshape(depth, B_KV_RANK, B_WIDTH).astype(BF16),
        "w_out": w_out.astype(BF16),
        "norm2_g": norm2_g[:, None, :],
        "w_gate": ffn_w_gate.astype(BF16),
        "w_up": ffn_w_up.astype(BF16),
        "w_down": ffn_w_down.astype(BF16),
    }


def kernel(x, c, positions, rel_bias, ada_w, ada_b, norm1_g, w_in, mla_q_norm, mla_kv_norm, mla_w_uq,
           mla_w_ukv, mix_gain, w_out, norm2_g, ffn_w_gate, ffn_w_up, ffn_w_down, final_norm):
    batch, seq, d = x.shape
    depth = w_in.shape[0]
    tm = 512
    assert seq % tm == 0 and seq % C_CHUNK == 0

    prep = _prepare(norm1_g, w_in, mla_q_norm, mla_kv_norm, mla_w_uq, mla_w_ukv, w_out, norm2_g,
                    ffn_w_gate, ffn_w_up, ffn_w_down)
    mods = _mods(c, ada_w, ada_b).reshape(depth, batch, 6, d)
    tables = _rope_tables(positions)
    bias = _bias_tables(rel_bias)
    ret_consts = _retention_consts()
    final_g = final_norm[None, :]

    xf = x.reshape(batch * seq, d)
    for l in range(depth):
        a_qkv, bq, bk, bv, cqk, cv, cg = _inproj(xf, mods, l, prep, tables, batch, seq, tm)
        gain = mix_gain[l][None, :]
        mix_a = _dilated(a_qkv, bias, gain[:, :A_WIDTH], batch, seq)
        mix_b = _mla(bq, bk, bv, gain[:, A_WIDTH:A_WIDTH + B_WIDTH], batch, seq)
        mix_c = _retention(cqk, cv, cg, gain[:, A_WIDTH + B_WIDTH:], ret_consts, batch, seq)
        xf = _outffn(xf, mix_a, mix_b, mix_c, mods, l, prep, final_g, batch, seq, tm, final=(l == depth - 1))
    return xf.reshape(batch, seq, d)
```

```python
import functools

import numpy as np
import jax
import jax.numpy as jnp
from jax import lax
from jax.experimental import pallas as pl
from jax.experimental.pallas import tpu as pltpu

F32 = jnp.float32
BF16 = jnp.bfloat16

HEAD_DIM = 64
A_HEADS = 6
A_PATTERNS = ((128, 1), (512, 4), (2048, 16))
A_WIDTH = A_HEADS * HEAD_DIM
B_HEADS = 4
B_NOPE = 64
B_ROPE = 32
B_V = 64
B_Q_RANK = 256
B_KV_RANK = 128
B_WIDTH = B_HEADS * B_V
C_HEADS = 6
C_KEY = 32
C_VAL = 64
C_WIDTH = C_HEADS * C_VAL
C_QK = C_HEADS * C_KEY
C_CHUNK = 128
N_BUCKETS = 32
MAX_DISTANCE = 2048
ROPE_BASE = 10000.0
EPS = 1e-6

LANES = 128
WIN = 128
STEP = 4
NEG = -1e30
LOG2E = 1.4426950408889634
VMEM_LIMIT = 56 * 1024 * 1024

OFF_A = 0
OFF_BQ = 3 * A_WIDTH
OFF_BKV = OFF_BQ + B_Q_RANK
OFF_C = OFF_BKV + B_KV_RANK
C_QPAD = 2 * LANES
OFF_CV = OFF_C + 2 * C_QPAD
OFF_CG = OFF_CV + C_WIDTH
IN_COLS = OFF_CG + C_WIDTH


def _const_spec(shape, index_map):
    return pl.BlockSpec(shape, index_map, pipeline_mode=pl.Buffered(1))


def _silu(x):
    return x / (1.0 + jnp.exp(-x))


def _rms(x):
    return x * lax.rsqrt(jnp.mean(x * x, axis=-1, keepdims=True) + EPS)


def _mods_kernel(c_ref, w_ref, b_ref, o_ref):
    c = c_ref[...]
    o_ref[...] = jnp.dot(_silu(c), w_ref[...], preferred_element_type=F32,
                         precision=lax.Precision.HIGHEST) + b_ref[...]


def _mods(c, ada_w, ada_b):
    depth, d, n = ada_w.shape
    b = c.shape[0]
    tn = 1536
    return pl.pallas_call(
        _mods_kernel,
        grid=(depth, n // tn),
        in_specs=[
            pl.BlockSpec((b, d), lambda l, j: (0, 0)),
            pl.BlockSpec((None, d, tn), lambda l, j: (l, 0, j)),
            pl.BlockSpec((None, 1, tn), lambda l, j: (l, 0, j)),
        ],
        out_specs=pl.BlockSpec((None, b, tn), lambda l, j: (l, 0, j)),
        out_shape=jax.ShapeDtypeStruct((depth, b, n), F32),
        compiler_params=pltpu.CompilerParams(vmem_limit_bytes=VMEM_LIMIT),
        name="adaln_mods",
    )(c, ada_w, ada_b.reshape(depth, 1, n))


def _rope_kernel(pos_ref, f_ref, s_ref, cb_ref, sb_ref, cc_ref, sc_ref):
    p = pos_ref[...].astype(F32)
    ang_b = p * f_ref[0:1, :]
    ang_c = p * f_ref[1:2, :]
    cb_ref[...] = jnp.cos(ang_b)
    sb_ref[...] = jnp.sin(ang_b) * s_ref[0:1, :]
    cc_ref[...] = jnp.cos(ang_c)
    sc_ref[...] = jnp.sin(ang_c) * s_ref[1:2, :]


def _rope_tables(positions):
    t = positions.size
    half = B_ROPE // 2
    inv_freq = (1.0 / (ROPE_BASE ** (np.arange(half, dtype=np.float32) / half))).astype(np.float32)
    zeros = np.zeros(half, np.float32)
    ones = np.ones(half, np.float32)
    f_b = np.concatenate([np.zeros(B_NOPE, np.float32), inv_freq, inv_freq, np.zeros(32, np.float32)])
    s_b = np.concatenate([np.zeros(B_NOPE, np.float32), -ones, ones, np.zeros(32, np.float32)])
    f_c = np.tile(np.concatenate([inv_freq, inv_freq]), LANES // C_KEY)
    s_c = np.tile(np.concatenate([-ones, ones]), LANES // C_KEY)
    del zeros
    freqs = jnp.asarray(np.stack([f_b, f_c]))
    signs = jnp.asarray(np.stack([s_b, s_c]))
    tr = 2048
    out = jax.ShapeDtypeStruct((t, LANES), F32)
    row = pl.BlockSpec((tr, LANES), lambda i: (i, 0))
    return pl.pallas_call(
        _rope_kernel,
        grid=(t // tr,),
        in_specs=[
            pl.BlockSpec((tr, 1), lambda i: (i, 0)),
            pl.BlockSpec((2, LANES), lambda i: (0, 0)),
            pl.BlockSpec((2, LANES), lambda i: (0, 0)),
        ],
        out_specs=[row, row, row, row],
        out_shape=[out, out, out, out],
        name="rope_tables",
    )(positions.reshape(t, 1), freqs, signs)


def _t5_bucket(dist):
    max_exact = N_BUCKETS // 2
    safe = np.maximum(dist, 1).astype(np.float32)
    large = max_exact + (np.log(safe / max_exact) / np.log(MAX_DISTANCE / max_exact)
                         * (N_BUCKETS - max_exact)).astype(np.int32)
    large = np.minimum(large, N_BUCKETS - 1)
    return np.where(dist < max_exact, dist, large).astype(np.int32)


def _bias_kernel(bmap_ref, rb_ref, o_ref):
    h = pl.program_id(1)
    bm = bmap_ref[...]
    t = jnp.full(bm.shape, NEG, F32)
    for b in range(N_BUCKETS):
        t = jnp.where(bm == b, rb_ref[b, h] * LOG2E, t)
    o_ref[...] = t


def _bias_tables(rel_bias):
    qi = np.arange(WIN)[None, :]
    c = np.arange(2 * WIN)[:, None]
    j = qi - c + WIN
    maps = []
    for (w, d) in A_PATTERNS:
        assert w // d == WIN
        bucket = _t5_bucket(np.arange(WIN + 1, dtype=np.int32) * d)
        maps.append(np.where((j >= 0) & (j <= WIN), bucket[np.clip(j, 0, WIN)], -1).astype(np.int32))
    bmap = jnp.asarray(np.stack(maps))
    npat = len(A_PATTERNS)
    return pl.pallas_call(
        _bias_kernel,
        grid=(npat, A_HEADS),
        in_specs=[
            pl.BlockSpec((None, 2 * WIN, WIN), lambda p, h: (p, 0, 0)),
            pl.BlockSpec(memory_space=pltpu.SMEM),
        ],
        out_specs=pl.BlockSpec((None, 2 * WIN, WIN), lambda p, h: (p, 0, h)),
        out_shape=jax.ShapeDtypeStruct((npat, 2 * WIN, A_HEADS * WIN), F32),
        name="t5_bias_tables",
    )(bmap, rel_bias)


def _rot_half(x, x1_mask):
    w = x.shape[-1]
    half = B_ROPE // 2
    return jnp.where(x1_mask, pltpu.roll(x, w - half, 1), pltpu.roll(x, half, 1))


def _inproj_kernel(x_ref, mod_ref, g_ref, w_ref, qg_ref, kvg_ref, wuq_ref, wuk_ref, wuv_ref,
                   cb_ref, sb_ref, cc_ref, sc_ref, ks_ref,
                   a_ref, bq_ref, bk_ref, bv_ref, cqk_ref, cv_ref, cg_ref):
    x = x_ref[...]
    tm = x.shape[0]
    h = (_rms(x) * g_ref[...] * (1.0 + mod_ref[1:2, :]) + mod_ref[0:1, :]).astype(BF16)

    def proj(lo, hi):
        return jnp.dot(h, w_ref[:, lo:hi], preferred_element_type=F32)

    a_ref[:, 0:A_WIDTH] = (proj(OFF_A, OFF_A + A_WIDTH) * (HEAD_DIM ** -0.5 * LOG2E)).astype(BF16)
    a_ref[:, A_WIDTH:3 * A_WIDTH] = proj(OFF_A + A_WIDTH, OFF_BQ).astype(BF16)

    lane = lax.broadcasted_iota(jnp.int32, (tm, LANES), 1)
    b_x1 = (lane >= B_NOPE) & (lane < B_NOPE + B_ROPE // 2)
    c_x1 = (lane & (C_KEY // 2)) == 0
    cos_b = cb_ref[...]
    sin_b = sb_ref[...]

    q_lat = proj(OFF_BQ, OFF_BKV)
    qn = (_rms(q_lat) * qg_ref[...]).astype(BF16)
    q = jnp.dot(qn, wuq_ref[...], preferred_element_type=F32)
    kv_lat = proj(OFF_BKV, OFF_C)
    kvn = (_rms(kv_lat) * kvg_ref[...]).astype(BF16)
    k_nope = jnp.dot(kvn, wuk_ref[...], preferred_element_type=F32)
    bv_ref[...] = jnp.dot(kvn, wuv_ref[...], preferred_element_type=F32).astype(BF16)
    qk = proj(OFF_C, OFF_CV)
    k_rope = jnp.where(lane >= B_NOPE, qk[:, LANES:2 * LANES], 0.0)
    k_pe = k_rope * cos_b + _rot_half(k_rope, b_x1) * sin_b
    b_scale = (B_NOPE + B_ROPE) ** -0.5 * LOG2E
    for hh in range(B_HEADS):
        sl = slice(hh * LANES, (hh + 1) * LANES)
        qh = q[:, sl]
        bq_ref[:, sl] = ((qh * cos_b + _rot_half(qh, b_x1) * sin_b) * b_scale).astype(BF16)
        bk_ref[:, sl] = (k_nope[:, sl] + k_pe).astype(BF16)

    cos_c = cc_ref[...]
    sin_c = sc_ref[...]
    for t in range(2 * C_QPAD // LANES):
        sl = slice(t * LANES, (t + 1) * LANES)
        xt = qk[:, sl]
        cqk_ref[:, sl] = ((xt * cos_c + _rot_half(xt, c_x1) * sin_c) * ks_ref[:, sl]).astype(BF16)
    cv_ref[...] = proj(OFF_CV, OFF_CG).astype(BF16)
    cg_ref[...] = _silu(proj(OFF_CG, IN_COLS)).astype(BF16)


def _inproj(x, mods, layer, prep, tables, batch, seq, tm):
    t, d = x.shape
    nt = seq // tm
    row = lambda w: pl.BlockSpec((tm, w), lambda b, i: (b * nt + i, 0))
    wl = lambda a: _const_spec((None,) + a.shape[1:], lambda b, i: (layer,) + (0,) * (a.ndim - 1))
    cos_b, sin_b, cos_c, sin_c = tables
    kscale = jnp.asarray(np.concatenate([np.ones(C_QPAD, np.float32),
                                         np.full(C_QPAD, C_KEY ** -0.5, np.float32)])[None, :])
    outs = [(3 * A_WIDTH, BF16), (B_HEADS * LANES, BF16), (B_HEADS * LANES, BF16), (B_WIDTH, BF16),
            (2 * C_QPAD, BF16), (C_WIDTH, BF16), (C_WIDTH, BF16)]
    return pl.pallas_call(
        _inproj_kernel,
        grid=(batch, nt),
        in_specs=[
            row(d),
            pl.BlockSpec((None, None, 6, d), lambda b, i: (layer, b, 0, 0)),
            wl(prep["norm1_g"]), wl(prep["w_in"]), wl(prep["q_norm"]), wl(prep["kv_norm"]),
            wl(prep["w_uq"]), wl(prep["w_uk"]), wl(prep["w_uv"]),
            row(LANES), row(LANES), row(LANES), row(LANES),
            _const_spec((1, 2 * C_QPAD), lambda b, i: (0, 0)),
        ],
        out_specs=[row(w) for w, _ in outs],
        out_shape=[jax.ShapeDtypeStruct((t, w), dt) for w, dt in outs],
        compiler_params=pltpu.CompilerParams(
            dimension_semantics=("arbitrary", "arbitrary"), vmem_limit_bytes=VMEM_LIMIT),
        name="inproj",
    )(x, mods, prep["norm1_g"], prep["w_in"], prep["q_norm"], prep["kv_norm"],
      prep["w_uq"], prep["w_uk"], prep["w_uv"], cos_b, sin_b, cos_c, sin_c, kscale)


def _dilated_kernel(q_ref, k_ref, v_ref, bias_ref, gain_ref, o_ref,
                    qf, kf, vf, qlo, qhi, kp, vp, acc_s, m_s, l_s, *, seq):
    nblk = seq // WIN
    lane = lax.broadcasted_iota(jnp.int32, (WIN, LANES), 1)
    lo_lane = lane < HEAD_DIM
    lo_row = lax.broadcasted_iota(jnp.int32, (LANES, WIN), 0) < HEAD_DIM

    qf[0] = q_ref[...].astype(F32)
    kf[0] = k_ref[...].astype(F32)
    vf[0] = v_ref[...].astype(F32)

    for p in range(len(A_PATTERNS)):
        if p > 0:
            n = seq // STEP
            for r in range(STEP):
                dst = slice(r * n, (r + 1) * n)
                src = pl.ds(r, n, stride=STEP)
                for f in (qf, kf, vf):
                    f[p % 2, dst, :] = f[(p - 1) % 2, src, :]
        for c in range(0, seq, 4 * WIN):
            rows = slice(c, c + 4 * WIN)
            q = qf[p % 2, rows, :]
            lo_q = lax.broadcasted_iota(jnp.int32, q.shape, 1) < HEAD_DIM
            qlo[p, rows, :] = jnp.where(lo_q, q, 0.0).astype(BF16)
            qhi[p, rows, :] = jnp.where(lo_q, 0.0, q).astype(BF16)
            if p > 0:
                kp[p - 1, rows, :] = kf[p % 2, rows, :].astype(BF16)
                vp[p - 1, rows, :] = vf[p % 2, rows, :].astype(BF16)

    def block(p, blk, first):
        r0 = blk * WIN
        k0 = r0 if first else r0 - WIN
        q2 = jnp.concatenate([qlo[p, r0:r0 + WIN, :], qhi[p, r0:r0 + WIN, :]], axis=0)
        if p == 0:
            kcat, vcat = k_ref[k0:r0 + WIN, :], v_ref[k0:r0 + WIN, :]
        else:
            kcat, vcat = kp[p - 1, k0:r0 + WIN, :], vp[p - 1, k0:r0 + WIN, :]
        bias = bias_ref[p, WIN:2 * WIN, :] if first else bias_ref[p]
        s = lax.dot_general(kcat, q2, (((1,), (1,)), ((), ())), preferred_element_type=F32) + bias
        m = jnp.max(s, axis=0, keepdims=True)
        e = jnp.exp2(s - m)
        l = jnp.sum(e, axis=0, keepdims=True)
        acc = lax.dot_general(vcat, e.astype(BF16), (((0,), (0,)), ((), ())),
                              preferred_element_type=F32)
        acc = jnp.where(lo_row, acc[:, :WIN], acc[:, WIN:])
        m = jnp.where(lo_row, m[:, :WIN], m[:, WIN:])
        l = jnp.where(lo_row, l[:, :WIN], l[:, WIN:])
        return acc.T, m.T, l.T

    def merged(res, level, sl):
        acc1, m1, l1 = res
        m2 = m_s[level, sl, :]
        mx = jnp.maximum(m1, m2)
        w1, w2 = jnp.exp2(m1 - mx), jnp.exp2(m2 - mx)
        return w1 * acc1 + w2 * acc_s[level, sl, :], mx, w1 * l1 + w2 * l_s[level, sl, :]

    last = len(A_PATTERNS) - 1
    for p in range(last, 0, -1):
        per_class = nblk // A_PATTERNS[p][1]
        for blk in range(nblk):
            res = block(p, blk, blk % per_class == 0)
            if p < last:
                res = merged(res, p, slice(blk * WIN, (blk + 1) * WIN))
            seg, off = divmod(blk * WIN, seq // STEP)
            dst = pl.ds(off * STEP + seg, WIN, stride=STEP)
            acc_s[p - 1, dst, :], m_s[p - 1, dst, :], l_s[p - 1, dst, :] = res

    gain = gain_ref[...]
    for blk in range(nblk):
        sl = slice(blk * WIN, (blk + 1) * WIN)
        num, _, den = merged(block(0, blk, blk == 0), 0, sl)
        o = num / den
        sq = o * o
        ms_lo = jnp.sum(jnp.where(lo_lane, sq, 0.0), axis=1, keepdims=True)
        ms_hi = jnp.sum(jnp.where(lo_lane, 0.0, sq), axis=1, keepdims=True)
        inv = lax.rsqrt(jnp.where(lo_lane, ms_lo, ms_hi) * (1.0 / HEAD_DIM) + EPS)
        o_ref[sl, :] = (o * inv * gain).astype(o_ref.dtype)


def _dilated(a_qkv, bias, gain_a, batch, seq):
    t = a_qkv.shape[0]
    pairs = A_WIDTH // LANES
    assert all(d == STEP ** p and seq % (d * WIN) == 0 for p, (_, d) in enumerate(A_PATTERNS))
    npat = bias.shape[0]
    col = lambda off: pl.BlockSpec((seq, LANES), lambda b, g: (b, off + g))
    scratch = ([pltpu.VMEM((2, seq, LANES), F32) for _ in range(3)]
               + [pltpu.VMEM((npat, seq, LANES), BF16) for _ in range(2)]
               + [pltpu.VMEM((npat - 1, seq, LANES), BF16) for _ in range(2)]
               + [pltpu.VMEM((npat - 1, seq, LANES), F32) for _ in range(3)])
    return pl.pallas_call(
        functools.partial(_dilated_kernel, seq=seq),
        grid=(batch, pairs),
        in_specs=[
            col(0), col(pairs), col(2 * pairs),
            pl.BlockSpec((npat, 2 * WIN, 2 * WIN), lambda b, g: (0, 0, g)),
            pl.BlockSpec((1, LANES), lambda b, g: (0, g)),
        ],
        out_specs=pl.BlockSpec((seq, LANES), lambda b, g: (b, g)),
        out_shape=jax.ShapeDtypeStruct((t, A_WIDTH), BF16),
        scratch_shapes=scratch,
        compiler_params=pltpu.CompilerParams(
            dimension_semantics=("arbitrary", "arbitrary"), vmem_limit_bytes=VMEM_LIMIT),
        name="dilated_attention",
    )(a_qkv, a_qkv, a_qkv, bias, gain_a)


def _mla_kernel(q_ref, k_ref, v_ref, gain_ref, o_ref, *, seq, tq):
    key_idx = lax.broadcasted_iota(jnp.int32, (tq, tq), 0)
    qry_idx = lax.broadcasted_iota(jnp.int32, (tq, tq), 1)
    causal = key_idx <= qry_idx
    vrow = lax.broadcasted_iota(jnp.int32, (LANES, tq), 0)
    lo = vrow < B_V
    gain = gain_ref[...]

    def head(hh, i):
        hs = slice(hh * LANES, (hh + 1) * LANES)
        q = q_ref[i * tq:(i + 1) * tq, hs]
        m = l = acc = None
        for j in range(i + 1):
            ks = slice(j * tq, (j + 1) * tq)
            s = lax.dot_general(k_ref[ks, hs], q, (((1,), (1,)), ((), ())), preferred_element_type=F32)
            if j == i:
                s = jnp.where(causal, s, NEG)
            bm = jnp.max(s, axis=0, keepdims=True)
            m_new = bm if m is None else jnp.maximum(m, bm)
            e = jnp.exp2(s - m_new)
            ls = jnp.sum(e, axis=0, keepdims=True)
            pv = lax.dot_general(v_ref[ks, :], e.astype(BF16), (((0,), (0,)), ((), ())),
                                 preferred_element_type=F32)
            if m is None:
                l, acc = ls, pv
            else:
                alpha = jnp.exp2(m - m_new)
                l = alpha * l + ls
                acc = alpha * acc + pv
            m = m_new
        return acc / l

    for i in range(seq // tq):
        o = jnp.where(lo, head(0, i), head(1, i))
        sq = o * o
        ms_lo = jnp.sum(jnp.where(lo, sq, 0.0), axis=0, keepdims=True)
        ms_hi = jnp.sum(jnp.where(lo, 0.0, sq), axis=0, keepdims=True)
        inv = lax.rsqrt(jnp.where(lo, ms_lo, ms_hi) * (1.0 / B_V) + EPS)
        o_ref[i * tq:(i + 1) * tq, :] = ((o * inv).T * gain).astype(o_ref.dtype)


def _mla(bq, bk, bv, gain_b, batch, seq):
    t = bq.shape[0]
    pairs = B_HEADS // 2
    tq = 512
    assert seq % tq == 0
    return pl.pallas_call(
        functools.partial(_mla_kernel, seq=seq, tq=tq),
        grid=(batch, pairs),
        in_specs=[
            pl.BlockSpec((seq, 2 * LANES), lambda b, g: (b, g)),
            pl.BlockSpec((seq, 2 * LANES), lambda b, g: (b, g)),
            pl.BlockSpec((seq, LANES), lambda b, g: (b, g)),
            pl.BlockSpec((1, LANES), lambda b, g: (0, g)),
        ],
        out_specs=pl.BlockSpec((seq, LANES), lambda b, g: (b, g)),
        out_shape=jax.ShapeDtypeStruct((t, B_WIDTH), BF16),
        compiler_params=pltpu.CompilerParams(
            dimension_semantics=("arbitrary", "arbitrary"), vmem_limit_bytes=VMEM_LIMIT),
        name="latent_attention",
    )(bq, bk, bv, gain_b)


def _retention_consts():
    h = C_HEADS
    log_g = np.log(1.0 - 2.0 ** (-5.0 - np.arange(h))).astype(np.float32)
    i = np.arange(C_CHUNK, dtype=np.float32)
    rel = i[:, None] - i[None, :]
    decay = (np.exp(np.maximum(rel, 0.0)[None] * log_g[:, None, None]) * (rel >= 0)[None]).astype(np.float32)
    xi = np.exp((i + 1.0)[None, :] * log_g[:, None]).astype(np.float32)
    zeta = np.exp((C_CHUNK - 1.0 - i)[None, :] * log_g[:, None]).astype(np.float32)
    chunk_decay = np.exp(C_CHUNK * log_g).astype(np.float32)
    decay_all = np.concatenate(list(decay), axis=1)
    xi_mat = np.repeat(xi.T, C_VAL, axis=1)
    zeta_t = np.zeros((C_QPAD, C_CHUNK), np.float32)
    zeta_t[:C_QK] = np.repeat(zeta, C_KEY, axis=0)
    cd = np.repeat(chunk_decay, C_VAL)[None, :]
    bd = np.zeros((C_QPAD, C_WIDTH), np.float32)
    kmask = np.zeros((C_QPAD, h * C_CHUNK), np.float32)
    vmask = np.zeros((h * C_CHUNK, C_WIDTH), np.float32)
    for hh in range(h):
        bd[hh * C_KEY:(hh + 1) * C_KEY, hh * C_VAL:(hh + 1) * C_VAL] = 1.0
        kmask[hh * C_KEY:(hh + 1) * C_KEY, hh * C_CHUNK:(hh + 1) * C_CHUNK] = 1.0
        vmask[hh * C_CHUNK:(hh + 1) * C_CHUNK, hh * C_VAL:(hh + 1) * C_VAL] = 1.0
    f32s = tuple(jnp.asarray(a) for a in (decay_all, xi_mat, zeta_t, cd, bd))
    return f32s + (jnp.asarray(kmask, BF16), jnp.asarray(vmask, BF16))


def _retention_kernel(qk_ref, v_ref, g_ref, decay_ref, xi_ref, zeta_ref, cd_ref, bd_ref, kmask_ref,
                      vmask_ref, gain_ref, o_ref, *, seq):
    lane = lax.broadcasted_iota(jnp.int32, (C_CHUNK, LANES), 1)
    lo = lane < C_VAL
    gain = gain_ref[...]
    state = jnp.zeros((C_QPAD, C_WIDTH), F32)

    for n in range(seq // C_CHUNK):
        rows = slice(n * C_CHUNK, (n + 1) * C_CHUNK)
        q = qk_ref[rows, 0:C_QPAD]
        k_t = qk_ref[rows, C_QPAD:2 * C_QPAD].astype(F32).T
        v = v_ref[rows, :]
        k_bd = jnp.concatenate([k_t.astype(BF16)] * C_HEADS, axis=1) * kmask_ref[...]
        s = jnp.dot(q, k_bd, preferred_element_type=F32) * decay_ref[...]
        v_bd = jnp.concatenate([v] * C_HEADS, axis=0) * vmask_ref[...]
        o = (jnp.dot(s.astype(BF16), v_bd, preferred_element_type=F32)
             + jnp.dot(q, state.astype(BF16), preferred_element_type=F32) * xi_ref[...])
        upd = jnp.dot((k_t * zeta_ref[...]).astype(BF16), v, preferred_element_type=F32)
        state = state * cd_ref[...] + upd * bd_ref[...]
        for t in range(C_WIDTH // LANES):
            sl = slice(t * LANES, (t + 1) * LANES)
            x = o[:, sl]
            mu_lo = jnp.sum(jnp.where(lo, x, 0.0), axis=1, keepdims=True)
            mu_hi = jnp.sum(jnp.where(lo, 0.0, x), axis=1, keepdims=True)
            dlt = x - jnp.where(lo, mu_lo, mu_hi) * (1.0 / C_VAL)
            sq = dlt * dlt
            var_lo = jnp.sum(jnp.where(lo, sq, 0.0), axis=1, keepdims=True)
            var_hi = jnp.sum(jnp.where(lo, 0.0, sq), axis=1, keepdims=True)
            y = dlt * lax.rsqrt(jnp.where(lo, var_lo, var_hi) * (1.0 / C_VAL) + EPS)
            o_ref[rows, sl] = (y * g_ref[rows, sl].astype(F32) * gain[:, sl]).astype(o_ref.dtype)


def _retention(cqk, cv, cg, gain_c, consts, batch, seq):
    t = cqk.shape[0]
    tok = lambda width: pl.BlockSpec((seq, width), lambda b: (b, 0))
    full = lambda a: _const_spec(a.shape, lambda b: (0,) * a.ndim)
    return pl.pallas_call(
        functools.partial(_retention_kernel, seq=seq),
        grid=(batch,),
        in_specs=[tok(2 * C_QPAD), tok(C_WIDTH), tok(C_WIDTH)] + [full(a) for a in consts] + [full(gain_c)],
        out_specs=tok(C_WIDTH),
        out_shape=jax.ShapeDtypeStruct((t, C_WIDTH), BF16),
        compiler_params=pltpu.CompilerParams(
            dimension_semantics=("arbitrary",), vmem_limit_bytes=VMEM_LIMIT),
        name="retention",
    )(cqk, cv, cg, *consts, gain_c)


def _outffn_kernel(x_ref, ma_ref, mb_ref, mc_ref, mod_ref, wo_ref, g_ref, wg_ref, wu_ref, wd_ref, fg_ref,
                   o_ref, mix_ref, hid_ref, *, final, fchunk):
    mix_ref[:, 0:A_WIDTH] = ma_ref[...]
    mix_ref[:, A_WIDTH:A_WIDTH + B_WIDTH] = mb_ref[...]
    mix_ref[:, A_WIDTH + B_WIDTH:] = mc_ref[...]
    att = jnp.dot(mix_ref[...], wo_ref[...], preferred_element_type=F32)
    x = x_ref[...] + mod_ref[2:3, :] * att
    h = (_rms(x) * g_ref[...] * (1.0 + mod_ref[4:5, :]) + mod_ref[3:4, :]).astype(BF16)
    hidden = wg_ref.shape[1]
    for j in range(hidden // fchunk):
        sl = slice(j * fchunk, (j + 1) * fchunk)
        gate = jnp.dot(h, wg_ref[:, sl], preferred_element_type=F32)
        up = jnp.dot(h, wu_ref[:, sl], preferred_element_type=F32)
        hid_ref[:, sl] = (_silu(gate) * up).astype(BF16)
    ffn = jnp.dot(hid_ref[...], wd_ref[...], preferred_element_type=F32)
    y = x + mod_ref[5:6, :] * ffn
    if final:
        y = _rms(y) * fg_ref[...]
    o_ref[...] = y


def _outffn(x, mix_a, mix_b, mix_c, mods, layer, prep, final_g, batch, seq, tm, final):
    t, d = x.shape
    nt = seq // tm
    hidden = prep["w_gate"].shape[2]
    row = lambda w: pl.BlockSpec((tm, w), lambda b, i: (b * nt + i, 0))
    wl = lambda a: _const_spec((None,) + a.shape[1:], lambda b, i: (layer,) + (0,) * (a.ndim - 1))
    return pl.pallas_call(
        functools.partial(_outffn_kernel, final=final, fchunk=256),
        grid=(batch, nt),
        in_specs=[
            row(d), row(A_WIDTH), row(B_WIDTH), row(C_WIDTH),
            pl.BlockSpec((None, None, 6, d), lambda b, i: (layer, b, 0, 0)),
            wl(prep["w_out"]), wl(prep["norm2_g"]), wl(prep["w_gate"]), wl(prep["w_up"]), wl(prep["w_down"]),
            _const_spec((1, d), lambda b, i: (0, 0)),
        ],
        out_specs=row(d),
        out_shape=jax.ShapeDtypeStruct((t, d), F32),
        scratch_shapes=[pltpu.VMEM((tm, d), BF16), pltpu.VMEM((tm, hidden), BF16)],
        compiler_params=pltpu.CompilerParams(
            dimension_semantics=("arbitrary", "arbitrary"), vmem_limit_bytes=VMEM_LIMIT),
        name="outproj_ffn",
    )(x, mix_a, mix_b, mix_c, mods, prep["w_out"], prep["norm2_g"], prep["w_gate"], prep["w_up"],
      prep["w_down"], final_g)


def _prepare(norm1_g, w_in, mla_q_norm, mla_kv_norm, mla_w_uq, mla_w_ukv, w_out, norm2_g,
             ffn_w_gate, ffn_w_up, ffn_w_down):
    depth, d, _ = w_in.shape
    b_lat0 = 3 * A_WIDTH
    rope0 = b_lat0 + B_Q_RANK + B_KV_RANK
    c0 = rope0 + B_ROPE
    pad = lambda n: jnp.zeros((depth, d, n), w_in.dtype)
    w_in_p = jnp.concatenate([
        w_in[:, :, :rope0],
        w_in[:, :, c0:c0 + C_QK], w_in[:, :, rope0:c0], pad(C_QPAD - C_QK - B_ROPE),
        w_in[:, :, c0 + C_QK:c0 + 2 * C_QK], pad(C_QPAD - C_QK),
        w_in[:, :, c0 + 2 * C_QK:],
    ], axis=2).astype(BF16)
    assert C_QK + B_ROPE <= C_QPAD and C_QK % LANES == B_NOPE
    assert w_in_p.shape[2] == IN_COLS
    uq = mla_w_uq.reshape(depth, B_Q_RANK, B_HEADS, B_NOPE + B_ROPE)
    uq = jnp.pad(uq, ((0, 0), (0, 0), (0, 0), (0, LANES - B_NOPE - B_ROPE)))
    ukv = mla_w_ukv.reshape(depth, B_KV_RANK, B_HEADS, B_NOPE + B_V)
    uk = jnp.pad(ukv[..., :B_NOPE], ((0, 0), (0, 0), (0, 0), (0, LANES - B_NOPE)))
    uv = ukv[..., B_NOPE:]
    return {
        "norm1_g": norm1_g[:, None, :],
        "w_in": w_in_p,
        "q_norm": mla_q_norm[:, None, :],
        "kv_norm": mla_kv_norm[:, None, :],
        "w_uq": uq.reshape(depth, B_Q_RANK, B_HEADS * LANES).astype(BF16),
        "w_uk": uk.reshape(depth, B_KV_RANK, B_HEADS * LANES).astype(BF16),
        "w_uv": uv.reshape(depth, B_KV_RANK, B_WIDTH).astype(BF16),
        "w_out": w_out.astype(BF16),
        "norm2_g": norm2_g[:, None, :],
        "w_gate": ffn_w_gate.astype(BF16),
        "w_up": ffn_w_up.astype(BF16),
        "w_down": ffn_w_down.astype(BF16),
    }


def kernel(x, c, positions, rel_bias, ada_w, ada_b, norm1_g, w_in, mla_q_norm, mla_kv_norm, mla_w_uq,
           mla_w_ukv, mix_gain, w_out, norm2_g, ffn_w_gate, ffn_w_up, ffn_w_down, final_norm):
    batch, seq, d = x.shape
    depth = w_in.shape[0]
    tm = 512
    assert seq % tm == 0 and seq % C_CHUNK == 0

    prep = _prepare(norm1_g, w_in, mla_q_norm, mla_kv_norm, mla_w_uq, mla_w_ukv, w_out, norm2_g,
                    ffn_w_gate, ffn_w_up, ffn_w_down)
    mods = _mods(c, ada_w, ada_b).reshape(depth, batch, 6, d)
    tables = _rope_tables(positions)
    bias = _bias_tables(rel_bias)
    ret_consts = _retention_consts()
    final_g = final_norm[None, :]

    xf = x.reshape(batch * seq, d)
    for l in range(depth):
        a_qkv, bq, bk, bv, cqk, cv, cg = _inproj(xf, mods, l, prep, tables, batch, seq, tm)
        gain = mix_gain[l][None, :]
        mix_a = _dilated(a_qkv, bias, gain[:, :A_WIDTH], batch, seq)
        mix_b = _mla(bq, bk, bv, gain[:, A_WIDTH:A_WIDTH + B_WIDTH], batch, seq)
        mix_c = _retention(cqk, cv, cg, gain[:, A_WIDTH + B_WIDTH:], ret_consts, batch, seq)
        xf = _outffn(xf, mix_a, mix_b, mix_c, mods, l, prep, final_g, batch, seq, tm, final=(l == depth - 1))
    return xf.reshape(batch, seq, d)
```

```python
import functools

import numpy as np
import jax
import jax.numpy as jnp
from jax import lax
from jax.experimental import pallas as pl
from jax.experimental.pallas import tpu as pltpu

F32 = jnp.float32
BF16 = jnp.bfloat16

HEAD_DIM = 64
A_HEADS = 6
A_PATTERNS = ((128, 1), (512, 4), (2048, 16))
A_WIDTH = A_HEADS * HEAD_DIM
B_HEADS = 4
B_NOPE = 64
B_ROPE = 32
B_V = 64
B_Q_RANK = 256
B_KV_RANK = 128
B_WIDTH = B_HEADS * B_V
C_HEADS = 6
C_KEY = 32
C_VAL = 64
C_WIDTH = C_HEADS * C_VAL
C_QK = C_HEADS * C_KEY
C_CHUNK = 128
N_BUCKETS = 32
MAX_DISTANCE = 2048
ROPE_BASE = 10000.0
EPS = 1e-6

LANES = 128
WIN = 128
STEP = 4
NEG = -1e30
LOG2E = 1.4426950408889634
VMEM_LIMIT = 56 * 1024 * 1024

OFF_A = 0
OFF_BQ = 3 * A_WIDTH
OFF_BKV = OFF_BQ + B_Q_RANK
OFF_C = OFF_BKV + B_KV_RANK
C_QPAD = 2 * LANES
OFF_CV = OFF_C + 2 * C_QPAD
OFF_CG = OFF_CV + C_WIDTH
IN_COLS = OFF_CG + C_WIDTH


def _const_spec(shape, index_map):
    return pl.BlockSpec(shape, index_map, pipeline_mode=pl.Buffered(1))


def _silu(x):
    return x / (1.0 + jnp.exp(-x))


def _rms(x):
    return x * lax.rsqrt(jnp.mean(x * x, axis=-1, keepdims=True) + EPS)


def _mods_kernel(c_ref, w_ref, b_ref, o_ref):
    c = c_ref[...]
    o_ref[...] = jnp.dot(_silu(c), w_ref[...], preferred_element_type=F32,
                         precision=lax.Precision.HIGHEST) + b_ref[...]


def _mods(c, ada_w, ada_b):
    depth, d, n = ada_w.shape
    b = c.shape[0]
    tn = 1536
    return pl.pallas_call(
        _mods_kernel,
        grid=(depth, n // tn),
        in_specs=[
            pl.BlockSpec((b, d), lambda l, j: (0, 0)),
            pl.BlockSpec((None, d, tn), lambda l, j: (l, 0, j)),
            pl.BlockSpec((None, 1, tn), lambda l, j: (l, 0, j)),
        ],
        out_specs=pl.BlockSpec((None, b, tn), lambda l, j: (l, 0, j)),
        out_shape=jax.ShapeDtypeStruct((depth, b, n), F32),
        compiler_params=pltpu.CompilerParams(vmem_limit_bytes=VMEM_LIMIT),
        name="adaln_mods",
    )(c, ada_w, ada_b.reshape(depth, 1, n))


def _rope_kernel(pos_ref, f_ref, s_ref, cb_ref, sb_ref, cc_ref, sc_ref):
    p = pos_ref[...].astype(F32)
    ang = p * f_ref[1:2, :]
    cos = jnp.cos(ang)
    sin = jnp.sin(ang)
    rope_b = f_ref[0:1, :] > 0.0
    cb_ref[...] = jnp.where(rope_b, cos, 1.0)
    sb_ref[...] = jnp.where(rope_b, sin, 0.0) * s_ref[0:1, :]
    cc_ref[...] = cos
    sc_ref[...] = sin * s_ref[1:2, :]


def _rope_tables(positions):
    t = positions.size
    half = B_ROPE // 2
    inv_freq = (1.0 / (ROPE_BASE ** (np.arange(half, dtype=np.float32) / half))).astype(np.float32)
    zeros = np.zeros(half, np.float32)
    ones = np.ones(half, np.float32)
    f_b = np.concatenate([np.zeros(B_NOPE, np.float32), inv_freq, inv_freq, np.zeros(32, np.float32)])
    s_b = np.concatenate([np.zeros(B_NOPE, np.float32), -ones, ones, np.zeros(32, np.float32)])
    f_c = np.tile(np.concatenate([inv_freq, inv_freq]), LANES // C_KEY)
    s_c = np.tile(np.concatenate([-ones, ones]), LANES // C_KEY)
    del zeros
    freqs = jnp.asarray(np.stack([f_b, f_c]))
    signs = jnp.asarray(np.stack([s_b, s_c]))
    tr = 2048
    out = jax.ShapeDtypeStruct((t, LANES), F32)
    row = pl.BlockSpec((tr, LANES), lambda i: (i, 0))
    return pl.pallas_call(
        _rope_kernel,
        grid=(t // tr,),
        in_specs=[
            pl.BlockSpec((tr, 1), lambda i: (i, 0)),
            pl.BlockSpec((2, LANES), lambda i: (0, 0)),
            pl.BlockSpec((2, LANES), lambda i: (0, 0)),
        ],
        out_specs=[row, row, row, row],
        out_shape=[out, out, out, out],
        name="rope_tables",
    )(positions.reshape(t, 1), freqs, signs)


def _t5_bucket(dist):
    max_exact = N_BUCKETS // 2
    safe = np.maximum(dist, 1).astype(np.float32)
    large = max_exact + (np.log(safe / max_exact) / np.log(MAX_DISTANCE / max_exact)
                         * (N_BUCKETS - max_exact)).astype(np.int32)
    large = np.minimum(large, N_BUCKETS - 1)
    return np.where(dist < max_exact, dist, large).astype(np.int32)


def _bias_kernel(bmap_ref, rb_ref, o_ref):
    h = pl.program_id(1)
    bm = bmap_ref[...]
    t = jnp.full(bm.shape, NEG, F32)
    for b in range(N_BUCKETS):
        t = jnp.where(bm == b, rb_ref[b, h] * LOG2E, t)
    o_ref[...] = t


def _bias_tables(rel_bias):
    qi = np.arange(WIN)[None, :]
    c = np.arange(2 * WIN)[:, None]
    j = qi - c + WIN
    maps = []
    for (w, d) in A_PATTERNS:
        assert w // d == WIN
        bucket = _t5_bucket(np.arange(WIN + 1, dtype=np.int32) * d)
        maps.append(np.where((j >= 0) & (j <= WIN), bucket[np.clip(j, 0, WIN)], -1).astype(np.int32))
    bmap = jnp.asarray(np.stack(maps))
    npat = len(A_PATTERNS)
    return pl.pallas_call(
        _bias_kernel,
        grid=(npat, A_HEADS),
        in_specs=[
            pl.BlockSpec((None, 2 * WIN, WIN), lambda p, h: (p, 0, 0)),
            pl.BlockSpec(memory_space=pltpu.SMEM),
        ],
        out_specs=pl.BlockSpec((None, 2 * WIN, WIN), lambda p, h: (p, 0, h)),
        out_shape=jax.ShapeDtypeStruct((npat, 2 * WIN, A_HEADS * WIN), F32),
        name="t5_bias_tables",
    )(bmap, rel_bias)


def _rot_half(x, x1_mask):
    w = x.shape[-1]
    half = B_ROPE // 2
    return jnp.where(x1_mask, pltpu.roll(x, w - half, 1), pltpu.roll(x, half, 1))


def _inproj_kernel(x_ref, mod_ref, g_ref, w_ref, qg_ref, kvg_ref, wuq_ref, wuk_ref, wuv_ref,
                   cb_ref, sb_ref, cc_ref, sc_ref, ks_ref,
                   a_ref, bq_ref, bk_ref, bv_ref, cqk_ref, cv_ref, cg_ref):
    rows = slice(None)
    x = x_ref[rows, :]
    tm = x.shape[0]
    h = (_rms(x) * g_ref[...] * (1.0 + mod_ref[1:2, :]) + mod_ref[0:1, :]).astype(BF16)

    def proj(lo, hi):
        return jnp.dot(h, w_ref[:, lo:hi], preferred_element_type=F32)

    a_ref[rows, 0:A_WIDTH] = (proj(OFF_A, OFF_A + A_WIDTH) * (HEAD_DIM ** -0.5 * LOG2E)).astype(BF16)
    a_ref[rows, A_WIDTH:3 * A_WIDTH] = proj(OFF_A + A_WIDTH, OFF_BQ).astype(BF16)

    lane = lax.broadcasted_iota(jnp.int32, (tm, LANES), 1)
    b_x1 = (lane >= B_NOPE) & (lane < B_NOPE + B_ROPE // 2)
    c_x1 = (lane & (C_KEY // 2)) == 0
    cos_b = cb_ref[rows, :]
    sin_b = sb_ref[rows, :]

    q_lat = proj(OFF_BQ, OFF_BKV)
    qn = (_rms(q_lat) * qg_ref[...]).astype(BF16)
    q = jnp.dot(qn, wuq_ref[...], preferred_element_type=F32)
    kv_lat = proj(OFF_BKV, OFF_C)
    kvn = (_rms(kv_lat) * kvg_ref[...]).astype(BF16)
    k_nope = jnp.dot(kvn, wuk_ref[...], preferred_element_type=F32)
    bv_ref[rows, :] = jnp.dot(kvn, wuv_ref[...], preferred_element_type=F32).astype(BF16)
    qk = proj(OFF_C, OFF_CV)
    k_rope = jnp.where(lane >= B_NOPE, qk[:, LANES:2 * LANES], 0.0)
    k_pe = k_rope * cos_b + _rot_half(k_rope, b_x1) * sin_b
    b_scale = (B_NOPE + B_ROPE) ** -0.5 * LOG2E
    for hh in range(B_HEADS):
        sl = slice(hh * LANES, (hh + 1) * LANES)
        qh = q[:, sl]
        bq_ref[rows, sl] = ((qh * cos_b + _rot_half(qh, b_x1) * sin_b) * b_scale).astype(BF16)
        bk_ref[rows, sl] = (k_nope[:, sl] + k_pe).astype(BF16)

    cos_c = cc_ref[rows, :]
    sin_c = sc_ref[rows, :]
    for t in range(2 * C_QPAD // LANES):
        sl = slice(t * LANES, (t + 1) * LANES)
        xt = qk[:, sl]
        cqk_ref[rows, sl] = ((xt * cos_c + _rot_half(xt, c_x1) * sin_c) * ks_ref[:, sl]).astype(BF16)
    cv_ref[rows, :] = proj(OFF_CV, OFF_CG).astype(BF16)
    cg_ref[rows, :] = _silu(proj(OFF_CG, IN_COLS)).astype(BF16)


def _inproj(x, mods, layer, prep, tables, batch, seq, tm):
    t, d = x.shape
    nt = seq // tm
    row = lambda w: pl.BlockSpec((tm, w), lambda b, i: (b * nt + i, 0))
    wl = lambda a: _const_spec((None,) + a.shape[1:], lambda b, i: (layer,) + (0,) * (a.ndim - 1))
    cos_b, sin_b, cos_c, sin_c = tables
    kscale = jnp.asarray(np.concatenate([np.ones(C_QPAD, np.float32),
                                         np.full(C_QPAD, C_KEY ** -0.5, np.float32)])[None, :])
    outs = [(3 * A_WIDTH, BF16), (B_HEADS * LANES, BF16), (B_HEADS * LANES, BF16), (B_WIDTH, BF16),
            (2 * C_QPAD, BF16), (C_WIDTH, BF16), (C_WIDTH, BF16)]
    return pl.pallas_call(
        _inproj_kernel,
        grid=(batch, nt),
        in_specs=[
            row(d),
            pl.BlockSpec((None, None, 6, d), lambda b, i: (layer, b, 0, 0)),
            wl(prep["norm1_g"]), wl(prep["w_in"]), wl(prep["q_norm"]), wl(prep["kv_norm"]),
            wl(prep["w_uq"]), wl(prep["w_uk"]), wl(prep["w_uv"]),
            row(LANES), row(LANES), row(LANES), row(LANES),
            _const_spec((1, 2 * C_QPAD), lambda b, i: (0, 0)),
        ],
        out_specs=[row(w) for w, _ in outs],
        out_shape=[jax.ShapeDtypeStruct((t, w), dt) for w, dt in outs],
        compiler_params=pltpu.CompilerParams(
            dimension_semantics=("arbitrary", "arbitrary"), vmem_limit_bytes=VMEM_LIMIT),
        name="inproj",
    )(x, mods, prep["norm1_g"], prep["w_in"], prep["q_norm"], prep["kv_norm"],
      prep["w_uq"], prep["w_uk"], prep["w_uv"], cos_b, sin_b, cos_c, sin_c, kscale)


def _class_major_blocks(seq, level):
    n = seq // STEP
    perm = np.arange(seq)
    for _ in range(level):
        perm = perm.reshape(n, STEP).T.reshape(seq)
    blocks = perm.reshape(seq // WIN, WIN)
    stride = STEP ** level
    assert (blocks == blocks[:, :1] + stride * np.arange(WIN)).all()
    return [(int(b[0]), stride) for b in blocks]


def _dilated_kernel(q_ref, k_ref, v_ref, bias_ref, gain_ref, o_ref,
                    qf, kf, vf, qlo, qhi, kp, vt, *stage_refs, seq):
    npat = len(A_PATTERNS)
    nblk = seq // WIN
    s_sc, e_sc, stat_sc, out_s, lse_s = (stage_refs[i * npat:(i + 1) * npat] for i in range(5))
    lane = lax.broadcasted_iota(jnp.int32, (WIN, LANES), 1)
    lo_lane = lane < HEAD_DIM
    lo_row = lax.broadcasted_iota(jnp.int32, (LANES, WIN), 0) < HEAD_DIM

    qf[0] = q_ref[...].astype(F32)
    kf[0] = k_ref[...].astype(F32)
    vf[0] = v_ref[...].astype(F32)

    for p in range(npat):
        if p > 0:
            n = seq // STEP
            for r in range(STEP):
                dst = slice(r * n, (r + 1) * n)
                src = pl.ds(r, n, stride=STEP)
                for f in (qf, kf, vf):
                    f[p % 2, dst, :] = f[(p - 1) % 2, src, :]
        for c in range(0, seq, 4 * WIN):
            rows = slice(c, c + 4 * WIN)
            q = qf[p % 2, rows, :]
            lo_q = lax.broadcasted_iota(jnp.int32, q.shape, 1) < HEAD_DIM
            qlo[p, rows, :] = jnp.where(lo_q, q, 0.0).astype(BF16)
            qhi[p, rows, :] = jnp.where(lo_q, 0.0, q).astype(BF16)
            vt[p, :, rows] = vf[p % 2, rows, :].T.astype(BF16)
            if p > 0:
                kp[p - 1, rows, :] = kf[p % 2, rows, :].astype(BF16)

    def key_rows(p, blk):
        r0 = blk * WIN
        first = blk % (nblk // A_PATTERNS[p][1]) == 0
        return (r0 if first else r0 - WIN), r0 + WIN

    for p in range(npat - 1, -1, -1):
        for blk in range(nblk):
            k0, k1 = key_rows(p, blk)
            r0 = blk * WIN
            q2 = jnp.concatenate([qlo[p, r0:r0 + WIN, :], qhi[p, r0:r0 + WIN, :]], axis=0)
            kcat = k_ref[k0:k1, :] if p == 0 else kp[p - 1, k0:k1, :]
            bias = bias_ref[p, 2 * WIN - (k1 - k0):, :]
            s_sc[p][blk, 0:k1 - k0, :] = lax.dot_general(
                kcat, q2, (((1,), (1,)), ((), ())), preferred_element_type=F32) + bias
        for blk in range(nblk):
            k0, k1 = key_rows(p, blk)
            s = s_sc[p][blk, 0:k1 - k0, :]
            m = jnp.max(s, axis=0, keepdims=True)
            e = jnp.exp2(s - m)
            l = jnp.sum(e, axis=0, keepdims=True)
            e_sc[p][blk, 0:k1 - k0, :] = e.astype(BF16)
            stat_sc[p][blk, 0:1, :] = 1.0 / l
            stat_sc[p][blk, 1:2, :] = m + jnp.log2(l)
        for blk, (start, stride) in enumerate(_class_major_blocks(seq, p)):
            k0, k1 = key_rows(p, blk)
            acc = jnp.dot(vt[p, :, k0:k1], e_sc[p][blk, 0:k1 - k0, :], preferred_element_type=F32)
            out = acc * stat_sc[p][blk, 0:1, :]
            lse = stat_sc[p][blk, 1:2, :]
            out = jnp.where(lo_row, out[:, :WIN], out[:, WIN:])
            lse = jnp.where(lo_row, lse[:, :WIN], lse[:, WIN:])
            dst = pl.ds(start, WIN, stride=stride) if stride > 1 else slice(start, start + WIN)
            out_s[p][dst, :] = out.T
            lse_s[p][dst, :] = lse.T

    gain = gain_ref[...]
    for blk in range(nblk):
        sl = slice(blk * WIN, (blk + 1) * WIN)
        lses = [s[sl, :] for s in lse_s]
        mx = functools.reduce(jnp.maximum, lses)
        ws = [jnp.exp2(s - mx) for s in lses]
        o = sum(w * x[sl, :] for w, x in zip(ws, out_s)) / sum(ws)
        sq = o * o
        ms_lo = jnp.sum(jnp.where(lo_lane, sq, 0.0), axis=1, keepdims=True)
        ms_hi = jnp.sum(jnp.where(lo_lane, 0.0, sq), axis=1, keepdims=True)
        inv = lax.rsqrt(jnp.where(lo_lane, ms_lo, ms_hi) * (1.0 / HEAD_DIM) + EPS)
        o_ref[sl, :] = (o * inv * gain).astype(o_ref.dtype)


def _dilated(a_qkv, bias, gain_a, batch, seq):
    t = a_qkv.shape[0]
    pairs = A_WIDTH // LANES
    assert all(d == STEP ** p and seq % (d * WIN) == 0 for p, (_, d) in enumerate(A_PATTERNS))
    npat = bias.shape[0]
    col = lambda off: pl.BlockSpec((seq, LANES), lambda b, g: (b, off + g))
    scratch = ([pltpu.VMEM((2, seq, LANES), F32) for _ in range(3)]
               + [pltpu.VMEM((npat, seq, LANES), BF16) for _ in range(2)]
               + [pltpu.VMEM((npat - 1, seq, LANES), BF16)]
               + [pltpu.VMEM((npat, LANES, seq), BF16)]
               + [pltpu.VMEM((seq // WIN, 2 * WIN, 2 * WIN), F32) for _ in range(npat)]
               + [pltpu.VMEM((seq // WIN, 2 * WIN, 2 * WIN), BF16) for _ in range(npat)]
               + [pltpu.VMEM((seq // WIN, 2, 2 * WIN), F32) for _ in range(npat)]
               + [pltpu.VMEM((seq, LANES), F32) for _ in range(2 * npat)])
    return pl.pallas_call(
        functools.partial(_dilated_kernel, seq=seq),
        grid=(batch, pairs),
        in_specs=[
            col(0), col(pairs), col(2 * pairs),
            pl.BlockSpec((npat, 2 * WIN, 2 * WIN), lambda b, g: (0, 0, g)),
            pl.BlockSpec((1, LANES), lambda b, g: (0, g)),
        ],
        out_specs=pl.BlockSpec((seq, LANES), lambda b, g: (b, g)),
        out_shape=jax.ShapeDtypeStruct((t, A_WIDTH), BF16),
        scratch_shapes=scratch,
        compiler_params=pltpu.CompilerParams(
            dimension_semantics=("arbitrary", "arbitrary"), vmem_limit_bytes=VMEM_LIMIT),
        name="dilated_attention",
    )(a_qkv, a_qkv, a_qkv, bias, gain_a)


def _mla_kernel(q_ref, k_ref, v_ref, gain_ref, o_ref, vt, s_sc, e_sc, *, seq, tq):
    key_idx = lax.broadcasted_iota(jnp.int32, (tq, tq), 0)
    qry_idx = lax.broadcasted_iota(jnp.int32, (tq, tq), 1)
    causal = key_idx <= qry_idx
    gain = gain_ref[...]
    for c in range(0, seq, tq):
        vt[:, c:c + tq] = v_ref[c:c + tq, :].astype(F32).T.astype(BF16)

    def head(hh, i, base):
        hs = slice(hh * LANES, (hh + 1) * LANES)
        q = q_ref[i * tq:(i + 1) * tq, hs]
        m = None
        for j in range(i + 1):
            s = lax.dot_general(k_ref[j * tq:(j + 1) * tq, hs], q, (((1,), (1,)), ((), ())),
                                preferred_element_type=F32)
            if j == i:
                s = jnp.where(causal, s, NEG)
            s_sc[base + j] = s
            bm = jnp.max(s, axis=0, keepdims=True)
            m = bm if m is None else jnp.maximum(m, bm)
        l = None
        for j in range(i + 1):
            e = jnp.exp2(s_sc[base + j] - m)
            ls = jnp.sum(e, axis=0, keepdims=True)
            l = ls if l is None else l + ls
            e_sc[(base + j) * tq:(base + j + 1) * tq, :] = e.astype(BF16)
        acc = jnp.dot(vt[hh * B_V:(hh + 1) * B_V, 0:(i + 1) * tq], e_sc[base * tq:(base + i + 1) * tq, :],
                      preferred_element_type=F32)
        return acc / l

    base = 0
    for i in range(seq // tq):
        normed = []
        for hh in range(2):
            o = head(hh, i, base)
            base += i + 1
            normed.append(o * lax.rsqrt(jnp.mean(o * o, axis=0, keepdims=True) + EPS))
        o_ref[i * tq:(i + 1) * tq, :] = (jnp.concatenate(normed, axis=0).T * gain).astype(o_ref.dtype)


def _mla(bq, bk, bv, gain_b, batch, seq):
    t = bq.shape[0]
    pairs = B_HEADS // 2
    tq = 512
    assert seq % tq == 0
    nq = seq // tq
    nslots = 2 * (nq * (nq + 1) // 2)
    return pl.pallas_call(
        functools.partial(_mla_kernel, seq=seq, tq=tq),
        grid=(batch, pairs),
        in_specs=[
            pl.BlockSpec((seq, 2 * LANES), lambda b, g: (b, g)),
            pl.BlockSpec((seq, 2 * LANES), lambda b, g: (b, g)),
            pl.BlockSpec((seq, LANES), lambda b, g: (b, g)),
            pl.BlockSpec((1, LANES), lambda b, g: (0, g)),
        ],
        out_specs=pl.BlockSpec((seq, LANES), lambda b, g: (b, g)),
        out_shape=jax.ShapeDtypeStruct((t, B_WIDTH), BF16),
        scratch_shapes=[pltpu.VMEM((LANES, seq), BF16),
                        pltpu.VMEM((nslots, tq, tq), F32), pltpu.VMEM((nslots * tq, tq), BF16)],
        compiler_params=pltpu.CompilerParams(
            dimension_semantics=("arbitrary", "arbitrary"), vmem_limit_bytes=VMEM_LIMIT),
        name="latent_attention",
    )(bq, bk, bv, gain_b)


def _retention_consts():
    h = C_HEADS
    log_g = np.log(1.0 - 2.0 ** (-5.0 - np.arange(h))).astype(np.float32)
    i = np.arange(C_CHUNK, dtype=np.float32)
    rel = i[:, None] - i[None, :]
    decay = (np.exp(np.maximum(rel, 0.0)[None] * log_g[:, None, None]) * (rel >= 0)[None]).astype(np.float32)
    xi = np.exp((i + 1.0)[None, :] * log_g[:, None]).astype(np.float32)
    zeta = np.exp((C_CHUNK - 1.0 - i)[None, :] * log_g[:, None]).astype(np.float32)
    chunk_decay = np.exp(C_CHUNK * log_g).astype(np.float32)
    decay_all = np.concatenate(list(decay), axis=1)
    xi_mat = np.repeat(xi.T, C_VAL, axis=1)
    zeta_t = np.zeros((C_QPAD, C_CHUNK), np.float32)
    zeta_t[:C_QK] = np.repeat(zeta, C_KEY, axis=0)
    cd = np.repeat(chunk_decay, C_VAL)[None, :]
    bd = np.zeros((C_QPAD, C_WIDTH), np.float32)
    kmask = np.zeros((C_QPAD, h * C_CHUNK), np.float32)
    vmask = np.zeros((h * C_CHUNK, C_WIDTH), np.float32)
    for hh in range(h):
        bd[hh * C_KEY:(hh + 1) * C_KEY, hh * C_VAL:(hh + 1) * C_VAL] = 1.0
        kmask[hh * C_KEY:(hh + 1) * C_KEY, hh * C_CHUNK:(hh + 1) * C_CHUNK] = 1.0
        vmask[hh * C_CHUNK:(hh + 1) * C_CHUNK, hh * C_VAL:(hh + 1) * C_VAL] = 1.0
    f32s = tuple(jnp.asarray(a) for a in (decay_all, xi_mat, zeta_t, cd, bd))
    return f32s + (jnp.asarray(kmask, BF16), jnp.asarray(vmask, BF16))


def _retention_kernel(qk_ref, v_ref, g_ref, decay_ref, xi_ref, zeta_ref, cd_ref, bd_ref, kmask_ref,
                      vmask_ref, gain_ref, o_ref, *, seq):
    lane = lax.broadcasted_iota(jnp.int32, (C_CHUNK, LANES), 1)
    lo = lane < C_VAL
    gain = gain_ref[...]
    state = jnp.zeros((C_QPAD, C_WIDTH), F32)

    for n in range(seq // C_CHUNK):
        rows = slice(n * C_CHUNK, (n + 1) * C_CHUNK)
        q = qk_ref[rows, 0:C_QPAD]
        k_t = qk_ref[rows, C_QPAD:2 * C_QPAD].astype(F32).T
        v = v_ref[rows, :]
        k_bd = jnp.concatenate([k_t.astype(BF16)] * C_HEADS, axis=1) * kmask_ref[...]
        s = jnp.dot(q, k_bd, preferred_element_type=F32) * decay_ref[...]
        v_bd = jnp.concatenate([v] * C_HEADS, axis=0) * vmask_ref[...]
        o = (jnp.dot(s.astype(BF16), v_bd, preferred_element_type=F32)
             + jnp.dot(q, state.astype(BF16), preferred_element_type=F32) * xi_ref[...])
        upd = jnp.dot((k_t * zeta_ref[...]).astype(BF16), v, preferred_element_type=F32)
        state = state * cd_ref[...] + upd * bd_ref[...]
        for t in range(C_WIDTH // LANES):
            sl = slice(t * LANES, (t + 1) * LANES)
            x = o[:, sl]
            mu_lo = jnp.sum(jnp.where(lo, x, 0.0), axis=1, keepdims=True)
            mu_hi = jnp.sum(jnp.where(lo, 0.0, x), axis=1, keepdims=True)
            dlt = x - jnp.where(lo, mu_lo, mu_hi) * (1.0 / C_VAL)
            sq = dlt * dlt
            var_lo = jnp.sum(jnp.where(lo, sq, 0.0), axis=1, keepdims=True)
            var_hi = jnp.sum(jnp.where(lo, 0.0, sq), axis=1, keepdims=True)
            y = dlt * lax.rsqrt(jnp.where(lo, var_lo, var_hi) * (1.0 / C_VAL) + EPS)
            o_ref[rows, sl] = (y * g_ref[rows, sl].astype(F32) * gain[:, sl]).astype(o_ref.dtype)


def _retention(cqk, cv, cg, gain_c, consts, batch, seq):
    t = cqk.shape[0]
    tok = lambda width: pl.BlockSpec((seq, width), lambda b: (b, 0))
    full = lambda a: _const_spec(a.shape, lambda b: (0,) * a.ndim)
    return pl.pallas_call(
        functools.partial(_retention_kernel, seq=seq),
        grid=(batch,),
        in_specs=[tok(2 * C_QPAD), tok(C_WIDTH), tok(C_WIDTH)] + [full(a) for a in consts] + [full(gain_c)],
        out_specs=tok(C_WIDTH),
        out_shape=jax.ShapeDtypeStruct((t, C_WIDTH), BF16),
        compiler_params=pltpu.CompilerParams(
            dimension_semantics=("arbitrary",), vmem_limit_bytes=VMEM_LIMIT),
        name="retention",
    )(cqk, cv, cg, *consts, gain_c)


def _outffn_kernel(x_ref, ma_ref, mb_ref, mc_ref, mod_ref, wo_ref, g_ref, wg_ref, wu_ref, wd_ref, fg_ref,
                   o_ref, mix_ref, hid_ref, *, final, fchunk):
    mix_ref[:, 0:A_WIDTH] = ma_ref[...]
    mix_ref[:, A_WIDTH:A_WIDTH + B_WIDTH] = mb_ref[...]
    mix_ref[:, A_WIDTH + B_WIDTH:] = mc_ref[...]
    att = jnp.dot(mix_ref[...], wo_ref[...], preferred_element_type=F32)
    x = x_ref[...] + mod_ref[2:3, :] * att
    h = (_rms(x) * g_ref[...] * (1.0 + mod_ref[4:5, :]) + mod_ref[3:4, :]).astype(BF16)
    hidden = wg_ref.shape[1]
    for j in range(hidden // fchunk):
        sl = slice(j * fchunk, (j + 1) * fchunk)
        gate = jnp.dot(h, wg_ref[:, sl], preferred_element_type=F32)
        up = jnp.dot(h, wu_ref[:, sl], preferred_element_type=F32)
        hid_ref[:, sl] = (_silu(gate) * up).astype(BF16)
    ffn = jnp.dot(hid_ref[...], wd_ref[...], preferred_element_type=F32)
    y = x + mod_ref[5:6, :] * ffn
    if final:
        y = _rms(y) * fg_ref[...]
    o_ref[...] = y


def _outffn(x, mix_a, mix_b, mix_c, mods, layer, prep, final_g, batch, seq, tm, final):
    t, d = x.shape
    nt = seq // tm
    hidden = prep["w_gate"].shape[2]
    row = lambda w: pl.BlockSpec((tm, w), lambda b, i: (b * nt + i, 0))
    wl = lambda a: _const_spec((None,) + a.shape[1:], lambda b, i: (layer,) + (0,) * (a.ndim - 1))
    return pl.pallas_call(
        functools.partial(_outffn_kernel, final=final, fchunk=256),
        grid=(batch, nt),
        in_specs=[
            row(d), row(A_WIDTH), row(B_WIDTH), row(C_WIDTH),
            pl.BlockSpec((None, None, 6, d), lambda b, i: (layer, b, 0, 0)),
            wl(prep["w_out"]), wl(prep["norm2_g"]), wl(prep["w_gate"]), wl(prep["w_up"]), wl(prep["w_down"]),
            _const_spec((1, d), lambda b, i: (0, 0)),
        ],
        out_specs=row(d),
        out_shape=jax.ShapeDtypeStruct((t, d), F32),
        scratch_shapes=[pltpu.VMEM((tm, d), BF16), pltpu.VMEM((tm, hidden), BF16)],
        compiler_params=pltpu.CompilerParams(
            dimension_semantics=("arbitrary", "arbitrary"), vmem_limit_bytes=VMEM_LIMIT),
        name="outproj_ffn",
    )(x, mix_a, mix_b, mix_c, mods, prep["w_out"], prep["norm2_g"], prep["w_gate"], prep["w_up"],
      prep["w_down"], final_g)


def _prepare(norm1_g, w_in, mla_q_norm, mla_kv_norm, mla_w_uq, mla_w_ukv, w_out, norm2_g,
             ffn_w_gate, ffn_w_up, ffn_w_down):
    depth, d, _ = w_in.shape
    b_lat0 = 3 * A_WIDTH
    rope0 = b_lat0 + B_Q_RANK + B_KV_RANK
    c0 = rope0 + B_ROPE
    pad = lambda n: jnp.zeros((depth, d, n), w_in.dtype)
    w_in_p = jnp.concatenate([
        w_in[:, :, :rope0],
        w_in[:, :, c0:c0 + C_QK], w_in[:, :, rope0:c0], pad(C_QPAD - C_QK - B_ROPE),
        w_in[:, :, c0 + C_QK:c0 + 2 * C_QK], pad(C_QPAD - C_QK),
        w_in[:, :, c0 + 2 * C_QK:],
    ], axis=2).astype(BF16)
    assert C_QK + B_ROPE <= C_QPAD and C_QK % LANES == B_NOPE
    assert w_in_p.shape[2] == IN_COLS
    uq = mla_w_uq.reshape(depth, B_Q_RANK, B_HEADS, B_NOPE + B_ROPE)
    uq = jnp.pad(uq, ((0, 0), (0, 0), (0, 0), (0, LANES - B_NOPE - B_ROPE)))
    ukv = mla_w_ukv.reshape(depth, B_KV_RANK, B_HEADS, B_NOPE + B_V)
    uk = jnp.pad(ukv[..., :B_NOPE], ((0, 0), (0, 0), (0, 0), (0, LANES - B_NOPE)))
    uv = ukv[..., B_NOPE:]
    return {
        "norm1_g": norm1_g[:, None, :],
        "w_in": w_in_p,
        "q_norm": mla_q_norm[:, None, :],
        "kv_norm": mla_kv_norm[:, None, :],
        "w_uq": uq.reshape(depth, B_Q_RANK, B_HEADS * LANES).astype(BF16),
        "w_uk": uk.reshape(depth, B_KV_RANK, B_HEADS * LANES).astype(BF16),
        "w_uv": uv.reshape(depth, B_KV_RANK, B_WIDTH).astype(BF16),
        "w_out": w_out.astype(BF16),
        "norm2_g": norm2_g[:, None, :],
        "w_gate": ffn_w_gate.astype(BF16),
        "w_up": ffn_w_up.astype(BF16),
        "w_down": ffn_w_down.astype(BF16),
    }


def kernel(x, c, positions, rel_bias, ada_w, ada_b, norm1_g, w_in, mla_q_norm, mla_kv_norm, mla_w_uq,
           mla_w_ukv, mix_gain, w_out, norm2_g, ffn_w_gate, ffn_w_up, ffn_w_down, final_norm):
    batch, seq, d = x.shape
    depth = w_in.shape[0]
    tm = 512
    assert seq % tm == 0 and seq % C_CHUNK == 0

    prep = _prepare(norm1_g, w_in, mla_q_norm, mla_kv_norm, mla_w_uq, mla_w_ukv, w_out, norm2_g,
                    ffn_w_gate, ffn_w_up, ffn_w_down)
    mods = _mods(c, ada_w, ada_b).reshape(depth, batch, 6, d)
    tables = _rope_tables(positions)
    bias = _bias_tables(rel_bias)
    ret_consts = _retention_consts()
    final_g = final_norm[None, :]

    xf = x.reshape(batch * seq, d)
    for l in range(depth):
        a_qkv, bq, bk, bv, cqk, cv, cg = _inproj(xf, mods, l, prep, tables, batch, seq, tm)
        gain = mix_gain[l][None, :]
        mix_a = _dilated(a_qkv, bias, gain[:, :A_WIDTH], batch, seq)
        mix_b = _mla(bq, bk, bv, gain[:, A_WIDTH:A_WIDTH + B_WIDTH], batch, seq)
        mix_c = _retention(cqk, cv, cg, gain[:, A_WIDTH + B_WIDTH:], ret_consts, batch, seq)
        xf = _outffn(xf, mix_a, mix_b, mix_c, mods, l, prep, final_g, batch, seq, tm, final=(l == depth - 1))
    return xf.reshape(batch, seq, d)
```

```python
import functools

import numpy as np
import jax
import jax.numpy as jnp
from jax import lax
from jax.experimental import pallas as pl
from jax.experimental.pallas import tpu as pltpu

F32 = jnp.float32
BF16 = jnp.bfloat16

HEAD_DIM = 64
A_HEADS = 6
A_PATTERNS = ((128, 1), (512, 4), (2048, 16))
A_WIDTH = A_HEADS * HEAD_DIM
B_HEADS = 4
B_NOPE = 64
B_ROPE = 32
B_V = 64
B_Q_RANK = 256
B_KV_RANK = 128
B_WIDTH = B_HEADS * B_V
C_HEADS = 6
C_KEY = 32
C_VAL = 64
C_WIDTH = C_HEADS * C_VAL
C_QK = C_HEADS * C_KEY
C_CHUNK = 128
N_BUCKETS = 32
MAX_DISTANCE = 2048
ROPE_BASE = 10000.0
EPS = 1e-6

LANES = 128
WIN = 128
STEP = 4
NEG = -1e30
LOG2E = 1.4426950408889634
VMEM_LIMIT = 56 * 1024 * 1024

OFF_A = 0
OFF_BQ = 3 * A_WIDTH
OFF_BKV = OFF_BQ + B_Q_RANK
OFF_C = OFF_BKV + B_KV_RANK
C_QPAD = 2 * LANES
OFF_CV = OFF_C + 2 * C_QPAD
OFF_CG = OFF_CV + C_WIDTH
IN_COLS = OFF_CG + C_WIDTH


def _const_spec(shape, index_map):
    return pl.BlockSpec(shape, index_map, pipeline_mode=pl.Buffered(1))


def _silu(x):
    return x / (1.0 + jnp.exp(-x))


def _rms(x):
    return x * lax.rsqrt(jnp.mean(x * x, axis=-1, keepdims=True) + EPS)


def _mods_kernel(c_ref, w_ref, b_ref, o_ref):
    def split(a):
        hi = a.astype(BF16)
        return hi, (a - hi.astype(F32)).astype(BF16)

    c_hi, c_lo = split(_silu(c_ref[...]))
    w_hi, w_lo = split(w_ref[...])
    nb = c_hi.shape[0]
    both = jnp.dot(jnp.concatenate([c_hi, c_lo], axis=0), w_hi, preferred_element_type=F32)
    o_ref[...] = (both[:nb] + both[nb:] + jnp.dot(c_hi, w_lo, preferred_element_type=F32)) + b_ref[...]


def _mods(c, ada_w, ada_b):
    depth, d, n = ada_w.shape
    b = c.shape[0]
    tn = 1536
    return pl.pallas_call(
        _mods_kernel,
        grid=(depth, n // tn),
        in_specs=[
            pl.BlockSpec((b, d), lambda l, j: (0, 0)),
            pl.BlockSpec((None, d, tn), lambda l, j: (l, 0, j)),
            pl.BlockSpec((None, 1, tn), lambda l, j: (l, 0, j)),
        ],
        out_specs=pl.BlockSpec((None, b, tn), lambda l, j: (l, 0, j)),
        out_shape=jax.ShapeDtypeStruct((depth, b, n), F32),
        compiler_params=pltpu.CompilerParams(vmem_limit_bytes=VMEM_LIMIT),
        name="adaln_mods",
    )(c, ada_w, ada_b.reshape(depth, 1, n))


def _rope_kernel(pos_ref, f_ref, s_ref, cb_ref, sb_ref, cc_ref, sc_ref):
    p = pos_ref[...].astype(F32)
    ang = p * f_ref[1:2, :]
    cos = jnp.cos(ang)
    sin = jnp.sin(ang)
    rope_b = f_ref[0:1, :] > 0.0
    cb_ref[...] = jnp.where(rope_b, cos, 1.0)
    sb_ref[...] = jnp.where(rope_b, sin, 0.0) * s_ref[0:1, :]
    cc_ref[...] = cos
    sc_ref[...] = sin * s_ref[1:2, :]


def _rope_tables(positions):
    t = positions.size
    half = B_ROPE // 2
    inv_freq = (1.0 / (ROPE_BASE ** (np.arange(half, dtype=np.float32) / half))).astype(np.float32)
    zeros = np.zeros(half, np.float32)
    ones = np.ones(half, np.float32)
    f_b = np.concatenate([np.zeros(B_NOPE, np.float32), inv_freq, inv_freq, np.zeros(32, np.float32)])
    s_b = np.concatenate([np.zeros(B_NOPE, np.float32), -ones, ones, np.zeros(32, np.float32)])
    f_c = np.tile(np.concatenate([inv_freq, inv_freq]), LANES // C_KEY)
    s_c = np.tile(np.concatenate([-ones, ones]), LANES // C_KEY)
    del zeros
    freqs = jnp.asarray(np.stack([f_b, f_c]))
    signs = jnp.asarray(np.stack([s_b, s_c]))
    tr = 2048
    out = jax.ShapeDtypeStruct((t, LANES), F32)
    row = pl.BlockSpec((tr, LANES), lambda i: (i, 0))
    return pl.pallas_call(
        _rope_kernel,
        grid=(t // tr,),
        in_specs=[
            pl.BlockSpec((tr, 1), lambda i: (i, 0)),
            pl.BlockSpec((2, LANES), lambda i: (0, 0)),
            pl.BlockSpec((2, LANES), lambda i: (0, 0)),
        ],
        out_specs=[row, row, row, row],
        out_shape=[out, out, out, out],
        name="rope_tables",
    )(positions.reshape(t, 1), freqs, signs)


def _t5_bucket(dist):
    max_exact = N_BUCKETS // 2
    safe = np.maximum(dist, 1).astype(np.float32)
    large = max_exact + (np.log(safe / max_exact) / np.log(MAX_DISTANCE / max_exact)
                         * (N_BUCKETS - max_exact)).astype(np.int32)
    large = np.minimum(large, N_BUCKETS - 1)
    return np.where(dist < max_exact, dist, large).astype(np.int32)


def _bias_kernel(bmap_ref, rb_ref, o_ref):
    h = pl.program_id(1)
    bm = bmap_ref[...]
    t = jnp.full(bm.shape, NEG, F32)
    for b in range(N_BUCKETS):
        t = jnp.where(bm == b, rb_ref[b, h] * LOG2E, t)
    o_ref[...] = t


def _bias_tables(rel_bias):
    qi = np.arange(WIN)[None, :]
    c = np.arange(2 * WIN)[:, None]
    j = qi - c + WIN
    maps = []
    for (w, d) in A_PATTERNS:
        assert w // d == WIN
        bucket = _t5_bucket(np.arange(WIN + 1, dtype=np.int32) * d)
        maps.append(np.where((j >= 0) & (j <= WIN), bucket[np.clip(j, 0, WIN)], -1).astype(np.int32))
    bmap = jnp.asarray(np.stack(maps))
    npat = len(A_PATTERNS)
    return pl.pallas_call(
        _bias_kernel,
        grid=(npat, A_HEADS),
        in_specs=[
            pl.BlockSpec((None, 2 * WIN, WIN), lambda p, h: (p, 0, 0)),
            pl.BlockSpec(memory_space=pltpu.SMEM),
        ],
        out_specs=pl.BlockSpec((None, 2 * WIN, WIN), lambda p, h: (p, 0, h)),
        out_shape=jax.ShapeDtypeStruct((npat, 2 * WIN, A_HEADS * WIN), F32),
        name="t5_bias_tables",
    )(bmap, rel_bias)


def _rot_half(x, x1_mask):
    w = x.shape[-1]
    half = B_ROPE // 2
    return jnp.where(x1_mask, pltpu.roll(x, w - half, 1), pltpu.roll(x, half, 1))


def _inproj_kernel(x_ref, mod_ref, g_ref, wh_ref, wqk_ref, wvg_ref, qg_ref, kvg_ref, wuq_ref, wuk_ref, wuv_ref,
                   cb_ref, sb_ref, cc_ref, sc_ref, ks_ref,
                   a_ref, bq_ref, bk_ref, bv_ref, cqk_ref, cv_ref, cg_ref):
    x = x_ref[...]
    tm = x.shape[0]
    h = (_rms(x) * g_ref[...] * (1.0 + mod_ref[1:2, :]) + mod_ref[0:1, :]).astype(BF16)

    def proj(lo, hi):
        for ref, base in ((wvg_ref, OFF_CV), (wqk_ref, OFF_C), (wh_ref, 0)):
            if lo >= base:
                return jnp.dot(h, ref[:, lo - base:hi - base], preferred_element_type=F32)

    lane = lax.broadcasted_iota(jnp.int32, (tm, LANES), 1)
    b_x1 = (lane >= B_NOPE) & (lane < B_NOPE + B_ROPE // 2)
    c_x1 = (lane & (C_KEY // 2)) == 0
    cos_b = cb_ref[...]
    sin_b = sb_ref[...]

    q_lat = proj(OFF_BQ, OFF_BKV)
    kv_lat = proj(OFF_BKV, OFF_C)
    qk = proj(OFF_C, OFF_CV)
    qn = (_rms(q_lat) * qg_ref[...]).astype(BF16)
    kvn = (_rms(kv_lat) * kvg_ref[...]).astype(BF16)

    a_ref[:, 0:A_WIDTH] = (proj(OFF_A, OFF_A + A_WIDTH) * (HEAD_DIM ** -0.5 * LOG2E)).astype(BF16)
    a_ref[:, A_WIDTH:3 * A_WIDTH] = proj(OFF_A + A_WIDTH, OFF_BQ).astype(BF16)

    q = jnp.dot(qn, wuq_ref[...], preferred_element_type=F32)
    k_nope = jnp.dot(kvn, wuk_ref[...], preferred_element_type=F32)
    bv_ref[...] = jnp.dot(kvn, wuv_ref[...], preferred_element_type=F32).astype(BF16)
    cv_ref[...] = proj(OFF_CV, OFF_CG).astype(BF16)
    cg_ref[...] = _silu(proj(OFF_CG, IN_COLS)).astype(BF16)

    k_rope = jnp.where(lane >= B_NOPE, qk[:, LANES:2 * LANES], 0.0)
    k_pe = k_rope * cos_b + _rot_half(k_rope, b_x1) * sin_b
    b_scale = (B_NOPE + B_ROPE) ** -0.5 * LOG2E
    for hh in range(B_HEADS):
        sl = slice(hh * LANES, (hh + 1) * LANES)
        qh = q[:, sl]
        bq_ref[:, sl] = ((qh * cos_b + _rot_half(qh, b_x1) * sin_b) * b_scale).astype(BF16)
        bk_ref[:, sl] = (k_nope[:, sl] + k_pe).astype(BF16)

    cos_c = cc_ref[...]
    sin_c = sc_ref[...]
    for t in range(2 * C_QPAD // LANES):
        sl = slice(t * LANES, (t + 1) * LANES)
        xt = qk[:, sl]
        cqk_ref[:, sl] = ((xt * cos_c + _rot_half(xt, c_x1) * sin_c) * ks_ref[:, sl]).astype(BF16)


def _inproj(x, mods, layer, prep, tables, batch, seq, tm):
    t, d = x.shape
    nt = seq // tm
    row = lambda w: pl.BlockSpec((tm, w), lambda b, i: (b * nt + i, 0))
    wl = lambda a: _const_spec((None,) + a.shape[1:], lambda b, i: (layer,) + (0,) * (a.ndim - 1))
    cos_b, sin_b, cos_c, sin_c = tables
    kscale = jnp.asarray(np.concatenate([np.ones(C_QPAD, np.float32),
                                         np.full(C_QPAD, C_KEY ** -0.5, np.float32)])[None, :])
    outs = [(3 * A_WIDTH, BF16), (B_HEADS * LANES, BF16), (B_HEADS * LANES, BF16), (B_WIDTH, BF16),
            (2 * C_QPAD, BF16), (C_WIDTH, BF16), (C_WIDTH, BF16)]
    return pl.pallas_call(
        _inproj_kernel,
        grid=(batch, nt),
        in_specs=[
            row(d),
            pl.BlockSpec((None, None, 6, d), lambda b, i: (layer, b, 0, 0)),
            wl(prep["norm1_g"]), *[wl(w) for w in prep["w_in"]], wl(prep["q_norm"]), wl(prep["kv_norm"]),
            wl(prep["w_uq"]), wl(prep["w_uk"]), wl(prep["w_uv"]),
            row(LANES), row(LANES), row(LANES), row(LANES),
            _const_spec((1, 2 * C_QPAD), lambda b, i: (0, 0)),
        ],
        out_specs=[row(w) for w, _ in outs],
        out_shape=[jax.ShapeDtypeStruct((t, w), dt) for w, dt in outs],
        compiler_params=pltpu.CompilerParams(
            dimension_semantics=("arbitrary", "arbitrary"), vmem_limit_bytes=VMEM_LIMIT),
        name="inproj",
    )(x, mods, prep["norm1_g"], *prep["w_in"], prep["q_norm"], prep["kv_norm"],
      prep["w_uq"], prep["w_uk"], prep["w_uv"], cos_b, sin_b, cos_c, sin_c, kscale)


def _class_major_blocks(seq, level):
    n = seq // STEP
    perm = np.arange(seq)
    for _ in range(level):
        perm = perm.reshape(n, STEP).T.reshape(seq)
    blocks = perm.reshape(seq // WIN, WIN)
    stride = STEP ** level
    assert (blocks == blocks[:, :1] + stride * np.arange(WIN)).all()
    return [(int(b[0]), stride) for b in blocks]


def _dilated_kernel(q_ref, k_ref, v_ref, bias_ref, gain_ref, o_ref,
                    qf, kf, vf, qlo, qhi, kp, vt, *stage_refs, seq):
    npat = len(A_PATTERNS)
    nblk = seq // WIN
    s_sc, e_sc, stat_sc, out_s, lse_s = (stage_refs[i * npat:(i + 1) * npat] for i in range(5))
    lane = lax.broadcasted_iota(jnp.int32, (WIN, LANES), 1)
    lo_lane = lane < HEAD_DIM
    lo_row = lax.broadcasted_iota(jnp.int32, (LANES, WIN), 0) < HEAD_DIM

    qf[0] = q_ref[...].astype(F32)
    kf[0] = k_ref[...].astype(F32)
    vf[0] = v_ref[...].astype(F32)

    for p in range(npat):
        if p > 0:
            n = seq // STEP
            for r in range(STEP):
                dst = slice(r * n, (r + 1) * n)
                src = pl.ds(r, n, stride=STEP)
                for f in (qf, kf, vf):
                    f[p % 2, dst, :] = f[(p - 1) % 2, src, :]
        for c in range(0, seq, 4 * WIN):
            rows = slice(c, c + 4 * WIN)
            q = qf[p % 2, rows, :]
            lo_q = lax.broadcasted_iota(jnp.int32, q.shape, 1) < HEAD_DIM
            qlo[p, rows, :] = jnp.where(lo_q, q, 0.0).astype(BF16)
            qhi[p, rows, :] = jnp.where(lo_q, 0.0, q).astype(BF16)
            vt[p, :, rows] = vf[p % 2, rows, :].T.astype(BF16)
            if p > 0:
                kp[p - 1, rows, :] = kf[p % 2, rows, :].astype(BF16)

    def key_rows(p, blk):
        r0 = blk * WIN
        first = blk % (nblk // A_PATTERNS[p][1]) == 0
        return (r0 if first else r0 - WIN), r0 + WIN

    for p in range(npat - 1, -1, -1):
        for blk in range(nblk):
            k0, k1 = key_rows(p, blk)
            r0 = blk * WIN
            q2 = jnp.concatenate([qlo[p, r0:r0 + WIN, :], qhi[p, r0:r0 + WIN, :]], axis=0)
            kcat = k_ref[k0:k1, :] if p == 0 else kp[p - 1, k0:k1, :]
            bias = bias_ref[p, 2 * WIN - (k1 - k0):, :]
            s_sc[p][blk, 0:k1 - k0, :] = lax.dot_general(
                kcat, q2, (((1,), (1,)), ((), ())), preferred_element_type=F32) + bias
        for blk in range(nblk):
            k0, k1 = key_rows(p, blk)
            s = s_sc[p][blk, 0:k1 - k0, :]
            m = jnp.max(s, axis=0, keepdims=True)
            e = jnp.exp2(s - m)
            l = jnp.sum(e, axis=0, keepdims=True)
            e_sc[p][blk, 0:k1 - k0, :] = e.astype(BF16)
            stat_sc[p][blk, 0:1, :] = 1.0 / l
            stat_sc[p][blk, 1:2, :] = m + jnp.log2(l)
        for blk, (start, stride) in enumerate(_class_major_blocks(seq, p)):
            k0, k1 = key_rows(p, blk)
            acc = jnp.dot(vt[p, :, k0:k1], e_sc[p][blk, 0:k1 - k0, :], preferred_element_type=F32)
            out = acc * stat_sc[p][blk, 0:1, :]
            lse = stat_sc[p][blk, 1:2, :]
            out = jnp.where(lo_row, out[:, :WIN], out[:, WIN:])
            lse = jnp.where(lo_row, lse[:, :WIN], lse[:, WIN:])
            dst = pl.ds(start, WIN, stride=stride) if stride > 1 else slice(start, start + WIN)
            out_s[p][dst, :] = out.T
            lse_s[p][dst, :] = lse.T

    gain = gain_ref[...]
    for blk in range(nblk):
        sl = slice(blk * WIN, (blk + 1) * WIN)
        lses = [s[sl, :] for s in lse_s]
        mx = functools.reduce(jnp.maximum, lses)
        ws = [jnp.exp2(s - mx) for s in lses]
        o = sum(w * x[sl, :] for w, x in zip(ws, out_s)) / sum(ws)
        sq = o * o
        ms_lo = jnp.sum(jnp.where(lo_lane, sq, 0.0), axis=1, keepdims=True)
        ms_hi = jnp.sum(jnp.where(lo_lane, 0.0, sq), axis=1, keepdims=True)
        inv = lax.rsqrt(jnp.where(lo_lane, ms_lo, ms_hi) * (1.0 / HEAD_DIM) + EPS)
        o_ref[sl, :] = (o * inv * gain).astype(o_ref.dtype)


def _dilated(a_qkv, bias, gain_a, batch, seq):
    t = a_qkv.shape[0]
    pairs = A_WIDTH // LANES
    assert all(d == STEP ** p and seq % (d * WIN) == 0 for p, (_, d) in enumerate(A_PATTERNS))
    npat = bias.shape[0]
    col = lambda off: pl.BlockSpec((seq, LANES), lambda b, g: (b, off + g))
    scratch = ([pltpu.VMEM((2, seq, LANES), F32) for _ in range(3)]
               + [pltpu.VMEM((npat, seq, LANES), BF16) for _ in range(2)]
               + [pltpu.VMEM((npat - 1, seq, LANES), BF16)]
               + [pltpu.VMEM((npat, LANES, seq), BF16)]
               + [pltpu.VMEM((seq // WIN, 2 * WIN, 2 * WIN), F32) for _ in range(npat)]
               + [pltpu.VMEM((seq // WIN, 2 * WIN, 2 * WIN), BF16) for _ in range(npat)]
               + [pltpu.VMEM((seq // WIN, 2, 2 * WIN), F32) for _ in range(npat)]
               + [pltpu.VMEM((seq, LANES), F32) for _ in range(2 * npat)])
    return pl.pallas_call(
        functools.partial(_dilated_kernel, seq=seq),
        grid=(batch, pairs),
        in_specs=[
            col(0), col(pairs), col(2 * pairs),
            pl.BlockSpec((npat, 2 * WIN, 2 * WIN), lambda b, g: (0, 0, g)),
            pl.BlockSpec((1, LANES), lambda b, g: (0, g)),
        ],
        out_specs=pl.BlockSpec((seq, LANES), lambda b, g: (b, g)),
        out_shape=jax.ShapeDtypeStruct((t, A_WIDTH), BF16),
        scratch_shapes=scratch,
        compiler_params=pltpu.CompilerParams(
            dimension_semantics=("arbitrary", "arbitrary"), vmem_limit_bytes=VMEM_LIMIT),
        name="dilated_attention",
    )(a_qkv, a_qkv, a_qkv, bias, gain_a)


def _mla_kernel(q_ref, k_ref, v_ref, gain_ref, o_ref, vt, s_sc, e_sc, *, seq, tq):
    key_idx = lax.broadcasted_iota(jnp.int32, (tq, tq), 0)
    qry_idx = lax.broadcasted_iota(jnp.int32, (tq, tq), 1)
    causal = key_idx <= qry_idx
    gain = gain_ref[...]
    for c in range(0, seq, tq):
        vt[:, c:c + tq] = v_ref[c:c + tq, :].astype(F32).T.astype(BF16)

    def head(hh, i, base):
        hs = slice(hh * LANES, (hh + 1) * LANES)
        q = q_ref[i * tq:(i + 1) * tq, hs]
        m = None
        for j in range(i + 1):
            s = lax.dot_general(k_ref[j * tq:(j + 1) * tq, hs], q, (((1,), (1,)), ((), ())),
                                preferred_element_type=F32)
            if j == i:
                s = jnp.where(causal, s, NEG)
            s_sc[base + j] = s
            bm = jnp.max(s, axis=0, keepdims=True)
            m = bm if m is None else jnp.maximum(m, bm)
        l = None
        for j in range(i + 1):
            e = jnp.exp2(s_sc[base + j] - m)
            ls = jnp.sum(e, axis=0, keepdims=True)
            l = ls if l is None else l + ls
            e_sc[(base + j) * tq:(base + j + 1) * tq, :] = e.astype(BF16)
        acc = jnp.dot(vt[hh * B_V:(hh + 1) * B_V, 0:(i + 1) * tq], e_sc[base * tq:(base + i + 1) * tq, :],
                      preferred_element_type=F32)
        return acc / l

    base = 0
    for i in range(seq // tq):
        normed = []
        for hh in range(2):
            o = head(hh, i, base)
            base += i + 1
            normed.append(o * lax.rsqrt(jnp.mean(o * o, axis=0, keepdims=True) + EPS))
        o_ref[i * tq:(i + 1) * tq, :] = (jnp.concatenate(normed, axis=0).T * gain).astype(o_ref.dtype)


def _mla(bq, bk, bv, gain_b, batch, seq):
    t = bq.shape[0]
    pairs = B_HEADS // 2
    tq = 512
    assert seq % tq == 0
    nq = seq // tq
    nslots = 2 * (nq * (nq + 1) // 2)
    return pl.pallas_call(
        functools.partial(_mla_kernel, seq=seq, tq=tq),
        grid=(batch, pairs),
        in_specs=[
            pl.BlockSpec((seq, 2 * LANES), lambda b, g: (b, g)),
            pl.BlockSpec((seq, 2 * LANES), lambda b, g: (b, g)),
            pl.BlockSpec((seq, LANES), lambda b, g: (b, g)),
            pl.BlockSpec((1, LANES), lambda b, g: (0, g)),
        ],
        out_specs=pl.BlockSpec((seq, LANES), lambda b, g: (b, g)),
        out_shape=jax.ShapeDtypeStruct((t, B_WIDTH), BF16),
        scratch_shapes=[pltpu.VMEM((LANES, seq), BF16),
                        pltpu.VMEM((nslots, tq, tq), F32), pltpu.VMEM((nslots * tq, tq), BF16)],
        compiler_params=pltpu.CompilerParams(
            dimension_semantics=("arbitrary", "arbitrary"), vmem_limit_bytes=VMEM_LIMIT),
        name="latent_attention",
    )(bq, bk, bv, gain_b)


def _retention_consts():
    h = C_HEADS
    log_g = np.log(1.0 - 2.0 ** (-5.0 - np.arange(h))).astype(np.float32)
    i = np.arange(C_CHUNK, dtype=np.float32)
    rel = i[:, None] - i[None, :]
    decay = (np.exp(np.maximum(rel, 0.0)[None] * log_g[:, None, None]) * (rel >= 0)[None]).astype(np.float32)
    xi = np.exp((i + 1.0)[None, :] * log_g[:, None]).astype(np.float32)
    zeta = np.exp((C_CHUNK - 1.0 - i)[None, :] * log_g[:, None]).astype(np.float32)
    chunk_decay = np.exp(C_CHUNK * log_g).astype(np.float32)
    decay_all = np.concatenate(list(decay), axis=1)
    xi_mat = np.repeat(xi.T, C_VAL, axis=1)
    zeta_t = np.zeros((C_QPAD, C_CHUNK), np.float32)
    zeta_t[:C_QK] = np.repeat(zeta, C_KEY, axis=0)
    cd = np.repeat(chunk_decay, C_VAL)[None, :]
    bd = np.zeros((C_QPAD, C_WIDTH), np.float32)
    kmask = np.zeros((C_QPAD, h * C_CHUNK), np.float32)
    vmask = np.zeros((h * C_CHUNK, C_WIDTH), np.float32)
    for hh in range(h):
        bd[hh * C_KEY:(hh + 1) * C_KEY, hh * C_VAL:(hh + 1) * C_VAL] = 1.0
        kmask[hh * C_KEY:(hh + 1) * C_KEY, hh * C_CHUNK:(hh + 1) * C_CHUNK] = 1.0
        vmask[hh * C_CHUNK:(hh + 1) * C_CHUNK, hh * C_VAL:(hh + 1) * C_VAL] = 1.0
    f32s = tuple(jnp.asarray(a) for a in (decay_all, xi_mat, zeta_t, cd, bd))
    return f32s + (jnp.asarray(kmask, BF16), jnp.asarray(vmask, BF16))


def _retention_kernel(qk_ref, v_ref, g_ref, decay_ref, xi_ref, zeta_ref, cd_ref, bd_ref, kmask_ref,
                      vmask_ref, gain_ref, o_ref, *, seq):
    lane = lax.broadcasted_iota(jnp.int32, (C_CHUNK, LANES), 1)
    lo = lane < C_VAL
    gain = gain_ref[...]
    state = jnp.zeros((C_QPAD, C_WIDTH), F32)

    for n in range(seq // C_CHUNK):
        rows = slice(n * C_CHUNK, (n + 1) * C_CHUNK)
        q = qk_ref[rows, 0:C_QPAD]
        k_t = qk_ref[rows, C_QPAD:2 * C_QPAD].astype(F32).T
        v = v_ref[rows, :]
        k_bd = jnp.concatenate([k_t.astype(BF16)] * C_HEADS, axis=1) * kmask_ref[...]
        s = jnp.dot(q, k_bd, preferred_element_type=F32) * decay_ref[...]
        v_bd = jnp.concatenate([v] * C_HEADS, axis=0) * vmask_ref[...]
        o = (jnp.dot(s.astype(BF16), v_bd, preferred_element_type=F32)
             + jnp.dot(q, state.astype(BF16), preferred_element_type=F32) * xi_ref[...])
        upd = jnp.dot((k_t * zeta_ref[...]).astype(BF16), v, preferred_element_type=F32)
        state = state * cd_ref[...] + upd * bd_ref[...]
        for t in range(C_WIDTH // LANES):
            sl = slice(t * LANES, (t + 1) * LANES)
            x = o[:, sl]
            mu_lo = jnp.sum(jnp.where(lo, x, 0.0), axis=1, keepdims=True)
            mu_hi = jnp.sum(jnp.where(lo, 0.0, x), axis=1, keepdims=True)
            dlt = x - jnp.where(lo, mu_lo, mu_hi) * (1.0 / C_VAL)
            sq = dlt * dlt
            var_lo = jnp.sum(jnp.where(lo, sq, 0.0), axis=1, keepdims=True)
            var_hi = jnp.sum(jnp.where(lo, 0.0, sq), axis=1, keepdims=True)
            y = dlt * lax.rsqrt(jnp.where(lo, var_lo, var_hi) * (1.0 / C_VAL) + EPS)
            o_ref[rows, sl] = (y * g_ref[rows, sl].astype(F32) * gain[:, sl]).astype(o_ref.dtype)


def _retention(cqk, cv, cg, gain_c, consts, batch, seq):
    t = cqk.shape[0]
    tok = lambda width: pl.BlockSpec((seq, width), lambda b: (b, 0))
    full = lambda a: _const_spec(a.shape, lambda b: (0,) * a.ndim)
    return pl.pallas_call(
        functools.partial(_retention_kernel, seq=seq),
        grid=(batch,),
        in_specs=[tok(2 * C_QPAD), tok(C_WIDTH), tok(C_WIDTH)] + [full(a) for a in consts] + [full(gain_c)],
        out_specs=tok(C_WIDTH),
        out_shape=jax.ShapeDtypeStruct((t, C_WIDTH), BF16),
        compiler_params=pltpu.CompilerParams(
            dimension_semantics=("arbitrary",), vmem_limit_bytes=VMEM_LIMIT),
        name="retention",
    )(cqk, cv, cg, *consts, gain_c)


def _outffn_kernel(x_ref, ma_ref, mb_ref, mc_ref, mod_ref, wo_ref, g_ref, wg_ref, wu_ref, wd_ref, fg_ref,
                   o_ref, mix_ref, hid_ref, *, final, fchunk):
    mix_ref[:, 0:A_WIDTH] = ma_ref[...]
    mix_ref[:, A_WIDTH:A_WIDTH + B_WIDTH] = mb_ref[...]
    mix_ref[:, A_WIDTH + B_WIDTH:] = mc_ref[...]
    att = jnp.dot(mix_ref[...], wo_ref[...], preferred_element_type=F32)
    x = x_ref[...] + mod_ref[2:3, :] * att
    h = (_rms(x) * g_ref[...] * (1.0 + mod_ref[4:5, :]) + mod_ref[3:4, :]).astype(BF16)
    hidden = wg_ref.shape[1]
    for j in range(hidden // fchunk):
        sl = slice(j * fchunk, (j + 1) * fchunk)
        gate = jnp.dot(h, wg_ref[:, sl], preferred_element_type=F32)
        up = jnp.dot(h, wu_ref[:, sl], preferred_element_type=F32)
        hid_ref[:, sl] = (_silu(gate) * up).astype(BF16)
    ffn = jnp.dot(hid_ref[...], wd_ref[...], preferred_element_type=F32)
    y = x + mod_ref[5:6, :] * ffn
    if final:
        y = _rms(y) * fg_ref[...]
    o_ref[...] = y


def _outffn(x, mix_a, mix_b, mix_c, mods, layer, prep, final_g, batch, seq, tm, final):
    t, d = x.shape
    nt = seq // tm
    hidden = prep["w_gate"].shape[2]
    row = lambda w: pl.BlockSpec((tm, w), lambda b, i: (b * nt + i, 0))
    wl = lambda a: _const_spec((None,) + a.shape[1:], lambda b, i: (layer,) + (0,) * (a.ndim - 1))
    return pl.pallas_call(
        functools.partial(_outffn_kernel, final=final, fchunk=256),
        grid=(batch, nt),
        in_specs=[
            row(d), row(A_WIDTH), row(B_WIDTH), row(C_WIDTH),
            pl.BlockSpec((None, None, 6, d), lambda b, i: (layer, b, 0, 0)),
            wl(prep["w_out"]), wl(prep["norm2_g"]), wl(prep["w_gate"]), wl(prep["w_up"]), wl(prep["w_down"]),
            _const_spec((1, d), lambda b, i: (0, 0)),
        ],
        out_specs=row(d),
        out_shape=jax.ShapeDtypeStruct((t, d), F32),
        scratch_shapes=[pltpu.VMEM((tm, d), BF16), pltpu.VMEM((tm, hidden), BF16)],
        compiler_params=pltpu.CompilerParams(
            dimension_semantics=("arbitrary", "arbitrary"), vmem_limit_bytes=VMEM_LIMIT),
        name="outproj_ffn",
    )(x, mix_a, mix_b, mix_c, mods, prep["w_out"], prep["norm2_g"], prep["w_gate"], prep["w_up"],
      prep["w_down"], final_g)


def _prepare(norm1_g, w_in, mla_q_norm, mla_kv_norm, mla_w_uq, mla_w_ukv, w_out, norm2_g,
             ffn_w_gate, ffn_w_up, ffn_w_down):
    depth, d, _ = w_in.shape
    b_lat0 = 3 * A_WIDTH
    rope0 = b_lat0 + B_Q_RANK + B_KV_RANK
    c0 = rope0 + B_ROPE
    pad = lambda n: jnp.zeros((depth, d, n), BF16)
    cast = lambda lo, hi: w_in[:, :, lo:hi].astype(BF16)
    w_head = cast(0, rope0)
    w_cqk = jnp.concatenate([
        cast(c0, c0 + C_QK), cast(rope0, c0), pad(C_QPAD - C_QK - B_ROPE),
        cast(c0 + C_QK, c0 + 2 * C_QK), pad(C_QPAD - C_QK),
    ], axis=2)
    w_cvg = cast(c0 + 2 * C_QK, w_in.shape[2])
    assert C_QK + B_ROPE <= C_QPAD and C_QK % LANES == B_NOPE
    assert (w_head.shape[2], w_cqk.shape[2], w_cvg.shape[2]) == (OFF_C, OFF_CV - OFF_C, IN_COLS - OFF_CV)
    uq = mla_w_uq.reshape(depth, B_Q_RANK, B_HEADS, B_NOPE + B_ROPE)
    uq = jnp.pad(uq, ((0, 0), (0, 0), (0, 0), (0, LANES - B_NOPE - B_ROPE)))
    ukv = mla_w_ukv.reshape(depth, B_KV_RANK, B_HEADS, B_NOPE + B_V)
    uk = jnp.pad(ukv[..., :B_NOPE], ((0, 0), (0, 0), (0, 0), (0, LANES - B_NOPE)))
    uv = ukv[..., B_NOPE:]
    return {
        "norm1_g": norm1_g[:, None, :],
        "w_in": (w_head, w_cqk, w_cvg),
        "q_norm": mla_q_norm[:, None, :],
        "kv_norm": mla_kv_norm[:, None, :],
        "w_uq": uq.reshape(depth, B_Q_RANK, B_HEADS * LANES).astype(BF16),
        "w_uk": uk.reshape(depth, B_KV_RANK, B_HEADS * LANES).astype(BF16),
        "w_uv": uv.reshape(depth, B_KV_RANK, B_WIDTH).astype(BF16),
        "w_out": w_out.astype(BF16),
        "norm2_g": norm2_g[:, None, :],
        "w_gate": ffn_w_gate.astype(BF16),
        "w_up": ffn_w_up.astype(BF16),
        "w_down": ffn_w_down.astype(BF16),
    }


def kernel(x, c, positions, rel_bias, ada_w, ada_b, norm1_g, w_in, mla_q_norm, mla_kv_norm, mla_w_uq,
           mla_w_ukv, mix_gain, w_out, norm2_g, ffn_w_gate, ffn_w_up, ffn_w_down, final_norm):
    batch, seq, d = x.shape
    depth = w_in.shape[0]
    tm = 512
    assert seq % tm == 0 and seq % C_CHUNK == 0

    prep = _prepare(norm1_g, w_in, mla_q_norm, mla_kv_norm, mla_w_uq, mla_w_ukv, w_out, norm2_g,
                    ffn_w_gate, ffn_w_up, ffn_w_down)
    mods = _mods(c, ada_w, ada_b).reshape(depth, batch, 6, d)
    tables = _rope_tables(positions)
    bias = _bias_tables(rel_bias)
    ret_consts = _retention_consts()
    final_g = final_norm[None, :]

    xf = x.reshape(batch * seq, d)
    for l in range(depth):
        a_qkv, bq, bk, bv, cqk, cv, cg = _inproj(xf, mods, l, prep, tables, batch, seq, tm)
        gain = mix_gain[l][None, :]
        mix_a = _dilated(a_qkv, bias, gain[:, :A_WIDTH], batch, seq)
        mix_b = _mla(bq, bk, bv, gain[:, A_WIDTH:A_WIDTH + B_WIDTH], batch, seq)
        mix_c = _retention(cqk, cv, cg, gain[:, A_WIDTH + B_WIDTH:], ret_consts, batch, seq)
        xf = _outffn(xf, mix_a, mix_b, mix_c, mods, l, prep, final_g, batch, seq, tm, final=(l == depth - 1))
    return xf.reshape(batch, seq, d)
```

```python
import functools

import numpy as np
import jax
import jax.numpy as jnp
from jax import lax
from jax.experimental import pallas as pl
from jax.experimental.pallas import tpu as pltpu

F32 = jnp.float32
BF16 = jnp.bfloat16

HEAD_DIM = 64
A_HEADS = 6
A_PATTERNS = ((128, 1), (512, 4), (2048, 16))
A_WIDTH = A_HEADS * HEAD_DIM
B_HEADS = 4
B_NOPE = 64
B_ROPE = 32
B_V = 64
B_Q_RANK = 256
B_KV_RANK = 128
B_WIDTH = B_HEADS * B_V
C_HEADS = 6
C_KEY = 32
C_VAL = 64
C_WIDTH = C_HEADS * C_VAL
C_QK = C_HEADS * C_KEY
C_CHUNK = 128
N_BUCKETS = 32
MAX_DISTANCE = 2048
ROPE_BASE = 10000.0
EPS = 1e-6

LANES = 128
WIN = 128
STEP = 4
NEG = -1e30
LOG2E = 1.4426950408889634
VMEM_LIMIT = 56 * 1024 * 1024

OFF_A = 0
OFF_BQ = 3 * A_WIDTH
OFF_BKV = OFF_BQ + B_Q_RANK
OFF_C = OFF_BKV + B_KV_RANK
C_QPAD = 2 * LANES
OFF_CV = OFF_C + 2 * C_QPAD
OFF_CG = OFF_CV + C_WIDTH
IN_COLS = OFF_CG + C_WIDTH


def _const_spec(shape, index_map):
    return pl.BlockSpec(shape, index_map, pipeline_mode=pl.Buffered(1))


def _silu(x):
    return x / (1.0 + jnp.exp(-x))


def _rms(x):
    return x * lax.rsqrt(jnp.mean(x * x, axis=-1, keepdims=True) + EPS)


def _mods_kernel(c_ref, w_ref, b_ref, o_ref):
    def split(a):
        hi = a.astype(BF16)
        return hi, (a - hi.astype(F32)).astype(BF16)

    c_hi, c_lo = split(_silu(c_ref[...]))
    w_hi, w_lo = split(w_ref[...])
    nb = c_hi.shape[0]
    both = jnp.dot(jnp.concatenate([c_hi, c_lo], axis=0), w_hi, preferred_element_type=F32)
    o_ref[...] = (both[:nb] + both[nb:] + jnp.dot(c_hi, w_lo, preferred_element_type=F32)) + b_ref[...]


def _mods(c, ada_w, ada_b):
    depth, d, n = ada_w.shape
    b = c.shape[0]
    tn = 1536
    return pl.pallas_call(
        _mods_kernel,
        grid=(depth, n // tn),
        in_specs=[
            pl.BlockSpec((b, d), lambda l, j: (0, 0)),
            pl.BlockSpec((None, d, tn), lambda l, j: (l, 0, j)),
            pl.BlockSpec((None, 1, tn), lambda l, j: (l, 0, j)),
        ],
        out_specs=pl.BlockSpec((None, b, tn), lambda l, j: (l, 0, j)),
        out_shape=jax.ShapeDtypeStruct((depth, b, n), F32),
        compiler_params=pltpu.CompilerParams(vmem_limit_bytes=VMEM_LIMIT),
        name="adaln_mods",
    )(c, ada_w, ada_b.reshape(depth, 1, n))


def _rope_kernel(pos_ref, f_ref, s_ref, cb_ref, sb_ref, cc_ref, sc_ref):
    p = pos_ref[...].astype(F32)
    ang = p * f_ref[1:2, :]
    cos = jnp.cos(ang)
    sin = jnp.sin(ang)
    rope_b = f_ref[0:1, :] > 0.0
    cb_ref[...] = jnp.where(rope_b, cos, 1.0)
    sb_ref[...] = jnp.where(rope_b, sin, 0.0) * s_ref[0:1, :]
    cc_ref[...] = cos
    sc_ref[...] = sin * s_ref[1:2, :]


def _rope_tables(positions):
    t = positions.size
    half = B_ROPE // 2
    inv_freq = (1.0 / (ROPE_BASE ** (np.arange(half, dtype=np.float32) / half))).astype(np.float32)
    zeros = np.zeros(half, np.float32)
    ones = np.ones(half, np.float32)
    f_b = np.concatenate([np.zeros(B_NOPE, np.float32), inv_freq, inv_freq, np.zeros(32, np.float32)])
    s_b = np.concatenate([np.zeros(B_NOPE, np.float32), -ones, ones, np.zeros(32, np.float32)])
    f_c = np.tile(np.concatenate([inv_freq, inv_freq]), LANES // C_KEY)
    s_c = np.tile(np.concatenate([-ones, ones]), LANES // C_KEY)
    del zeros
    freqs = jnp.asarray(np.stack([f_b, f_c]))
    signs = jnp.asarray(np.stack([s_b, s_c]))
    tr = 2048
    out = jax.ShapeDtypeStruct((t, LANES), F32)
    row = pl.BlockSpec((tr, LANES), lambda i: (i, 0))
    return pl.pallas_call(
        _rope_kernel,
        grid=(t // tr,),
        in_specs=[
            pl.BlockSpec((tr, 1), lambda i: (i, 0)),
            pl.BlockSpec((2, LANES), lambda i: (0, 0)),
            pl.BlockSpec((2, LANES), lambda i: (0, 0)),
        ],
        out_specs=[row, row, row, row],
        out_shape=[out, out, out, out],
        name="rope_tables",
    )(positions.reshape(t, 1), freqs, signs)


def _t5_bucket(dist):
    max_exact = N_BUCKETS // 2
    safe = np.maximum(dist, 1).astype(np.float32)
    large = max_exact + (np.log(safe / max_exact) / np.log(MAX_DISTANCE / max_exact)
                         * (N_BUCKETS - max_exact)).astype(np.int32)
    large = np.minimum(large, N_BUCKETS - 1)
    return np.where(dist < max_exact, dist, large).astype(np.int32)


def _bias_kernel(bmap_ref, rb_ref, o_ref):
    h = pl.program_id(1)
    bm = bmap_ref[...]
    t = jnp.full(bm.shape, NEG, F32)
    for b in range(N_BUCKETS):
        t = jnp.where(bm == b, rb_ref[b, h] * LOG2E, t)
    o_ref[...] = t


def _bias_tables(rel_bias):
    qi = np.arange(WIN)[None, :]
    c = np.arange(2 * WIN)[:, None]
    j = qi - c + WIN
    maps = []
    for (w, d) in A_PATTERNS:
        assert w // d == WIN
        bucket = _t5_bucket(np.arange(WIN + 1, dtype=np.int32) * d)
        maps.append(np.where((j >= 0) & (j <= WIN), bucket[np.clip(j, 0, WIN)], -1).astype(np.int32))
    bmap = jnp.asarray(np.stack(maps))
    npat = len(A_PATTERNS)
    return pl.pallas_call(
        _bias_kernel,
        grid=(npat, A_HEADS),
        in_specs=[
            pl.BlockSpec((None, 2 * WIN, WIN), lambda p, h: (p, 0, 0)),
            pl.BlockSpec(memory_space=pltpu.SMEM),
        ],
        out_specs=pl.BlockSpec((None, 2 * WIN, WIN), lambda p, h: (p, 0, h)),
        out_shape=jax.ShapeDtypeStruct((npat, 2 * WIN, A_HEADS * WIN), F32),
        name="t5_bias_tables",
    )(bmap, rel_bias)


def _rot_half(x, x1_mask):
    w = x.shape[-1]
    half = B_ROPE // 2
    return jnp.where(x1_mask, pltpu.roll(x, w - half, 1), pltpu.roll(x, half, 1))


def _inproj_kernel(x_ref, mod_ref, g_ref, win_ref, qg_ref, kvg_ref, wuq_ref, wuk_ref, wuv_ref,
                   cb_ref, sb_ref, cc_ref, sc_ref, ks_ref,
                   a_ref, bq_ref, bk_ref, bv_ref, cqk_ref, cv_ref, cg_ref, w_ref):
    @pl.when((pl.program_id(0) == 0) & (pl.program_id(1) == 0))
    def _():
        rope0 = OFF_C
        c0 = rope0 + B_ROPE
        chunk = 256
        for r0 in range(0, win_ref.shape[0], chunk):
            rs = slice(r0, r0 + chunk)
            cast = lambda lo, hi: win_ref[rs, lo:hi].astype(BF16)
            zeros = lambda n: jnp.zeros((chunk, n), BF16)
            w_ref[rs, 0:OFF_C] = cast(0, rope0)
            qcol = OFF_C + C_QK
            w_ref[rs, OFF_C:qcol] = cast(c0, c0 + C_QK)
            w_ref[rs, qcol:qcol + B_ROPE] = cast(rope0, c0)
            w_ref[rs, qcol + B_ROPE:OFF_C + C_QPAD] = zeros(C_QPAD - C_QK - B_ROPE)
            kcol = OFF_C + C_QPAD
            w_ref[rs, kcol:kcol + C_QK] = cast(c0 + C_QK, c0 + 2 * C_QK)
            w_ref[rs, kcol + C_QK:OFF_CV] = zeros(C_QPAD - C_QK)
            w_ref[rs, OFF_CV:IN_COLS] = cast(c0 + 2 * C_QK, win_ref.shape[1])

    x = x_ref[...]
    tm = x.shape[0]
    h = (_rms(x) * g_ref[...] * (1.0 + mod_ref[1:2, :]) + mod_ref[0:1, :]).astype(BF16)

    def proj(lo, hi):
        return jnp.dot(h, w_ref[:, lo:hi], preferred_element_type=F32)

    lane = lax.broadcasted_iota(jnp.int32, (tm, LANES), 1)
    b_x1 = (lane >= B_NOPE) & (lane < B_NOPE + B_ROPE // 2)
    c_x1 = (lane & (C_KEY // 2)) == 0
    cos_b = cb_ref[...]
    sin_b = sb_ref[...]

    q_lat = proj(OFF_BQ, OFF_BKV)
    kv_lat = proj(OFF_BKV, OFF_C)
    qk = proj(OFF_C, OFF_CV)
    qn = (_rms(q_lat) * qg_ref[...]).astype(BF16)
    kvn = (_rms(kv_lat) * kvg_ref[...]).astype(BF16)

    a_ref[:, 0:A_WIDTH] = (proj(OFF_A, OFF_A + A_WIDTH) * (HEAD_DIM ** -0.5 * LOG2E)).astype(BF16)
    a_ref[:, A_WIDTH:3 * A_WIDTH] = proj(OFF_A + A_WIDTH, OFF_BQ).astype(BF16)

    q = jnp.dot(qn, wuq_ref[...], preferred_element_type=F32)
    k_nope = jnp.dot(kvn, wuk_ref[...], preferred_element_type=F32)
    bv_ref[...] = jnp.dot(kvn, wuv_ref[...], preferred_element_type=F32).astype(BF16)
    cv_ref[...] = proj(OFF_CV, OFF_CG).astype(BF16)
    cg_ref[...] = _silu(proj(OFF_CG, IN_COLS)).astype(BF16)

    k_rope = jnp.where(lane >= B_NOPE, qk[:, LANES:2 * LANES], 0.0)
    k_pe = k_rope * cos_b + _rot_half(k_rope, b_x1) * sin_b
    b_scale = (B_NOPE + B_ROPE) ** -0.5 * LOG2E
    for hh in range(B_HEADS):
        sl = slice(hh * LANES, (hh + 1) * LANES)
        qh = q[:, sl]
        bq_ref[:, sl] = ((qh * cos_b + _rot_half(qh, b_x1) * sin_b) * b_scale).astype(BF16)
        bk_ref[:, sl] = (k_nope[:, sl] + k_pe).astype(BF16)

    cos_c = cc_ref[...]
    sin_c = sc_ref[...]
    for t in range(2 * C_QPAD // LANES):
        sl = slice(t * LANES, (t + 1) * LANES)
        xt = qk[:, sl]
        cqk_ref[:, sl] = ((xt * cos_c + _rot_half(xt, c_x1) * sin_c) * ks_ref[:, sl]).astype(BF16)


def _inproj(x, mods, layer, prep, tables, batch, seq, tm):
    t, d = x.shape
    nt = seq // tm
    row = lambda w: pl.BlockSpec((tm, w), lambda b, i: (b * nt + i, 0))
    wl = lambda a: _const_spec((None,) + a.shape[1:], lambda b, i: (layer,) + (0,) * (a.ndim - 1))
    cos_b, sin_b, cos_c, sin_c = tables
    kscale = jnp.asarray(np.concatenate([np.ones(C_QPAD, np.float32),
                                         np.full(C_QPAD, C_KEY ** -0.5, np.float32)])[None, :])
    outs = [(3 * A_WIDTH, BF16), (B_HEADS * LANES, BF16), (B_HEADS * LANES, BF16), (B_WIDTH, BF16),
            (2 * C_QPAD, BF16), (C_WIDTH, BF16), (C_WIDTH, BF16)]
    return pl.pallas_call(
        _inproj_kernel,
        grid=(batch, nt),
        in_specs=[
            row(d),
            pl.BlockSpec((None, None, 6, d), lambda b, i: (layer, b, 0, 0)),
            wl(prep["norm1_g"]), wl(prep["w_in"]), wl(prep["q_norm"]), wl(prep["kv_norm"]),
            wl(prep["w_uq"]), wl(prep["w_uk"]), wl(prep["w_uv"]),
            row(LANES), row(LANES), row(LANES), row(LANES),
            _const_spec((1, 2 * C_QPAD), lambda b, i: (0, 0)),
        ],
        out_specs=[row(w) for w, _ in outs],
        out_shape=[jax.ShapeDtypeStruct((t, w), dt) for w, dt in outs],
        scratch_shapes=[pltpu.VMEM((d, IN_COLS), BF16)],
        compiler_params=pltpu.CompilerParams(
            dimension_semantics=("arbitrary", "arbitrary"), vmem_limit_bytes=VMEM_LIMIT),
        name="inproj",
    )(x, mods, prep["norm1_g"], prep["w_in"], prep["q_norm"], prep["kv_norm"],
      prep["w_uq"], prep["w_uk"], prep["w_uv"], cos_b, sin_b, cos_c, sin_c, kscale)


def _class_major_blocks(seq, level):
    n = seq // STEP
    perm = np.arange(seq)
    for _ in range(level):
        perm = perm.reshape(n, STEP).T.reshape(seq)
    blocks = perm.reshape(seq // WIN, WIN)
    stride = STEP ** level
    assert (blocks == blocks[:, :1] + stride * np.arange(WIN)).all()
    return [(int(b[0]), stride) for b in blocks]


def _dilated_kernel(q_ref, k_ref, v_ref, bias_ref, gain_ref, o_ref,
                    qf, kf, vf, qlo, qhi, kp, vt, *stage_refs, seq):
    npat = len(A_PATTERNS)
    nblk = seq // WIN
    s_sc, e_sc, stat_sc, out_s, lse_s = (stage_refs[i * npat:(i + 1) * npat] for i in range(5))
    lane = lax.broadcasted_iota(jnp.int32, (WIN, LANES), 1)
    lo_lane = lane < HEAD_DIM
    lo_row = lax.broadcasted_iota(jnp.int32, (LANES, WIN), 0) < HEAD_DIM

    qf[0] = q_ref[...].astype(F32)
    kf[0] = k_ref[...].astype(F32)
    vf[0] = v_ref[...].astype(F32)

    for p in range(npat):
        if p > 0:
            n = seq // STEP
            for r in range(STEP):
                dst = slice(r * n, (r + 1) * n)
                src = pl.ds(r, n, stride=STEP)
                for f in (qf, kf, vf):
                    f[p % 2, dst, :] = f[(p - 1) % 2, src, :]
        for c in range(0, seq, 4 * WIN):
            rows = slice(c, c + 4 * WIN)
            q = qf[p % 2, rows, :]
            lo_q = lax.broadcasted_iota(jnp.int32, q.shape, 1) < HEAD_DIM
            qlo[p, rows, :] = jnp.where(lo_q, q, 0.0).astype(BF16)
            qhi[p, rows, :] = jnp.where(lo_q, 0.0, q).astype(BF16)
            vt[p, :, rows] = vf[p % 2, rows, :].T.astype(BF16)
            if p > 0:
                kp[p - 1, rows, :] = kf[p % 2, rows, :].astype(BF16)

    def key_rows(p, blk):
        r0 = blk * WIN
        first = blk % (nblk // A_PATTERNS[p][1]) == 0
        return (r0 if first else r0 - WIN), r0 + WIN

    for p in range(npat - 1, -1, -1):
        for blk in range(nblk):
            k0, k1 = key_rows(p, blk)
            r0 = blk * WIN
            q2 = jnp.concatenate([qlo[p, r0:r0 + WIN, :], qhi[p, r0:r0 + WIN, :]], axis=0)
            kcat = k_ref[k0:k1, :] if p == 0 else kp[p - 1, k0:k1, :]
            bias = bias_ref[p, 2 * WIN - (k1 - k0):, :]
            s_sc[p][blk, 0:k1 - k0, :] = lax.dot_general(
                kcat, q2, (((1,), (1,)), ((), ())), preferred_element_type=F32) + bias
        for blk in range(nblk):
            k0, k1 = key_rows(p, blk)
            s = s_sc[p][blk, 0:k1 - k0, :]
            m = jnp.max(s, axis=0, keepdims=True)
            e = jnp.exp2(s - m)
            l = jnp.sum(e, axis=0, keepdims=True)
            e_sc[p][blk, 0:k1 - k0, :] = e.astype(BF16)
            stat_sc[p][blk, 0:1, :] = 1.0 / l
            stat_sc[p][blk, 1:2, :] = m + jnp.log2(l)
        for blk, (start, stride) in enumerate(_class_major_blocks(seq, p)):
            k0, k1 = key_rows(p, blk)
            acc = jnp.dot(vt[p, :, k0:k1], e_sc[p][blk, 0:k1 - k0, :], preferred_element_type=F32)
            out = acc * stat_sc[p][blk, 0:1, :]
            lse = stat_sc[p][blk, 1:2, :]
            out = jnp.where(lo_row, out[:, :WIN], out[:, WIN:])
            lse = jnp.where(lo_row, lse[:, :WIN], lse[:, WIN:])
            dst = pl.ds(start, WIN, stride=stride) if stride > 1 else slice(start, start + WIN)
            out_s[p][dst, :] = out.T
            lse_s[p][dst, :] = lse.T

    gain = gain_ref[...]
    for blk in range(nblk):
        sl = slice(blk * WIN, (blk + 1) * WIN)
        lses = [s[sl, :] for s in lse_s]
        mx = functools.reduce(jnp.maximum, lses)
        ws = [jnp.exp2(s - mx) for s in lses]
        o = sum(w * x[sl, :] for w, x in zip(ws, out_s)) / sum(ws)
        sq = o * o
        ms_lo = jnp.sum(jnp.where(lo_lane, sq, 0.0), axis=1, keepdims=True)
        ms_hi = jnp.sum(jnp.where(lo_lane, 0.0, sq), axis=1, keepdims=True)
        inv = lax.rsqrt(jnp.where(lo_lane, ms_lo, ms_hi) * (1.0 / HEAD_DIM) + EPS)
        o_ref[sl, :] = (o * inv * gain).astype(o_ref.dtype)


def _dilated(a_qkv, bias, gain_a, batch, seq):
    t = a_qkv.shape[0]
    pairs = A_WIDTH // LANES
    assert all(d == STEP ** p and seq % (d * WIN) == 0 for p, (_, d) in enumerate(A_PATTERNS))
    npat = bias.shape[0]
    col = lambda off: pl.BlockSpec((seq, LANES), lambda b, g: (b, off + g))
    scratch = ([pltpu.VMEM((2, seq, LANES), F32) for _ in range(3)]
               + [pltpu.VMEM((npat, seq, LANES), BF16) for _ in range(2)]
               + [pltpu.VMEM((npat - 1, seq, LANES), BF16)]
               + [pltpu.VMEM((npat, LANES, seq), BF16)]
               + [pltpu.VMEM((seq // WIN, 2 * WIN, 2 * WIN), F32) for _ in range(npat)]
               + [pltpu.VMEM((seq // WIN, 2 * WIN, 2 * WIN), BF16) for _ in range(npat)]
               + [pltpu.VMEM((seq // WIN, 2, 2 * WIN), F32) for _ in range(npat)]
               + [pltpu.VMEM((seq, LANES), F32) for _ in range(2 * npat)])
    return pl.pallas_call(
        functools.partial(_dilated_kernel, seq=seq),
        grid=(batch, pairs),
        in_specs=[
            col(0), col(pairs), col(2 * pairs),
            pl.BlockSpec((npat, 2 * WIN, 2 * WIN), lambda b, g: (0, 0, g)),
            pl.BlockSpec((1, LANES), lambda b, g: (0, g)),
        ],
        out_specs=pl.BlockSpec((seq, LANES), lambda b, g: (b, g)),
        out_shape=jax.ShapeDtypeStruct((t, A_WIDTH), BF16),
        scratch_shapes=scratch,
        compiler_params=pltpu.CompilerParams(
            dimension_semantics=("arbitrary", "arbitrary"), vmem_limit_bytes=VMEM_LIMIT),
        name="dilated_attention",
    )(a_qkv, a_qkv, a_qkv, bias, gain_a)


def _mla_kernel(q_ref, k_ref, v_ref, gain_ref, o_ref, vt, s_sc, e_sc, *, seq, tq):
    key_idx = lax.broadcasted_iota(jnp.int32, (tq, tq), 0)
    qry_idx = lax.broadcasted_iota(jnp.int32, (tq, tq), 1)
    causal = key_idx <= qry_idx
    gain = gain_ref[...]
    for c in range(0, seq, tq):
        vt[:, c:c + tq] = v_ref[c:c + tq, :].astype(F32).T.astype(BF16)

    def head(hh, i, base):
        hs = slice(hh * LANES, (hh + 1) * LANES)
        q = q_ref[i * tq:(i + 1) * tq, hs]
        m = None
        for j in range(i + 1):
            s = lax.dot_general(k_ref[j * tq:(j + 1) * tq, hs], q, (((1,), (1,)), ((), ())),
                                preferred_element_type=F32)
            if j == i:
                s = jnp.where(causal, s, NEG)
            s_sc[base + j] = s
            bm = jnp.max(s, axis=0, keepdims=True)
            m = bm if m is None else jnp.maximum(m, bm)
        l = None
        for j in range(i + 1):
            e = jnp.exp2(s_sc[base + j] - m)
            ls = jnp.sum(e, axis=0, keepdims=True)
            l = ls if l is None else l + ls
            e_sc[(base + j) * tq:(base + j + 1) * tq, :] = e.astype(BF16)
        acc = jnp.dot(vt[hh * B_V:(hh + 1) * B_V, 0:(i + 1) * tq], e_sc[base * tq:(base + i + 1) * tq, :],
                      preferred_element_type=F32)
        return acc / l

    base = 0
    for i in range(seq // tq):
        normed = []
        for hh in range(2):
            o = head(hh, i, base)
            base += i + 1
            normed.append(o * lax.rsqrt(jnp.mean(o * o, axis=0, keepdims=True) + EPS))
        o_ref[i * tq:(i + 1) * tq, :] = (jnp.concatenate(normed, axis=0).T * gain).astype(o_ref.dtype)


def _mla(bq, bk, bv, gain_b, batch, seq):
    t = bq.shape[0]
    pairs = B_HEADS // 2
    tq = 512
    assert seq % tq == 0
    nq = seq // tq
    nslots = 2 * (nq * (nq + 1) // 2)
    return pl.pallas_call(
        functools.partial(_mla_kernel, seq=seq, tq=tq),
        grid=(batch, pairs),
        in_specs=[
            pl.BlockSpec((seq, 2 * LANES), lambda b, g: (b, g)),
            pl.BlockSpec((seq, 2 * LANES), lambda b, g: (b, g)),
            pl.BlockSpec((seq, LANES), lambda b, g: (b, g)),
            pl.BlockSpec((1, LANES), lambda b, g: (0, g)),
        ],
        out_specs=pl.BlockSpec((seq, LANES), lambda b, g: (b, g)),
        out_shape=jax.ShapeDtypeStruct((t, B_WIDTH), BF16),
        scratch_shapes=[pltpu.VMEM((LANES, seq), BF16),
                        pltpu.VMEM((nslots, tq, tq), F32), pltpu.VMEM((nslots * tq, tq), BF16)],
        compiler_params=pltpu.CompilerParams(
            dimension_semantics=("arbitrary", "arbitrary"), vmem_limit_bytes=VMEM_LIMIT),
        name="latent_attention",
    )(bq, bk, bv, gain_b)


def _retention_consts():
    h = C_HEADS
    log_g = np.log(1.0 - 2.0 ** (-5.0 - np.arange(h))).astype(np.float32)
    i = np.arange(C_CHUNK, dtype=np.float32)
    rel = i[:, None] - i[None, :]
    decay = (np.exp(np.maximum(rel, 0.0)[None] * log_g[:, None, None]) * (rel >= 0)[None]).astype(np.float32)
    xi = np.exp((i + 1.0)[None, :] * log_g[:, None]).astype(np.float32)
    zeta = np.exp((C_CHUNK - 1.0 - i)[None, :] * log_g[:, None]).astype(np.float32)
    chunk_decay = np.exp(C_CHUNK * log_g).astype(np.float32)
    decay_all = np.concatenate(list(decay), axis=1)
    xi_mat = np.repeat(xi.T, C_VAL, axis=1)
    zeta_t = np.zeros((C_QPAD, C_CHUNK), np.float32)
    zeta_t[:C_QK] = np.repeat(zeta, C_KEY, axis=0)
    cd = np.repeat(chunk_decay, C_VAL)[None, :]
    bd = np.zeros((C_QPAD, C_WIDTH), np.float32)
    kmask = np.zeros((C_QPAD, h * C_CHUNK), np.float32)
    vmask = np.zeros((h * C_CHUNK, C_WIDTH), np.float32)
    for hh in range(h):
        bd[hh * C_KEY:(hh + 1) * C_KEY, hh * C_VAL:(hh + 1) * C_VAL] = 1.0
        kmask[hh * C_KEY:(hh + 1) * C_KEY, hh * C_CHUNK:(hh + 1) * C_CHUNK] = 1.0
        vmask[hh * C_CHUNK:(hh + 1) * C_CHUNK, hh * C_VAL:(hh + 1) * C_VAL] = 1.0
    f32s = tuple(jnp.asarray(a) for a in (decay_all, xi_mat, zeta_t, cd, bd))
    return f32s + (jnp.asarray(kmask, BF16), jnp.asarray(vmask, BF16))


def _retention_kernel(qk_ref, v_ref, g_ref, decay_ref, xi_ref, zeta_ref, cd_ref, bd_ref, kmask_ref,
                      vmask_ref, gain_ref, o_ref, *, seq):
    lane = lax.broadcasted_iota(jnp.int32, (C_CHUNK, LANES), 1)
    lo = lane < C_VAL
    gain = gain_ref[...]
    state = jnp.zeros((C_QPAD, C_WIDTH), F32)

    for n in range(seq // C_CHUNK):
        rows = slice(n * C_CHUNK, (n + 1) * C_CHUNK)
        q = qk_ref[rows, 0:C_QPAD]
        k_t = qk_ref[rows, C_QPAD:2 * C_QPAD].astype(F32).T
        v = v_ref[rows, :]
        k_bd = jnp.concatenate([k_t.astype(BF16)] * C_HEADS, axis=1) * kmask_ref[...]
        s = jnp.dot(q, k_bd, preferred_element_type=F32) * decay_ref[...]
        v_bd = jnp.concatenate([v] * C_HEADS, axis=0) * vmask_ref[...]
        o = (jnp.dot(s.astype(BF16), v_bd, preferred_element_type=F32)
             + jnp.dot(q, state.astype(BF16), preferred_element_type=F32) * xi_ref[...])
        upd = jnp.dot((k_t * zeta_ref[...]).astype(BF16), v, preferred_element_type=F32)
        state = state * cd_ref[...] + upd * bd_ref[...]
        for t in range(C_WIDTH // LANES):
            sl = slice(t * LANES, (t + 1) * LANES)
            x = o[:, sl]
            mu_lo = jnp.sum(jnp.where(lo, x, 0.0), axis=1, keepdims=True)
            mu_hi = jnp.sum(jnp.where(lo, 0.0, x), axis=1, keepdims=True)
            dlt = x - jnp.where(lo, mu_lo, mu_hi) * (1.0 / C_VAL)
            sq = dlt * dlt
            var_lo = jnp.sum(jnp.where(lo, sq, 0.0), axis=1, keepdims=True)
            var_hi = jnp.sum(jnp.where(lo, 0.0, sq), axis=1, keepdims=True)
            y = dlt * lax.rsqrt(jnp.where(lo, var_lo, var_hi) * (1.0 / C_VAL) + EPS)
            o_ref[rows, sl] = (y * g_ref[rows, sl].astype(F32) * gain[:, sl]).astype(o_ref.dtype)


def _retention(cqk, cv, cg, gain_c, consts, batch, seq):
    t = cqk.shape[0]
    tok = lambda width: pl.BlockSpec((seq, width), lambda b: (b, 0))
    full = lambda a: _const_spec(a.shape, lambda b: (0,) * a.ndim)
    return pl.pallas_call(
        functools.partial(_retention_kernel, seq=seq),
        grid=(batch,),
        in_specs=[tok(2 * C_QPAD), tok(C_WIDTH), tok(C_WIDTH)] + [full(a) for a in consts] + [full(gain_c)],
        out_specs=tok(C_WIDTH),
        out_shape=jax.ShapeDtypeStruct((t, C_WIDTH), BF16),
        compiler_params=pltpu.CompilerParams(
            dimension_semantics=("arbitrary",), vmem_limit_bytes=VMEM_LIMIT),
        name="retention",
    )(cqk, cv, cg, *consts, gain_c)


def _outffn_kernel(x_ref, ma_ref, mb_ref, mc_ref, mod_ref, wo_ref, g_ref, wg_ref, wu_ref, wd_ref, fg_ref,
                   o_ref, mix_ref, hid_ref, *, final, fchunk):
    mix_ref[:, 0:A_WIDTH] = ma_ref[...]
    mix_ref[:, A_WIDTH:A_WIDTH + B_WIDTH] = mb_ref[...]
    mix_ref[:, A_WIDTH + B_WIDTH:] = mc_ref[...]
    att = jnp.dot(mix_ref[...], wo_ref[...], preferred_element_type=F32)
    x = x_ref[...] + mod_ref[2:3, :] * att
    h = (_rms(x) * g_ref[...] * (1.0 + mod_ref[4:5, :]) + mod_ref[3:4, :]).astype(BF16)
    hidden = wg_ref.shape[1]
    for j in range(hidden // fchunk):
        sl = slice(j * fchunk, (j + 1) * fchunk)
        gate = jnp.dot(h, wg_ref[:, sl], preferred_element_type=F32)
        up = jnp.dot(h, wu_ref[:, sl], preferred_element_type=F32)
        hid_ref[:, sl] = (_silu(gate) * up).astype(BF16)
    ffn = jnp.dot(hid_ref[...], wd_ref[...], preferred_element_type=F32)
    y = x + mod_ref[5:6, :] * ffn
    if final:
        y = _rms(y) * fg_ref[...]
    o_ref[...] = y


def _outffn(x, mix_a, mix_b, mix_c, mods, layer, prep, final_g, batch, seq, tm, final):
    t, d = x.shape
    nt = seq // tm
    hidden = prep["w_gate"].shape[2]
    row = lambda w: pl.BlockSpec((tm, w), lambda b, i: (b * nt + i, 0))
    wl = lambda a: _const_spec((None,) + a.shape[1:], lambda b, i: (layer,) + (0,) * (a.ndim - 1))
    return pl.pallas_call(
        functools.partial(_outffn_kernel, final=final, fchunk=256),
        grid=(batch, nt),
        in_specs=[
            row(d), row(A_WIDTH), row(B_WIDTH), row(C_WIDTH),
            pl.BlockSpec((None, None, 6, d), lambda b, i: (layer, b, 0, 0)),
            wl(prep["w_out"]), wl(prep["norm2_g"]), wl(prep["w_gate"]), wl(prep["w_up"]), wl(prep["w_down"]),
            _const_spec((1, d), lambda b, i: (0, 0)),
        ],
        out_specs=row(d),
        out_shape=jax.ShapeDtypeStruct((t, d), F32),
        scratch_shapes=[pltpu.VMEM((tm, d), BF16), pltpu.VMEM((tm, hidden), BF16)],
        compiler_params=pltpu.CompilerParams(
            dimension_semantics=("arbitrary", "arbitrary"), vmem_limit_bytes=VMEM_LIMIT),
        name="outproj_ffn",
    )(x, mix_a, mix_b, mix_c, mods, prep["w_out"], prep["norm2_g"], prep["w_gate"], prep["w_up"],
      prep["w_down"], final_g)


def _prepare(norm1_g, w_in, mla_q_norm, mla_kv_norm, mla_w_uq, mla_w_ukv, w_out, norm2_g,
             ffn_w_gate, ffn_w_up, ffn_w_down):
    depth, d, _ = w_in.shape
    b_lat0 = 3 * A_WIDTH
    rope0 = b_lat0 + B_Q_RANK + B_KV_RANK
    c0 = rope0 + B_ROPE
    assert C_QK + B_ROPE <= C_QPAD and C_QK % LANES == B_NOPE
    assert c0 + 2 * C_QK + 2 * C_WIDTH == w_in.shape[2] and rope0 == OFF_C
    uq = mla_w_uq.reshape(depth, B_Q_RANK, B_HEADS, B_NOPE + B_ROPE)
    uq = jnp.pad(uq, ((0, 0), (0, 0), (0, 0), (0, LANES - B_NOPE - B_ROPE)))
    ukv = mla_w_ukv.reshape(depth, B_KV_RANK, B_HEADS, B_NOPE + B_V)
    uk = jnp.pad(ukv[..., :B_NOPE], ((0, 0), (0, 0), (0, 0), (0, LANES - B_NOPE)))
    uv = ukv[..., B_NOPE:]
    return {
        "norm1_g": norm1_g[:, None, :],
        "w_in": w_in,
        "q_norm": mla_q_norm[:, None, :],
        "kv_norm": mla_kv_norm[:, None, :],
        "w_uq": uq.reshape(depth, B_Q_RANK, B_HEADS * LANES).astype(BF16),
        "w_uk": uk.reshape(depth, B_KV_RANK, B_HEADS * LANES).astype(BF16),
        "w_uv": uv.reshape(depth, B_KV_RANK, B_WIDTH).astype(BF16),
        "w_out": w_out.astype(BF16),
        "norm2_g": norm2_g[:, None, :],
        "w_gate": ffn_w_gate.astype(BF16),
        "w_up": ffn_w_up.astype(BF16),
        "w_down": ffn_w_down.astype(BF16),
    }


def kernel(x, c, positions, rel_bias, ada_w, ada_b, norm1_g, w_in, mla_q_norm, mla_kv_norm, mla_w_uq,
           mla_w_ukv, mix_gain, w_out, norm2_g, ffn_w_gate, ffn_w_up, ffn_w_down, final_norm):
    batch, seq, d = x.shape
    depth = w_in.shape[0]
    tm = 512
    assert seq % tm == 0 and seq % C_CHUNK == 0

    prep = _prepare(norm1_g, w_in, mla_q_norm, mla_kv_norm, mla_w_uq, mla_w_ukv, w_out, norm2_g,
                    ffn_w_gate, ffn_w_up, ffn_w_down)
    mods = _mods(c, ada_w, ada_b).reshape(depth, batch, 6, d)
    tables = _rope_tables(positions)
    bias = _bias_tables(rel_bias)
    ret_consts = _retention_consts()
    final_g = final_norm[None, :]

    xf = x.reshape(batch * seq, d)
    for l in range(depth):
        a_qkv, bq, bk, bv, cqk, cv, cg = _inproj(xf, mods, l, prep, tables, batch, seq, tm)
        gain = mix_gain[l][None, :]
        mix_a = _dilated(a_qkv, bias, gain[:, :A_WIDTH], batch, seq)
        mix_b = _mla(bq, bk, bv, gain[:, A_WIDTH:A_WIDTH + B_WIDTH], batch, seq)
        mix_c = _retention(cqk, cv, cg, gain[:, A_WIDTH + B_WIDTH:], ret_consts, batch, seq)
        xf = _outffn(xf, mix_a, mix_b, mix_c, mods, l, prep, final_g, batch, seq, tm, final=(l == depth - 1))
    return xf.reshape(batch, seq, d)
```

```python
import functools

import numpy as np
import jax
import jax.numpy as jnp
from jax import lax
from jax.experimental import pallas as pl
from jax.experimental.pallas import tpu as pltpu

F32 = jnp.float32
BF16 = jnp.bfloat16

HEAD_DIM = 64
A_HEADS = 6
A_PATTERNS = ((128, 1), (512, 4), (2048, 16))
A_WIDTH = A_HEADS * HEAD_DIM
B_HEADS = 4
B_NOPE = 64
B_ROPE = 32
B_V = 64
B_Q_RANK = 256
B_KV_RANK = 128
B_WIDTH = B_HEADS * B_V
C_HEADS = 6
C_KEY = 32
C_VAL = 64
C_WIDTH = C_HEADS * C_VAL
C_QK = C_HEADS * C_KEY
C_CHUNK = 128
N_BUCKETS = 32
MAX_DISTANCE = 2048
ROPE_BASE = 10000.0
EPS = 1e-6

LANES = 128
WIN = 128
STEP = 4
NEG = -1e30
LOG2E = 1.4426950408889634
FFN_CAST = ("w_out", "w_gate", "w_up", "w_down")
VMEM_LIMIT = 56 * 1024 * 1024

OFF_A = 0
OFF_BQ = 3 * A_WIDTH
OFF_BKV = OFF_BQ + B_Q_RANK
OFF_C = OFF_BKV + B_KV_RANK
C_QPAD = 2 * LANES
OFF_CV = OFF_C + 2 * C_QPAD
OFF_CG = OFF_CV + C_WIDTH
IN_COLS = OFF_CG + C_WIDTH


def _const_spec(shape, index_map):
    return pl.BlockSpec(shape, index_map, pipeline_mode=pl.Buffered(1))


def _silu(x):
    return x / (1.0 + jnp.exp(-x))


def _rms(x):
    return x * lax.rsqrt(jnp.mean(x * x, axis=-1, keepdims=True) + EPS)


def _mods_kernel(c_ref, w_ref, b_ref, o_ref):
    def split(a):
        hi = a.astype(BF16)
        return hi, (a - hi.astype(F32)).astype(BF16)

    c_hi, c_lo = split(_silu(c_ref[...]))
    w_hi, w_lo = split(w_ref[...])
    nb = c_hi.shape[0]
    both = jnp.dot(jnp.concatenate([c_hi, c_lo], axis=0), w_hi, preferred_element_type=F32)
    o_ref[...] = (both[:nb] + both[nb:] + jnp.dot(c_hi, w_lo, preferred_element_type=F32)) + b_ref[...]


def _mods(c, ada_w, ada_b):
    depth, d, n = ada_w.shape
    b = c.shape[0]
    tn = 1536
    return pl.pallas_call(
        _mods_kernel,
        grid=(depth, n // tn),
        in_specs=[
            pl.BlockSpec((b, d), lambda l, j: (0, 0)),
            pl.BlockSpec((None, d, tn), lambda l, j: (l, 0, j)),
            pl.BlockSpec((None, 1, tn), lambda l, j: (l, 0, j)),
        ],
        out_specs=pl.BlockSpec((None, b, tn), lambda l, j: (l, 0, j)),
        out_shape=jax.ShapeDtypeStruct((depth, b, n), F32),
        compiler_params=pltpu.CompilerParams(vmem_limit_bytes=VMEM_LIMIT),
        name="adaln_mods",
    )(c, ada_w, ada_b.reshape(depth, 1, n))


def _rope_kernel(pos_ref, f_ref, s_ref, cb_ref, sb_ref, cc_ref, sc_ref):
    p = pos_ref[...].astype(F32)
    ang = p * f_ref[1:2, :]
    cos = jnp.cos(ang)
    sin = jnp.sin(ang)
    rope_b = f_ref[0:1, :] > 0.0
    cb_ref[...] = jnp.where(rope_b, cos, 1.0)
    sb_ref[...] = jnp.where(rope_b, sin, 0.0) * s_ref[0:1, :]
    cc_ref[...] = cos
    sc_ref[...] = sin * s_ref[1:2, :]


def _rope_tables(positions):
    t = positions.size
    half = B_ROPE // 2
    inv_freq = (1.0 / (ROPE_BASE ** (np.arange(half, dtype=np.float32) / half))).astype(np.float32)
    zeros = np.zeros(half, np.float32)
    ones = np.ones(half, np.float32)
    f_b = np.concatenate([np.zeros(B_NOPE, np.float32), inv_freq, inv_freq, np.zeros(32, np.float32)])
    s_b = np.concatenate([np.zeros(B_NOPE, np.float32), -ones, ones, np.zeros(32, np.float32)])
    f_c = np.tile(np.concatenate([inv_freq, inv_freq]), LANES // C_KEY)
    s_c = np.tile(np.concatenate([-ones, ones]), LANES // C_KEY)
    del zeros
    freqs = jnp.asarray(np.stack([f_b, f_c]))
    signs = jnp.asarray(np.stack([s_b, s_c]))
    tr = 2048
    out = jax.ShapeDtypeStruct((t, LANES), F32)
    row = pl.BlockSpec((tr, LANES), lambda i: (i, 0))
    return pl.pallas_call(
        _rope_kernel,
        grid=(t // tr,),
        in_specs=[
            pl.BlockSpec((tr, 1), lambda i: (i, 0)),
            pl.BlockSpec((2, LANES), lambda i: (0, 0)),
            pl.BlockSpec((2, LANES), lambda i: (0, 0)),
        ],
        out_specs=[row, row, row, row],
        out_shape=[out, out, out, out],
        name="rope_tables",
    )(positions.reshape(t, 1), freqs, signs)


def _t5_bucket(dist):
    max_exact = N_BUCKETS // 2
    safe = np.maximum(dist, 1).astype(np.float32)
    large = max_exact + (np.log(safe / max_exact) / np.log(MAX_DISTANCE / max_exact)
                         * (N_BUCKETS - max_exact)).astype(np.int32)
    large = np.minimum(large, N_BUCKETS - 1)
    return np.where(dist < max_exact, dist, large).astype(np.int32)


def _bias_kernel(bmap_ref, rb_ref, o_ref):
    h = pl.program_id(1)
    bm = bmap_ref[...]
    t = jnp.full(bm.shape, NEG, F32)
    for b in range(N_BUCKETS):
        t = jnp.where(bm == b, rb_ref[b, h] * LOG2E, t)
    o_ref[...] = t


def _bias_tables(rel_bias):
    qi = np.arange(WIN)[None, :]
    c = np.arange(2 * WIN)[:, None]
    j = qi - c + WIN
    maps = []
    for (w, d) in A_PATTERNS:
        assert w // d == WIN
        bucket = _t5_bucket(np.arange(WIN + 1, dtype=np.int32) * d)
        maps.append(np.where((j >= 0) & (j <= WIN), bucket[np.clip(j, 0, WIN)], -1).astype(np.int32))
    bmap = jnp.asarray(np.stack(maps))
    npat = len(A_PATTERNS)
    return pl.pallas_call(
        _bias_kernel,
        grid=(npat, A_HEADS),
        in_specs=[
            pl.BlockSpec((None, 2 * WIN, WIN), lambda p, h: (p, 0, 0)),
            pl.BlockSpec(memory_space=pltpu.SMEM),
        ],
        out_specs=pl.BlockSpec((None, 2 * WIN, WIN), lambda p, h: (p, 0, h)),
        out_shape=jax.ShapeDtypeStruct((npat, 2 * WIN, A_HEADS * WIN), F32),
        name="t5_bias_tables",
    )(bmap, rel_bias)


def _rot_half(x, x1_mask):
    w = x.shape[-1]
    half = B_ROPE // 2
    return jnp.where(x1_mask, pltpu.roll(x, w - half, 1), pltpu.roll(x, half, 1))


def _inproj_kernel(x_ref, mod_ref, g_ref, win_ref, qg_ref, kvg_ref, wuq_ref, wuk_ref, wuv_ref,
                   cb_ref, sb_ref, cc_ref, sc_ref, ks_ref, *refs):
    ncast = len(FFN_CAST)
    cast_in, refs = refs[:ncast], refs[ncast:]
    a_ref, bq_ref, bk_ref, bv_ref, cqk_ref, cv_ref, cg_ref = refs[:7]
    cast_out, w_ref = refs[7:7 + ncast], refs[7 + ncast]
    for src, dst in zip(cast_in, cast_out):
        dst[...] = src[...].astype(BF16)

    @pl.when((pl.program_id(0) == 0) & (pl.program_id(1) == 0))
    def _():
        d = win_ref.shape[1]
        rope0 = OFF_C
        c0 = rope0 + B_ROPE
        zeros = lambda n: jnp.zeros((n, d), F32)
        chunk = 256
        for r0 in range(0, OFF_C, chunk):
            w_ref[:, r0:r0 + chunk] = win_ref[r0:r0 + chunk, :].T.astype(BF16)
        q_rows = jnp.concatenate([win_ref[c0:c0 + C_QK, :], win_ref[rope0:c0, :],
                                  zeros(C_QPAD - C_QK - B_ROPE)], axis=0)
        w_ref[:, OFF_C:OFF_C + C_QPAD] = q_rows.T.astype(BF16)
        k_rows = jnp.concatenate([win_ref[c0 + C_QK:c0 + 2 * C_QK, :], zeros(C_QPAD - C_QK)], axis=0)
        w_ref[:, OFF_C + C_QPAD:OFF_CV] = k_rows.T.astype(BF16)
        for r0 in range(0, IN_COLS - OFF_CV, chunk):
            src = c0 + 2 * C_QK + r0
            w_ref[:, OFF_CV + r0:OFF_CV + r0 + chunk] = win_ref[src:src + chunk, :].T.astype(BF16)

    x = x_ref[...]
    tm = x.shape[0]
    h = (_rms(x) * g_ref[...] * (1.0 + mod_ref[1:2, :]) + mod_ref[0:1, :]).astype(BF16)

    def proj(lo, hi):
        return jnp.dot(h, w_ref[:, lo:hi], preferred_element_type=F32)

    lane = lax.broadcasted_iota(jnp.int32, (tm, LANES), 1)
    b_x1 = (lane >= B_NOPE) & (lane < B_NOPE + B_ROPE // 2)
    c_x1 = (lane & (C_KEY // 2)) == 0
    cos_b = cb_ref[...]
    sin_b = sb_ref[...]

    q_lat = proj(OFF_BQ, OFF_BKV)
    kv_lat = proj(OFF_BKV, OFF_C)
    qk = proj(OFF_C, OFF_CV)
    qn = (_rms(q_lat) * qg_ref[...]).astype(BF16)
    kvn = (_rms(kv_lat) * kvg_ref[...]).astype(BF16)

    a_ref[:, 0:A_WIDTH] = (proj(OFF_A, OFF_A + A_WIDTH) * (HEAD_DIM ** -0.5 * LOG2E)).astype(BF16)
    a_ref[:, A_WIDTH:3 * A_WIDTH] = proj(OFF_A + A_WIDTH, OFF_BQ).astype(BF16)

    q = jnp.dot(qn, wuq_ref[...], preferred_element_type=F32)
    k_nope = jnp.dot(kvn, wuk_ref[...], preferred_element_type=F32)
    bv_ref[...] = jnp.dot(kvn, wuv_ref[...], preferred_element_type=F32).astype(BF16)
    cv_ref[...] = proj(OFF_CV, OFF_CG).astype(BF16)
    cg_ref[...] = _silu(proj(OFF_CG, IN_COLS)).astype(BF16)

    k_rope = jnp.where(lane >= B_NOPE, qk[:, LANES:2 * LANES], 0.0)
    k_pe = k_rope * cos_b + _rot_half(k_rope, b_x1) * sin_b
    b_scale = (B_NOPE + B_ROPE) ** -0.5 * LOG2E
    for hh in range(B_HEADS):
        sl = slice(hh * LANES, (hh + 1) * LANES)
        qh = q[:, sl]
        bq_ref[:, sl] = ((qh * cos_b + _rot_half(qh, b_x1) * sin_b) * b_scale).astype(BF16)
        bk_ref[:, sl] = (k_nope[:, sl] + k_pe).astype(BF16)

    cos_c = cc_ref[...]
    sin_c = sc_ref[...]
    for t in range(2 * C_QPAD // LANES):
        sl = slice(t * LANES, (t + 1) * LANES)
        xt = qk[:, sl]
        cqk_ref[:, sl] = ((xt * cos_c + _rot_half(xt, c_x1) * sin_c) * ks_ref[:, sl]).astype(BF16)


def _inproj(x, mods, layer, prep, tables, batch, seq, tm):
    t, d = x.shape
    nt = seq // tm
    row = lambda w: pl.BlockSpec((tm, w), lambda b, i: (b * nt + i, 0))
    wl = lambda a: _const_spec((None,) + a.shape[1:], lambda b, i: (layer,) + (0,) * (a.ndim - 1))
    cos_b, sin_b, cos_c, sin_c = tables
    kscale = jnp.asarray(np.concatenate([np.ones(C_QPAD, np.float32),
                                         np.full(C_QPAD, C_KEY ** -0.5, np.float32)])[None, :])
    outs = [(3 * A_WIDTH, BF16), (B_HEADS * LANES, BF16), (B_HEADS * LANES, BF16), (B_WIDTH, BF16),
            (2 * C_QPAD, BF16), (C_WIDTH, BF16), (C_WIDTH, BF16)]
    cast_ws = [prep[name] for name in FFN_CAST]
    cast_specs = []
    for w in cast_ws:
        nslab = batch * nt
        while w.shape[1] % (16 * nslab):
            assert nslab % 2 == 0
            nslab //= 2
        rep = batch * nt // nslab
        cast_specs.append(pl.BlockSpec((None, w.shape[1] // nslab, w.shape[2]),
                                       lambda b, i, rep=rep: (layer, (b * nt + i) // rep, 0)))
    cast_out_specs = [pl.BlockSpec(s.block_shape[1:], lambda b, i, f=s.index_map: f(b, i)[1:]) for s in cast_specs]
    return pl.pallas_call(
        _inproj_kernel,
        grid=(batch, nt),
        in_specs=[
            row(d),
            pl.BlockSpec((None, None, 6, d), lambda b, i: (layer, b, 0, 0)),
            wl(prep["norm1_g"]), wl(prep["w_in"]), wl(prep["q_norm"]), wl(prep["kv_norm"]),
            wl(prep["w_uq"]), wl(prep["w_uk"]), wl(prep["w_uv"]),
            row(LANES), row(LANES), row(LANES), row(LANES),
            _const_spec((1, 2 * C_QPAD), lambda b, i: (0, 0)),
            *cast_specs,
        ],
        out_specs=[row(w) for w, _ in outs] + cast_out_specs,
        out_shape=([jax.ShapeDtypeStruct((t, w), dt) for w, dt in outs]
                   + [jax.ShapeDtypeStruct(w.shape[1:], BF16) for w in cast_ws]),
        scratch_shapes=[pltpu.VMEM((d, IN_COLS), BF16)],
        compiler_params=pltpu.CompilerParams(
            dimension_semantics=("arbitrary", "arbitrary"), vmem_limit_bytes=VMEM_LIMIT),
        name="inproj",
    )(x, mods, prep["norm1_g"], prep["w_in"], prep["q_norm"], prep["kv_norm"],
      prep["w_uq"], prep["w_uk"], prep["w_uv"], cos_b, sin_b, cos_c, sin_c, kscale, *cast_ws)


def _class_major_blocks(seq, level):
    n = seq // STEP
    perm = np.arange(seq)
    for _ in range(level):
        perm = perm.reshape(n, STEP).T.reshape(seq)
    blocks = perm.reshape(seq // WIN, WIN)
    stride = STEP ** level
    assert (blocks == blocks[:, :1] + stride * np.arange(WIN)).all()
    return [(int(b[0]), stride) for b in blocks]


def _dilated_kernel(q_ref, k_ref, v_ref, bias_ref, gain_ref, o_ref,
                    qf, kf, vf, qlo, qhi, kp, vt, *stage_refs, seq):
    npat = len(A_PATTERNS)
    nblk = seq // WIN
    s_sc, e_sc, stat_sc, out_s, lse_s = (stage_refs[i * npat:(i + 1) * npat] for i in range(5))
    lane = lax.broadcasted_iota(jnp.int32, (WIN, LANES), 1)
    lo_lane = lane < HEAD_DIM
    lo_row = lax.broadcasted_iota(jnp.int32, (LANES, WIN), 0) < HEAD_DIM

    qf[0] = q_ref[...].astype(F32)
    kf[0] = k_ref[...].astype(F32)
    vf[0] = v_ref[...].astype(F32)

    for p in range(npat):
        if p > 0:
            n = seq // STEP
            for r in range(STEP):
                dst = slice(r * n, (r + 1) * n)
                src = pl.ds(r, n, stride=STEP)
                for f in (qf, kf, vf):
                    f[p % 2, dst, :] = f[(p - 1) % 2, src, :]
        for c in range(0, seq, 4 * WIN):
            rows = slice(c, c + 4 * WIN)
            q = qf[p % 2, rows, :]
            lo_q = lax.broadcasted_iota(jnp.int32, q.shape, 1) < HEAD_DIM
            qlo[p, rows, :] = jnp.where(lo_q, q, 0.0).astype(BF16)
            qhi[p, rows, :] = jnp.where(lo_q, 0.0, q).astype(BF16)
            vt[p, :, rows] = vf[p % 2, rows, :].T.astype(BF16)
            if p > 0:
                kp[p - 1, rows, :] = kf[p % 2, rows, :].astype(BF16)

    def key_rows(p, blk):
        r0 = blk * WIN
        first = blk % (nblk // A_PATTERNS[p][1]) == 0
        return (r0 if first else r0 - WIN), r0 + WIN

    for p in range(npat - 1, -1, -1):
        for blk in range(nblk):
            k0, k1 = key_rows(p, blk)
            r0 = blk * WIN
            q2 = jnp.concatenate([qlo[p, r0:r0 + WIN, :], qhi[p, r0:r0 + WIN, :]], axis=0)
            kcat = k_ref[k0:k1, :] if p == 0 else kp[p - 1, k0:k1, :]
            bias = bias_ref[p, 2 * WIN - (k1 - k0):, :]
            s_sc[p][blk, 0:k1 - k0, :] = lax.dot_general(
                kcat, q2, (((1,), (1,)), ((), ())), preferred_element_type=F32) + bias
        for blk in range(nblk):
            k0, k1 = key_rows(p, blk)
            s = s_sc[p][blk, 0:k1 - k0, :]
            m = jnp.max(s, axis=0, keepdims=True)
            e = jnp.exp2(s - m)
            l = jnp.sum(e, axis=0, keepdims=True)
            e_sc[p][blk, 0:k1 - k0, :] = e.astype(BF16)
            stat_sc[p][blk, 0:1, :] = 1.0 / l
            stat_sc[p][blk, 1:2, :] = m + jnp.log2(l)
        for blk, (start, stride) in enumerate(_class_major_blocks(seq, p)):
            k0, k1 = key_rows(p, blk)
            acc = jnp.dot(vt[p, :, k0:k1], e_sc[p][blk, 0:k1 - k0, :], preferred_element_type=F32)
            out = acc * stat_sc[p][blk, 0:1, :]
            lse = stat_sc[p][blk, 1:2, :]
            out = jnp.where(lo_row, out[:, :WIN], out[:, WIN:])
            lse = jnp.where(lo_row, lse[:, :WIN], lse[:, WIN:])
            dst = pl.ds(start, WIN, stride=stride) if stride > 1 else slice(start, start + WIN)
            out_s[p][dst, :] = out.T
            lse_s[p][dst, :] = lse.T

    gain = gain_ref[...]
    for blk in range(nblk):
        sl = slice(blk * WIN, (blk + 1) * WIN)
        lses = [s[sl, :] for s in lse_s]
        mx = functools.reduce(jnp.maximum, lses)
        ws = [jnp.exp2(s - mx) for s in lses]
        o = sum(w * x[sl, :] for w, x in zip(ws, out_s)) / sum(ws)
        sq = o * o
        ms_lo = jnp.sum(jnp.where(lo_lane, sq, 0.0), axis=1, keepdims=True)
        ms_hi = jnp.sum(jnp.where(lo_lane, 0.0, sq), axis=1, keepdims=True)
        inv = lax.rsqrt(jnp.where(lo_lane, ms_lo, ms_hi) * (1.0 / HEAD_DIM) + EPS)
        o_ref[sl, :] = (o * inv * gain).astype(o_ref.dtype)


def _dilated(a_qkv, bias, gain_a, batch, seq):
    t = a_qkv.shape[0]
    pairs = A_WIDTH // LANES
    assert all(d == STEP ** p and seq % (d * WIN) == 0 for p, (_, d) in enumerate(A_PATTERNS))
    npat = bias.shape[0]
    col = lambda off: pl.BlockSpec((seq, LANES), lambda b, g: (b, off + g))
    scratch = ([pltpu.VMEM((2, seq, LANES), F32) for _ in range(3)]
               + [pltpu.VMEM((npat, seq, LANES), BF16) for _ in range(2)]
               + [pltpu.VMEM((npat - 1, seq, LANES), BF16)]
               + [pltpu.VMEM((npat, LANES, seq), BF16)]
               + [pltpu.VMEM((seq // WIN, 2 * WIN, 2 * WIN), F32) for _ in range(npat)]
               + [pltpu.VMEM((seq // WIN, 2 * WIN, 2 * WIN), BF16) for _ in range(npat)]
               + [pltpu.VMEM((seq // WIN, 2, 2 * WIN), F32) for _ in range(npat)]
               + [pltpu.VMEM((seq, LANES), F32) for _ in range(2 * npat)])
    return pl.pallas_call(
        functools.partial(_dilated_kernel, seq=seq),
        grid=(batch, pairs),
        in_specs=[
            col(0), col(pairs), col(2 * pairs),
            pl.BlockSpec((npat, 2 * WIN, 2 * WIN), lambda b, g: (0, 0, g)),
            pl.BlockSpec((1, LANES), lambda b, g: (0, g)),
        ],
        out_specs=pl.BlockSpec((seq, LANES), lambda b, g: (b, g)),
        out_shape=jax.ShapeDtypeStruct((t, A_WIDTH), BF16),
        scratch_shapes=scratch,
        compiler_params=pltpu.CompilerParams(
            dimension_semantics=("arbitrary", "arbitrary"), vmem_limit_bytes=VMEM_LIMIT),
        name="dilated_attention",
    )(a_qkv, a_qkv, a_qkv, bias, gain_a)


def _mla_kernel(q_ref, k_ref, v_ref, gain_ref, o_ref, vt, s_sc, e_sc, *, seq, tq):
    key_idx = lax.broadcasted_iota(jnp.int32, (tq, tq), 0)
    qry_idx = lax.broadcasted_iota(jnp.int32, (tq, tq), 1)
    causal = key_idx <= qry_idx
    gain = gain_ref[...]
    for c in range(0, seq, tq):
        vt[:, c:c + tq] = v_ref[c:c + tq, :].astype(F32).T.astype(BF16)

    def head(hh, i, base):
        hs = slice(hh * LANES, (hh + 1) * LANES)
        q = q_ref[i * tq:(i + 1) * tq, hs]
        m = None
        for j in range(i + 1):
            s = lax.dot_general(k_ref[j * tq:(j + 1) * tq, hs], q, (((1,), (1,)), ((), ())),
                                preferred_element_type=F32)
            if j == i:
                s = jnp.where(causal, s, NEG)
            s_sc[base + j] = s
            bm = jnp.max(s, axis=0, keepdims=True)
            m = bm if m is None else jnp.maximum(m, bm)
        l = None
        for j in range(i + 1):
            e = jnp.exp2(s_sc[base + j] - m)
            ls = jnp.sum(e, axis=0, keepdims=True)
            l = ls if l is None else l + ls
            e_sc[(base + j) * tq:(base + j + 1) * tq, :] = e.astype(BF16)
        acc = jnp.dot(vt[hh * B_V:(hh + 1) * B_V, 0:(i + 1) * tq], e_sc[base * tq:(base + i + 1) * tq, :],
                      preferred_element_type=F32)
        return acc / l

    base = 0
    for i in range(seq // tq):
        normed = []
        for hh in range(2):
            o = head(hh, i, base)
            base += i + 1
            normed.append(o * lax.rsqrt(jnp.mean(o * o, axis=0, keepdims=True) + EPS))
        o_ref[i * tq:(i + 1) * tq, :] = (jnp.concatenate(normed, axis=0).T * gain).astype(o_ref.dtype)


def _mla(bq, bk, bv, gain_b, batch, seq):
    t = bq.shape[0]
    pairs = B_HEADS // 2
    tq = 512
    assert seq % tq == 0
    nq = seq // tq
    nslots = 2 * (nq * (nq + 1) // 2)
    return pl.pallas_call(
        functools.partial(_mla_kernel, seq=seq, tq=tq),
        grid=(batch, pairs),
        in_specs=[
            pl.BlockSpec((seq, 2 * LANES), lambda b, g: (b, g)),
            pl.BlockSpec((seq, 2 * LANES), lambda b, g: (b, g)),
            pl.BlockSpec((seq, LANES), lambda b, g: (b, g)),
            pl.BlockSpec((1, LANES), lambda b, g: (0, g)),
        ],
        out_specs=pl.BlockSpec((seq, LANES), lambda b, g: (b, g)),
        out_shape=jax.ShapeDtypeStruct((t, B_WIDTH), BF16),
        scratch_shapes=[pltpu.VMEM((LANES, seq), BF16),
                        pltpu.VMEM((nslots, tq, tq), F32), pltpu.VMEM((nslots * tq, tq), BF16)],
        compiler_params=pltpu.CompilerParams(
            dimension_semantics=("arbitrary", "arbitrary"), vmem_limit_bytes=VMEM_LIMIT),
        name="latent_attention",
    )(bq, bk, bv, gain_b)


def _retention_consts():
    h = C_HEADS
    log_g = np.log(1.0 - 2.0 ** (-5.0 - np.arange(h))).astype(np.float32)
    i = np.arange(C_CHUNK, dtype=np.float32)
    rel = i[:, None] - i[None, :]
    decay = (np.exp(np.maximum(rel, 0.0)[None] * log_g[:, None, None]) * (rel >= 0)[None]).astype(np.float32)
    xi = np.exp((i + 1.0)[None, :] * log_g[:, None]).astype(np.float32)
    zeta = np.exp((C_CHUNK - 1.0 - i)[None, :] * log_g[:, None]).astype(np.float32)
    chunk_decay = np.exp(C_CHUNK * log_g).astype(np.float32)
    decay_all = np.concatenate(list(decay), axis=1)
    xi_mat = np.repeat(xi.T, C_VAL, axis=1)
    zeta_t = np.zeros((C_QPAD, C_CHUNK), np.float32)
    zeta_t[:C_QK] = np.repeat(zeta, C_KEY, axis=0)
    cd = np.repeat(chunk_decay, C_VAL)[None, :]
    bd = np.zeros((C_QPAD, C_WIDTH), np.float32)
    kmask = np.zeros((C_QPAD, h * C_CHUNK), np.float32)
    vmask = np.zeros((h * C_CHUNK, C_WIDTH), np.float32)
    for hh in range(h):
        bd[hh * C_KEY:(hh + 1) * C_KEY, hh * C_VAL:(hh + 1) * C_VAL] = 1.0
        kmask[hh * C_KEY:(hh + 1) * C_KEY, hh * C_CHUNK:(hh + 1) * C_CHUNK] = 1.0
        vmask[hh * C_CHUNK:(hh + 1) * C_CHUNK, hh * C_VAL:(hh + 1) * C_VAL] = 1.0
    f32s = tuple(jnp.asarray(a) for a in (decay_all, xi_mat, zeta_t, cd, bd))
    return f32s + (jnp.asarray(kmask, BF16), jnp.asarray(vmask, BF16))


def _retention_kernel(qk_ref, v_ref, g_ref, decay_ref, xi_ref, zeta_ref, cd_ref, bd_ref, kmask_ref,
                      vmask_ref, gain_ref, o_ref, *, seq):
    lane = lax.broadcasted_iota(jnp.int32, (C_CHUNK, LANES), 1)
    lo = lane < C_VAL
    gain = gain_ref[...]
    state = jnp.zeros((C_QPAD, C_WIDTH), F32)

    for n in range(seq // C_CHUNK):
        rows = slice(n * C_CHUNK, (n + 1) * C_CHUNK)
        q = qk_ref[rows, 0:C_QPAD]
        k_t = qk_ref[rows, C_QPAD:2 * C_QPAD].astype(F32).T
        v = v_ref[rows, :]
        k_bd = jnp.concatenate([k_t.astype(BF16)] * C_HEADS, axis=1) * kmask_ref[...]
        s = jnp.dot(q, k_bd, preferred_element_type=F32) * decay_ref[...]
        v_bd = jnp.concatenate([v] * C_HEADS, axis=0) * vmask_ref[...]
        o = (jnp.dot(s.astype(BF16), v_bd, preferred_element_type=F32)
             + jnp.dot(q, state.astype(BF16), preferred_element_type=F32) * xi_ref[...])
        upd = jnp.dot((k_t * zeta_ref[...]).astype(BF16), v, preferred_element_type=F32)
        state = state * cd_ref[...] + upd * bd_ref[...]
        for t in range(C_WIDTH // LANES):
            sl = slice(t * LANES, (t + 1) * LANES)
            x = o[:, sl]
            mu_lo = jnp.sum(jnp.where(lo, x, 0.0), axis=1, keepdims=True)
            mu_hi = jnp.sum(jnp.where(lo, 0.0, x), axis=1, keepdims=True)
            dlt = x - jnp.where(lo, mu_lo, mu_hi) * (1.0 / C_VAL)
            sq = dlt * dlt
            var_lo = jnp.sum(jnp.where(lo, sq, 0.0), axis=1, keepdims=True)
            var_hi = jnp.sum(jnp.where(lo, 0.0, sq), axis=1, keepdims=True)
            y = dlt * lax.rsqrt(jnp.where(lo, var_lo, var_hi) * (1.0 / C_VAL) + EPS)
            o_ref[rows, sl] = (y * g_ref[rows, sl].astype(F32) * gain[:, sl]).astype(o_ref.dtype)


def _retention(cqk, cv, cg, gain_c, consts, batch, seq):
    t = cqk.shape[0]
    tok = lambda width: pl.BlockSpec((seq, width), lambda b: (b, 0))
    full = lambda a: _const_spec(a.shape, lambda b: (0,) * a.ndim)
    return pl.pallas_call(
        functools.partial(_retention_kernel, seq=seq),
        grid=(batch,),
        in_specs=[tok(2 * C_QPAD), tok(C_WIDTH), tok(C_WIDTH)] + [full(a) for a in consts] + [full(gain_c)],
        out_specs=tok(C_WIDTH),
        out_shape=jax.ShapeDtypeStruct((t, C_WIDTH), BF16),
        compiler_params=pltpu.CompilerParams(
            dimension_semantics=("arbitrary",), vmem_limit_bytes=VMEM_LIMIT),
        name="retention",
    )(cqk, cv, cg, *consts, gain_c)


def _outffn_kernel(x_ref, ma_ref, mb_ref, mc_ref, mod_ref, wo_ref, g_ref, wg_ref, wu_ref, wd_ref, fg_ref,
                   o_ref, mix_ref, hid_ref, *, final, fchunk):
    mix_ref[:, 0:A_WIDTH] = ma_ref[...]
    mix_ref[:, A_WIDTH:A_WIDTH + B_WIDTH] = mb_ref[...]
    mix_ref[:, A_WIDTH + B_WIDTH:] = mc_ref[...]
    att = jnp.dot(mix_ref[...], wo_ref[...], preferred_element_type=F32)
    x = x_ref[...] + mod_ref[2:3, :] * att
    h = (_rms(x) * g_ref[...] * (1.0 + mod_ref[4:5, :]) + mod_ref[3:4, :]).astype(BF16)
    hidden = wg_ref.shape[1]
    for j in range(hidden // fchunk):
        sl = slice(j * fchunk, (j + 1) * fchunk)
        gate = jnp.dot(h, wg_ref[:, sl], preferred_element_type=F32)
        up = jnp.dot(h, wu_ref[:, sl], preferred_element_type=F32)
        hid_ref[:, sl] = (_silu(gate) * up).astype(BF16)
    ffn = jnp.dot(hid_ref[...], wd_ref[...], preferred_element_type=F32)
    y = x + mod_ref[5:6, :] * ffn
    if final:
        y = _rms(y) * fg_ref[...]
    o_ref[...] = y


def _outffn(x, mix_a, mix_b, mix_c, mods, layer, prep, ffn_w, final_g, batch, seq, tm, final):
    t, d = x.shape
    nt = seq // tm
    w_out, w_gate, w_up, w_down = ffn_w
    hidden = w_gate.shape[1]
    whole = lambda a: _const_spec(a.shape, lambda b, i: (0,) * a.ndim)
    row = lambda w: pl.BlockSpec((tm, w), lambda b, i: (b * nt + i, 0))
    wl = lambda a: _const_spec((None,) + a.shape[1:], lambda b, i: (layer,) + (0,) * (a.ndim - 1))
    return pl.pallas_call(
        functools.partial(_outffn_kernel, final=final, fchunk=256),
        grid=(batch, nt),
        in_specs=[
            row(d), row(A_WIDTH), row(B_WIDTH), row(C_WIDTH),
            pl.BlockSpec((None, None, 6, d), lambda b, i: (layer, b, 0, 0)),
            whole(w_out), wl(prep["norm2_g"]), whole(w_gate), whole(w_up), whole(w_down),
            _const_spec((1, d), lambda b, i: (0, 0)),
        ],
        out_specs=row(d),
        out_shape=jax.ShapeDtypeStruct((t, d), F32),
        scratch_shapes=[pltpu.VMEM((tm, d), BF16), pltpu.VMEM((tm, hidden), BF16)],
        compiler_params=pltpu.CompilerParams(
            dimension_semantics=("arbitrary", "arbitrary"), vmem_limit_bytes=VMEM_LIMIT),
        name="outproj_ffn",
    )(x, mix_a, mix_b, mix_c, mods, w_out, prep["norm2_g"], w_gate, w_up, w_down, final_g)


def _prepare(norm1_g, w_in, mla_q_norm, mla_kv_norm, mla_w_uq, mla_w_ukv, w_out, norm2_g,
             ffn_w_gate, ffn_w_up, ffn_w_down):
    depth, d, _ = w_in.shape
    b_lat0 = 3 * A_WIDTH
    rope0 = b_lat0 + B_Q_RANK + B_KV_RANK
    c0 = rope0 + B_ROPE
    assert C_QK + B_ROPE <= C_QPAD and C_QK % LANES == B_NOPE
    assert c0 + 2 * C_QK + 2 * C_WIDTH == w_in.shape[2] and rope0 == OFF_C
    uq = mla_w_uq.reshape(depth, B_Q_RANK, B_HEADS, B_NOPE + B_ROPE)
    uq = jnp.pad(uq, ((0, 0), (0, 0), (0, 0), (0, LANES - B_NOPE - B_ROPE)))
    ukv = mla_w_ukv.reshape(depth, B_KV_RANK, B_HEADS, B_NOPE + B_V)
    uk = jnp.pad(ukv[..., :B_NOPE], ((0, 0), (0, 0), (0, 0), (0, LANES - B_NOPE)))
    uv = ukv[..., B_NOPE:]
    return {
        "norm1_g": norm1_g[:, None, :],
        "w_in": jnp.swapaxes(w_in, 1, 2),
        "q_norm": mla_q_norm[:, None, :],
        "kv_norm": mla_kv_norm[:, None, :],
        "w_uq": uq.reshape(depth, B_Q_RANK, B_HEADS * LANES).astype(BF16),
        "w_uk": uk.reshape(depth, B_KV_RANK, B_HEADS * LANES).astype(BF16),
        "w_uv": uv.reshape(depth, B_KV_RANK, B_WIDTH).astype(BF16),
        "w_out": w_out,
        "norm2_g": norm2_g[:, None, :],
        "w_gate": ffn_w_gate,
        "w_up": ffn_w_up,
        "w_down": ffn_w_down,
    }


def kernel(x, c, positions, rel_bias, ada_w, ada_b, norm1_g, w_in, mla_q_norm, mla_kv_norm, mla_w_uq,
           mla_w_ukv, mix_gain, w_out, norm2_g, ffn_w_gate, ffn_w_up, ffn_w_down, final_norm):
    batch, seq, d = x.shape
    depth = w_in.shape[0]
    tm = 512
    assert seq % tm == 0 and seq % C_CHUNK == 0

    prep = _prepare(norm1_g, w_in, mla_q_norm, mla_kv_norm, mla_w_uq, mla_w_ukv, w_out, norm2_g,
                    ffn_w_gate, ffn_w_up, ffn_w_down)
    mods = _mods(c, ada_w, ada_b).reshape(depth, batch, 6, d)
    tables = _rope_tables(positions)
    bias = _bias_tables(rel_bias)
    ret_consts = _retention_consts()
    final_g = final_norm[None, :]

    xf = x.reshape(batch * seq, d)
    for l in range(depth):
        a_qkv, bq, bk, bv, cqk, cv, cg, *ffn_w = _inproj(xf, mods, l, prep, tables, batch, seq, tm)
        gain = mix_gain[l][None, :]
        mix_a = _dilated(a_qkv, bias, gain[:, :A_WIDTH], batch, seq)
        mix_b = _mla(bq, bk, bv, gain[:, A_WIDTH:A_WIDTH + B_WIDTH], batch, seq)
        mix_c = _retention(cqk, cv, cg, gain[:, A_WIDTH + B_WIDTH:], ret_consts, batch, seq)
        xf = _outffn(xf, mix_a, mix_b, mix_c, mods, l, prep, ffn_w, final_g, batch, seq, tm,
                     final=(l == depth - 1))
    return xf.reshape(batch, seq, d)
```

```python
import functools

import numpy as np
import jax
import jax.numpy as jnp
from jax import lax
from jax.experimental import pallas as pl
from jax.experimental.pallas import tpu as pltpu

F32 = jnp.float32
BF16 = jnp.bfloat16

HEAD_DIM = 64
A_HEADS = 6
A_PATTERNS = ((128, 1), (512, 4), (2048, 16))
A_WIDTH = A_HEADS * HEAD_DIM
B_HEADS = 4
B_NOPE = 64
B_ROPE = 32
B_V = 64
B_Q_RANK = 256
B_KV_RANK = 128
B_WIDTH = B_HEADS * B_V
C_HEADS = 6
C_KEY = 32
C_VAL = 64
C_WIDTH = C_HEADS * C_VAL
C_QK = C_HEADS * C_KEY
C_CHUNK = 128
N_BUCKETS = 32
MAX_DISTANCE = 2048
ROPE_BASE = 10000.0
EPS = 1e-6

LANES = 128
WIN = 128
STEP = 4
NEG = -1e30
LOG2E = 1.4426950408889634
FFN_CAST = ("w_out", "w_gate", "w_up", "w_down")
VMEM_LIMIT = 56 * 1024 * 1024
MXU_WIDTH = 256
ROW_TILE = 512
MLA_BLOCK = 512
MODS_COL_TILE = 1536
ROPE_ROW_TILE = 2048
FFN_SPLIT = 2

OFF_A = 0
OFF_BQ = 3 * A_WIDTH
OFF_BKV = OFF_BQ + B_Q_RANK
OFF_C = OFF_BKV + B_KV_RANK
C_QPAD = 2 * LANES
OFF_CV = OFF_C + 2 * C_QPAD
OFF_CG = OFF_CV + C_WIDTH
IN_COLS = OFF_CG + C_WIDTH


def _const_spec(shape, index_map):
    return pl.BlockSpec(shape, index_map, pipeline_mode=pl.Buffered(1))


def _silu(x):
    return x / (1.0 + jnp.exp(-x))


def _rms(x):
    return x * lax.rsqrt(jnp.mean(x * x, axis=-1, keepdims=True) + EPS)


def _mods_kernel(c_ref, w_ref, b_ref, o_ref):
    def split(a):
        hi = a.astype(BF16)
        return hi, (a - hi.astype(F32)).astype(BF16)

    c_hi, c_lo = split(_silu(c_ref[...]))
    w_hi, w_lo = split(w_ref[...])
    nb = c_hi.shape[0]
    both = jnp.dot(jnp.concatenate([c_hi, c_lo], axis=0), w_hi, preferred_element_type=F32)
    o_ref[...] = (both[:nb] + both[nb:] + jnp.dot(c_hi, w_lo, preferred_element_type=F32)) + b_ref[...]


def _mods(c, ada_w, ada_b):
    depth, d, n = ada_w.shape
    b = c.shape[0]
    tn = MODS_COL_TILE
    return pl.pallas_call(
        _mods_kernel,
        grid=(depth, n // tn),
        in_specs=[
            pl.BlockSpec((b, d), lambda l, j: (0, 0)),
            pl.BlockSpec((None, d, tn), lambda l, j: (l, 0, j)),
            pl.BlockSpec((None, 1, tn), lambda l, j: (l, 0, j)),
        ],
        out_specs=pl.BlockSpec((None, b, tn), lambda l, j: (l, 0, j)),
        out_shape=jax.ShapeDtypeStruct((depth, b, n), F32),
        compiler_params=pltpu.CompilerParams(vmem_limit_bytes=VMEM_LIMIT),
        name="adaln_mods",
    )(c, ada_w, ada_b.reshape(depth, 1, n))


def _rope_kernel(pos_ref, f_ref, s_ref, cb_ref, sb_ref, cc_ref, sc_ref):
    p = pos_ref[...].astype(F32)
    ang = p * f_ref[1:2, :]
    cos = jnp.cos(ang)
    sin = jnp.sin(ang)
    rope_b = f_ref[0:1, :] > 0.0
    cb_ref[...] = jnp.where(rope_b, cos, 1.0)
    sb_ref[...] = jnp.where(rope_b, sin, 0.0) * s_ref[0:1, :]
    cc_ref[...] = cos
    sc_ref[...] = sin * s_ref[1:2, :]


def _rope_tables(positions):
    t = positions.size
    half = B_ROPE // 2
    inv_freq = (1.0 / (ROPE_BASE ** (np.arange(half, dtype=np.float32) / half))).astype(np.float32)
    ones = np.ones(half, np.float32)
    f_b = np.concatenate([np.zeros(B_NOPE, np.float32), inv_freq, inv_freq, np.zeros(32, np.float32)])
    s_b = np.concatenate([np.zeros(B_NOPE, np.float32), -ones, ones, np.zeros(32, np.float32)])
    f_c = np.tile(np.concatenate([inv_freq, inv_freq]), LANES // C_KEY)
    s_c = np.tile(np.concatenate([-ones, ones]), LANES // C_KEY)
    freqs = jnp.asarray(np.stack([f_b, f_c]))
    signs = jnp.asarray(np.stack([s_b, s_c]))
    tr = ROPE_ROW_TILE
    out = jax.ShapeDtypeStruct((t, LANES), F32)
    row = pl.BlockSpec((tr, LANES), lambda i: (i, 0))
    return pl.pallas_call(
        _rope_kernel,
        grid=(t // tr,),
        in_specs=[
            pl.BlockSpec((tr, 1), lambda i: (i, 0)),
            pl.BlockSpec((2, LANES), lambda i: (0, 0)),
            pl.BlockSpec((2, LANES), lambda i: (0, 0)),
        ],
        out_specs=[row, row, row, row],
        out_shape=[out, out, out, out],
        name="rope_tables",
    )(positions.reshape(t, 1), freqs, signs)


def _t5_bucket(dist):
    max_exact = N_BUCKETS // 2
    safe = np.maximum(dist, 1).astype(np.float32)
    large = max_exact + (np.log(safe / max_exact) / np.log(MAX_DISTANCE / max_exact)
                         * (N_BUCKETS - max_exact)).astype(np.int32)
    large = np.minimum(large, N_BUCKETS - 1)
    return np.where(dist < max_exact, dist, large).astype(np.int32)


def _bias_kernel(bmap_ref, rb_ref, o_ref):
    h = pl.program_id(1)
    bm = bmap_ref[...]
    t = jnp.full(bm.shape, NEG, F32)
    for b in range(N_BUCKETS):
        t = jnp.where(bm == b, rb_ref[b, h] * LOG2E, t)
    o_ref[...] = t


def _bias_tables(rel_bias):
    qi = np.arange(WIN)[None, :]
    c = np.arange(2 * WIN)[:, None]
    j = qi - c + WIN
    maps = []
    for (w, d) in A_PATTERNS:
        assert w // d == WIN
        bucket = _t5_bucket(np.arange(WIN + 1, dtype=np.int32) * d)
        maps.append(np.where((j >= 0) & (j <= WIN), bucket[np.clip(j, 0, WIN)], -1).astype(np.int32))
    bmap = jnp.asarray(np.stack(maps))
    npat = len(A_PATTERNS)
    return pl.pallas_call(
        _bias_kernel,
        grid=(npat, A_HEADS),
        in_specs=[
            pl.BlockSpec((None, 2 * WIN, WIN), lambda p, h: (p, 0, 0)),
            pl.BlockSpec(memory_space=pltpu.SMEM),
        ],
        out_specs=pl.BlockSpec((None, 2 * WIN, WIN), lambda p, h: (p, 0, h)),
        out_shape=jax.ShapeDtypeStruct((npat, 2 * WIN, A_HEADS * WIN), F32),
        name="t5_bias_tables",
    )(bmap, rel_bias)


def _rot_half(x, x1_mask):
    w = x.shape[-1]
    half = B_ROPE // 2
    return jnp.where(x1_mask, pltpu.roll(x, w - half, 1), pltpu.roll(x, half, 1))


def _inproj_kernel(x_ref, mod_ref, g_ref, win_ref, qg_ref, kvg_ref, wuq_ref, wuk_ref, wuv_ref,
                   cb_ref, sb_ref, cc_ref, sc_ref, ks_ref, *refs):
    ncast = len(FFN_CAST)
    cast_in, refs = refs[:ncast], refs[ncast:]
    a_ref, bq_ref, bk_ref, bv_ref, cqk_ref, cv_ref, cg_ref = refs[:7]
    cast_out, w_ref = refs[7:7 + ncast], refs[7 + ncast]
    for src, dst in zip(cast_in, cast_out):
        dst[...] = src[...].astype(BF16)

    @pl.when((pl.program_id(0) == 0) & (pl.program_id(1) == 0))
    def _():
        d = win_ref.shape[1]
        rope0 = OFF_C
        c0 = rope0 + B_ROPE
        zeros = lambda n: jnp.zeros((n, d), F32)
        chunk = MXU_WIDTH
        for r0 in range(0, OFF_C, chunk):
            w_ref[:, r0:r0 + chunk] = win_ref[r0:r0 + chunk, :].T.astype(BF16)
        q_rows = jnp.concatenate([win_ref[c0:c0 + C_QK, :], win_ref[rope0:c0, :],
                                  zeros(C_QPAD - C_QK - B_ROPE)], axis=0)
        w_ref[:, OFF_C:OFF_C + C_QPAD] = q_rows.T.astype(BF16)
        k_rows = jnp.concatenate([win_ref[c0 + C_QK:c0 + 2 * C_QK, :], zeros(C_QPAD - C_QK)], axis=0)
        w_ref[:, OFF_C + C_QPAD:OFF_CV] = k_rows.T.astype(BF16)
        for r0 in range(0, IN_COLS - OFF_CV, chunk):
            src = c0 + 2 * C_QK + r0
            w_ref[:, OFF_CV + r0:OFF_CV + r0 + chunk] = win_ref[src:src + chunk, :].T.astype(BF16)

    x = x_ref[...]
    tm = x.shape[0]
    h = (_rms(x) * g_ref[...] * (1.0 + mod_ref[1:2, :]) + mod_ref[0:1, :]).astype(BF16)

    def proj(lo, hi):
        return jnp.dot(h, w_ref[:, lo:hi], preferred_element_type=F32)

    lane = lax.broadcasted_iota(jnp.int32, (tm, LANES), 1)
    b_x1 = (lane >= B_NOPE) & (lane < B_NOPE + B_ROPE // 2)
    c_x1 = (lane & (C_KEY // 2)) == 0
    cos_b = cb_ref[...]
    sin_b = sb_ref[...]

    q_lat = proj(OFF_BQ, OFF_BKV)
    kv_lat = proj(OFF_BKV, OFF_C)
    qk = proj(OFF_C, OFF_CV)
    qn = (_rms(q_lat) * qg_ref[...]).astype(BF16)
    kvn = (_rms(kv_lat) * kvg_ref[...]).astype(BF16)

    a_ref[:, 0:A_WIDTH] = (proj(OFF_A, OFF_A + A_WIDTH) * (HEAD_DIM ** -0.5 * LOG2E)).astype(BF16)
    a_ref[:, A_WIDTH:3 * A_WIDTH] = proj(OFF_A + A_WIDTH, OFF_BQ).astype(BF16)

    q = jnp.dot(qn, wuq_ref[...], preferred_element_type=F32)
    k_nope = jnp.dot(kvn, wuk_ref[...], preferred_element_type=F32)
    bv_ref[...] = jnp.dot(kvn, wuv_ref[...], preferred_element_type=F32).astype(BF16)
    cv_ref[...] = proj(OFF_CV, OFF_CG).astype(BF16)
    cg_ref[...] = _silu(proj(OFF_CG, IN_COLS)).astype(BF16)

    k_rope = jnp.where(lane >= B_NOPE, qk[:, LANES:2 * LANES], 0.0)
    k_pe = k_rope * cos_b + _rot_half(k_rope, b_x1) * sin_b
    b_scale = (B_NOPE + B_ROPE) ** -0.5 * LOG2E
    for hh in range(B_HEADS):
        sl = slice(hh * LANES, (hh + 1) * LANES)
        qh = q[:, sl]
        bq_ref[:, sl] = ((qh * cos_b + _rot_half(qh, b_x1) * sin_b) * b_scale).astype(BF16)
        bk_ref[:, sl] = (k_nope[:, sl] + k_pe).astype(BF16)

    cos_c = cc_ref[...]
    sin_c = sc_ref[...]
    for t in range(2 * C_QPAD // LANES):
        sl = slice(t * LANES, (t + 1) * LANES)
        xt = qk[:, sl]
        cqk_ref[:, sl] = ((xt * cos_c + _rot_half(xt, c_x1) * sin_c) * ks_ref[:, sl]).astype(BF16)


def _inproj(x, mods, layer, prep, tables, batch, seq, tm):
    t, d = x.shape
    nt = seq // tm
    row = lambda w: pl.BlockSpec((tm, w), lambda b, i: (b * nt + i, 0))
    wl = lambda a: _const_spec((None,) + a.shape[1:], lambda b, i: (layer,) + (0,) * (a.ndim - 1))
    cos_b, sin_b, cos_c, sin_c = tables
    kscale = jnp.asarray(np.concatenate([np.ones(C_QPAD, np.float32),
                                         np.full(C_QPAD, C_KEY ** -0.5, np.float32)])[None, :])
    outs = [(3 * A_WIDTH, BF16), (B_HEADS * LANES, BF16), (B_HEADS * LANES, BF16), (B_WIDTH, BF16),
            (2 * C_QPAD, BF16), (C_WIDTH, BF16), (C_WIDTH, BF16)]
    cast_ws = [prep[name] for name in FFN_CAST]
    cast_specs = []
    for w in cast_ws:
        nslab = batch * nt
        while w.shape[1] % (16 * nslab):
            assert nslab % 2 == 0
            nslab //= 2
        rep = batch * nt // nslab
        cast_specs.append(pl.BlockSpec((None, w.shape[1] // nslab, w.shape[2]),
                                       lambda b, i, rep=rep: (layer, (b * nt + i) // rep, 0)))
    cast_out_specs = [pl.BlockSpec(s.block_shape[1:], lambda b, i, f=s.index_map: f(b, i)[1:]) for s in cast_specs]
    return pl.pallas_call(
        _inproj_kernel,
        grid=(batch, nt),
        in_specs=[
            row(d),
            pl.BlockSpec((None, None, 6, d), lambda b, i: (layer, b, 0, 0)),
            wl(prep["norm1_g"]), wl(prep["w_in"]), wl(prep["q_norm"]), wl(prep["kv_norm"]),
            wl(prep["w_uq"]), wl(prep["w_uk"]), wl(prep["w_uv"]),
            row(LANES), row(LANES), row(LANES), row(LANES),
            _const_spec((1, 2 * C_QPAD), lambda b, i: (0, 0)),
            *cast_specs,
        ],
        out_specs=[row(w) for w, _ in outs] + cast_out_specs,
        out_shape=([jax.ShapeDtypeStruct((t, w), dt) for w, dt in outs]
                   + [jax.ShapeDtypeStruct(w.shape[1:], BF16) for w in cast_ws]),
        scratch_shapes=[pltpu.VMEM((d, IN_COLS), BF16)],
        compiler_params=pltpu.CompilerParams(
            dimension_semantics=("arbitrary", "arbitrary"), vmem_limit_bytes=VMEM_LIMIT),
        name="inproj",
    )(x, mods, prep["norm1_g"], prep["w_in"], prep["q_norm"], prep["kv_norm"],
      prep["w_uq"], prep["w_uk"], prep["w_uv"], cos_b, sin_b, cos_c, sin_c, kscale, *cast_ws)


def _class_major_blocks(seq, level):
    n = seq // STEP
    perm = np.arange(seq)
    for _ in range(level):
        perm = perm.reshape(n, STEP).T.reshape(seq)
    blocks = perm.reshape(seq // WIN, WIN)
    stride = STEP ** level
    assert (blocks == blocks[:, :1] + stride * np.arange(WIN)).all()
    return [(int(b[0]), stride) for b in blocks]


def _dilated_kernel(q_ref, k_ref, v_ref, bias_ref, gain_ref, o_ref,
                    qf, kf, vf, qlo, qhi, kp, vt, *stage_refs, seq):
    npat = len(A_PATTERNS)
    nblk = seq // WIN
    s_sc, out_s, lse_s = (stage_refs[i * npat:(i + 1) * npat] for i in range(3))
    lane = lax.broadcasted_iota(jnp.int32, (WIN, LANES), 1)
    lo_lane = lane < HEAD_DIM
    lo_row = lax.broadcasted_iota(jnp.int32, (LANES, WIN), 0) < HEAD_DIM

    qf[0] = q_ref[...].astype(F32)
    kf[0] = k_ref[...].astype(F32)
    vf[0] = v_ref[...].astype(F32)

    n = seq // STEP
    for p in range(npat):
        for r in range(STEP):
            rows = slice(r * n, (r + 1) * n)
            src = rows if p == 0 else pl.ds(r, n, stride=STEP)
            q, k, v = (f[(p - 1) % 2 if p else 0, src, :] for f in (qf, kf, vf))
            if 0 < p < npat - 1:
                qf[p % 2, rows, :], kf[p % 2, rows, :], vf[p % 2, rows, :] = q, k, v
            lo_q = lax.broadcasted_iota(jnp.int32, q.shape, 1) < HEAD_DIM
            qlo[p, rows, :] = jnp.where(lo_q, q, 0.0).astype(BF16)
            qhi[p, rows, :] = jnp.where(lo_q, 0.0, q).astype(BF16)
            vt[p, :, rows] = v.T.astype(BF16)
            if p > 0:
                kp[p - 1, rows, :] = k.astype(BF16)

    def key_rows(p, blk):
        r0 = blk * WIN
        first = blk % (nblk // A_PATTERNS[p][1]) == 0
        return (r0 if first else r0 - WIN), r0 + WIN

    for p in range(npat - 1, -1, -1):
        for blk in range(nblk):
            k0, k1 = key_rows(p, blk)
            r0 = blk * WIN
            q2 = jnp.concatenate([qlo[p, r0:r0 + WIN, :], qhi[p, r0:r0 + WIN, :]], axis=0)
            kcat = k_ref[k0:k1, :] if p == 0 else kp[p - 1, k0:k1, :]
            bias = bias_ref[p, 2 * WIN - (k1 - k0):, :]
            s_sc[p][blk, 0:k1 - k0, :] = lax.dot_general(
                kcat, q2, (((1,), (1,)), ((), ())), preferred_element_type=F32) + bias
        for blk, (start, stride) in enumerate(_class_major_blocks(seq, p)):
            k0, k1 = key_rows(p, blk)
            s = s_sc[p][blk, 0:k1 - k0, :]
            m = jnp.max(s, axis=0, keepdims=True)
            e = jnp.exp2(s - m)
            l = jnp.sum(e, axis=0, keepdims=True)
            acc = jnp.dot(vt[p, :, k0:k1], e.astype(BF16), preferred_element_type=F32)
            out = acc * (1.0 / l)
            lse = m + jnp.log2(l)
            out = jnp.where(lo_row, out[:, :WIN], out[:, WIN:])
            lse = jnp.where(lo_row, lse[:, :WIN], lse[:, WIN:])
            dst = pl.ds(start, WIN, stride=stride) if stride > 1 else slice(start, start + WIN)
            out_s[p][dst, :] = out.T
            lse_s[p][dst, :] = lse.T

    gain = gain_ref[...]
    for blk in range(nblk):
        sl = slice(blk * WIN, (blk + 1) * WIN)
        lses = [s[sl, :] for s in lse_s]
        mx = functools.reduce(jnp.maximum, lses)
        ws = [jnp.exp2(s - mx) for s in lses]
        o = sum(w * x[sl, :] for w, x in zip(ws, out_s)) / sum(ws)
        sq = o * o
        ms_lo = jnp.sum(jnp.where(lo_lane, sq, 0.0), axis=1, keepdims=True)
        ms_hi = jnp.sum(jnp.where(lo_lane, 0.0, sq), axis=1, keepdims=True)
        inv = lax.rsqrt(jnp.where(lo_lane, ms_lo, ms_hi) * (1.0 / HEAD_DIM) + EPS)
        o_ref[sl, :] = (o * inv * gain).astype(o_ref.dtype)


def _dilated(a_qkv, bias, gain_a, batch, seq):
    t = a_qkv.shape[0]
    pairs = A_WIDTH // LANES
    assert all(d == STEP ** p and seq % (d * WIN) == 0 for p, (_, d) in enumerate(A_PATTERNS))
    npat = bias.shape[0]
    col = lambda off: pl.BlockSpec((seq, LANES), lambda b, g: (b, off + g))
    scratch = ([pltpu.VMEM((2, seq, LANES), F32) for _ in range(3)]
               + [pltpu.VMEM((npat, seq, LANES), BF16) for _ in range(2)]
               + [pltpu.VMEM((npat - 1, seq, LANES), BF16)]
               + [pltpu.VMEM((npat, LANES, seq), BF16)]
               + [pltpu.VMEM((seq // WIN, 2 * WIN, 2 * WIN), F32) for _ in range(npat)]
               + [pltpu.VMEM((seq, LANES), F32) for _ in range(2 * npat)])
    return pl.pallas_call(
        functools.partial(_dilated_kernel, seq=seq),
        grid=(batch, pairs),
        in_specs=[
            col(0), col(pairs), col(2 * pairs),
            pl.BlockSpec((npat, 2 * WIN, 2 * WIN), lambda b, g: (0, 0, g)),
            pl.BlockSpec((1, LANES), lambda b, g: (0, g)),
        ],
        out_specs=pl.BlockSpec((seq, LANES), lambda b, g: (b, g)),
        out_shape=jax.ShapeDtypeStruct((t, A_WIDTH), BF16),
        scratch_shapes=scratch,
        compiler_params=pltpu.CompilerParams(
            dimension_semantics=("arbitrary", "arbitrary"), vmem_limit_bytes=VMEM_LIMIT),
        name="dilated_attention",
    )(a_qkv, a_qkv, a_qkv, bias, gain_a)


def _mla_kernel(q_ref, k_ref, v_ref, gain_ref, o_ref, vt, s_sc, e_sc, *, seq, tq):
    key_idx = lax.broadcasted_iota(jnp.int32, (tq, tq), 0)
    qry_idx = lax.broadcasted_iota(jnp.int32, (tq, tq), 1)
    causal = key_idx <= qry_idx
    gain = gain_ref[...]
    for c in range(0, seq, tq):
        vt[:, c:c + tq] = v_ref[c:c + tq, :].astype(F32).T.astype(BF16)

    units = [(i, hh) for i in range(seq // tq) for hh in range(2)]
    base = np.concatenate([[0], np.cumsum([i + 1 for i, _ in units])]).tolist()

    def logits(u):
        i, hh = units[u]
        hs = slice(hh * LANES, (hh + 1) * LANES)
        q = q_ref[i * tq:(i + 1) * tq, hs]
        m = None
        for j in range(i + 1):
            s = lax.dot_general(k_ref[j * tq:(j + 1) * tq, hs], q, (((1,), (1,)), ((), ())),
                                preferred_element_type=F32)
            if j == i:
                s = jnp.where(causal, s, NEG)
            s_sc[base[u] + j] = s
            bm = jnp.max(s, axis=0, keepdims=True)
            m = bm if m is None else jnp.maximum(m, bm)
        return m

    def attend(u, m):
        i, hh = units[u]
        l = None
        for j in range(i + 1):
            e = jnp.exp2(s_sc[base[u] + j] - m)
            ls = jnp.sum(e, axis=0, keepdims=True)
            l = ls if l is None else l + ls
            e_sc[(base[u] + j) * tq:(base[u] + j + 1) * tq, :] = e.astype(BF16)
        acc = jnp.dot(vt[hh * B_V:(hh + 1) * B_V, 0:(i + 1) * tq], e_sc[base[u] * tq:base[u + 1] * tq, :],
                      preferred_element_type=F32)
        o = acc / l
        return o * lax.rsqrt(jnp.mean(o * o, axis=0, keepdims=True) + EPS)

    m_next = logits(0)
    normed = []
    for u, (i, hh) in enumerate(units):
        m_cur = m_next
        if u + 1 < len(units):
            m_next = logits(u + 1)
        normed.append(attend(u, m_cur))
        if hh == 1:
            o_ref[i * tq:(i + 1) * tq, :] = (jnp.concatenate(normed, axis=0).T * gain).astype(o_ref.dtype)
            normed = []


def _mla(bq, bk, bv, gain_b, batch, seq):
    t = bq.shape[0]
    pairs = B_HEADS // 2
    tq = MLA_BLOCK
    assert seq % tq == 0
    nq = seq // tq
    nslots = 2 * (nq * (nq + 1) // 2)
    return pl.pallas_call(
        functools.partial(_mla_kernel, seq=seq, tq=tq),
        grid=(batch, pairs),
        in_specs=[
            pl.BlockSpec((seq, 2 * LANES), lambda b, g: (b, g)),
            pl.BlockSpec((seq, 2 * LANES), lambda b, g: (b, g)),
            pl.BlockSpec((seq, LANES), lambda b, g: (b, g)),
            pl.BlockSpec((1, LANES), lambda b, g: (0, g)),
        ],
        out_specs=pl.BlockSpec((seq, LANES), lambda b, g: (b, g)),
        out_shape=jax.ShapeDtypeStruct((t, B_WIDTH), BF16),
        scratch_shapes=[pltpu.VMEM((LANES, seq), BF16),
                        pltpu.VMEM((nslots, tq, tq), F32), pltpu.VMEM((nslots * tq, tq), BF16)],
        compiler_params=pltpu.CompilerParams(
            dimension_semantics=("arbitrary", "arbitrary"), vmem_limit_bytes=VMEM_LIMIT),
        name="latent_attention",
    )(bq, bk, bv, gain_b)


def _retention_consts():
    h = C_HEADS
    log_g = np.log(1.0 - 2.0 ** (-5.0 - np.arange(h))).astype(np.float32)
    i = np.arange(C_CHUNK, dtype=np.float32)
    rel = i[:, None] - i[None, :]
    decay = (np.exp(np.maximum(rel, 0.0)[None] * log_g[:, None, None]) * (rel >= 0)[None]).astype(np.float32)
    xi = np.exp((i + 1.0)[None, :] * log_g[:, None]).astype(np.float32)
    zeta = np.exp((C_CHUNK - 1.0 - i)[None, :] * log_g[:, None]).astype(np.float32)
    chunk_decay = np.exp(C_CHUNK * log_g).astype(np.float32)
    decay_all = np.concatenate(list(decay), axis=1)
    xi_mat = np.repeat(xi.T, C_VAL, axis=1)
    zeta_t = np.zeros((C_QPAD, C_CHUNK), np.float32)
    zeta_t[:C_QK] = np.repeat(zeta, C_KEY, axis=0)
    cd = np.repeat(chunk_decay, C_VAL)[None, :]
    bd = np.zeros((C_QPAD, C_WIDTH), np.float32)
    for hh in range(h):
        bd[hh * C_KEY:(hh + 1) * C_KEY, hh * C_VAL:(hh + 1) * C_VAL] = 1.0
    return tuple(jnp.asarray(a) for a in (decay_all, xi_mat, zeta_t, cd, bd))


def _retention_kernel(qk_ref, v_ref, g_ref, decay_ref, xi_ref, zeta_ref, cd_ref, bd_ref, gain_ref,
                      o_ref, *, seq):
    lane = lax.broadcasted_iota(jnp.int32, (C_CHUNK, LANES), 1)
    lo = lane < C_VAL
    gain = gain_ref[...]
    state = jnp.zeros((C_QPAD, C_WIDTH), F32)
    kshape, vshape = (C_QPAD, C_HEADS * C_CHUNK), (C_HEADS * C_CHUNK, C_WIDTH)
    k_on = (lax.broadcasted_iota(jnp.int32, kshape, 0) // C_KEY) == (lax.broadcasted_iota(jnp.int32, kshape, 1) // C_CHUNK)
    v_on = (lax.broadcasted_iota(jnp.int32, vshape, 0) // C_CHUNK) == (lax.broadcasted_iota(jnp.int32, vshape, 1) // C_VAL)

    for n in range(seq // C_CHUNK):
        rows = slice(n * C_CHUNK, (n + 1) * C_CHUNK)
        q = qk_ref[rows, 0:C_QPAD]
        k_t = qk_ref[rows, C_QPAD:2 * C_QPAD].astype(F32).T
        v = v_ref[rows, :]
        k_bd = jnp.where(k_on, jnp.concatenate([k_t] * C_HEADS, axis=1), 0.0).astype(BF16)
        s = jnp.dot(q, k_bd, preferred_element_type=F32) * decay_ref[...]
        v_bd = jnp.where(v_on, jnp.concatenate([v] * C_HEADS, axis=0), jnp.zeros((), BF16))
        o = (jnp.dot(s.astype(BF16), v_bd, preferred_element_type=F32)
             + jnp.dot(q, state.astype(BF16), preferred_element_type=F32) * xi_ref[...])
        upd = jnp.dot((k_t * zeta_ref[...]).astype(BF16), v, preferred_element_type=F32)
        state = state * cd_ref[...] + upd * bd_ref[...]
        for t in range(C_WIDTH // LANES):
            sl = slice(t * LANES, (t + 1) * LANES)
            x = o[:, sl]
            mu_lo = jnp.sum(jnp.where(lo, x, 0.0), axis=1, keepdims=True)
            mu_hi = jnp.sum(jnp.where(lo, 0.0, x), axis=1, keepdims=True)
            dlt = x - jnp.where(lo, mu_lo, mu_hi) * (1.0 / C_VAL)
            sq = dlt * dlt
            var_lo = jnp.sum(jnp.where(lo, sq, 0.0), axis=1, keepdims=True)
            var_hi = jnp.sum(jnp.where(lo, 0.0, sq), axis=1, keepdims=True)
            y = dlt * lax.rsqrt(jnp.where(lo, var_lo, var_hi) * (1.0 / C_VAL) + EPS)
            o_ref[rows, sl] = (y * g_ref[rows, sl].astype(F32) * gain[:, sl]).astype(o_ref.dtype)


def _retention(cqk, cv, cg, gain_c, consts, batch, seq):
    t = cqk.shape[0]
    tok = lambda width: pl.BlockSpec((seq, width), lambda b: (b, 0))
    full = lambda a: _const_spec(a.shape, lambda b: (0,) * a.ndim)
    return pl.pallas_call(
        functools.partial(_retention_kernel, seq=seq),
        grid=(batch,),
        in_specs=[tok(2 * C_QPAD), tok(C_WIDTH), tok(C_WIDTH)] + [full(a) for a in consts] + [full(gain_c)],
        out_specs=tok(C_WIDTH),
        out_shape=jax.ShapeDtypeStruct((t, C_WIDTH), BF16),
        compiler_params=pltpu.CompilerParams(
            dimension_semantics=("arbitrary",), vmem_limit_bytes=VMEM_LIMIT),
        name="retention",
    )(cqk, cv, cg, *consts, gain_c)


def _outffn_kernel(x_ref, ma_ref, mb_ref, mc_ref, mod_ref, wo_ref, g_ref, wg_ref, wu_ref, wd_ref, fg_ref,
                   o_ref, mix_ref, h_ref, hid_ref, *, final, fchunk, nsplit):
    tm = x_ref.shape[0]
    parts = [slice(k * tm // nsplit, (k + 1) * tm // nsplit) for k in range(nsplit)]
    mix_ref[:, 0:A_WIDTH] = ma_ref[...]
    mix_ref[:, A_WIDTH:A_WIDTH + B_WIDTH] = mb_ref[...]
    mix_ref[:, A_WIDTH + B_WIDTH:] = mc_ref[...]
    for r in parts:
        att = jnp.dot(mix_ref[r, :], wo_ref[...], preferred_element_type=F32)
        o_ref[r, :] = x_ref[r, :] + mod_ref[2:3, :] * att
    for r in parts:
        h_ref[r, :] = (_rms(o_ref[r, :]) * g_ref[...] * (1.0 + mod_ref[4:5, :]) + mod_ref[3:4, :]).astype(BF16)
    hidden = wg_ref.shape[1]
    for r in parts:
        for j in range(hidden // fchunk):
            sl = slice(j * fchunk, (j + 1) * fchunk)
            gate = jnp.dot(h_ref[r, :], wg_ref[:, sl], preferred_element_type=F32)
            up = jnp.dot(h_ref[r, :], wu_ref[:, sl], preferred_element_type=F32)
            hid_ref[r, sl] = (_silu(gate) * up).astype(BF16)
    for r in parts:
        ffn = jnp.dot(hid_ref[r, :], wd_ref[...], preferred_element_type=F32)
        y = o_ref[r, :] + mod_ref[5:6, :] * ffn
        if final:
            y = _rms(y) * fg_ref[...]
        o_ref[r, :] = y


def _outffn(x, mix_a, mix_b, mix_c, mods, layer, prep, ffn_w, final_g, batch, seq, tm, final):
    t, d = x.shape
    nt = seq // tm
    w_out, w_gate, w_up, w_down = ffn_w
    hidden = w_gate.shape[1]
    whole = lambda a: _const_spec(a.shape, lambda b, i: (0,) * a.ndim)
    row = lambda w: pl.BlockSpec((tm, w), lambda b, i: (b * nt + i, 0))
    wl = lambda a: _const_spec((None,) + a.shape[1:], lambda b, i: (layer,) + (0,) * (a.ndim - 1))
    return pl.pallas_call(
        functools.partial(_outffn_kernel, final=final, fchunk=MXU_WIDTH, nsplit=FFN_SPLIT),
        grid=(batch, nt),
        in_specs=[
            row(d), row(A_WIDTH), row(B_WIDTH), row(C_WIDTH),
            pl.BlockSpec((None, None, 6, d), lambda b, i: (layer, b, 0, 0)),
            whole(w_out), wl(prep["norm2_g"]), whole(w_gate), whole(w_up), whole(w_down),
            _const_spec((1, d), lambda b, i: (0, 0)),
        ],
        out_specs=row(d),
        out_shape=jax.ShapeDtypeStruct((t, d), F32),
        scratch_shapes=[pltpu.VMEM((tm, d), BF16), pltpu.VMEM((tm, d), BF16), pltpu.VMEM((tm, hidden), BF16)],
        compiler_params=pltpu.CompilerParams(
            dimension_semantics=("arbitrary", "arbitrary"), vmem_limit_bytes=VMEM_LIMIT),
        name="outproj_ffn",
    )(x, mix_a, mix_b, mix_c, mods, w_out, prep["norm2_g"], w_gate, w_up, w_down, final_g)


def _prepare(norm1_g, w_in, mla_q_norm, mla_kv_norm, mla_w_uq, mla_w_ukv, w_out, norm2_g,
             ffn_w_gate, ffn_w_up, ffn_w_down):
    depth, d, _ = w_in.shape
    b_lat0 = 3 * A_WIDTH
    rope0 = b_lat0 + B_Q_RANK + B_KV_RANK
    c0 = rope0 + B_ROPE
    assert C_QK + B_ROPE <= C_QPAD and C_QK % LANES == B_NOPE
    assert c0 + 2 * C_QK + 2 * C_WIDTH == w_in.shape[2] and rope0 == OFF_C
    uq = mla_w_uq.reshape(depth, B_Q_RANK, B_HEADS, B_NOPE + B_ROPE)
    uq = jnp.pad(uq, ((0, 0), (0, 0), (0, 0), (0, LANES - B_NOPE - B_ROPE)))
    ukv = mla_w_ukv.reshape(depth, B_KV_RANK, B_HEADS, B_NOPE + B_V)
    uk = jnp.pad(ukv[..., :B_NOPE], ((0, 0), (0, 0), (0, 0), (0, LANES - B_NOPE)))
    uv = ukv[..., B_NOPE:]
    return {
        "norm1_g": norm1_g[:, None, :],
        "w_in": jnp.swapaxes(w_in, 1, 2),
        "q_norm": mla_q_norm[:, None, :],
        "kv_norm": mla_kv_norm[:, None, :],
        "w_uq": uq.reshape(depth, B_Q_RANK, B_HEADS * LANES).astype(BF16),
        "w_uk": uk.reshape(depth, B_KV_RANK, B_HEADS * LANES).astype(BF16),
        "w_uv": uv.reshape(depth, B_KV_RANK, B_WIDTH).astype(BF16),
        "w_out": w_out,
        "norm2_g": norm2_g[:, None, :],
        "w_gate": ffn_w_gate,
        "w_up": ffn_w_up,
        "w_down": ffn_w_down,
    }


def kernel(x, c, positions, rel_bias, ada_w, ada_b, norm1_g, w_in, mla_q_norm, mla_kv_norm, mla_w_uq,
           mla_w_ukv, mix_gain, w_out, norm2_g, ffn_w_gate, ffn_w_up, ffn_w_down, final_norm):
    batch, seq, d = x.shape
    depth = w_in.shape[0]
    tm = ROW_TILE
    assert seq % tm == 0 and seq % C_CHUNK == 0

    prep = _prepare(norm1_g, w_in, mla_q_norm, mla_kv_norm, mla_w_uq, mla_w_ukv, w_out, norm2_g,
                    ffn_w_gate, ffn_w_up, ffn_w_down)
    mods = _mods(c, ada_w, ada_b).reshape(depth, batch, 6, d)
    tables = _rope_tables(positions)
    bias = _bias_tables(rel_bias)
    ret_consts = _retention_consts()
    final_g = final_norm[None, :]

    xf = x.reshape(batch * seq, d)
    for l in range(depth):
        a_qkv, bq, bk, bv, cqk, cv, cg, *ffn_w = _inproj(xf, mods, l, prep, tables, batch, seq, tm)
        gain = mix_gain[l][None, :]
        mix_a = _dilated(a_qkv, bias, gain[:, :A_WIDTH], batch, seq)
        mix_b = _mla(bq, bk, bv, gain[:, A_WIDTH:A_WIDTH + B_WIDTH], batch, seq)
        mix_c = _retention(cqk, cv, cg, gain[:, A_WIDTH + B_WIDTH:], ret_consts, batch, seq)
        xf = _outffn(xf, mix_a, mix_b, mix_c, mods, l, prep, ffn_w, final_g, batch, seq, tm,
                     final=(l == depth - 1))
    return xf.reshape(batch, seq, d)
```

```python
import functools

import numpy as np
import jax
import jax.numpy as jnp
from jax import lax
from jax.experimental import pallas as pl
from jax.experimental.pallas import tpu as pltpu

F32 = jnp.float32
BF16 = jnp.bfloat16

HEAD_DIM = 64
A_HEADS = 6
A_PATTERNS = ((128, 1), (512, 4), (2048, 16))
A_WIDTH = A_HEADS * HEAD_DIM
B_HEADS = 4
B_NOPE = 64
B_ROPE = 32
B_V = 64
B_Q_RANK = 256
B_KV_RANK = 128
B_WIDTH = B_HEADS * B_V
C_HEADS = 6
C_KEY = 32
C_VAL = 64
C_WIDTH = C_HEADS * C_VAL
C_QK = C_HEADS * C_KEY
C_CHUNK = 128
N_BUCKETS = 32
MAX_DISTANCE = 2048
ROPE_BASE = 10000.0
EPS = 1e-6

LANES = 128
WIN = 128
STEP = 4
NEG = -1e30
LOG2E = 1.4426950408889634
FFN_CAST = ("w_out", "w_gate", "w_up", "w_down")
VMEM_LIMIT = 56 * 1024 * 1024
MXU_WIDTH = 256
ROW_TILE = 512
MLA_BLOCK = 512
MODS_COL_TILE = 1536
ROPE_ROW_TILE = 2048
FFN_SPLIT = 2
INPROJ_SPLIT = 2

OFF_A = 0
OFF_BQ = 3 * A_WIDTH
OFF_BKV = OFF_BQ + B_Q_RANK
OFF_C = OFF_BKV + B_KV_RANK
C_QPAD = 2 * LANES
OFF_CV = OFF_C + 2 * C_QPAD
OFF_CG = OFF_CV + C_WIDTH
IN_COLS = OFF_CG + C_WIDTH


def _const_spec(shape, index_map):
    return pl.BlockSpec(shape, index_map, pipeline_mode=pl.Buffered(1))


def _silu(x):
    return x / (1.0 + jnp.exp(-x))


def _rms(x):
    return x * lax.rsqrt(jnp.mean(x * x, axis=-1, keepdims=True) + EPS)


def _mods_kernel(c_ref, w_ref, b_ref, o_ref):
    def split(a):
        hi = a.astype(BF16)
        return hi, (a - hi.astype(F32)).astype(BF16)

    c_hi, c_lo = split(_silu(c_ref[...]))
    w_hi, w_lo = split(w_ref[...])
    nb = c_hi.shape[0]
    both = jnp.dot(jnp.concatenate([c_hi, c_lo], axis=0), w_hi, preferred_element_type=F32)
    o_ref[...] = (both[:nb] + both[nb:] + jnp.dot(c_hi, w_lo, preferred_element_type=F32)) + b_ref[...]


def _mods(c, ada_w, ada_b):
    depth, d, n = ada_w.shape
    b = c.shape[0]
    tn = MODS_COL_TILE
    return pl.pallas_call(
        _mods_kernel,
        grid=(depth, n // tn),
        in_specs=[
            pl.BlockSpec((b, d), lambda l, j: (0, 0)),
            pl.BlockSpec((None, d, tn), lambda l, j: (l, 0, j)),
            pl.BlockSpec((None, 1, tn), lambda l, j: (l, 0, j)),
        ],
        out_specs=pl.BlockSpec((None, b, tn), lambda l, j: (l, 0, j)),
        out_shape=jax.ShapeDtypeStruct((depth, b, n), F32),
        compiler_params=pltpu.CompilerParams(vmem_limit_bytes=VMEM_LIMIT),
        name="adaln_mods",
    )(c, ada_w, ada_b.reshape(depth, 1, n))


def _rope_kernel(pos_ref, f_ref, s_ref, cb_ref, sb_ref, cc_ref, sc_ref):
    p = pos_ref[...].astype(F32)
    ang = p * f_ref[1:2, :]
    cos = jnp.cos(ang)
    sin = jnp.sin(ang)
    rope_b = f_ref[0:1, :] > 0.0
    cb_ref[...] = jnp.where(rope_b, cos, 1.0)
    sb_ref[...] = jnp.where(rope_b, sin, 0.0) * s_ref[0:1, :]
    cc_ref[...] = cos
    sc_ref[...] = sin * s_ref[1:2, :]


def _rope_tables(positions):
    t = positions.size
    half = B_ROPE // 2
    inv_freq = (1.0 / (ROPE_BASE ** (np.arange(half, dtype=np.float32) / half))).astype(np.float32)
    ones = np.ones(half, np.float32)
    f_b = np.concatenate([np.zeros(B_NOPE, np.float32), inv_freq, inv_freq, np.zeros(32, np.float32)])
    s_b = np.concatenate([np.zeros(B_NOPE, np.float32), -ones, ones, np.zeros(32, np.float32)])
    f_c = np.tile(np.concatenate([inv_freq, inv_freq]), LANES // C_KEY)
    s_c = np.tile(np.concatenate([-ones, ones]), LANES // C_KEY)
    freqs = jnp.asarray(np.stack([f_b, f_c]))
    signs = jnp.asarray(np.stack([s_b, s_c]))
    tr = ROPE_ROW_TILE
    out = jax.ShapeDtypeStruct((t, LANES), F32)
    row = pl.BlockSpec((tr, LANES), lambda i: (i, 0))
    return pl.pallas_call(
        _rope_kernel,
        grid=(t // tr,),
        in_specs=[
            pl.BlockSpec((tr, 1), lambda i: (i, 0)),
            pl.BlockSpec((2, LANES), lambda i: (0, 0)),
            pl.BlockSpec((2, LANES), lambda i: (0, 0)),
        ],
        out_specs=[row, row, row, row],
        out_shape=[out, out, out, out],
        name="rope_tables",
    )(positions.reshape(t, 1), freqs, signs)


def _t5_bucket(dist):
    max_exact = N_BUCKETS // 2
    safe = np.maximum(dist, 1).astype(np.float32)
    large = max_exact + (np.log(safe / max_exact) / np.log(MAX_DISTANCE / max_exact)
                         * (N_BUCKETS - max_exact)).astype(np.int32)
    large = np.minimum(large, N_BUCKETS - 1)
    return np.where(dist < max_exact, dist, large).astype(np.int32)


def _bias_kernel(bmap_ref, rb_ref, o_ref):
    h = pl.program_id(1)
    bm = bmap_ref[...]
    t = jnp.full(bm.shape, NEG, F32)
    for b in range(N_BUCKETS):
        t = jnp.where(bm == b, rb_ref[b, h] * LOG2E, t)
    o_ref[...] = t


def _bias_tables(rel_bias):
    qi = np.arange(WIN)[None, :]
    c = np.arange(2 * WIN)[:, None]
    j = qi - c + WIN
    maps = []
    for (w, d) in A_PATTERNS:
        assert w // d == WIN
        bucket = _t5_bucket(np.arange(WIN + 1, dtype=np.int32) * d)
        maps.append(np.where((j >= 0) & (j <= WIN), bucket[np.clip(j, 0, WIN)], -1).astype(np.int32))
    bmap = jnp.asarray(np.stack(maps))
    npat = len(A_PATTERNS)
    return pl.pallas_call(
        _bias_kernel,
        grid=(npat, A_HEADS),
        in_specs=[
            pl.BlockSpec((None, 2 * WIN, WIN), lambda p, h: (p, 0, 0)),
            pl.BlockSpec(memory_space=pltpu.SMEM),
        ],
        out_specs=pl.BlockSpec((None, 2 * WIN, WIN), lambda p, h: (p, 0, h)),
        out_shape=jax.ShapeDtypeStruct((npat, 2 * WIN, A_HEADS * WIN), F32),
        name="t5_bias_tables",
    )(bmap, rel_bias)


def _rot_half(x, x1_mask):
    w = x.shape[-1]
    half = B_ROPE // 2
    return jnp.where(x1_mask, pltpu.roll(x, w - half, 1), pltpu.roll(x, half, 1))


def _inproj_kernel(x_ref, mod_ref, g_ref, win_ref, qg_ref, kvg_ref, wuq_ref, wuk_ref, wuv_ref,
                   cb_ref, sb_ref, cc_ref, sc_ref, ks_ref, *refs):
    ncast = len(FFN_CAST)
    cast_in, refs = refs[:ncast], refs[ncast:]
    a_ref, bq_ref, bk_ref, bv_ref, cqk_ref, cv_ref, cg_ref = refs[:7]
    cast_out, w_ref = refs[7:7 + ncast], refs[7 + ncast]
    for src, dst in zip(cast_in, cast_out):
        dst[...] = src[...].astype(BF16)

    @pl.when((pl.program_id(0) == 0) & (pl.program_id(1) == 0))
    def _():
        d = win_ref.shape[1]
        rope0 = OFF_C
        c0 = rope0 + B_ROPE
        zeros = lambda n: jnp.zeros((n, d), F32)
        chunk = MXU_WIDTH
        for r0 in range(0, OFF_C, chunk):
            w_ref[:, r0:r0 + chunk] = win_ref[r0:r0 + chunk, :].T.astype(BF16)
        q_rows = jnp.concatenate([win_ref[c0:c0 + C_QK, :], win_ref[rope0:c0, :],
                                  zeros(C_QPAD - C_QK - B_ROPE)], axis=0)
        w_ref[:, OFF_C:OFF_C + C_QPAD] = q_rows.T.astype(BF16)
        k_rows = jnp.concatenate([win_ref[c0 + C_QK:c0 + 2 * C_QK, :], zeros(C_QPAD - C_QK)], axis=0)
        w_ref[:, OFF_C + C_QPAD:OFF_CV] = k_rows.T.astype(BF16)
        for r0 in range(0, IN_COLS - OFF_CV, chunk):
            src = c0 + 2 * C_QK + r0
            w_ref[:, OFF_CV + r0:OFF_CV + r0 + chunk] = win_ref[src:src + chunk, :].T.astype(BF16)

    tm = x_ref.shape[0] // INPROJ_SPLIT
    parts = [slice(k * tm, (k + 1) * tm) for k in range(INPROJ_SPLIT)]
    lane = lax.broadcasted_iota(jnp.int32, (tm, LANES), 1)
    b_x1 = (lane >= B_NOPE) & (lane < B_NOPE + B_ROPE // 2)
    c_x1 = (lane & (C_KEY // 2)) == 0

    def proj(h, lo, hi):
        return jnp.dot(h, w_ref[:, lo:hi], preferred_element_type=F32)

    hs = [(_rms(x_ref[r, :]) * g_ref[...] * (1.0 + mod_ref[1:2, :]) + mod_ref[0:1, :]).astype(BF16)
          for r in parts]
    lat = [(proj(h, OFF_BQ, OFF_BKV), proj(h, OFF_BKV, OFF_C), proj(h, OFF_C, OFF_CV)) for h in hs]
    normed = [((_rms(q_lat) * qg_ref[...]).astype(BF16), (_rms(kv_lat) * kvg_ref[...]).astype(BF16))
              for q_lat, kv_lat, _ in lat]
    for r, h in zip(parts, hs):
        a_ref[r, 0:A_WIDTH] = proj(h, OFF_A, OFF_A + A_WIDTH) * (HEAD_DIM ** -0.5 * LOG2E)
        a_ref[r, A_WIDTH:3 * A_WIDTH] = proj(h, OFF_A + A_WIDTH, OFF_BQ)
    ups = []
    for r, h, (qn, kvn) in zip(parts, hs, normed):
        ups.append((jnp.dot(qn, wuq_ref[...], preferred_element_type=F32),
                    jnp.dot(kvn, wuk_ref[...], preferred_element_type=F32)))
        bv_ref[r, :] = jnp.dot(kvn, wuv_ref[...], preferred_element_type=F32).astype(BF16)
        cv_ref[r, :] = proj(h, OFF_CV, OFF_CG).astype(BF16)
        cg_ref[r, :] = _silu(proj(h, OFF_CG, IN_COLS)).astype(BF16)
    b_scale = (B_NOPE + B_ROPE) ** -0.5 * LOG2E
    for r, (_, _, qk), (q, k_nope) in zip(parts, lat, ups):
        cos_b, sin_b, cos_c, sin_c = cb_ref[r, :], sb_ref[r, :], cc_ref[r, :], sc_ref[r, :]
        k_rope = jnp.where(lane >= B_NOPE, qk[:, LANES:2 * LANES], 0.0)
        k_pe = k_rope * cos_b + _rot_half(k_rope, b_x1) * sin_b
        for hh in range(B_HEADS):
            sl = slice(hh * LANES, (hh + 1) * LANES)
            qh = q[:, sl]
            bq_ref[r, sl] = ((qh * cos_b + _rot_half(qh, b_x1) * sin_b) * b_scale).astype(BF16)
            bk_ref[r, sl] = (k_nope[:, sl] + k_pe).astype(BF16)
        for t in range(2 * C_QPAD // LANES):
            sl = slice(t * LANES, (t + 1) * LANES)
            xt = qk[:, sl]
            cqk_ref[r, sl] = ((xt * cos_c + _rot_half(xt, c_x1) * sin_c) * ks_ref[:, sl]).astype(BF16)


def _inproj(x, mods, layer, prep, tables, batch, seq, tm):
    t, d = x.shape
    nt = seq // tm
    row = lambda w: pl.BlockSpec((tm, w), lambda b, i: (b * nt + i, 0))
    wl = lambda a: _const_spec((None,) + a.shape[1:], lambda b, i: (layer,) + (0,) * (a.ndim - 1))
    cos_b, sin_b, cos_c, sin_c = tables
    kscale = jnp.asarray(np.concatenate([np.ones(C_QPAD, np.float32),
                                         np.full(C_QPAD, C_KEY ** -0.5, np.float32)])[None, :])
    outs = [(3 * A_WIDTH, F32), (B_HEADS * LANES, BF16), (B_HEADS * LANES, BF16), (B_WIDTH, BF16),
            (2 * C_QPAD, BF16), (C_WIDTH, BF16), (C_WIDTH, BF16)]
    cast_ws = [prep[name] for name in FFN_CAST]
    cast_specs = []
    for w in cast_ws:
        nslab = batch * nt
        while w.shape[1] % (16 * nslab):
            assert nslab % 2 == 0
            nslab //= 2
        rep = batch * nt // nslab
        cast_specs.append(pl.BlockSpec((None, w.shape[1] // nslab, w.shape[2]),
                                       lambda b, i, rep=rep: (layer, (b * nt + i) // rep, 0)))
    cast_out_specs = [pl.BlockSpec(s.block_shape[1:], lambda b, i, f=s.index_map: f(b, i)[1:]) for s in cast_specs]
    return pl.pallas_call(
        _inproj_kernel,
        grid=(batch, nt),
        in_specs=[
            row(d),
            pl.BlockSpec((None, None, 6, d), lambda b, i: (layer, b, 0, 0)),
            wl(prep["norm1_g"]), wl(prep["w_in"]), wl(prep["q_norm"]), wl(prep["kv_norm"]),
            wl(prep["w_uq"]), wl(prep["w_uk"]), wl(prep["w_uv"]),
            row(LANES), row(LANES), row(LANES), row(LANES),
            _const_spec((1, 2 * C_QPAD), lambda b, i: (0, 0)),
            *cast_specs,
        ],
        out_specs=[row(w) for w, _ in outs] + cast_out_specs,
        out_shape=([jax.ShapeDtypeStruct((t, w), dt) for w, dt in outs]
                   + [jax.ShapeDtypeStruct(w.shape[1:], BF16) for w in cast_ws]),
        scratch_shapes=[pltpu.VMEM((d, IN_COLS), BF16)],
        compiler_params=pltpu.CompilerParams(
            dimension_semantics=("arbitrary", "arbitrary"), vmem_limit_bytes=VMEM_LIMIT),
        name="inproj",
    )(x, mods, prep["norm1_g"], prep["w_in"], prep["q_norm"], prep["kv_norm"],
      prep["w_uq"], prep["w_uk"], prep["w_uv"], cos_b, sin_b, cos_c, sin_c, kscale, *cast_ws)


def _class_major_blocks(seq, level):
    n = seq // STEP
    perm = np.arange(seq)
    for _ in range(level):
        perm = perm.reshape(n, STEP).T.reshape(seq)
    blocks = perm.reshape(seq // WIN, WIN)
    stride = STEP ** level
    assert (blocks == blocks[:, :1] + stride * np.arange(WIN)).all()
    return [(int(b[0]), stride) for b in blocks]


def _dilated_kernel(q_ref, k_ref, v_ref, bias_ref, gain_ref, o_ref,
                    qf, kf, vf, qlo, qhi, kp, vt, *stage_refs, seq):
    npat = len(A_PATTERNS)
    nblk = seq // WIN
    s_sc, out_s, lse_s = (stage_refs[i * npat:(i + 1) * npat] for i in range(3))
    lane = lax.broadcasted_iota(jnp.int32, (WIN, LANES), 1)
    lo_lane = lane < HEAD_DIM
    lo_row = lax.broadcasted_iota(jnp.int32, (LANES, WIN), 0) < HEAD_DIM

    n = seq // STEP
    for p in range(npat):
        for r in range(STEP):
            rows = slice(r * n, (r + 1) * n)
            src = rows if p == 0 else pl.ds(r, n, stride=STEP)
            level_below = (q_ref, k_ref, v_ref) if p <= 1 else (qf.at[p - 2], kf.at[p - 2], vf.at[p - 2])
            q, k, v = (f[src, :] for f in level_below)
            if 0 < p < npat - 1:
                qf[p - 1, rows, :], kf[p - 1, rows, :], vf[p - 1, rows, :] = q, k, v
            lo_q = lax.broadcasted_iota(jnp.int32, q.shape, 1) < HEAD_DIM
            qlo[p, rows, :] = jnp.where(lo_q, q, 0.0).astype(BF16)
            qhi[p, rows, :] = jnp.where(lo_q, 0.0, q).astype(BF16)
            vt[p, :, rows] = v.T.astype(BF16)
            kp[p, rows, :] = k.astype(BF16)

    def key_rows(p, blk):
        r0 = blk * WIN
        first = blk % (nblk // A_PATTERNS[p][1]) == 0
        return (r0 if first else r0 - WIN), r0 + WIN

    for p in range(npat - 1, -1, -1):
        for blk in range(nblk):
            k0, k1 = key_rows(p, blk)
            r0 = blk * WIN
            q2 = jnp.concatenate([qlo[p, r0:r0 + WIN, :], qhi[p, r0:r0 + WIN, :]], axis=0)
            kcat = kp[p, k0:k1, :]
            bias = bias_ref[p, 2 * WIN - (k1 - k0):, :]
            s_sc[p][blk, 0:k1 - k0, :] = lax.dot_general(
                kcat, q2, (((1,), (1,)), ((), ())), preferred_element_type=F32) + bias
        for blk, (start, stride) in enumerate(_class_major_blocks(seq, p)):
            k0, k1 = key_rows(p, blk)
            s = s_sc[p][blk, 0:k1 - k0, :]
            m = jnp.max(s, axis=0, keepdims=True)
            e = jnp.exp2(s - m)
            l = jnp.sum(e, axis=0, keepdims=True)
            acc = jnp.dot(vt[p, :, k0:k1], e.astype(BF16), preferred_element_type=F32)
            out = acc * (1.0 / l)
            lse = m + jnp.log2(l)
            out = jnp.where(lo_row, out[:, :WIN], out[:, WIN:])
            lse = jnp.where(lo_row, lse[:, :WIN], lse[:, WIN:])
            dst = pl.ds(start, WIN, stride=stride) if stride > 1 else slice(start, start + WIN)
            out_s[p][dst, :] = out.T
            lse_s[p][dst, :] = lse.T

    gain = gain_ref[...]
    for blk in range(nblk):
        sl = slice(blk * WIN, (blk + 1) * WIN)
        lses = [s[sl, :] for s in lse_s]
        mx = functools.reduce(jnp.maximum, lses)
        ws = [jnp.exp2(s - mx) for s in lses]
        o = sum(w * x[sl, :] for w, x in zip(ws, out_s)) / sum(ws)
        sq = o * o
        ms_lo = jnp.sum(jnp.where(lo_lane, sq, 0.0), axis=1, keepdims=True)
        ms_hi = jnp.sum(jnp.where(lo_lane, 0.0, sq), axis=1, keepdims=True)
        inv = lax.rsqrt(jnp.where(lo_lane, ms_lo, ms_hi) * (1.0 / HEAD_DIM) + EPS)
        o_ref[sl, :] = (o * inv * gain).astype(o_ref.dtype)


def _dilated(a_qkv, bias, gain_a, batch, seq):
    t = a_qkv.shape[0]
    pairs = A_WIDTH // LANES
    assert all(d == STEP ** p and seq % (d * WIN) == 0 for p, (_, d) in enumerate(A_PATTERNS))
    npat = bias.shape[0]
    col = lambda off: pl.BlockSpec((seq, LANES), lambda b, g: (b, off + g))
    scratch = ([pltpu.VMEM((npat - 2, seq, LANES), F32) for _ in range(3)]
               + [pltpu.VMEM((npat, seq, LANES), BF16) for _ in range(2)]
               + [pltpu.VMEM((npat, seq, LANES), BF16)]
               + [pltpu.VMEM((npat, LANES, seq), BF16)]
               + [pltpu.VMEM((seq // WIN, 2 * WIN, 2 * WIN), F32) for _ in range(npat)]
               + [pltpu.VMEM((seq, LANES), F32) for _ in range(2 * npat)])
    return pl.pallas_call(
        functools.partial(_dilated_kernel, seq=seq),
        grid=(batch, pairs),
        in_specs=[
            col(0), col(pairs), col(2 * pairs),
            pl.BlockSpec((npat, 2 * WIN, 2 * WIN), lambda b, g: (0, 0, g)),
            pl.BlockSpec((1, LANES), lambda b, g: (0, g)),
        ],
        out_specs=pl.BlockSpec((seq, LANES), lambda b, g: (b, g)),
        out_shape=jax.ShapeDtypeStruct((t, A_WIDTH), BF16),
        scratch_shapes=scratch,
        compiler_params=pltpu.CompilerParams(
            dimension_semantics=("arbitrary", "arbitrary"), vmem_limit_bytes=VMEM_LIMIT),
        name="dilated_attention",
    )(a_qkv, a_qkv, a_qkv, bias, gain_a)


def _mla_kernel(q_ref, k_ref, v_ref, gain_ref, o_ref, vt, s_sc, e_sc, *, seq, tq):
    key_idx = lax.broadcasted_iota(jnp.int32, (tq, tq), 0)
    qry_idx = lax.broadcasted_iota(jnp.int32, (tq, tq), 1)
    causal = key_idx <= qry_idx
    gain = gain_ref[...]
    for c in range(0, seq, tq):
        vt[:, c:c + tq] = v_ref[c:c + tq, :].astype(F32).T.astype(BF16)

    units = [(i, hh) for i in range(seq // tq) for hh in range(2)]
    base = np.concatenate([[0], np.cumsum([i + 1 for i, _ in units])]).tolist()

    def logits(u):
        i, hh = units[u]
        hs = slice(hh * LANES, (hh + 1) * LANES)
        q = q_ref[i * tq:(i + 1) * tq, hs]
        m = None
        for j in range(i + 1):
            s = lax.dot_general(k_ref[j * tq:(j + 1) * tq, hs], q, (((1,), (1,)), ((), ())),
                                preferred_element_type=F32)
            if j == i:
                s = jnp.where(causal, s, NEG)
            s_sc[base[u] + j] = s
            bm = jnp.max(s, axis=0, keepdims=True)
            m = bm if m is None else jnp.maximum(m, bm)
        return m

    def attend(u, m):
        i, hh = units[u]
        l = None
        for j in range(i + 1):
            e = jnp.exp2(s_sc[base[u] + j] - m)
            ls = jnp.sum(e, axis=0, keepdims=True)
            l = ls if l is None else l + ls
            e_sc[(base[u] + j) * tq:(base[u] + j + 1) * tq, :] = e.astype(BF16)
        acc = jnp.dot(vt[hh * B_V:(hh + 1) * B_V, 0:(i + 1) * tq], e_sc[base[u] * tq:base[u + 1] * tq, :],
                      preferred_element_type=F32)
        o = acc / l
        return o * lax.rsqrt(jnp.mean(o * o, axis=0, keepdims=True) + EPS)

    m_next = logits(0)
    normed = []
    for u, (i, hh) in enumerate(units):
        m_cur = m_next
        if u + 1 < len(units):
            m_next = logits(u + 1)
        normed.append(attend(u, m_cur))
        if hh == 1:
            o_ref[i * tq:(i + 1) * tq, :] = (jnp.concatenate(normed, axis=0).T * gain).astype(o_ref.dtype)
            normed = []


def _mla(bq, bk, bv, gain_b, batch, seq):
    t = bq.shape[0]
    pairs = B_HEADS // 2
    tq = MLA_BLOCK
    assert seq % tq == 0
    nq = seq // tq
    nslots = 2 * (nq * (nq + 1) // 2)
    return pl.pallas_call(
        functools.partial(_mla_kernel, seq=seq, tq=tq),
        grid=(batch, pairs),
        in_specs=[
            pl.BlockSpec((seq, 2 * LANES), lambda b, g: (b, g)),
            pl.BlockSpec((seq, 2 * LANES), lambda b, g: (b, g)),
            pl.BlockSpec((seq, LANES), lambda b, g: (b, g)),
            pl.BlockSpec((1, LANES), lambda b, g: (0, g)),
        ],
        out_specs=pl.BlockSpec((seq, LANES), lambda b, g: (b, g)),
        out_shape=jax.ShapeDtypeStruct((t, B_WIDTH), BF16),
        scratch_shapes=[pltpu.VMEM((LANES, seq), BF16),
                        pltpu.VMEM((nslots, tq, tq), F32), pltpu.VMEM((nslots * tq, tq), BF16)],
        compiler_params=pltpu.CompilerParams(
            dimension_semantics=("arbitrary", "arbitrary"), vmem_limit_bytes=VMEM_LIMIT),
        name="latent_attention",
    )(bq, bk, bv, gain_b)


def _retention_consts():
    h = C_HEADS
    log_g = np.log(1.0 - 2.0 ** (-5.0 - np.arange(h))).astype(np.float32)
    i = np.arange(C_CHUNK, dtype=np.float32)
    rel = i[:, None] - i[None, :]
    decay = (np.exp(np.maximum(rel, 0.0)[None] * log_g[:, None, None]) * (rel >= 0)[None]).astype(np.float32)
    xi = np.exp((i + 1.0)[None, :] * log_g[:, None]).astype(np.float32)
    zeta = np.exp((C_CHUNK - 1.0 - i)[None, :] * log_g[:, None]).astype(np.float32)
    chunk_decay = np.exp(C_CHUNK * log_g).astype(np.float32)
    decay_all = np.concatenate(list(decay), axis=1)
    xi_mat = np.repeat(xi.T, C_VAL, axis=1)
    zeta_t = np.zeros((C_QPAD, C_CHUNK), np.float32)
    zeta_t[:C_QK] = np.repeat(zeta, C_KEY, axis=0)
    cd = np.repeat(chunk_decay, C_VAL)[None, :]
    bd = np.zeros((C_QPAD, C_WIDTH), np.float32)
    for hh in range(h):
        bd[hh * C_KEY:(hh + 1) * C_KEY, hh * C_VAL:(hh + 1) * C_VAL] = 1.0
    return tuple(jnp.asarray(a) for a in (decay_all, xi_mat, zeta_t, cd, bd))


def _retention_kernel(qk_ref, v_ref, g_ref, decay_ref, xi_ref, zeta_ref, cd_ref, bd_ref, gain_ref,
                      o_ref, *, seq):
    lane = lax.broadcasted_iota(jnp.int32, (C_CHUNK, LANES), 1)
    lo = lane < C_VAL
    gain = gain_ref[...]
    state = jnp.zeros((C_QPAD, C_WIDTH), F32)
    kshape, vshape = (C_QPAD, C_HEADS * C_CHUNK), (C_HEADS * C_CHUNK, C_WIDTH)
    k_on = (lax.broadcasted_iota(jnp.int32, kshape, 0) // C_KEY) == (lax.broadcasted_iota(jnp.int32, kshape, 1) // C_CHUNK)
    v_on = (lax.broadcasted_iota(jnp.int32, vshape, 0) // C_CHUNK) == (lax.broadcasted_iota(jnp.int32, vshape, 1) // C_VAL)

    for n in range(seq // C_CHUNK):
        rows = slice(n * C_CHUNK, (n + 1) * C_CHUNK)
        q = qk_ref[rows, 0:C_QPAD]
        k_t = qk_ref[rows, C_QPAD:2 * C_QPAD].astype(F32).T
        v = v_ref[rows, :]
        k_bd = jnp.where(k_on, jnp.concatenate([k_t] * C_HEADS, axis=1), 0.0).astype(BF16)
        s = jnp.dot(q, k_bd, preferred_element_type=F32) * decay_ref[...]
        v_bd = jnp.where(v_on, jnp.concatenate([v] * C_HEADS, axis=0), jnp.zeros((), BF16))
        o = (jnp.dot(s.astype(BF16), v_bd, preferred_element_type=F32)
             + jnp.dot(q, state.astype(BF16), preferred_element_type=F32) * xi_ref[...])
        upd = jnp.dot((k_t * zeta_ref[...]).astype(BF16), v, preferred_element_type=F32)
        state = state * cd_ref[...] + upd * bd_ref[...]
        for t in range(C_WIDTH // LANES):
            sl = slice(t * LANES, (t + 1) * LANES)
            x = o[:, sl]
            mu_lo = jnp.sum(jnp.where(lo, x, 0.0), axis=1, keepdims=True)
            mu_hi = jnp.sum(jnp.where(lo, 0.0, x), axis=1, keepdims=True)
            dlt = x - jnp.where(lo, mu_lo, mu_hi) * (1.0 / C_VAL)
            sq = dlt * dlt
            var_lo = jnp.sum(jnp.where(lo, sq, 0.0), axis=1, keepdims=True)
            var_hi = jnp.sum(jnp.where(lo, 0.0, sq), axis=1, keepdims=True)
            y = dlt * lax.rsqrt(jnp.where(lo, var_lo, var_hi) * (1.0 / C_VAL) + EPS)
            o_ref[rows, sl] = (y * g_ref[rows, sl].astype(F32) * gain[:, sl]).astype(o_ref.dtype)


def _retention(cqk, cv, cg, gain_c, consts, batch, seq):
    t = cqk.shape[0]
    tok = lambda width: pl.BlockSpec((seq, width), lambda b: (b, 0))
    full = lambda a: _const_spec(a.shape, lambda b: (0,) * a.ndim)
    return pl.pallas_call(
        functools.partial(_retention_kernel, seq=seq),
        grid=(batch,),
        in_specs=[tok(2 * C_QPAD), tok(C_WIDTH), tok(C_WIDTH)] + [full(a) for a in consts] + [full(gain_c)],
        out_specs=tok(C_WIDTH),
        out_shape=jax.ShapeDtypeStruct((t, C_WIDTH), BF16),
        compiler_params=pltpu.CompilerParams(
            dimension_semantics=("arbitrary",), vmem_limit_bytes=VMEM_LIMIT),
        name="retention",
    )(cqk, cv, cg, *consts, gain_c)


def _outffn_kernel(x_ref, ma_ref, mb_ref, mc_ref, mod_ref, wo_ref, g_ref, wg_ref, wu_ref, wd_ref, fg_ref,
                   o_ref, mix_ref, h_ref, hid_ref, *, final, fchunk, nsplit):
    tm = x_ref.shape[0]
    parts = [slice(k * tm // nsplit, (k + 1) * tm // nsplit) for k in range(nsplit)]
    mix_ref[:, 0:A_WIDTH] = ma_ref[...]
    mix_ref[:, A_WIDTH:A_WIDTH + B_WIDTH] = mb_ref[...]
    mix_ref[:, A_WIDTH + B_WIDTH:] = mc_ref[...]
    for r in parts:
        att = jnp.dot(mix_ref[r, :], wo_ref[...], preferred_element_type=F32)
        o_ref[r, :] = x_ref[r, :] + mod_ref[2:3, :] * att
    for r in parts:
        h_ref[r, :] = (_rms(o_ref[r, :]) * g_ref[...] * (1.0 + mod_ref[4:5, :]) + mod_ref[3:4, :]).astype(BF16)
    hidden = wg_ref.shape[1]
    for r in parts:
        for j in range(hidden // fchunk):
            sl = slice(j * fchunk, (j + 1) * fchunk)
            gate = jnp.dot(h_ref[r, :], wg_ref[:, sl], preferred_element_type=F32)
            up = jnp.dot(h_ref[r, :], wu_ref[:, sl], preferred_element_type=F32)
            hid_ref[r, sl] = (_silu(gate) * up).astype(BF16)
    for r in parts:
        ffn = jnp.dot(hid_ref[r, :], wd_ref[...], preferred_element_type=F32)
        y = o_ref[r, :] + mod_ref[5:6, :] * ffn
        if final:
            y = _rms(y) * fg_ref[...]
        o_ref[r, :] = y


def _outffn(x, mix_a, mix_b, mix_c, mods, layer, prep, ffn_w, final_g, batch, seq, tm, final):
    t, d = x.shape
    nt = seq // tm
    w_out, w_gate, w_up, w_down = ffn_w
    hidden = w_gate.shape[1]
    whole = lambda a: _const_spec(a.shape, lambda b, i: (0,) * a.ndim)
    row = lambda w: pl.BlockSpec((tm, w), lambda b, i: (b * nt + i, 0))
    wl = lambda a: _const_spec((None,) + a.shape[1:], lambda b, i: (layer,) + (0,) * (a.ndim - 1))
    return pl.pallas_call(
        functools.partial(_outffn_kernel, final=final, fchunk=MXU_WIDTH, nsplit=FFN_SPLIT),
        grid=(batch, nt),
        in_specs=[
            row(d), row(A_WIDTH), row(B_WIDTH), row(C_WIDTH),
            pl.BlockSpec((None, None, 6, d), lambda b, i: (layer, b, 0, 0)),
            whole(w_out), wl(prep["norm2_g"]), whole(w_gate), whole(w_up), whole(w_down),
            _const_spec((1, d), lambda b, i: (0, 0)),
        ],
        out_specs=row(d),
        out_shape=jax.ShapeDtypeStruct((t, d), F32),
        scratch_shapes=[pltpu.VMEM((tm, d), BF16), pltpu.VMEM((tm, d), BF16), pltpu.VMEM((tm, hidden), BF16)],
        compiler_params=pltpu.CompilerParams(
            dimension_semantics=("arbitrary", "arbitrary"), vmem_limit_bytes=VMEM_LIMIT),
        name="outproj_ffn",
    )(x, mix_a, mix_b, mix_c, mods, w_out, prep["norm2_g"], w_gate, w_up, w_down, final_g)


def _prepare(norm1_g, w_in, mla_q_norm, mla_kv_norm, mla_w_uq, mla_w_ukv, w_out, norm2_g,
             ffn_w_gate, ffn_w_up, ffn_w_down):
    depth, d, _ = w_in.shape
    b_lat0 = 3 * A_WIDTH
    rope0 = b_lat0 + B_Q_RANK + B_KV_RANK
    c0 = rope0 + B_ROPE
    assert C_QK + B_ROPE <= C_QPAD and C_QK % LANES == B_NOPE
    assert c0 + 2 * C_QK + 2 * C_WIDTH == w_in.shape[2] and rope0 == OFF_C
    uq = mla_w_uq.reshape(depth, B_Q_RANK, B_HEADS, B_NOPE + B_ROPE)
    uq = jnp.pad(uq, ((0, 0), (0, 0), (0, 0), (0, LANES - B_NOPE - B_ROPE)))
    ukv = mla_w_ukv.reshape(depth, B_KV_RANK, B_HEADS, B_NOPE + B_V)
    uk = jnp.pad(ukv[..., :B_NOPE], ((0, 0), (0, 0), (0, 0), (0, LANES - B_NOPE)))
    uv = ukv[..., B_NOPE:]
    return {
        "norm1_g": norm1_g[:, None, :],
        "w_in": jnp.swapaxes(w_in, 1, 2),
        "q_norm": mla_q_norm[:, None, :],
        "kv_norm": mla_kv_norm[:, None, :],
        "w_uq": uq.reshape(depth, B_Q_RANK, B_HEADS * LANES).astype(BF16),
        "w_uk": uk.reshape(depth, B_KV_RANK, B_HEADS * LANES).astype(BF16),
        "w_uv": uv.reshape(depth, B_KV_RANK, B_WIDTH).astype(BF16),
        "w_out": w_out,
        "norm2_g": norm2_g[:, None, :],
        "w_gate": ffn_w_gate,
        "w_up": ffn_w_up,
        "w_down": ffn_w_down,
    }


def kernel(x, c, positions, rel_bias, ada_w, ada_b, norm1_g, w_in, mla_q_norm, mla_kv_norm, mla_w_uq,
           mla_w_ukv, mix_gain, w_out, norm2_g, ffn_w_gate, ffn_w_up, ffn_w_down, final_norm):
    batch, seq, d = x.shape
    depth = w_in.shape[0]
    tm = ROW_TILE
    assert seq % tm == 0 and seq % C_CHUNK == 0

    prep = _prepare(norm1_g, w_in, mla_q_norm, mla_kv_norm, mla_w_uq, mla_w_ukv, w_out, norm2_g,
                    ffn_w_gate, ffn_w_up, ffn_w_down)
    mods = _mods(c, ada_w, ada_b).reshape(depth, batch, 6, d)
    tables = _rope_tables(positions)
    bias = _bias_tables(rel_bias)
    ret_consts = _retention_consts()
    final_g = final_norm[None, :]

    xf = x.reshape(batch * seq, d)
    for l in range(depth):
        a_qkv, bq, bk, bv, cqk, cv, cg, *ffn_w = _inproj(xf, mods, l, prep, tables, batch, seq, tm)
        gain = mix_gain[l][None, :]
        mix_a = _dilated(a_qkv, bias, gain[:, :A_WIDTH], batch, seq)
        mix_b = _mla(bq, bk, bv, gain[:, A_WIDTH:A_WIDTH + B_WIDTH], batch, seq)
        mix_c = _retention(cqk, cv, cg, gain[:, A_WIDTH + B_WIDTH:], ret_consts, batch, seq)
        xf = _outffn(xf, mix_a, mix_b, mix_c, mods, l, prep, ffn_w, final_g, batch, seq, tm,
                     final=(l == depth - 1))
    return xf.reshape(batch, seq, d)
```

```python
import functools

import numpy as np
import jax
import jax.numpy as jnp
from jax import lax
from jax.experimental import pallas as pl
from jax.experimental.pallas import tpu as pltpu

F32 = jnp.float32
BF16 = jnp.bfloat16

HEAD_DIM = 64
A_HEADS = 6
A_PATTERNS = ((128, 1), (512, 4), (2048, 16))
A_WIDTH = A_HEADS * HEAD_DIM
B_HEADS = 4
B_NOPE = 64
B_ROPE = 32
B_V = 64
B_Q_RANK = 256
B_KV_RANK = 128
B_WIDTH = B_HEADS * B_V
C_HEADS = 6
C_KEY = 32
C_VAL = 64
C_WIDTH = C_HEADS * C_VAL
C_QK = C_HEADS * C_KEY
C_CHUNK = 128
N_BUCKETS = 32
MAX_DISTANCE = 2048
ROPE_BASE = 10000.0
EPS = 1e-6

LANES = 128
WIN = 128
STEP = 4
NEG = -1e30
LOG2E = 1.4426950408889634
FFN_CAST = ("w_out", "w_gate", "w_up", "w_down")
VMEM_LIMIT = 56 * 1024 * 1024
MXU_WIDTH = 256
ROW_TILE = 512
MLA_BLOCK = 512
MODS_COL_TILE = 1536
ROPE_ROW_TILE = 2048
FFN_SPLIT = 2
INPROJ_SPLIT = 2

OFF_A = 0
OFF_BQ = 3 * A_WIDTH
OFF_BKV = OFF_BQ + B_Q_RANK
OFF_C = OFF_BKV + B_KV_RANK
C_QPAD = 2 * LANES
OFF_CV = OFF_C + 2 * C_QPAD
OFF_CG = OFF_CV + C_WIDTH
IN_COLS = OFF_CG + C_WIDTH


def _const_spec(shape, index_map):
    return pl.BlockSpec(shape, index_map, pipeline_mode=pl.Buffered(1))


def _silu(x):
    return x / (1.0 + jnp.exp(-x))


def _rms(x):
    return x * lax.rsqrt(jnp.mean(x * x, axis=-1, keepdims=True) + EPS)


def _mods_kernel(c_ref, w_ref, b_ref, o_ref):
    def split(a):
        hi = a.astype(BF16)
        return hi, (a - hi.astype(F32)).astype(BF16)

    c_hi, c_lo = split(_silu(c_ref[...]))
    w_hi, w_lo = split(w_ref[...])
    nb = c_hi.shape[0]
    both = jnp.dot(jnp.concatenate([c_hi, c_lo], axis=0), w_hi, preferred_element_type=F32)
    o_ref[...] = (both[:nb] + both[nb:] + jnp.dot(c_hi, w_lo, preferred_element_type=F32)) + b_ref[...]


def _mods(c, ada_w, ada_b):
    depth, d, n = ada_w.shape
    b = c.shape[0]
    tn = MODS_COL_TILE
    return pl.pallas_call(
        _mods_kernel,
        grid=(depth, n // tn),
        in_specs=[
            pl.BlockSpec((b, d), lambda l, j: (0, 0)),
            pl.BlockSpec((None, d, tn), lambda l, j: (l, 0, j)),
            pl.BlockSpec((None, 1, tn), lambda l, j: (l, 0, j)),
        ],
        out_specs=pl.BlockSpec((None, b, tn), lambda l, j: (l, 0, j)),
        out_shape=jax.ShapeDtypeStruct((depth, b, n), F32),
        compiler_params=pltpu.CompilerParams(vmem_limit_bytes=VMEM_LIMIT),
        name="adaln_mods",
    )(c, ada_w, ada_b.reshape(depth, 1, n))


def _rope_kernel(pos_ref, f_ref, s_ref, cos_ref, sin_ref):
    ang = pos_ref[...].astype(F32) * f_ref[...]
    cos_ref[...] = jnp.cos(ang)
    sin_ref[...] = jnp.sin(ang) * s_ref[...]


def _rope_tables(positions):
    t = positions.size
    half = C_KEY // 2
    assert B_ROPE == C_KEY and B_NOPE % C_KEY == 0
    inv_freq = (1.0 / (ROPE_BASE ** (np.arange(half, dtype=np.float32) / half))).astype(np.float32)
    ones = np.ones(half, np.float32)
    freqs = jnp.asarray(np.tile(np.concatenate([inv_freq, inv_freq]), LANES // C_KEY)[None, :])
    signs = jnp.asarray(np.tile(np.concatenate([-ones, ones]), LANES // C_KEY)[None, :])
    tr = ROPE_ROW_TILE
    out = jax.ShapeDtypeStruct((t, LANES), F32)
    row = pl.BlockSpec((tr, LANES), lambda i: (i, 0))
    return pl.pallas_call(
        _rope_kernel,
        grid=(t // tr,),
        in_specs=[
            pl.BlockSpec((tr, 1), lambda i: (i, 0)),
            pl.BlockSpec((1, LANES), lambda i: (0, 0)),
            pl.BlockSpec((1, LANES), lambda i: (0, 0)),
        ],
        out_specs=[row, row],
        out_shape=[out, out],
        name="rope_tables",
    )(positions.reshape(t, 1), freqs, signs)


def _t5_bucket(dist):
    max_exact = N_BUCKETS // 2
    safe = np.maximum(dist, 1).astype(np.float32)
    large = max_exact + (np.log(safe / max_exact) / np.log(MAX_DISTANCE / max_exact)
                         * (N_BUCKETS - max_exact)).astype(np.int32)
    large = np.minimum(large, N_BUCKETS - 1)
    return np.where(dist < max_exact, dist, large).astype(np.int32)


def _bias_kernel(bmap_ref, rb_ref, o_ref):
    h = pl.program_id(1)
    bm = bmap_ref[...]
    t = jnp.full(bm.shape, NEG, F32)
    for b in range(N_BUCKETS):
        t = jnp.where(bm == b, rb_ref[b, h] * LOG2E, t)
    o_ref[...] = t


def _bias_tables(rel_bias):
    qi = np.arange(WIN)[None, :]
    c = np.arange(2 * WIN)[:, None]
    j = qi - c + WIN
    maps = []
    for (w, d) in A_PATTERNS:
        assert w // d == WIN
        bucket = _t5_bucket(np.arange(WIN + 1, dtype=np.int32) * d)
        maps.append(np.where((j >= 0) & (j <= WIN), bucket[np.clip(j, 0, WIN)], -1).astype(np.int32))
    bmap = jnp.asarray(np.stack(maps))
    npat = len(A_PATTERNS)
    return pl.pallas_call(
        _bias_kernel,
        grid=(npat, A_HEADS),
        in_specs=[
            pl.BlockSpec((None, 2 * WIN, WIN), lambda p, h: (p, 0, 0)),
            pl.BlockSpec(memory_space=pltpu.SMEM),
        ],
        out_specs=pl.BlockSpec((None, 2 * WIN, WIN), lambda p, h: (p, 0, h)),
        out_shape=jax.ShapeDtypeStruct((npat, 2 * WIN, A_HEADS * WIN), F32),
        name="t5_bias_tables",
    )(bmap, rel_bias)


def _cast_plumbing(prep, cast_layer, steps, step_of):
    if cast_layer is None:
        return [], [], [], []
    ws, in_specs, out_specs = [prep[name] for name in FFN_CAST], [], []
    for w in ws:
        nslab = steps
        while w.shape[1] % (16 * nslab):
            assert nslab % 2 == 0
            nslab //= 2
        rep = steps // nslab
        block = (w.shape[1] // nslab, w.shape[2])
        in_specs.append(pl.BlockSpec((None,) + block, lambda *g, rep=rep: (cast_layer, step_of(*g) // rep, 0)))
        out_specs.append(pl.BlockSpec(block, lambda *g, rep=rep: (step_of(*g) // rep, 0)))
    return ws, in_specs, out_specs, [jax.ShapeDtypeStruct(w.shape[1:], BF16) for w in ws]


def _cast_slabs(cast_in, cast_out):
    for src, dst in zip(cast_in, cast_out):
        dst[...] = src[...].astype(BF16)


def _rot_half(x, x1_mask):
    w = x.shape[-1]
    half = B_ROPE // 2
    return jnp.where(x1_mask, pltpu.roll(x, w - half, 1), pltpu.roll(x, half, 1))


def _inproj_kernel(x_ref, mod_ref, g_ref, win_ref, qg_ref, kvg_ref, wuq_ref, wuk_ref, wuv_ref,
                   cc_ref, sc_ref, ks_ref, *refs, ncast):
    cast_in, refs = refs[:ncast], refs[ncast:]
    a_ref, bq_ref, bk_ref, bv_ref, cqk_ref, cv_ref, cg_ref = refs[:7]
    cast_out, w_ref = refs[7:7 + ncast], refs[7 + ncast]
    _cast_slabs(cast_in, cast_out)

    @pl.when((pl.program_id(0) == 0) & (pl.program_id(1) == 0))
    def _():
        d = win_ref.shape[1]
        rope0 = OFF_C
        c0 = rope0 + B_ROPE
        zeros = lambda n: jnp.zeros((n, d), F32)
        chunk = MXU_WIDTH
        for r0 in range(0, OFF_C, chunk):
            w_ref[:, r0:r0 + chunk] = win_ref[r0:r0 + chunk, :].T.astype(BF16)
        q_rows = jnp.concatenate([win_ref[c0:c0 + C_QK, :], win_ref[rope0:c0, :],
                                  zeros(C_QPAD - C_QK - B_ROPE)], axis=0)
        w_ref[:, OFF_C:OFF_C + C_QPAD] = q_rows.T.astype(BF16)
        k_rows = jnp.concatenate([win_ref[c0 + C_QK:c0 + 2 * C_QK, :], zeros(C_QPAD - C_QK)], axis=0)
        w_ref[:, OFF_C + C_QPAD:OFF_CV] = k_rows.T.astype(BF16)
        for r0 in range(0, IN_COLS - OFF_CV, chunk):
            src = c0 + 2 * C_QK + r0
            w_ref[:, OFF_CV + r0:OFF_CV + r0 + chunk] = win_ref[src:src + chunk, :].T.astype(BF16)

    tm = x_ref.shape[0] // INPROJ_SPLIT
    parts = [slice(k * tm, (k + 1) * tm) for k in range(INPROJ_SPLIT)]
    lane = lax.broadcasted_iota(jnp.int32, (tm, LANES), 1)
    b_x1 = (lane >= B_NOPE) & (lane < B_NOPE + B_ROPE // 2)
    b_rope = (lane >= B_NOPE) & (lane < B_NOPE + B_ROPE)
    c_x1 = (lane & (C_KEY // 2)) == 0

    def proj(h, lo, hi):
        return jnp.dot(h, w_ref[:, lo:hi], preferred_element_type=F32)

    hs = [(_rms(x_ref[r, :]) * g_ref[...] * (1.0 + mod_ref[1:2, :]) + mod_ref[0:1, :]).astype(BF16)
          for r in parts]
    lat = [(proj(h, OFF_BQ, OFF_BKV), proj(h, OFF_BKV, OFF_C), proj(h, OFF_C, OFF_CV)) for h in hs]
    normed = [((_rms(q_lat) * qg_ref[...]).astype(BF16), (_rms(kv_lat) * kvg_ref[...]).astype(BF16))
              for q_lat, kv_lat, _ in lat]
    for r, h in zip(parts, hs):
        a_ref[r, 0:A_WIDTH] = proj(h, OFF_A, OFF_A + A_WIDTH) * (HEAD_DIM ** -0.5 * LOG2E)
        a_ref[r, A_WIDTH:3 * A_WIDTH] = proj(h, OFF_A + A_WIDTH, OFF_BQ)
    ups = []
    for r, h, (qn, kvn) in zip(parts, hs, normed):
        ups.append((jnp.dot(qn, wuq_ref[...], preferred_element_type=F32),
                    jnp.dot(kvn, wuk_ref[...], preferred_element_type=F32)))
        bv_ref[r, :] = jnp.dot(kvn, wuv_ref[...], preferred_element_type=F32).astype(BF16)
        cv_ref[r, :] = proj(h, OFF_CV, OFF_CG).astype(BF16)
        cg_ref[r, :] = _silu(proj(h, OFF_CG, IN_COLS)).astype(BF16)
    b_scale = (B_NOPE + B_ROPE) ** -0.5 * LOG2E
    for r, (_, _, qk), (q, k_nope) in zip(parts, lat, ups):
        cos_c, sin_c = cc_ref[r, :], sc_ref[r, :]
        cos_b, sin_b = jnp.where(b_rope, cos_c, 1.0), jnp.where(b_rope, sin_c, 0.0)
        k_rope = jnp.where(lane >= B_NOPE, qk[:, LANES:2 * LANES], 0.0)
        k_pe = k_rope * cos_b + _rot_half(k_rope, b_x1) * sin_b
        for hh in range(B_HEADS):
            sl = slice(hh * LANES, (hh + 1) * LANES)
            qh = q[:, sl]
            bq_ref[r, sl] = ((qh * cos_b + _rot_half(qh, b_x1) * sin_b) * b_scale).astype(BF16)
            bk_ref[r, sl] = (k_nope[:, sl] + k_pe).astype(BF16)
        for t in range(2 * C_QPAD // LANES):
            sl = slice(t * LANES, (t + 1) * LANES)
            xt = qk[:, sl]
            cqk_ref[r, sl] = ((xt * cos_c + _rot_half(xt, c_x1) * sin_c) * ks_ref[:, sl]).astype(BF16)


def _inproj(x, mods, layer, prep, tables, batch, seq, tm, cast_layer):
    t, d = x.shape
    nt = seq // tm
    row = lambda w: pl.BlockSpec((tm, w), lambda b, i: (b * nt + i, 0))
    wl = lambda a: _const_spec((None,) + a.shape[1:], lambda b, i: (layer,) + (0,) * (a.ndim - 1))
    cos_c, sin_c = tables
    kscale = jnp.asarray(np.concatenate([np.ones(C_QPAD, np.float32),
                                         np.full(C_QPAD, C_KEY ** -0.5, np.float32)])[None, :])
    outs = [(3 * A_WIDTH, F32), (B_HEADS * LANES, BF16), (B_HEADS * LANES, BF16), (B_WIDTH, BF16),
            (2 * C_QPAD, BF16), (C_WIDTH, BF16), (C_WIDTH, BF16)]
    cast_ws, cast_in_specs, cast_out_specs, cast_shapes = _cast_plumbing(
        prep, cast_layer, batch * nt, lambda b, i: b * nt + i)
    return pl.pallas_call(
        functools.partial(_inproj_kernel, ncast=len(cast_ws)),
        grid=(batch, nt),
        in_specs=[
            row(d),
            pl.BlockSpec((None, None, 6, d), lambda b, i: (layer, b, 0, 0)),
            wl(prep["norm1_g"]), wl(prep["w_in"]), wl(prep["q_norm"]), wl(prep["kv_norm"]),
            wl(prep["w_uq"]), wl(prep["w_uk"]), wl(prep["w_uv"]),
            row(LANES), row(LANES),
            _const_spec((1, 2 * C_QPAD), lambda b, i: (0, 0)),
            *cast_in_specs,
        ],
        out_specs=[row(w) for w, _ in outs] + cast_out_specs,
        out_shape=[jax.ShapeDtypeStruct((t, w), dt) for w, dt in outs] + cast_shapes,
        scratch_shapes=[pltpu.VMEM((d, IN_COLS), BF16)],
        compiler_params=pltpu.CompilerParams(
            dimension_semantics=("arbitrary", "arbitrary"), vmem_limit_bytes=VMEM_LIMIT),
        name="inproj",
    )(x, mods, prep["norm1_g"], prep["w_in"], prep["q_norm"], prep["kv_norm"],
      prep["w_uq"], prep["w_uk"], prep["w_uv"], cos_c, sin_c, kscale, *cast_ws)


def _class_major_blocks(seq, level):
    n = seq // STEP
    perm = np.arange(seq)
    for _ in range(level):
        perm = perm.reshape(n, STEP).T.reshape(seq)
    blocks = perm.reshape(seq // WIN, WIN)
    stride = STEP ** level
    assert (blocks == blocks[:, :1] + stride * np.arange(WIN)).all()
    return [(int(b[0]), stride) for b in blocks]


def _dilated_kernel(q_ref, k_ref, v_ref, bias_ref, gain_ref, o_ref,
                    qf, kf, vf, qlo, qhi, kp, vt, *stage_refs, seq):
    npat = len(A_PATTERNS)
    nblk = seq // WIN
    s_sc, out_s, lse_s = (stage_refs[i * npat:(i + 1) * npat] for i in range(3))
    lane = lax.broadcasted_iota(jnp.int32, (WIN, LANES), 1)
    lo_lane = lane < HEAD_DIM
    lo_row = lax.broadcasted_iota(jnp.int32, (LANES, WIN), 0) < HEAD_DIM

    n = seq // STEP
    for p in range(npat):
        for r in range(STEP):
            rows = slice(r * n, (r + 1) * n)
            src = rows if p == 0 else pl.ds(r, n, stride=STEP)
            level_below = (q_ref, k_ref, v_ref) if p <= 1 else (qf.at[p - 2], kf.at[p - 2], vf.at[p - 2])
            q, k, v = (f[src, :] for f in level_below)
            if 0 < p < npat - 1:
                qf[p - 1, rows, :], kf[p - 1, rows, :], vf[p - 1, rows, :] = q, k, v
            lo_q = lax.broadcasted_iota(jnp.int32, q.shape, 1) < HEAD_DIM
            qlo[p, rows, :] = jnp.where(lo_q, q, 0.0).astype(BF16)
            qhi[p, rows, :] = jnp.where(lo_q, 0.0, q).astype(BF16)
            vt[p, :, rows] = v.T.astype(BF16)
            kp[p, rows, :] = k.astype(BF16)

    def key_rows(p, blk):
        r0 = blk * WIN
        first = blk % (nblk // A_PATTERNS[p][1]) == 0
        return (r0 if first else r0 - WIN), r0 + WIN

    for p in range(npat - 1, -1, -1):
        for blk in range(nblk):
            k0, k1 = key_rows(p, blk)
            r0 = blk * WIN
            q2 = jnp.concatenate([qlo[p, r0:r0 + WIN, :], qhi[p, r0:r0 + WIN, :]], axis=0)
            kcat = kp[p, k0:k1, :]
            bias = bias_ref[p, 2 * WIN - (k1 - k0):, :]
            s_sc[p][blk, 0:k1 - k0, :] = lax.dot_general(
                kcat, q2, (((1,), (1,)), ((), ())), preferred_element_type=F32) + bias
        for blk, (start, stride) in enumerate(_class_major_blocks(seq, p)):
            k0, k1 = key_rows(p, blk)
            s = s_sc[p][blk, 0:k1 - k0, :]
            m = jnp.max(s, axis=0, keepdims=True)
            e = jnp.exp2(s - m)
            l = jnp.sum(e, axis=0, keepdims=True)
            acc = jnp.dot(vt[p, :, k0:k1], e.astype(BF16), preferred_element_type=F32)
            out = acc * (1.0 / l)
            lse = m + jnp.log2(l)
            out = jnp.where(lo_row, out[:, :WIN], out[:, WIN:])
            lse = jnp.where(lo_row, lse[:, :WIN], lse[:, WIN:])
            dst = pl.ds(start, WIN, stride=stride) if stride > 1 else slice(start, start + WIN)
            out_s[p][dst, :] = out.T
            lse_s[p][dst, :] = lse.T

    gain = gain_ref[...]
    for blk in range(nblk):
        sl = slice(blk * WIN, (blk + 1) * WIN)
        lses = [s[sl, :] for s in lse_s]
        mx = functools.reduce(jnp.maximum, lses)
        ws = [jnp.exp2(s - mx) for s in lses]
        o = sum(w * x[sl, :] for w, x in zip(ws, out_s)) / sum(ws)
        sq = o * o
        ms_lo = jnp.sum(jnp.where(lo_lane, sq, 0.0), axis=1, keepdims=True)
        ms_hi = jnp.sum(jnp.where(lo_lane, 0.0, sq), axis=1, keepdims=True)
        inv = lax.rsqrt(jnp.where(lo_lane, ms_lo, ms_hi) * (1.0 / HEAD_DIM) + EPS)
        o_ref[sl, :] = (o * inv * gain).astype(o_ref.dtype)


def _dilated(a_qkv, bias, gain_a, batch, seq):
    t = a_qkv.shape[0]
    pairs = A_WIDTH // LANES
    assert all(d == STEP ** p and seq % (d * WIN) == 0 for p, (_, d) in enumerate(A_PATTERNS))
    npat = bias.shape[0]
    col = lambda off: pl.BlockSpec((seq, LANES), lambda b, g: (b, off + g))
    scratch = ([pltpu.VMEM((npat - 2, seq, LANES), F32) for _ in range(3)]
               + [pltpu.VMEM((npat, seq, LANES), BF16) for _ in range(2)]
               + [pltpu.VMEM((npat, seq, LANES), BF16)]
               + [pltpu.VMEM((npat, LANES, seq), BF16)]
               + [pltpu.VMEM((seq // WIN, 2 * WIN, 2 * WIN), F32) for _ in range(npat)]
               + [pltpu.VMEM((seq, LANES), F32) for _ in range(2 * npat)])
    return pl.pallas_call(
        functools.partial(_dilated_kernel, seq=seq),
        grid=(batch, pairs),
        in_specs=[
            col(0), col(pairs), col(2 * pairs),
            pl.BlockSpec((npat, 2 * WIN, 2 * WIN), lambda b, g: (0, 0, g)),
            pl.BlockSpec((1, LANES), lambda b, g: (0, g)),
        ],
        out_specs=pl.BlockSpec((seq, LANES), lambda b, g: (b, g)),
        out_shape=jax.ShapeDtypeStruct((t, A_WIDTH), BF16),
        scratch_shapes=scratch,
        compiler_params=pltpu.CompilerParams(
            dimension_semantics=("arbitrary", "arbitrary"), vmem_limit_bytes=VMEM_LIMIT),
        name="dilated_attention",
    )(a_qkv, a_qkv, a_qkv, bias, gain_a)


def _mla_kernel(q_ref, k_ref, v_ref, gain_ref, o_ref, vt, s_sc, e_sc, *, seq, tq):
    key_idx = lax.broadcasted_iota(jnp.int32, (tq, tq), 0)
    qry_idx = lax.broadcasted_iota(jnp.int32, (tq, tq), 1)
    causal = key_idx <= qry_idx
    gain = gain_ref[...]
    for c in range(0, seq, tq):
        vt[:, c:c + tq] = v_ref[c:c + tq, :].astype(F32).T.astype(BF16)

    units = [(i, hh) for i in range(seq // tq) for hh in range(2)]
    base = np.concatenate([[0], np.cumsum([i + 1 for i, _ in units])]).tolist()

    def logits(u):
        i, hh = units[u]
        hs = slice(hh * LANES, (hh + 1) * LANES)
        q = q_ref[i * tq:(i + 1) * tq, hs]
        m = None
        for j in range(i + 1):
            s = lax.dot_general(k_ref[j * tq:(j + 1) * tq, hs], q, (((1,), (1,)), ((), ())),
                                preferred_element_type=F32)
            if j == i:
                s = jnp.where(causal, s, NEG)
            s_sc[base[u] + j] = s
            bm = jnp.max(s, axis=0, keepdims=True)
            m = bm if m is None else jnp.maximum(m, bm)
        return m

    def attend(u, m):
        i, hh = units[u]
        l = None
        for j in range(i + 1):
            e = jnp.exp2(s_sc[base[u] + j] - m)
            ls = jnp.sum(e, axis=0, keepdims=True)
            l = ls if l is None else l + ls
            e_sc[(base[u] + j) * tq:(base[u] + j + 1) * tq, :] = e.astype(BF16)
        acc = jnp.dot(vt[hh * B_V:(hh + 1) * B_V, 0:(i + 1) * tq], e_sc[base[u] * tq:base[u + 1] * tq, :],
                      preferred_element_type=F32)
        o = acc / l
        return o * lax.rsqrt(jnp.mean(o * o, axis=0, keepdims=True) + EPS)

    m_next = logits(0)
    normed = []
    for u, (i, hh) in enumerate(units):
        m_cur = m_next
        if u + 1 < len(units):
            m_next = logits(u + 1)
        normed.append(attend(u, m_cur))
        if hh == 1:
            o_ref[i * tq:(i + 1) * tq, :] = (jnp.concatenate(normed, axis=0).T * gain).astype(o_ref.dtype)
            normed = []


def _mla(bq, bk, bv, gain_b, batch, seq):
    t = bq.shape[0]
    pairs = B_HEADS // 2
    tq = MLA_BLOCK
    assert seq % tq == 0
    nq = seq // tq
    nslots = 2 * (nq * (nq + 1) // 2)
    return pl.pallas_call(
        functools.partial(_mla_kernel, seq=seq, tq=tq),
        grid=(batch, pairs),
        in_specs=[
            pl.BlockSpec((seq, 2 * LANES), lambda b, g: (b, g)),
            pl.BlockSpec((seq, 2 * LANES), lambda b, g: (b, g)),
            pl.BlockSpec((seq, LANES), lambda b, g: (b, g)),
            pl.BlockSpec((1, LANES), lambda b, g: (0, g)),
        ],
        out_specs=pl.BlockSpec((seq, LANES), lambda b, g: (b, g)),
        out_shape=jax.ShapeDtypeStruct((t, B_WIDTH), BF16),
        scratch_shapes=[pltpu.VMEM((LANES, seq), BF16),
                        pltpu.VMEM((nslots, tq, tq), F32), pltpu.VMEM((nslots * tq, tq), BF16)],
        compiler_params=pltpu.CompilerParams(
            dimension_semantics=("arbitrary", "arbitrary"), vmem_limit_bytes=VMEM_LIMIT),
        name="latent_attention",
    )(bq, bk, bv, gain_b)


def _retention_consts():
    h = C_HEADS
    log_g = np.log(1.0 - 2.0 ** (-5.0 - np.arange(h))).astype(np.float32)
    i = np.arange(C_CHUNK, dtype=np.float32)
    rel = i[:, None] - i[None, :]
    decay = (np.exp(np.maximum(rel, 0.0)[None] * log_g[:, None, None]) * (rel >= 0)[None]).astype(np.float32)
    xi = np.exp((i + 1.0)[None, :] * log_g[:, None]).astype(np.float32)
    zeta = np.exp((C_CHUNK - 1.0 - i)[None, :] * log_g[:, None]).astype(np.float32)
    chunk_decay = np.exp(C_CHUNK * log_g).astype(np.float32)
    decay_all = np.concatenate(list(decay), axis=1)
    xi_mat = np.repeat(xi.T, C_VAL, axis=1)
    zeta_t = np.zeros((C_QPAD, C_CHUNK), np.float32)
    zeta_t[:C_QK] = np.repeat(zeta, C_KEY, axis=0)
    cd = np.repeat(chunk_decay, C_VAL)[None, :]
    bd = np.zeros((C_QPAD, C_WIDTH), np.float32)
    for hh in range(h):
        bd[hh * C_KEY:(hh + 1) * C_KEY, hh * C_VAL:(hh + 1) * C_VAL] = 1.0
    return tuple(jnp.asarray(a) for a in (decay_all, xi_mat, zeta_t, cd, bd))


def _retention_kernel(qk_ref, v_ref, g_ref, decay_ref, xi_ref, zeta_ref, cd_ref, bd_ref, gain_ref,
                      o_ref, *, seq):
    lane = lax.broadcasted_iota(jnp.int32, (C_CHUNK, LANES), 1)
    lo = lane < C_VAL
    gain = gain_ref[...]
    state = jnp.zeros((C_QPAD, C_WIDTH), F32)
    kshape, vshape = (C_QPAD, C_HEADS * C_CHUNK), (C_HEADS * C_CHUNK, C_WIDTH)
    k_on = (lax.broadcasted_iota(jnp.int32, kshape, 0) // C_KEY) == (lax.broadcasted_iota(jnp.int32, kshape, 1) // C_CHUNK)
    v_on = (lax.broadcasted_iota(jnp.int32, vshape, 0) // C_CHUNK) == (lax.broadcasted_iota(jnp.int32, vshape, 1) // C_VAL)

    for n in range(seq // C_CHUNK):
        rows = slice(n * C_CHUNK, (n + 1) * C_CHUNK)
        q = qk_ref[rows, 0:C_QPAD]
        k_t = qk_ref[rows, C_QPAD:2 * C_QPAD].astype(F32).T
        v = v_ref[rows, :]
        k_bd = jnp.where(k_on, jnp.concatenate([k_t] * C_HEADS, axis=1), 0.0).astype(BF16)
        s = jnp.dot(q, k_bd, preferred_element_type=F32) * decay_ref[...]
        v_bd = jnp.where(v_on, jnp.concatenate([v] * C_HEADS, axis=0), jnp.zeros((), BF16))
        o = (jnp.dot(s.astype(BF16), v_bd, preferred_element_type=F32)
             + jnp.dot(q, state.astype(BF16), preferred_element_type=F32) * xi_ref[...])
        upd = jnp.dot((k_t * zeta_ref[...]).astype(BF16), v, preferred_element_type=F32)
        state = state * cd_ref[...] + upd * bd_ref[...]
        for t in range(C_WIDTH // LANES):
            sl = slice(t * LANES, (t + 1) * LANES)
            x = o[:, sl]
            mu_lo = jnp.sum(jnp.where(lo, x, 0.0), axis=1, keepdims=True)
            mu_hi = jnp.sum(jnp.where(lo, 0.0, x), axis=1, keepdims=True)
            dlt = x - jnp.where(lo, mu_lo, mu_hi) * (1.0 / C_VAL)
            sq = dlt * dlt
            var_lo = jnp.sum(jnp.where(lo, sq, 0.0), axis=1, keepdims=True)
            var_hi = jnp.sum(jnp.where(lo, 0.0, sq), axis=1, keepdims=True)
            y = dlt * lax.rsqrt(jnp.where(lo, var_lo, var_hi) * (1.0 / C_VAL) + EPS)
            o_ref[rows, sl] = (y * g_ref[rows, sl].astype(F32) * gain[:, sl]).astype(o_ref.dtype)


def _retention(cqk, cv, cg, gain_c, consts, batch, seq):
    t = cqk.shape[0]
    tok = lambda width: pl.BlockSpec((seq, width), lambda b: (b, 0))
    full = lambda a: _const_spec(a.shape, lambda b: (0,) * a.ndim)
    return pl.pallas_call(
        functools.partial(_retention_kernel, seq=seq),
        grid=(batch,),
        in_specs=[tok(2 * C_QPAD), tok(C_WIDTH), tok(C_WIDTH)] + [full(a) for a in consts] + [full(gain_c)],
        out_specs=tok(C_WIDTH),
        out_shape=jax.ShapeDtypeStruct((t, C_WIDTH), BF16),
        compiler_params=pltpu.CompilerParams(
            dimension_semantics=("arbitrary",), vmem_limit_bytes=VMEM_LIMIT),
        name="retention",
    )(cqk, cv, cg, *consts, gain_c)


def _outffn_kernel(x_ref, ma_ref, mb_ref, mc_ref, mod_ref, wo_ref, g_ref, wg_ref, wu_ref, wd_ref, fg_ref,
                   *refs, final, fchunk, nsplit, ncast):
    cast_in, o_ref, cast_out = refs[:ncast], refs[ncast], refs[ncast + 1:2 * ncast + 1]
    mix_ref, h_ref, hid_ref = refs[2 * ncast + 1:]
    _cast_slabs(cast_in, cast_out)
    tm = x_ref.shape[0]
    parts = [slice(k * tm // nsplit, (k + 1) * tm // nsplit) for k in range(nsplit)]
    mix_ref[:, 0:A_WIDTH] = ma_ref[...]
    mix_ref[:, A_WIDTH:A_WIDTH + B_WIDTH] = mb_ref[...]
    mix_ref[:, A_WIDTH + B_WIDTH:] = mc_ref[...]
    for r in parts:
        att = jnp.dot(mix_ref[r, :], wo_ref[...], preferred_element_type=F32)
        o_ref[r, :] = x_ref[r, :] + mod_ref[2:3, :] * att
    for r in parts:
        h_ref[r, :] = (_rms(o_ref[r, :]) * g_ref[...] * (1.0 + mod_ref[4:5, :]) + mod_ref[3:4, :]).astype(BF16)
    hidden = wg_ref.shape[1]
    for r in parts:
        for j in range(hidden // fchunk):
            sl = slice(j * fchunk, (j + 1) * fchunk)
            gate = jnp.dot(h_ref[r, :], wg_ref[:, sl], preferred_element_type=F32)
            up = jnp.dot(h_ref[r, :], wu_ref[:, sl], preferred_element_type=F32)
            hid_ref[r, sl] = (_silu(gate) * up).astype(BF16)
    for r in parts:
        ffn = jnp.dot(hid_ref[r, :], wd_ref[...], preferred_element_type=F32)
        y = o_ref[r, :] + mod_ref[5:6, :] * ffn
        if final:
            y = _rms(y) * fg_ref[...]
        o_ref[r, :] = y


def _outffn(x, mix_a, mix_b, mix_c, mods, layer, prep, ffn_w, final_g, batch, seq, tm, final, cast_layer):
    t, d = x.shape
    nt = seq // tm
    w_out, w_gate, w_up, w_down = ffn_w
    hidden = w_gate.shape[1]
    whole = lambda a: _const_spec(a.shape, lambda b, i: (0,) * a.ndim)
    row = lambda w: pl.BlockSpec((tm, w), lambda b, i: (b * nt + i, 0))
    wl = lambda a: _const_spec((None,) + a.shape[1:], lambda b, i: (layer,) + (0,) * (a.ndim - 1))
    cast_ws, cast_in_specs, cast_out_specs, cast_shapes = _cast_plumbing(
        prep, cast_layer, batch * nt, lambda b, i: b * nt + i)
    return pl.pallas_call(
        functools.partial(_outffn_kernel, final=final, fchunk=MXU_WIDTH, nsplit=FFN_SPLIT, ncast=len(cast_ws)),
        grid=(batch, nt),
        in_specs=[
            row(d), row(A_WIDTH), row(B_WIDTH), row(C_WIDTH),
            pl.BlockSpec((None, None, 6, d), lambda b, i: (layer, b, 0, 0)),
            whole(w_out), wl(prep["norm2_g"]), whole(w_gate), whole(w_up), whole(w_down),
            _const_spec((1, d), lambda b, i: (0, 0)),
            *cast_in_specs,
        ],
        out_specs=[row(d)] + cast_out_specs,
        out_shape=[jax.ShapeDtypeStruct((t, d), F32)] + cast_shapes,
        scratch_shapes=[pltpu.VMEM((tm, d), BF16), pltpu.VMEM((tm, d), BF16), pltpu.VMEM((tm, hidden), BF16)],
        compiler_params=pltpu.CompilerParams(
            dimension_semantics=("arbitrary", "arbitrary"), vmem_limit_bytes=VMEM_LIMIT),
        name="outproj_ffn",
    )(x, mix_a, mix_b, mix_c, mods, w_out, prep["norm2_g"], w_gate, w_up, w_down, final_g, *cast_ws)


def _prepare(norm1_g, w_in, mla_q_norm, mla_kv_norm, mla_w_uq, mla_w_ukv, w_out, norm2_g,
             ffn_w_gate, ffn_w_up, ffn_w_down):
    depth, d, _ = w_in.shape
    b_lat0 = 3 * A_WIDTH
    rope0 = b_lat0 + B_Q_RANK + B_KV_RANK
    c0 = rope0 + B_ROPE
    assert C_QK + B_ROPE <= C_QPAD and C_QK % LANES == B_NOPE
    assert c0 + 2 * C_QK + 2 * C_WIDTH == w_in.shape[2] and rope0 == OFF_C
    uq = mla_w_uq.reshape(depth, B_Q_RANK, B_HEADS, B_NOPE + B_ROPE)
    uq = jnp.pad(uq, ((0, 0), (0, 0), (0, 0), (0, LANES - B_NOPE - B_ROPE)))
    ukv = mla_w_ukv.reshape(depth, B_KV_RANK, B_HEADS, B_NOPE + B_V)
    uk = jnp.pad(ukv[..., :B_NOPE], ((0, 0), (0, 0), (0, 0), (0, LANES - B_NOPE)))
    uv = ukv[..., B_NOPE:]
    return {
        "norm1_g": norm1_g[:, None, :],
        "w_in": jnp.swapaxes(w_in, 1, 2),
        "q_norm": mla_q_norm[:, None, :],
        "kv_norm": mla_kv_norm[:, None, :],
        "w_uq": uq.reshape(depth, B_Q_RANK, B_HEADS * LANES).astype(BF16),
        "w_uk": uk.reshape(depth, B_KV_RANK, B_HEADS * LANES).astype(BF16),
        "w_uv": uv.reshape(depth, B_KV_RANK, B_WIDTH).astype(BF16),
        "w_out": w_out,
        "norm2_g": norm2_g[:, None, :],
        "w_gate": ffn_w_gate,
        "w_up": ffn_w_up,
        "w_down": ffn_w_down,
    }


def kernel(x, c, positions, rel_bias, ada_w, ada_b, norm1_g, w_in, mla_q_norm, mla_kv_norm, mla_w_uq,
           mla_w_ukv, mix_gain, w_out, norm2_g, ffn_w_gate, ffn_w_up, ffn_w_down, final_norm):
    batch, seq, d = x.shape
    depth = w_in.shape[0]
    tm = ROW_TILE
    assert seq % tm == 0 and seq % C_CHUNK == 0

    prep = _prepare(norm1_g, w_in, mla_q_norm, mla_kv_norm, mla_w_uq, mla_w_ukv, w_out, norm2_g,
                    ffn_w_gate, ffn_w_up, ffn_w_down)
    mods = _mods(c, ada_w, ada_b).reshape(depth, batch, 6, d)
    tables = _rope_tables(positions)
    bias = _bias_tables(rel_bias)
    ret_consts = _retention_consts()
    final_g = final_norm[None, :]

    xf = x.reshape(batch * seq, d)
    for l in range(depth):
        a_qkv, bq, bk, bv, cqk, cv, cg, *cast0 = _inproj(xf, mods, l, prep, tables, batch, seq, tm,
                                                         cast_layer=0 if l == 0 else None)
        ffn_w = cast0 if l == 0 else ffn_w_next
        gain = mix_gain[l][None, :]
        mix_a = _dilated(a_qkv, bias, gain[:, :A_WIDTH], batch, seq)
        mix_b = _mla(bq, bk, bv, gain[:, A_WIDTH:A_WIDTH + B_WIDTH], batch, seq)
        mix_c = _retention(cqk, cv, cg, gain[:, A_WIDTH + B_WIDTH:], ret_consts, batch, seq)
        xf, *ffn_w_next = _outffn(xf, mix_a, mix_b, mix_c, mods, l, prep, ffn_w, final_g, batch, seq, tm,
                                  final=(l == depth - 1), cast_layer=l + 1 if l + 1 < depth else None)
    return xf.reshape(batch, seq, d)
```

```python
import functools

import numpy as np
import jax
import jax.numpy as jnp
from jax import lax
from jax.experimental import pallas as pl
from jax.experimental.pallas import tpu as pltpu

F32 = jnp.float32
BF16 = jnp.bfloat16

HEAD_DIM = 64
A_HEADS = 6
A_PATTERNS = ((128, 1), (512, 4), (2048, 16))
A_WIDTH = A_HEADS * HEAD_DIM
B_HEADS = 4
B_NOPE = 64
B_ROPE = 32
B_V = 64
B_Q_RANK = 256
B_KV_RANK = 128
B_WIDTH = B_HEADS * B_V
C_HEADS = 6
C_KEY = 32
C_VAL = 64
C_WIDTH = C_HEADS * C_VAL
C_QK = C_HEADS * C_KEY
C_CHUNK = 128
N_BUCKETS = 32
MAX_DISTANCE = 2048
ROPE_BASE = 10000.0
EPS = 1e-6

LANES = 128
WIN = 128
STEP = 4
NEG = -1e30
LOG2E = 1.4426950408889634
FFN_CAST = ("w_out", "w_gate", "w_up", "w_down")
VMEM_LIMIT = 56 * 1024 * 1024
MXU_WIDTH = 256
ROW_TILE = 512
MLA_BLOCK = 512
MODS_COL_TILE = 1536
ROPE_ROW_TILE = 2048
FFN_SPLIT = 2
INPROJ_SPLIT = 2

OFF_AQ = 0
OFF_BKV = OFF_AQ + A_WIDTH
OFF_AKV = OFF_BKV + B_KV_RANK
OFF_BQ = OFF_AKV + 2 * A_WIDTH
OFF_C = OFF_BQ + B_Q_RANK
C_QPAD = 2 * LANES
OFF_CV = OFF_C + 2 * C_QPAD
OFF_CG = OFF_CV + C_WIDTH
IN_COLS = OFF_CG + C_WIDTH
assert all(off % MXU_WIDTH == 0 for off in (OFF_AKV, OFF_BQ, OFF_C, OFF_CV, IN_COLS))


def _const_spec(shape, index_map):
    return pl.BlockSpec(shape, index_map, pipeline_mode=pl.Buffered(1))


def _silu(x):
    return x / (1.0 + jnp.exp(-x))


def _rms(x):
    return x * lax.rsqrt(jnp.mean(x * x, axis=-1, keepdims=True) + EPS)


def _mods_kernel(c_ref, w_ref, b_ref, o_ref):
    def split(a):
        hi = a.astype(BF16)
        return hi, (a - hi.astype(F32)).astype(BF16)

    c_hi, c_lo = split(_silu(c_ref[...]))
    w_hi, w_lo = split(w_ref[...])
    nb = c_hi.shape[0]
    both = jnp.dot(jnp.concatenate([c_hi, c_lo], axis=0), w_hi, preferred_element_type=F32)
    o_ref[...] = (both[:nb] + both[nb:] + jnp.dot(c_hi, w_lo, preferred_element_type=F32)) + b_ref[...]


def _mods(c, ada_w, ada_b):
    depth, d, n = ada_w.shape
    b = c.shape[0]
    tn = MODS_COL_TILE
    return pl.pallas_call(
        _mods_kernel,
        grid=(depth, n // tn),
        in_specs=[
            pl.BlockSpec((b, d), lambda l, j: (0, 0)),
            pl.BlockSpec((None, d, tn), lambda l, j: (l, 0, j)),
            pl.BlockSpec((None, 1, tn), lambda l, j: (l, 0, j)),
        ],
        out_specs=pl.BlockSpec((None, b, tn), lambda l, j: (l, 0, j)),
        out_shape=jax.ShapeDtypeStruct((depth, b, n), F32),
        compiler_params=pltpu.CompilerParams(vmem_limit_bytes=VMEM_LIMIT),
        name="adaln_mods",
    )(c, ada_w, ada_b.reshape(depth, 1, n))


def _rope_kernel(pos_ref, f_ref, s_ref, cos_ref, sin_ref):
    ang = pos_ref[...].astype(F32) * f_ref[...]
    cos_ref[...] = jnp.cos(ang)
    sin_ref[...] = jnp.sin(ang) * s_ref[...]


def _rope_tables(positions):
    t = positions.size
    half = C_KEY // 2
    assert B_ROPE == C_KEY and B_NOPE % C_KEY == 0
    inv_freq = (1.0 / (ROPE_BASE ** (np.arange(half, dtype=np.float32) / half))).astype(np.float32)
    ones = np.ones(half, np.float32)
    freqs = jnp.asarray(np.tile(np.concatenate([inv_freq, inv_freq]), LANES // C_KEY)[None, :])
    signs = jnp.asarray(np.tile(np.concatenate([-ones, ones]), LANES // C_KEY)[None, :])
    tr = ROPE_ROW_TILE
    out = jax.ShapeDtypeStruct((t, LANES), F32)
    row = pl.BlockSpec((tr, LANES), lambda i: (i, 0))
    return pl.pallas_call(
        _rope_kernel,
        grid=(t // tr,),
        in_specs=[
            pl.BlockSpec((tr, 1), lambda i: (i, 0)),
            pl.BlockSpec((1, LANES), lambda i: (0, 0)),
            pl.BlockSpec((1, LANES), lambda i: (0, 0)),
        ],
        out_specs=[row, row],
        out_shape=[out, out],
        name="rope_tables",
    )(positions.reshape(t, 1), freqs, signs)


def _t5_bucket(dist):
    max_exact = N_BUCKETS // 2
    safe = np.maximum(dist, 1).astype(np.float32)
    large = max_exact + (np.log(safe / max_exact) / np.log(MAX_DISTANCE / max_exact)
                         * (N_BUCKETS - max_exact)).astype(np.int32)
    large = np.minimum(large, N_BUCKETS - 1)
    return np.where(dist < max_exact, dist, large).astype(np.int32)


def _bias_kernel(bmap_ref, rb_ref, o_ref):
    h = pl.program_id(1)
    bm = bmap_ref[...]
    t = jnp.full(bm.shape, NEG, F32)
    for b in range(N_BUCKETS):
        t = jnp.where(bm == b, rb_ref[b, h] * LOG2E, t)
    o_ref[...] = t


def _bias_tables(rel_bias):
    qi = np.arange(WIN)[None, :]
    c = np.arange(2 * WIN)[:, None]
    j = qi - c + WIN
    maps = []
    for (w, d) in A_PATTERNS:
        assert w // d == WIN
        bucket = _t5_bucket(np.arange(WIN + 1, dtype=np.int32) * d)
        maps.append(np.where((j >= 0) & (j <= WIN), bucket[np.clip(j, 0, WIN)], -1).astype(np.int32))
    bmap = jnp.asarray(np.stack(maps))
    npat = len(A_PATTERNS)
    return pl.pallas_call(
        _bias_kernel,
        grid=(npat, A_HEADS),
        in_specs=[
            pl.BlockSpec((None, 2 * WIN, WIN), lambda p, h: (p, 0, 0)),
            pl.BlockSpec(memory_space=pltpu.SMEM),
        ],
        out_specs=pl.BlockSpec((None, 2 * WIN, WIN), lambda p, h: (p, 0, h)),
        out_shape=jax.ShapeDtypeStruct((npat, 2 * WIN, A_HEADS * WIN), F32),
        name="t5_bias_tables",
    )(bmap, rel_bias)


def _cast_plumbing(prep, cast_layer, steps, step_of):
    if cast_layer is None:
        return [], [], [], []
    ws, in_specs, out_specs = [prep[name] for name in FFN_CAST], [], []
    for w in ws:
        nslab = steps
        while w.shape[1] % (16 * nslab):
            assert nslab % 2 == 0
            nslab //= 2
        rep = steps // nslab
        block = (w.shape[1] // nslab, w.shape[2])
        in_specs.append(pl.BlockSpec((None,) + block, lambda *g, rep=rep: (cast_layer, step_of(*g) // rep, 0)))
        out_specs.append(pl.BlockSpec(block, lambda *g, rep=rep: (step_of(*g) // rep, 0)))
    return ws, in_specs, out_specs, [jax.ShapeDtypeStruct(w.shape[1:], BF16) for w in ws]


def _cast_slabs(cast_in, cast_out):
    for src, dst in zip(cast_in, cast_out):
        dst[...] = src[...].astype(BF16)


def _rot_half(x, x1_mask):
    w = x.shape[-1]
    half = B_ROPE // 2
    return jnp.where(x1_mask, pltpu.roll(x, w - half, 1), pltpu.roll(x, half, 1))


def _inproj_kernel(x_ref, mod_ref, g_ref, win_ref, qg_ref, kvg_ref, wuq_ref, wuk_ref, wuv_ref,
                   cc_ref, sc_ref, ks_ref, *refs, ncast):
    cast_in, refs = refs[:ncast], refs[ncast:]
    a_ref, bq_ref, bk_ref, bv_ref, cqk_ref, cv_ref, cg_ref = refs[:7]
    cast_out, w_ref = refs[7:7 + ncast], refs[7 + ncast]
    _cast_slabs(cast_in, cast_out)

    @pl.when((pl.program_id(0) == 0) & (pl.program_id(1) == 0))
    def _():
        d = win_ref.shape[1]
        bq0 = 3 * A_WIDTH
        bkv0 = bq0 + B_Q_RANK
        rope0 = bkv0 + B_KV_RANK
        c0 = rope0 + B_ROPE
        zeros = lambda n: jnp.zeros((n, d), F32)

        def place(dst, src, n):
            for k in range(0, n, LANES):
                w_ref[:, dst + k:dst + k + LANES] = win_ref[src + k:src + k + LANES, :].T.astype(BF16)

        place(OFF_AQ, 0, A_WIDTH)
        place(OFF_BKV, bkv0, B_KV_RANK)
        place(OFF_AKV, A_WIDTH, 2 * A_WIDTH)
        place(OFF_BQ, bq0, B_Q_RANK)
        q_rows = jnp.concatenate([win_ref[c0:c0 + C_QK, :], win_ref[rope0:c0, :],
                                  zeros(C_QPAD - C_QK - B_ROPE)], axis=0)
        w_ref[:, OFF_C:OFF_C + C_QPAD] = q_rows.T.astype(BF16)
        k_rows = jnp.concatenate([win_ref[c0 + C_QK:c0 + 2 * C_QK, :], zeros(C_QPAD - C_QK)], axis=0)
        w_ref[:, OFF_C + C_QPAD:OFF_CV] = k_rows.T.astype(BF16)
        place(OFF_CV, c0 + 2 * C_QK, 2 * C_WIDTH)

    tm = x_ref.shape[0] // INPROJ_SPLIT
    parts = [slice(k * tm, (k + 1) * tm) for k in range(INPROJ_SPLIT)]
    lane = lax.broadcasted_iota(jnp.int32, (tm, LANES), 1)
    b_x1 = (lane >= B_NOPE) & (lane < B_NOPE + B_ROPE // 2)
    b_rope = (lane >= B_NOPE) & (lane < B_NOPE + B_ROPE)
    c_x1 = (lane & (C_KEY // 2)) == 0

    def proj(h, lo, hi):
        return jnp.dot(h, w_ref[:, lo:hi], preferred_element_type=F32)

    hs = [(_rms(x_ref[r, :]) * g_ref[...] * (1.0 + mod_ref[1:2, :]) + mod_ref[0:1, :]).astype(BF16)
          for r in parts]
    lat = []
    for r, h in zip(parts, hs):
        aq_kv = proj(h, OFF_AQ, OFF_AKV)
        a_ref[r, 0:A_WIDTH] = aq_kv[:, :A_WIDTH] * (HEAD_DIM ** -0.5 * LOG2E)
        lat.append((proj(h, OFF_BQ, OFF_C), aq_kv[:, OFF_BKV:OFF_AKV], proj(h, OFF_C, OFF_CV)))
    normed = [((_rms(q_lat) * qg_ref[...]).astype(BF16), (_rms(kv_lat) * kvg_ref[...]).astype(BF16))
              for q_lat, kv_lat, _ in lat]
    for r, h in zip(parts, hs):
        a_ref[r, A_WIDTH:3 * A_WIDTH] = proj(h, OFF_AKV, OFF_BQ)
    ups = []
    for r, h, (qn, kvn) in zip(parts, hs, normed):
        ups.append((jnp.dot(qn, wuq_ref[...], preferred_element_type=F32),
                    jnp.dot(kvn, wuk_ref[...], preferred_element_type=F32)))
        bv_ref[r, :] = jnp.dot(kvn, wuv_ref[...], preferred_element_type=F32).astype(BF16)
        v_gate = proj(h, OFF_CV, IN_COLS)
        cv_ref[r, :] = v_gate[:, :C_WIDTH].astype(BF16)
        cg_ref[r, :] = _silu(v_gate[:, C_WIDTH:]).astype(BF16)
    b_scale = (B_NOPE + B_ROPE) ** -0.5 * LOG2E
    for r, (_, _, qk), (q, k_nope) in zip(parts, lat, ups):
        cos_c, sin_c = cc_ref[r, :], sc_ref[r, :]
        cos_b, sin_b = jnp.where(b_rope, cos_c, 1.0), jnp.where(b_rope, sin_c, 0.0)
        k_rope = jnp.where(lane >= B_NOPE, qk[:, LANES:2 * LANES], 0.0)
        k_pe = k_rope * cos_b + _rot_half(k_rope, b_x1) * sin_b
        for hh in range(B_HEADS):
            sl = slice(hh * LANES, (hh + 1) * LANES)
            qh = q[:, sl]
            bq_ref[r, sl] = ((qh * cos_b + _rot_half(qh, b_x1) * sin_b) * b_scale).astype(BF16)
            bk_ref[r, sl] = (k_nope[:, sl] + k_pe).astype(BF16)
        for t in range(2 * C_QPAD // LANES):
            sl = slice(t * LANES, (t + 1) * LANES)
            xt = qk[:, sl]
            cqk_ref[r, sl] = ((xt * cos_c + _rot_half(xt, c_x1) * sin_c) * ks_ref[:, sl]).astype(BF16)


def _inproj(x, mods, layer, prep, tables, batch, seq, tm, cast_layer):
    t, d = x.shape
    nt = seq // tm
    row = lambda w: pl.BlockSpec((tm, w), lambda b, i: (b * nt + i, 0))
    wl = lambda a: _const_spec((None,) + a.shape[1:], lambda b, i: (layer,) + (0,) * (a.ndim - 1))
    cos_c, sin_c = tables
    kscale = jnp.asarray(np.concatenate([np.ones(C_QPAD, np.float32),
                                         np.full(C_QPAD, C_KEY ** -0.5, np.float32)])[None, :])
    outs = [(3 * A_WIDTH, F32), (B_HEADS * LANES, BF16), (B_HEADS * LANES, BF16), (B_WIDTH, BF16),
            (2 * C_QPAD, BF16), (C_WIDTH, BF16), (C_WIDTH, BF16)]
    cast_ws, cast_in_specs, cast_out_specs, cast_shapes = _cast_plumbing(
        prep, cast_layer, batch * nt, lambda b, i: b * nt + i)
    return pl.pallas_call(
        functools.partial(_inproj_kernel, ncast=len(cast_ws)),
        grid=(batch, nt),
        in_specs=[
            row(d),
            pl.BlockSpec((None, None, 6, d), lambda b, i: (layer, b, 0, 0)),
            wl(prep["norm1_g"]), wl(prep["w_in"]), wl(prep["q_norm"]), wl(prep["kv_norm"]),
            wl(prep["w_uq"]), wl(prep["w_uk"]), wl(prep["w_uv"]),
            row(LANES), row(LANES),
            _const_spec((1, 2 * C_QPAD), lambda b, i: (0, 0)),
            *cast_in_specs,
        ],
        out_specs=[row(w) for w, _ in outs] + cast_out_specs,
        out_shape=[jax.ShapeDtypeStruct((t, w), dt) for w, dt in outs] + cast_shapes,
        scratch_shapes=[pltpu.VMEM((d, IN_COLS), BF16)],
        compiler_params=pltpu.CompilerParams(
            dimension_semantics=("arbitrary", "arbitrary"), vmem_limit_bytes=VMEM_LIMIT),
        name="inproj",
    )(x, mods, prep["norm1_g"], prep["w_in"], prep["q_norm"], prep["kv_norm"],
      prep["w_uq"], prep["w_uk"], prep["w_uv"], cos_c, sin_c, kscale, *cast_ws)


def _class_major_blocks(seq, level):
    n = seq // STEP
    perm = np.arange(seq)
    for _ in range(level):
        perm = perm.reshape(n, STEP).T.reshape(seq)
    blocks = perm.reshape(seq // WIN, WIN)
    stride = STEP ** level
    assert (blocks == blocks[:, :1] + stride * np.arange(WIN)).all()
    return [(int(b[0]), stride) for b in blocks]


def _dilated_kernel(q_ref, k_ref, v_ref, bias_ref, gain_ref, o_ref,
                    qf, kf, vf, qlo, qhi, kp, vt, *stage_refs, seq):
    npat = len(A_PATTERNS)
    nblk = seq // WIN
    s_sc, out_s, lse_s = (stage_refs[i * npat:(i + 1) * npat] for i in range(3))
    lane = lax.broadcasted_iota(jnp.int32, (WIN, LANES), 1)
    lo_lane = lane < HEAD_DIM
    lo_row = lax.broadcasted_iota(jnp.int32, (LANES, WIN), 0) < HEAD_DIM

    n = seq // STEP
    for p in range(npat):
        for r in range(STEP):
            rows = slice(r * n, (r + 1) * n)
            src = rows if p == 0 else pl.ds(r, n, stride=STEP)
            level_below = (q_ref, k_ref, v_ref) if p <= 1 else (qf.at[p - 2], kf.at[p - 2], vf.at[p - 2])
            q, k, v = (f[src, :] for f in level_below)
            if 0 < p < npat - 1:
                qf[p - 1, rows, :], kf[p - 1, rows, :], vf[p - 1, rows, :] = q, k, v
            lo_q = lax.broadcasted_iota(jnp.int32, q.shape, 1) < HEAD_DIM
            qlo[p, rows, :] = jnp.where(lo_q, q, 0.0).astype(BF16)
            qhi[p, rows, :] = jnp.where(lo_q, 0.0, q).astype(BF16)
            vt[p, :, rows] = v.T.astype(BF16)
            kp[p, rows, :] = k.astype(BF16)

    def key_rows(p, blk):
        r0 = blk * WIN
        first = blk % (nblk // A_PATTERNS[p][1]) == 0
        return (r0 if first else r0 - WIN), r0 + WIN

    for p in range(npat - 1, -1, -1):
        for blk in range(nblk):
            k0, k1 = key_rows(p, blk)
            r0 = blk * WIN
            q2 = jnp.concatenate([qlo[p, r0:r0 + WIN, :], qhi[p, r0:r0 + WIN, :]], axis=0)
            kcat = kp[p, k0:k1, :]
            bias = bias_ref[p, 2 * WIN - (k1 - k0):, :]
            s_sc[p][blk, 0:k1 - k0, :] = lax.dot_general(
                kcat, q2, (((1,), (1,)), ((), ())), preferred_element_type=F32) + bias
        for blk, (start, stride) in enumerate(_class_major_blocks(seq, p)):
            k0, k1 = key_rows(p, blk)
            s = s_sc[p][blk, 0:k1 - k0, :]
            m = jnp.max(s, axis=0, keepdims=True)
            e = jnp.exp2(s - m)
            l = jnp.sum(e, axis=0, keepdims=True)
            acc = jnp.dot(vt[p, :, k0:k1], e.astype(BF16), preferred_element_type=F32)
            out = acc * (1.0 / l)
            lse = m + jnp.log2(l)
            out = jnp.where(lo_row, out[:, :WIN], out[:, WIN:])
            lse = jnp.where(lo_row, lse[:, :WIN], lse[:, WIN:])
            dst = pl.ds(start, WIN, stride=stride) if stride > 1 else slice(start, start + WIN)
            out_s[p][dst, :] = out.T
            lse_s[p][dst, :] = lse.T

    gain = gain_ref[...]
    for blk in range(nblk):
        sl = slice(blk * WIN, (blk + 1) * WIN)
        lses = [s[sl, :] for s in lse_s]
        mx = functools.reduce(jnp.maximum, lses)
        ws = [jnp.exp2(s - mx) for s in lses]
        o = sum(w * x[sl, :] for w, x in zip(ws, out_s)) / sum(ws)
        sq = o * o
        ms_lo = jnp.sum(jnp.where(lo_lane, sq, 0.0), axis=1, keepdims=True)
        ms_hi = jnp.sum(jnp.where(lo_lane, 0.0, sq), axis=1, keepdims=True)
        inv = lax.rsqrt(jnp.where(lo_lane, ms_lo, ms_hi) * (1.0 / HEAD_DIM) + EPS)
        o_ref[sl, :] = (o * inv * gain).astype(o_ref.dtype)


def _dilated(a_qkv, bias, gain_a, batch, seq):
    t = a_qkv.shape[0]
    pairs = A_WIDTH // LANES
    assert all(d == STEP ** p and seq % (d * WIN) == 0 for p, (_, d) in enumerate(A_PATTERNS))
    npat = bias.shape[0]
    col = lambda off: pl.BlockSpec((seq, LANES), lambda b, g: (b, off + g))
    scratch = ([pltpu.VMEM((npat - 2, seq, LANES), F32) for _ in range(3)]
               + [pltpu.VMEM((npat, seq, LANES), BF16) for _ in range(2)]
               + [pltpu.VMEM((npat, seq, LANES), BF16)]
               + [pltpu.VMEM((npat, LANES, seq), BF16)]
               + [pltpu.VMEM((seq // WIN, 2 * WIN, 2 * WIN), F32) for _ in range(npat)]
               + [pltpu.VMEM((seq, LANES), F32) for _ in range(2 * npat)])
    return pl.pallas_call(
        functools.partial(_dilated_kernel, seq=seq),
        grid=(batch, pairs),
        in_specs=[
            col(0), col(pairs), col(2 * pairs),
            pl.BlockSpec((npat, 2 * WIN, 2 * WIN), lambda b, g: (0, 0, g)),
            pl.BlockSpec((1, LANES), lambda b, g: (0, g)),
        ],
        out_specs=pl.BlockSpec((seq, LANES), lambda b, g: (b, g)),
        out_shape=jax.ShapeDtypeStruct((t, A_WIDTH), BF16),
        scratch_shapes=scratch,
        compiler_params=pltpu.CompilerParams(
            dimension_semantics=("arbitrary", "arbitrary"), vmem_limit_bytes=VMEM_LIMIT),
        name="dilated_attention",
    )(a_qkv, a_qkv, a_qkv, bias, gain_a)


def _mla_kernel(q_ref, k_ref, v_ref, gain_ref, o_ref, vt, s_sc, e_sc, *, seq, tq):
    key_idx = lax.broadcasted_iota(jnp.int32, (tq, tq), 0)
    qry_idx = lax.broadcasted_iota(jnp.int32, (tq, tq), 1)
    causal = key_idx <= qry_idx
    gain = gain_ref[...]
    for c in range(0, seq, tq):
        vt[:, c:c + tq] = v_ref[c:c + tq, :].astype(F32).T.astype(BF16)

    units = [(i, hh) for i in range(seq // tq) for hh in range(2)]
    base = np.concatenate([[0], np.cumsum([i + 1 for i, _ in units])]).tolist()

    def logits(u):
        i, hh = units[u]
        hs = slice(hh * LANES, (hh + 1) * LANES)
        q = q_ref[i * tq:(i + 1) * tq, hs]
        m = None
        for j in range(i + 1):
            s = lax.dot_general(k_ref[j * tq:(j + 1) * tq, hs], q, (((1,), (1,)), ((), ())),
                                preferred_element_type=F32)
            if j == i:
                s = jnp.where(causal, s, NEG)
            s_sc[base[u] + j] = s
            bm = jnp.max(s, axis=0, keepdims=True)
            m = bm if m is None else jnp.maximum(m, bm)
        return m

    def attend(u, m):
        i, hh = units[u]
        l = None
        for j in range(i + 1):
            e = jnp.exp2(s_sc[base[u] + j] - m)
            ls = jnp.sum(e, axis=0, keepdims=True)
            l = ls if l is None else l + ls
            e_sc[(base[u] + j) * tq:(base[u] + j + 1) * tq, :] = e.astype(BF16)
        acc = jnp.dot(vt[hh * B_V:(hh + 1) * B_V, 0:(i + 1) * tq], e_sc[base[u] * tq:base[u + 1] * tq, :],
                      preferred_element_type=F32)
        o = acc / l
        return o * lax.rsqrt(jnp.mean(o * o, axis=0, keepdims=True) + EPS)

    m_next = logits(0)
    normed = []
    for u, (i, hh) in enumerate(units):
        m_cur = m_next
        if u + 1 < len(units):
            m_next = logits(u + 1)
        normed.append(attend(u, m_cur))
        if hh == 1:
            o_ref[i * tq:(i + 1) * tq, :] = (jnp.concatenate(normed, axis=0).T * gain).astype(o_ref.dtype)
            normed = []


def _mla(bq, bk, bv, gain_b, batch, seq):
    t = bq.shape[0]
    pairs = B_HEADS // 2
    tq = MLA_BLOCK
    assert seq % tq == 0
    nq = seq // tq
    nslots = 2 * (nq * (nq + 1) // 2)
    return pl.pallas_call(
        functools.partial(_mla_kernel, seq=seq, tq=tq),
        grid=(batch, pairs),
        in_specs=[
            pl.BlockSpec((seq, 2 * LANES), lambda b, g: (b, g)),
            pl.BlockSpec((seq, 2 * LANES), lambda b, g: (b, g)),
            pl.BlockSpec((seq, LANES), lambda b, g: (b, g)),
            pl.BlockSpec((1, LANES), lambda b, g: (0, g)),
        ],
        out_specs=pl.BlockSpec((seq, LANES), lambda b, g: (b, g)),
        out_shape=jax.ShapeDtypeStruct((t, B_WIDTH), BF16),
        scratch_shapes=[pltpu.VMEM((LANES, seq), BF16),
                        pltpu.VMEM((nslots, tq, tq), F32), pltpu.VMEM((nslots * tq, tq), BF16)],
        compiler_params=pltpu.CompilerParams(
            dimension_semantics=("arbitrary", "arbitrary"), vmem_limit_bytes=VMEM_LIMIT),
        name="latent_attention",
    )(bq, bk, bv, gain_b)


def _retention_consts():
    h = C_HEADS
    log_g = np.log(1.0 - 2.0 ** (-5.0 - np.arange(h))).astype(np.float32)
    i = np.arange(C_CHUNK, dtype=np.float32)
    rel = i[:, None] - i[None, :]
    decay = (np.exp(np.maximum(rel, 0.0)[None] * log_g[:, None, None]) * (rel >= 0)[None]).astype(np.float32)
    xi = np.exp((i + 1.0)[None, :] * log_g[:, None]).astype(np.float32)
    zeta = np.exp((C_CHUNK - 1.0 - i)[None, :] * log_g[:, None]).astype(np.float32)
    chunk_decay = np.exp(C_CHUNK * log_g).astype(np.float32)
    decay_all = np.concatenate(list(decay), axis=1)
    xi_mat = np.repeat(xi.T, C_VAL, axis=1)
    zeta_t = np.zeros((C_QPAD, C_CHUNK), np.float32)
    zeta_t[:C_QK] = np.repeat(zeta, C_KEY, axis=0)
    cd = np.repeat(chunk_decay, C_VAL)[None, :]
    bd = np.zeros((C_QPAD, C_WIDTH), np.float32)
    for hh in range(h):
        bd[hh * C_KEY:(hh + 1) * C_KEY, hh * C_VAL:(hh + 1) * C_VAL] = 1.0
    return tuple(jnp.asarray(a) for a in (decay_all, xi_mat, zeta_t, cd, bd))


def _retention_kernel(qk_ref, v_ref, g_ref, decay_ref, xi_ref, zeta_ref, cd_ref, bd_ref, gain_ref,
                      o_ref, *, seq):
    lane = lax.broadcasted_iota(jnp.int32, (C_CHUNK, LANES), 1)
    lo = lane < C_VAL
    gain = gain_ref[...]
    state = jnp.zeros((C_QPAD, C_WIDTH), F32)
    kshape, vshape = (C_QPAD, C_HEADS * C_CHUNK), (C_HEADS * C_CHUNK, C_WIDTH)
    k_on = (lax.broadcasted_iota(jnp.int32, kshape, 0) // C_KEY) == (lax.broadcasted_iota(jnp.int32, kshape, 1) // C_CHUNK)
    v_on = (lax.broadcasted_iota(jnp.int32, vshape, 0) // C_CHUNK) == (lax.broadcasted_iota(jnp.int32, vshape, 1) // C_VAL)

    for n in range(seq // C_CHUNK):
        rows = slice(n * C_CHUNK, (n + 1) * C_CHUNK)
        q = qk_ref[rows, 0:C_QPAD]
        k_t = qk_ref[rows, C_QPAD:2 * C_QPAD].astype(F32).T
        v = v_ref[rows, :]
        k_bd = jnp.where(k_on, jnp.concatenate([k_t] * C_HEADS, axis=1), 0.0).astype(BF16)
        s = jnp.dot(q, k_bd, preferred_element_type=F32) * decay_ref[...]
        v_bd = jnp.where(v_on, jnp.concatenate([v] * C_HEADS, axis=0), jnp.zeros((), BF16))
        o = (jnp.dot(s.astype(BF16), v_bd, preferred_element_type=F32)
             + jnp.dot(q, state.astype(BF16), preferred_element_type=F32) * xi_ref[...])
        upd = jnp.dot((k_t * zeta_ref[...]).astype(BF16), v, preferred_element_type=F32)
        state = state * cd_ref[...] + upd * bd_ref[...]
        for t in range(C_WIDTH // LANES):
            sl = slice(t * LANES, (t + 1) * LANES)
            x = o[:, sl]
            mu_lo = jnp.sum(jnp.where(lo, x, 0.0), axis=1, keepdims=True)
            mu_hi = jnp.sum(jnp.where(lo, 0.0, x), axis=1, keepdims=True)
            dlt = x - jnp.where(lo, mu_lo, mu_hi) * (1.0 / C_VAL)
            sq = dlt * dlt
            var_lo = jnp.sum(jnp.where(lo, sq, 0.0), axis=1, keepdims=True)
            var_hi = jnp.sum(jnp.where(lo, 0.0, sq), axis=1, keepdims=True)
            y = dlt * lax.rsqrt(jnp.where(lo, var_lo, var_hi) * (1.0 / C_VAL) + EPS)
            o_ref[rows, sl] = (y * g_ref[rows, sl].astype(F32) * gain[:, sl]).astype(o_ref.dtype)


def _retention(cqk, cv, cg, gain_c, consts, batch, seq):
    t = cqk.shape[0]
    tok = lambda width: pl.BlockSpec((seq, width), lambda b: (b, 0))
    full = lambda a: _const_spec(a.shape, lambda b: (0,) * a.ndim)
    return pl.pallas_call(
        functools.partial(_retention_kernel, seq=seq),
        grid=(batch,),
        in_specs=[tok(2 * C_QPAD), tok(C_WIDTH), tok(C_WIDTH)] + [full(a) for a in consts] + [full(gain_c)],
        out_specs=tok(C_WIDTH),
        out_shape=jax.ShapeDtypeStruct((t, C_WIDTH), BF16),
        compiler_params=pltpu.CompilerParams(
            dimension_semantics=("arbitrary",), vmem_limit_bytes=VMEM_LIMIT),
        name="retention",
    )(cqk, cv, cg, *consts, gain_c)


def _outffn_kernel(x_ref, ma_ref, mb_ref, mc_ref, mod_ref, wo_ref, g_ref, wg_ref, wu_ref, wd_ref, fg_ref,
                   *refs, final, fchunk, nsplit, ncast):
    cast_in, o_ref, cast_out = refs[:ncast], refs[ncast], refs[ncast + 1:2 * ncast + 1]
    mix_ref, h_ref, hid_ref = refs[2 * ncast + 1:]
    _cast_slabs(cast_in, cast_out)
    tm = x_ref.shape[0]
    parts = [slice(k * tm // nsplit, (k + 1) * tm // nsplit) for k in range(nsplit)]
    mix_ref[:, 0:A_WIDTH] = ma_ref[...]
    mix_ref[:, A_WIDTH:A_WIDTH + B_WIDTH] = mb_ref[...]
    mix_ref[:, A_WIDTH + B_WIDTH:] = mc_ref[...]
    for r in parts:
        att = jnp.dot(mix_ref[r, :], wo_ref[...], preferred_element_type=F32)
        o_ref[r, :] = x_ref[r, :] + mod_ref[2:3, :] * att
    for r in parts:
        h_ref[r, :] = (_rms(o_ref[r, :]) * g_ref[...] * (1.0 + mod_ref[4:5, :]) + mod_ref[3:4, :]).astype(BF16)
    hidden = wg_ref.shape[1]
    for r in parts:
        for j in range(hidden // fchunk):
            sl = slice(j * fchunk, (j + 1) * fchunk)
            gate = jnp.dot(h_ref[r, :], wg_ref[:, sl], preferred_element_type=F32)
            up = jnp.dot(h_ref[r, :], wu_ref[:, sl], preferred_element_type=F32)
            hid_ref[r, sl] = (_silu(gate) * up).astype(BF16)
    for r in parts:
        ffn = jnp.dot(hid_ref[r, :], wd_ref[...], preferred_element_type=F32)
        y = o_ref[r, :] + mod_ref[5:6, :] * ffn
        if final:
            y = _rms(y) * fg_ref[...]
        o_ref[r, :] = y


def _outffn(x, mix_a, mix_b, mix_c, mods, layer, prep, ffn_w, final_g, batch, seq, tm, final, cast_layer):
    t, d = x.shape
    nt = seq // tm
    w_out, w_gate, w_up, w_down = ffn_w
    hidden = w_gate.shape[1]
    whole = lambda a: _const_spec(a.shape, lambda b, i: (0,) * a.ndim)
    row = lambda w: pl.BlockSpec((tm, w), lambda b, i: (b * nt + i, 0))
    wl = lambda a: _const_spec((None,) + a.shape[1:], lambda b, i: (layer,) + (0,) * (a.ndim - 1))
    cast_ws, cast_in_specs, cast_out_specs, cast_shapes = _cast_plumbing(
        prep, cast_layer, batch * nt, lambda b, i: b * nt + i)
    return pl.pallas_call(
        functools.partial(_outffn_kernel, final=final, fchunk=MXU_WIDTH, nsplit=FFN_SPLIT, ncast=len(cast_ws)),
        grid=(batch, nt),
        in_specs=[
            row(d), row(A_WIDTH), row(B_WIDTH), row(C_WIDTH),
            pl.BlockSpec((None, None, 6, d), lambda b, i: (layer, b, 0, 0)),
            whole(w_out), wl(prep["norm2_g"]), whole(w_gate), whole(w_up), whole(w_down),
            _const_spec((1, d), lambda b, i: (0, 0)),
            *cast_in_specs,
        ],
        out_specs=[row(d)] + cast_out_specs,
        out_shape=[jax.ShapeDtypeStruct((t, d), F32)] + cast_shapes,
        scratch_shapes=[pltpu.VMEM((tm, d), BF16), pltpu.VMEM((tm, d), BF16), pltpu.VMEM((tm, hidden), BF16)],
        compiler_params=pltpu.CompilerParams(
            dimension_semantics=("arbitrary", "arbitrary"), vmem_limit_bytes=VMEM_LIMIT),
        name="outproj_ffn",
    )(x, mix_a, mix_b, mix_c, mods, w_out, prep["norm2_g"], w_gate, w_up, w_down, final_g, *cast_ws)


def _prepare(norm1_g, w_in, mla_q_norm, mla_kv_norm, mla_w_uq, mla_w_ukv, w_out, norm2_g,
             ffn_w_gate, ffn_w_up, ffn_w_down):
    depth, d, _ = w_in.shape
    b_lat0 = 3 * A_WIDTH
    rope0 = b_lat0 + B_Q_RANK + B_KV_RANK
    c0 = rope0 + B_ROPE
    assert C_QK + B_ROPE <= C_QPAD and C_QK % LANES == B_NOPE
    assert c0 + 2 * C_QK + 2 * C_WIDTH == w_in.shape[2]
    uq = mla_w_uq.reshape(depth, B_Q_RANK, B_HEADS, B_NOPE + B_ROPE)
    uq = jnp.pad(uq, ((0, 0), (0, 0), (0, 0), (0, LANES - B_NOPE - B_ROPE)))
    ukv = mla_w_ukv.reshape(depth, B_KV_RANK, B_HEADS, B_NOPE + B_V)
    uk = jnp.pad(ukv[..., :B_NOPE], ((0, 0), (0, 0), (0, 0), (0, LANES - B_NOPE)))
    uv = ukv[..., B_NOPE:]
    return {
        "norm1_g": norm1_g[:, None, :],
        "w_in": jnp.swapaxes(w_in, 1, 2),
        "q_norm": mla_q_norm[:, None, :],
        "kv_norm": mla_kv_norm[:, None, :],
        "w_uq": uq.reshape(depth, B_Q_RANK, B_HEADS * LANES).astype(BF16),
        "w_uk": uk.reshape(depth, B_KV_RANK, B_HEADS * LANES).astype(BF16),
        "w_uv": uv.reshape(depth, B_KV_RANK, B_WIDTH).astype(BF16),
        "w_out": w_out,
        "norm2_g": norm2_g[:, None, :],
        "w_gate": ffn_w_gate,
        "w_up": ffn_w_up,
        "w_down": ffn_w_down,
    }


def kernel(x, c, positions, rel_bias, ada_w, ada_b, norm1_g, w_in, mla_q_norm, mla_kv_norm, mla_w_uq,
           mla_w_ukv, mix_gain, w_out, norm2_g, ffn_w_gate, ffn_w_up, ffn_w_down, final_norm):
    batch, seq, d = x.shape
    depth = w_in.shape[0]
    tm = ROW_TILE
    assert seq % tm == 0 and seq % C_CHUNK == 0

    prep = _prepare(norm1_g, w_in, mla_q_norm, mla_kv_norm, mla_w_uq, mla_w_ukv, w_out, norm2_g,
                    ffn_w_gate, ffn_w_up, ffn_w_down)
    mods = _mods(c, ada_w, ada_b).reshape(depth, batch, 6, d)
    tables = _rope_tables(positions)
    bias = _bias_tables(rel_bias)
    ret_consts = _retention_consts()
    final_g = final_norm[None, :]

    xf = x.reshape(batch * seq, d)
    for l in range(depth):
        a_qkv, bq, bk, bv, cqk, cv, cg, *cast0 = _inproj(xf, mods, l, prep, tables, batch, seq, tm,
                                                         cast_layer=0 if l == 0 else None)
        ffn_w = cast0 if l == 0 else ffn_w_next
        gain = mix_gain[l][None, :]
        mix_a = _dilated(a_qkv, bias, gain[:, :A_WIDTH], batch, seq)
        mix_b = _mla(bq, bk, bv, gain[:, A_WIDTH:A_WIDTH + B_WIDTH], batch, seq)
        mix_c = _retention(cqk, cv, cg, gain[:, A_WIDTH + B_WIDTH:], ret_consts, batch, seq)
        xf, *ffn_w_next = _outffn(xf, mix_a, mix_b, mix_c, mods, l, prep, ffn_w, final_g, batch, seq, tm,
                                  final=(l == depth - 1), cast_layer=l + 1 if l + 1 < depth else None)
    return xf.reshape(batch, seq, d)
```

```python
import functools

import numpy as np
import jax
import jax.numpy as jnp
from jax import lax
from jax.experimental import pallas as pl
from jax.experimental.pallas import tpu as pltpu

F32 = jnp.float32
BF16 = jnp.bfloat16

HEAD_DIM = 64
A_HEADS = 6
A_PATTERNS = ((128, 1), (512, 4), (2048, 16))
A_WIDTH = A_HEADS * HEAD_DIM
B_HEADS = 4
B_NOPE = 64
B_ROPE = 32
B_V = 64
B_Q_RANK = 256
B_KV_RANK = 128
B_WIDTH = B_HEADS * B_V
C_HEADS = 6
C_KEY = 32
C_VAL = 64
C_WIDTH = C_HEADS * C_VAL
C_QK = C_HEADS * C_KEY
C_CHUNK = 128
N_BUCKETS = 32
MAX_DISTANCE = 2048
ROPE_BASE = 10000.0
EPS = 1e-6

LANES = 128
WIN = 128
STEP = 4
NEG = -1e30
LOG2E = 1.4426950408889634
FFN_CAST = ("w_out", "w_gate", "w_up", "w_down")
VMEM_LIMIT = 56 * 1024 * 1024
MXU_WIDTH = 256
ROW_TILE = 512
MLA_BLOCK = 512
MODS_COL_TILE = 1536
ROPE_ROW_TILE = 2048
FFN_SPLIT = 2
INPROJ_SPLIT = 2

OFF_AQ = 0
OFF_BKV = OFF_AQ + A_WIDTH
OFF_AKV = OFF_BKV + B_KV_RANK
OFF_BQ = OFF_AKV + 2 * A_WIDTH
OFF_C = OFF_BQ + B_Q_RANK
C_QPAD = 2 * LANES
OFF_CV = OFF_C + 2 * C_QPAD
OFF_CG = OFF_CV + C_WIDTH
IN_COLS = OFF_CG + C_WIDTH
assert all(off % MXU_WIDTH == 0 for off in (OFF_AKV, OFF_BQ, OFF_C, OFF_CV, IN_COLS))


def _const_spec(shape, index_map):
    return pl.BlockSpec(shape, index_map, pipeline_mode=pl.Buffered(1))


def _silu(x):
    return x / (1.0 + jnp.exp(-x))


def _rms(x):
    return x * lax.rsqrt(jnp.mean(x * x, axis=-1, keepdims=True) + EPS)


def _mods_kernel(c_ref, w_ref, b_ref, o_ref):
    def split(a):
        hi = a.astype(BF16)
        return hi, (a - hi.astype(F32)).astype(BF16)

    c_hi, c_lo = split(_silu(c_ref[...]))
    w_hi, w_lo = split(w_ref[...])
    nb = c_hi.shape[0]
    both = jnp.dot(jnp.concatenate([c_hi, c_lo], axis=0), w_hi, preferred_element_type=F32)
    o_ref[...] = (both[:nb] + both[nb:] + jnp.dot(c_hi, w_lo, preferred_element_type=F32)) + b_ref[...]


def _mods(c, ada_w, ada_b):
    depth, d, n = ada_w.shape
    b = c.shape[0]
    tn = MODS_COL_TILE
    return pl.pallas_call(
        _mods_kernel,
        grid=(depth, n // tn),
        in_specs=[
            pl.BlockSpec((b, d), lambda l, j: (0, 0)),
            pl.BlockSpec((None, d, tn), lambda l, j: (l, 0, j)),
            pl.BlockSpec((None, 1, tn), lambda l, j: (l, 0, j)),
        ],
        out_specs=pl.BlockSpec((None, b, tn), lambda l, j: (l, 0, j)),
        out_shape=jax.ShapeDtypeStruct((depth, b, n), F32),
        compiler_params=pltpu.CompilerParams(vmem_limit_bytes=VMEM_LIMIT),
        name="adaln_mods",
    )(c, ada_w, ada_b.reshape(depth, 1, n))


def _rope_kernel(pos_ref, f_ref, s_ref, cos_ref, sin_ref):
    ang = pos_ref[...].astype(F32) * f_ref[...]
    cos_ref[...] = jnp.cos(ang)
    sin_ref[...] = jnp.sin(ang) * s_ref[...]


def _rope_tables(positions):
    t = positions.size
    half = C_KEY // 2
    assert B_ROPE == C_KEY and B_NOPE % C_KEY == 0
    inv_freq = (1.0 / (ROPE_BASE ** (np.arange(half, dtype=np.float32) / half))).astype(np.float32)
    ones = np.ones(half, np.float32)
    freqs = jnp.asarray(np.tile(np.concatenate([inv_freq, inv_freq]), LANES // C_KEY)[None, :])
    signs = jnp.asarray(np.tile(np.concatenate([-ones, ones]), LANES // C_KEY)[None, :])
    tr = ROPE_ROW_TILE
    out = jax.ShapeDtypeStruct((t, LANES), F32)
    row = pl.BlockSpec((tr, LANES), lambda i: (i, 0))
    return pl.pallas_call(
        _rope_kernel,
        grid=(t // tr,),
        in_specs=[
            pl.BlockSpec((tr, 1), lambda i: (i, 0)),
            pl.BlockSpec((1, LANES), lambda i: (0, 0)),
            pl.BlockSpec((1, LANES), lambda i: (0, 0)),
        ],
        out_specs=[row, row],
        out_shape=[out, out],
        name="rope_tables",
    )(positions.reshape(t, 1), freqs, signs)


def _t5_bucket(dist):
    max_exact = N_BUCKETS // 2
    safe = np.maximum(dist, 1).astype(np.float32)
    large = max_exact + (np.log(safe / max_exact) / np.log(MAX_DISTANCE / max_exact)
                         * (N_BUCKETS - max_exact)).astype(np.int32)
    large = np.minimum(large, N_BUCKETS - 1)
    return np.where(dist < max_exact, dist, large).astype(np.int32)


def _bias_kernel(bmap_ref, rb_ref, o_ref):
    h = pl.program_id(1)
    bm = bmap_ref[...]
    t = jnp.full(bm.shape, NEG, F32)
    for b in range(N_BUCKETS):
        t = jnp.where(bm == b, rb_ref[b, h] * LOG2E, t)
    o_ref[...] = t


def _bias_tables(rel_bias):
    qi = np.arange(WIN)[None, :]
    c = np.arange(2 * WIN)[:, None]
    j = qi - c + WIN
    maps = []
    for (w, d) in A_PATTERNS:
        assert w // d == WIN
        bucket = _t5_bucket(np.arange(WIN + 1, dtype=np.int32) * d)
        maps.append(np.where((j >= 0) & (j <= WIN), bucket[np.clip(j, 0, WIN)], -1).astype(np.int32))
    bmap = jnp.asarray(np.stack(maps))
    npat = len(A_PATTERNS)
    return pl.pallas_call(
        _bias_kernel,
        grid=(npat, A_HEADS),
        in_specs=[
            pl.BlockSpec((None, 2 * WIN, WIN), lambda p, h: (p, 0, 0)),
            pl.BlockSpec(memory_space=pltpu.SMEM),
        ],
        out_specs=pl.BlockSpec((None, 2 * WIN, WIN), lambda p, h: (p, 0, h)),
        out_shape=jax.ShapeDtypeStruct((npat, 2 * WIN, A_HEADS * WIN), F32),
        name="t5_bias_tables",
    )(bmap, rel_bias)


def _cast_plumbing(prep, cast_layer, steps, step_of):
    if cast_layer is None:
        return [], [], [], []
    ws, in_specs, out_specs = [prep[name] for name in FFN_CAST], [], []
    for w in ws:
        nslab = steps
        while w.shape[1] % (16 * nslab):
            assert nslab % 2 == 0
            nslab //= 2
        rep = steps // nslab
        block = (w.shape[1] // nslab, w.shape[2])
        in_specs.append(pl.BlockSpec((None,) + block, lambda *g, rep=rep: (cast_layer, step_of(*g) // rep, 0)))
        out_specs.append(pl.BlockSpec(block, lambda *g, rep=rep: (step_of(*g) // rep, 0)))
    return ws, in_specs, out_specs, [jax.ShapeDtypeStruct(w.shape[1:], BF16) for w in ws]


def _cast_slabs(cast_in, cast_out):
    for src, dst in zip(cast_in, cast_out):
        dst[...] = src[...].astype(BF16)


def _rot_half(x, x1_mask):
    w = x.shape[-1]
    half = B_ROPE // 2
    return jnp.where(x1_mask, pltpu.roll(x, w - half, 1), pltpu.roll(x, half, 1))


def _inproj_kernel(x_ref, mod_ref, g_ref, win_ref, qg_ref, kvg_ref, wuq_ref, wuk_ref, wuv_ref,
                   cc_ref, sc_ref, ks_ref, *refs, ncast):
    cast_in, refs = refs[:ncast], refs[ncast:]
    a_ref, bq_ref, bk_ref, bv_ref, cqk_ref, cv_ref, cg_ref = refs[:7]
    cast_out, w_ref = refs[7:7 + ncast], refs[7 + ncast]
    _cast_slabs(cast_in, cast_out)

    @pl.when((pl.program_id(0) == 0) & (pl.program_id(1) == 0))
    def _():
        d = win_ref.shape[1]
        bq0 = 3 * A_WIDTH
        bkv0 = bq0 + B_Q_RANK
        rope0 = bkv0 + B_KV_RANK
        c0 = rope0 + B_ROPE
        zeros = lambda n: jnp.zeros((n, d), F32)

        def place(dst, src, n):
            for k in range(0, n, LANES):
                w_ref[:, dst + k:dst + k + LANES] = win_ref[src + k:src + k + LANES, :].T.astype(BF16)

        place(OFF_AQ, 0, A_WIDTH)
        place(OFF_BKV, bkv0, B_KV_RANK)
        place(OFF_AKV, A_WIDTH, 2 * A_WIDTH)
        place(OFF_BQ, bq0, B_Q_RANK)
        q_rows = jnp.concatenate([win_ref[c0:c0 + C_QK, :], win_ref[rope0:c0, :],
                                  zeros(C_QPAD - C_QK - B_ROPE)], axis=0)
        w_ref[:, OFF_C:OFF_C + C_QPAD] = q_rows.T.astype(BF16)
        k_rows = jnp.concatenate([win_ref[c0 + C_QK:c0 + 2 * C_QK, :], zeros(C_QPAD - C_QK)], axis=0)
        w_ref[:, OFF_C + C_QPAD:OFF_CV] = k_rows.T.astype(BF16)
        place(OFF_CV, c0 + 2 * C_QK, 2 * C_WIDTH)

    tm = x_ref.shape[0] // INPROJ_SPLIT
    parts = [slice(k * tm, (k + 1) * tm) for k in range(INPROJ_SPLIT)]
    lane = lax.broadcasted_iota(jnp.int32, (tm, LANES), 1)
    b_x1 = (lane >= B_NOPE) & (lane < B_NOPE + B_ROPE // 2)
    b_rope = (lane >= B_NOPE) & (lane < B_NOPE + B_ROPE)
    c_x1 = (lane & (C_KEY // 2)) == 0

    def proj(h, lo, hi):
        return jnp.dot(h, w_ref[:, lo:hi], preferred_element_type=F32)

    hs = [(_rms(x_ref[r, :]) * g_ref[...] * (1.0 + mod_ref[1:2, :]) + mod_ref[0:1, :]).astype(BF16)
          for r in parts]
    lat = []
    for r, h in zip(parts, hs):
        aq_kv = proj(h, OFF_AQ, OFF_AKV)
        a_ref[r, 0:A_WIDTH] = aq_kv[:, :A_WIDTH] * (HEAD_DIM ** -0.5 * LOG2E)
        lat.append((proj(h, OFF_BQ, OFF_C), aq_kv[:, OFF_BKV:OFF_AKV], proj(h, OFF_C, OFF_CV)))
    normed = [((_rms(q_lat) * qg_ref[...]).astype(BF16), (_rms(kv_lat) * kvg_ref[...]).astype(BF16))
              for q_lat, kv_lat, _ in lat]
    for r, h in zip(parts, hs):
        a_ref[r, A_WIDTH:3 * A_WIDTH] = proj(h, OFF_AKV, OFF_BQ)
    ups = []
    for r, h, (qn, kvn) in zip(parts, hs, normed):
        ups.append((jnp.dot(qn, wuq_ref[...], preferred_element_type=F32),
                    jnp.dot(kvn, wuk_ref[...], preferred_element_type=F32)))
        bv_ref[r, :] = jnp.dot(kvn, wuv_ref[...], preferred_element_type=F32).astype(BF16)
        v_gate = proj(h, OFF_CV, IN_COLS)
        cv_ref[r, :] = v_gate[:, :C_WIDTH].astype(BF16)
        cg_ref[r, :] = _silu(v_gate[:, C_WIDTH:]).astype(BF16)
    b_scale = (B_NOPE + B_ROPE) ** -0.5 * LOG2E
    for r, (_, _, qk), (q, k_nope) in zip(parts, lat, ups):
        cos_c, sin_c = cc_ref[r, :], sc_ref[r, :]
        cos_b, sin_b = jnp.where(b_rope, cos_c, 1.0), jnp.where(b_rope, sin_c, 0.0)
        k_rope = jnp.where(lane >= B_NOPE, qk[:, LANES:2 * LANES], 0.0)
        k_pe = k_rope * cos_b + _rot_half(k_rope, b_x1) * sin_b
        for hh in range(B_HEADS):
            sl = slice(hh * LANES, (hh + 1) * LANES)
            qh = q[:, sl]
            bq_ref[r, sl] = ((qh * cos_b + _rot_half(qh, b_x1) * sin_b) * b_scale).astype(BF16)
            bk_ref[r, sl] = (k_nope[:, sl] + k_pe).astype(BF16)
        for t in range(2 * C_QPAD // LANES):
            sl = slice(t * LANES, (t + 1) * LANES)
            xt = qk[:, sl]
            cqk_ref[r, sl] = ((xt * cos_c + _rot_half(xt, c_x1) * sin_c) * ks_ref[:, sl]).astype(BF16)


def _inproj(x, mods, layer, prep, tables, batch, seq, tm, cast_layer):
    t, d = x.shape
    nt = seq // tm
    row = lambda w: pl.BlockSpec((tm, w), lambda b, i: (b * nt + i, 0))
    wl = lambda a: _const_spec((None,) + a.shape[1:], lambda b, i: (layer,) + (0,) * (a.ndim - 1))
    cos_c, sin_c = tables
    kscale = jnp.asarray(np.concatenate([np.ones(C_QPAD, np.float32),
                                         np.full(C_QPAD, C_KEY ** -0.5, np.float32)])[None, :])
    outs = [(3 * A_WIDTH, F32), (B_HEADS * LANES, BF16), (B_HEADS * LANES, BF16), (B_WIDTH, BF16),
            (2 * C_QPAD, BF16), (C_WIDTH, BF16), (C_WIDTH, BF16)]
    cast_ws, cast_in_specs, cast_out_specs, cast_shapes = _cast_plumbing(
        prep, cast_layer, batch * nt, lambda b, i: b * nt + i)
    return pl.pallas_call(
        functools.partial(_inproj_kernel, ncast=len(cast_ws)),
        grid=(batch, nt),
        in_specs=[
            row(d),
            pl.BlockSpec((None, None, 6, d), lambda b, i: (layer, b, 0, 0)),
            wl(prep["norm1_g"]), wl(prep["w_in"]), wl(prep["q_norm"]), wl(prep["kv_norm"]),
            wl(prep["w_uq"]), wl(prep["w_uk"]), wl(prep["w_uv"]),
            row(LANES), row(LANES),
            _const_spec((1, 2 * C_QPAD), lambda b, i: (0, 0)),
            *cast_in_specs,
        ],
        out_specs=[row(w) for w, _ in outs] + cast_out_specs,
        out_shape=[jax.ShapeDtypeStruct((t, w), dt) for w, dt in outs] + cast_shapes,
        scratch_shapes=[pltpu.VMEM((d, IN_COLS), BF16)],
        compiler_params=pltpu.CompilerParams(
            dimension_semantics=("arbitrary", "arbitrary"), vmem_limit_bytes=VMEM_LIMIT),
        name="inproj",
    )(x, mods, prep["norm1_g"], prep["w_in"], prep["q_norm"], prep["kv_norm"],
      prep["w_uq"], prep["w_uk"], prep["w_uv"], cos_c, sin_c, kscale, *cast_ws)


def _class_major_blocks(seq, level):
    n = seq // STEP
    perm = np.arange(seq)
    for _ in range(level):
        perm = perm.reshape(n, STEP).T.reshape(seq)
    blocks = perm.reshape(seq // WIN, WIN)
    stride = STEP ** level
    assert (blocks == blocks[:, :1] + stride * np.arange(WIN)).all()
    return [(int(b[0]), stride) for b in blocks]


def _dilated_kernel(q_ref, k_ref, v_ref, bias_ref, gain_ref, o_ref,
                    qf, kf, vf, qlo, qhi, kp, vt, *stage_refs, seq):
    npat = len(A_PATTERNS)
    nblk = seq // WIN
    s_sc, out_s, lse_s = (stage_refs[i * npat:(i + 1) * npat] for i in range(3))
    lane = lax.broadcasted_iota(jnp.int32, (WIN, LANES), 1)
    lo_lane = lane < HEAD_DIM
    lo_row = lax.broadcasted_iota(jnp.int32, (LANES, WIN), 0) < HEAD_DIM

    n = seq // STEP
    for p in range(npat):
        for r in range(STEP):
            rows = slice(r * n, (r + 1) * n)
            src = rows if p == 0 else pl.ds(r, n, stride=STEP)
            level_below = (q_ref, k_ref, v_ref) if p <= 1 else (qf.at[p - 2], kf.at[p - 2], vf.at[p - 2])
            q, k, v = (f[src, :] for f in level_below)
            if 0 < p < npat - 1:
                qf[p - 1, rows, :], kf[p - 1, rows, :], vf[p - 1, rows, :] = q, k, v
            lo_q = lax.broadcasted_iota(jnp.int32, q.shape, 1) < HEAD_DIM
            qlo[p, rows, :] = jnp.where(lo_q, q, 0.0).astype(BF16)
            qhi[p, rows, :] = jnp.where(lo_q, 0.0, q).astype(BF16)
            vt[p, :, rows] = v.T.astype(BF16)
            kp[p, rows, :] = k.astype(BF16)

    def key_rows(p, blk):
        r0 = blk * WIN
        first = blk % (nblk // A_PATTERNS[p][1]) == 0
        return (r0 if first else r0 - WIN), r0 + WIN

    for p in range(npat - 1, -1, -1):
        for blk in range(nblk):
            k0, k1 = key_rows(p, blk)
            r0 = blk * WIN
            q2 = jnp.concatenate([qlo[p, r0:r0 + WIN, :], qhi[p, r0:r0 + WIN, :]], axis=0)
            kcat = kp[p, k0:k1, :]
            bias = bias_ref[p, 2 * WIN - (k1 - k0):, :]
            s_sc[p][blk, 0:k1 - k0, :] = lax.dot_general(
                kcat, q2, (((1,), (1,)), ((), ())), preferred_element_type=F32) + bias
        for blk, (start, stride) in enumerate(_class_major_blocks(seq, p)):
            k0, k1 = key_rows(p, blk)
            s = s_sc[p][blk, 0:k1 - k0, :]
            m = jnp.max(s, axis=0, keepdims=True)
            e = jnp.exp2(s - m)
            l = jnp.sum(e, axis=0, keepdims=True)
            acc = jnp.dot(vt[p, :, k0:k1], e.astype(BF16), preferred_element_type=F32)
            out = acc * (1.0 / l)
            lse = m + jnp.log2(l)
            out = jnp.where(lo_row, out[:, :WIN], out[:, WIN:])
            lse = jnp.where(lo_row, lse[:, :WIN], lse[:, WIN:])
            dst = pl.ds(start, WIN, stride=stride) if stride > 1 else slice(start, start + WIN)
            out_s[p][dst, :] = out.T
            lse_s[p][dst, :] = lse.T

    gain = gain_ref[...]
    for blk in range(nblk):
        sl = slice(blk * WIN, (blk + 1) * WIN)
        lses = [s[sl, :] for s in lse_s]
        mx = functools.reduce(jnp.maximum, lses)
        ws = [jnp.exp2(s - mx) for s in lses]
        o = sum(w * x[sl, :] for w, x in zip(ws, out_s)) / sum(ws)
        sq = o * o
        ms_lo = jnp.sum(jnp.where(lo_lane, sq, 0.0), axis=1, keepdims=True)
        ms_hi = jnp.sum(jnp.where(lo_lane, 0.0, sq), axis=1, keepdims=True)
        inv = lax.rsqrt(jnp.where(lo_lane, ms_lo, ms_hi) * (1.0 / HEAD_DIM) + EPS)
        o_ref[sl, :] = (o * inv * gain).astype(o_ref.dtype)


def _dilated(a_qkv, bias, gain_a, batch, seq):
    t = a_qkv.shape[0]
    pairs = A_WIDTH // LANES
    assert all(d == STEP ** p and seq % (d * WIN) == 0 for p, (_, d) in enumerate(A_PATTERNS))
    npat = bias.shape[0]
    col = lambda off: pl.BlockSpec((seq, LANES), lambda b, g: (b, off + g))
    scratch = ([pltpu.VMEM((npat - 2, seq, LANES), F32) for _ in range(3)]
               + [pltpu.VMEM((npat, seq, LANES), BF16) for _ in range(2)]
               + [pltpu.VMEM((npat, seq, LANES), BF16)]
               + [pltpu.VMEM((npat, LANES, seq), BF16)]
               + [pltpu.VMEM((seq // WIN, 2 * WIN, 2 * WIN), F32) for _ in range(npat)]
               + [pltpu.VMEM((seq, LANES), F32) for _ in range(2 * npat)])
    return pl.pallas_call(
        functools.partial(_dilated_kernel, seq=seq),
        grid=(batch, pairs),
        in_specs=[
            col(0), col(pairs), col(2 * pairs),
            pl.BlockSpec((npat, 2 * WIN, 2 * WIN), lambda b, g: (0, 0, g)),
            pl.BlockSpec((1, LANES), lambda b, g: (0, g)),
        ],
        out_specs=pl.BlockSpec((seq, LANES), lambda b, g: (b, g)),
        out_shape=jax.ShapeDtypeStruct((t, A_WIDTH), BF16),
        scratch_shapes=scratch,
        compiler_params=pltpu.CompilerParams(
            dimension_semantics=("arbitrary", "arbitrary"), vmem_limit_bytes=VMEM_LIMIT),
        name="dilated_attention",
    )(a_qkv, a_qkv, a_qkv, bias, gain_a)


def _mla_kernel(q_ref, k_ref, v_ref, gain_ref, o_ref, vt, s_sc, e_sc, *, seq, tq):
    th = tq // 2
    causal = lax.broadcasted_iota(jnp.int32, (th, th), 0) <= lax.broadcasted_iota(jnp.int32, (th, th), 1)
    gain = gain_ref[...]
    for c in range(0, seq, tq):
        vt[:, c:c + tq] = v_ref[c:c + tq, :].astype(F32).T.astype(BF16)

    units = [(i, hh) for i in range(seq // tq) for hh in range(2)]
    base = np.concatenate([[0], np.cumsum([i + 1 for i, _ in units])]).tolist()

    def logits(u):
        i, hh = units[u]
        hs = slice(hh * LANES, (hh + 1) * LANES)
        q = q_ref[i * tq:(i + 1) * tq, hs]
        qk = lambda keys, qs: lax.dot_general(k_ref[keys, hs], qs, (((1,), (1,)), ((), ())),
                                              preferred_element_type=F32)
        m = None
        for j in range(i):
            s = qk(slice(j * tq, (j + 1) * tq), q)
            s_sc[base[u] + j] = s
            bm = jnp.max(s, axis=0, keepdims=True)
            m = bm if m is None else jnp.maximum(m, bm)
        d0 = i * tq
        top = qk(slice(d0, d0 + th), q)
        top = jnp.concatenate([jnp.where(causal, top[:, :th], NEG), top[:, th:]], axis=1)
        bot = jnp.where(causal, qk(slice(d0 + th, d0 + tq), q_ref[d0 + th:d0 + tq, hs]), NEG)
        s_sc[base[u] + i, 0:th, :] = top
        s_sc[base[u] + i, th:tq, th:tq] = bot
        bm = jnp.max(top, axis=0, keepdims=True)
        bm = jnp.concatenate([bm[:, :th], jnp.maximum(bm[:, th:], jnp.max(bot, axis=0, keepdims=True))], axis=1)
        return bm if m is None else jnp.maximum(m, bm)

    def attend(u, m):
        i, hh = units[u]
        l = None
        for j in range(i):
            e = jnp.exp2(s_sc[base[u] + j] - m)
            ls = jnp.sum(e, axis=0, keepdims=True)
            l = ls if l is None else l + ls
            e_sc[(base[u] + j) * tq:(base[u] + j + 1) * tq, :] = e.astype(BF16)
        r0 = (base[u] + i) * tq
        e_top = jnp.exp2(s_sc[base[u] + i, 0:th, :] - m)
        e_bot = jnp.exp2(s_sc[base[u] + i, th:tq, th:tq] - m[:, th:])
        ls = jnp.sum(e_top, axis=0, keepdims=True)
        ls = jnp.concatenate([ls[:, :th], ls[:, th:] + jnp.sum(e_bot, axis=0, keepdims=True)], axis=1)
        l = ls if l is None else l + ls
        e_sc[r0:r0 + th, :] = e_top.astype(BF16)
        e_sc[r0 + th:r0 + tq, 0:th] = jnp.zeros((th, th), BF16)
        e_sc[r0 + th:r0 + tq, th:tq] = e_bot.astype(BF16)
        acc = jnp.dot(vt[hh * B_V:(hh + 1) * B_V, 0:(i + 1) * tq], e_sc[base[u] * tq:base[u + 1] * tq, :],
                      preferred_element_type=F32)
        o = acc / l
        return o * lax.rsqrt(jnp.mean(o * o, axis=0, keepdims=True) + EPS)

    m_next = logits(0)
    normed = []
    for u, (i, hh) in enumerate(units):
        m_cur = m_next
        if u + 1 < len(units):
            m_next = logits(u + 1)
        normed.append(attend(u, m_cur))
        if hh == 1:
            o_ref[i * tq:(i + 1) * tq, :] = (jnp.concatenate(normed, axis=0).T * gain).astype(o_ref.dtype)
            normed = []


def _mla(bq, bk, bv, gain_b, batch, seq):
    t = bq.shape[0]
    pairs = B_HEADS // 2
    tq = MLA_BLOCK
    assert seq % tq == 0
    nq = seq // tq
    nslots = 2 * (nq * (nq + 1) // 2)
    return pl.pallas_call(
        functools.partial(_mla_kernel, seq=seq, tq=tq),
        grid=(batch, pairs),
        in_specs=[
            pl.BlockSpec((seq, 2 * LANES), lambda b, g: (b, g)),
            pl.BlockSpec((seq, 2 * LANES), lambda b, g: (b, g)),
            pl.BlockSpec((seq, LANES), lambda b, g: (b, g)),
            pl.BlockSpec((1, LANES), lambda b, g: (0, g)),
        ],
        out_specs=pl.BlockSpec((seq, LANES), lambda b, g: (b, g)),
        out_shape=jax.ShapeDtypeStruct((t, B_WIDTH), BF16),
        scratch_shapes=[pltpu.VMEM((LANES, seq), BF16),
                        pltpu.VMEM((nslots, tq, tq), F32), pltpu.VMEM((nslots * tq, tq), BF16)],
        compiler_params=pltpu.CompilerParams(
            dimension_semantics=("arbitrary", "arbitrary"), vmem_limit_bytes=VMEM_LIMIT),
        name="latent_attention",
    )(bq, bk, bv, gain_b)


def _retention_consts():
    h = C_HEADS
    log_g = np.log(1.0 - 2.0 ** (-5.0 - np.arange(h))).astype(np.float32)
    i = np.arange(C_CHUNK, dtype=np.float32)
    rel = i[:, None] - i[None, :]
    decay = (np.exp(np.maximum(rel, 0.0)[None] * log_g[:, None, None]) * (rel >= 0)[None]).astype(np.float32)
    xi = np.exp((i + 1.0)[None, :] * log_g[:, None]).astype(np.float32)
    zeta = np.exp((C_CHUNK - 1.0 - i)[None, :] * log_g[:, None]).astype(np.float32)
    chunk_decay = np.exp(C_CHUNK * log_g).astype(np.float32)
    decay_all = np.concatenate(list(decay), axis=1)
    xi_mat = np.repeat(xi.T, C_VAL, axis=1)
    zeta_t = np.zeros((C_QPAD, C_CHUNK), np.float32)
    zeta_t[:C_QK] = np.repeat(zeta, C_KEY, axis=0)
    cd = np.repeat(chunk_decay, C_VAL)[None, :]
    bd = np.zeros((C_QPAD, C_WIDTH), np.float32)
    for hh in range(h):
        bd[hh * C_KEY:(hh + 1) * C_KEY, hh * C_VAL:(hh + 1) * C_VAL] = 1.0
    return tuple(jnp.asarray(a) for a in (decay_all, xi_mat, zeta_t, cd, bd))


def _retention_kernel(qk_ref, v_ref, g_ref, decay_ref, xi_ref, zeta_ref, cd_ref, bd_ref, gain_ref,
                      o_ref, *, seq):
    lane = lax.broadcasted_iota(jnp.int32, (C_CHUNK, LANES), 1)
    lo = lane < C_VAL
    gain = gain_ref[...]
    state = jnp.zeros((C_QPAD, C_WIDTH), F32)
    kshape, vshape = (C_QPAD, C_HEADS * C_CHUNK), (C_HEADS * C_CHUNK, C_WIDTH)
    k_on = (lax.broadcasted_iota(jnp.int32, kshape, 0) // C_KEY) == (lax.broadcasted_iota(jnp.int32, kshape, 1) // C_CHUNK)
    v_on = (lax.broadcasted_iota(jnp.int32, vshape, 0) // C_CHUNK) == (lax.broadcasted_iota(jnp.int32, vshape, 1) // C_VAL)

    for n in range(seq // C_CHUNK):
        rows = slice(n * C_CHUNK, (n + 1) * C_CHUNK)
        q = qk_ref[rows, 0:C_QPAD]
        k_t = qk_ref[rows, C_QPAD:2 * C_QPAD].astype(F32).T
        v = v_ref[rows, :]
        k_bd = jnp.where(k_on, jnp.concatenate([k_t] * C_HEADS, axis=1), 0.0).astype(BF16)
        s = jnp.dot(q, k_bd, preferred_element_type=F32) * decay_ref[...]
        v_bd = jnp.where(v_on, jnp.concatenate([v] * C_HEADS, axis=0), jnp.zeros((), BF16))
        o = (jnp.dot(s.astype(BF16), v_bd, preferred_element_type=F32)
             + jnp.dot(q, state.astype(BF16), preferred_element_type=F32) * xi_ref[...])
        upd = jnp.dot((k_t * zeta_ref[...]).astype(BF16), v, preferred_element_type=F32)
        state = state * cd_ref[...] + upd * bd_ref[...]
        for t in range(C_WIDTH // LANES):
            sl = slice(t * LANES, (t + 1) * LANES)
            x = o[:, sl]
            mu_lo = jnp.sum(jnp.where(lo, x, 0.0), axis=1, keepdims=True)
            mu_hi = jnp.sum(jnp.where(lo, 0.0, x), axis=1, keepdims=True)
            dlt = x - jnp.where(lo, mu_lo, mu_hi) * (1.0 / C_VAL)
            sq = dlt * dlt
            var_lo = jnp.sum(jnp.where(lo, sq, 0.0), axis=1, keepdims=True)
            var_hi = jnp.sum(jnp.where(lo, 0.0, sq), axis=1, keepdims=True)
            y = dlt * lax.rsqrt(jnp.where(lo, var_lo, var_hi) * (1.0 / C_VAL) + EPS)
            o_ref[rows, sl] = (y * g_ref[rows, sl].astype(F32) * gain[:, sl]).astype(o_ref.dtype)


def _retention(cqk, cv, cg, gain_c, consts, batch, seq):
    t = cqk.shape[0]
    tok = lambda width: pl.BlockSpec((seq, width), lambda b: (b, 0))
    full = lambda a: _const_spec(a.shape, lambda b: (0,) * a.ndim)
    return pl.pallas_call(
        functools.partial(_retention_kernel, seq=seq),
        grid=(batch,),
        in_specs=[tok(2 * C_QPAD), tok(C_WIDTH), tok(C_WIDTH)] + [full(a) for a in consts] + [full(gain_c)],
        out_specs=tok(C_WIDTH),
        out_shape=jax.ShapeDtypeStruct((t, C_WIDTH), BF16),
        compiler_params=pltpu.CompilerParams(
            dimension_semantics=("arbitrary",), vmem_limit_bytes=VMEM_LIMIT),
        name="retention",
    )(cqk, cv, cg, *consts, gain_c)


def _outffn_kernel(x_ref, ma_ref, mb_ref, mc_ref, mod_ref, wo_ref, g_ref, wg_ref, wu_ref, wd_ref, fg_ref,
                   *refs, final, fchunk, nsplit, ncast):
    cast_in, o_ref, cast_out = refs[:ncast], refs[ncast], refs[ncast + 1:2 * ncast + 1]
    mix_ref, h_ref, hid_ref = refs[2 * ncast + 1:]
    _cast_slabs(cast_in, cast_out)
    tm = x_ref.shape[0]
    parts = [slice(k * tm // nsplit, (k + 1) * tm // nsplit) for k in range(nsplit)]
    mix_ref[:, 0:A_WIDTH] = ma_ref[...]
    mix_ref[:, A_WIDTH:A_WIDTH + B_WIDTH] = mb_ref[...]
    mix_ref[:, A_WIDTH + B_WIDTH:] = mc_ref[...]
    for r in parts:
        att = jnp.dot(mix_ref[r, :], wo_ref[...], preferred_element_type=F32)
        o_ref[r, :] = x_ref[r, :] + mod_ref[2:3, :] * att
    for r in parts:
        h_ref[r, :] = (_rms(o_ref[r, :]) * g_ref[...] * (1.0 + mod_ref[4:5, :]) + mod_ref[3:4, :]).astype(BF16)
    hidden = wg_ref.shape[1]
    for r in parts:
        for j in range(hidden // fchunk):
            sl = slice(j * fchunk, (j + 1) * fchunk)
            gate = jnp.dot(h_ref[r, :], wg_ref[:, sl], preferred_element_type=F32)
            up = jnp.dot(h_ref[r, :], wu_ref[:, sl], preferred_element_type=F32)
            hid_ref[r, sl] = (_silu(gate) * up).astype(BF16)
    for r in parts:
        ffn = jnp.dot(hid_ref[r, :], wd_ref[...], preferred_element_type=F32)
        y = o_ref[r, :] + mod_ref[5:6, :] * ffn
        if final:
            y = _rms(y) * fg_ref[...]
        o_ref[r, :] = y


def _outffn(x, mix_a, mix_b, mix_c, mods, layer, prep, ffn_w, final_g, batch, seq, tm, final, cast_layer):
    t, d = x.shape
    nt = seq // tm
    w_out, w_gate, w_up, w_down = ffn_w
    hidden = w_gate.shape[1]
    whole = lambda a: _const_spec(a.shape, lambda b, i: (0,) * a.ndim)
    row = lambda w: pl.BlockSpec((tm, w), lambda b, i: (b * nt + i, 0))
    wl = lambda a: _const_spec((None,) + a.shape[1:], lambda b, i: (layer,) + (0,) * (a.ndim - 1))
    cast_ws, cast_in_specs, cast_out_specs, cast_shapes = _cast_plumbing(
        prep, cast_layer, batch * nt, lambda b, i: b * nt + i)
    return pl.pallas_call(
        functools.partial(_outffn_kernel, final=final, fchunk=MXU_WIDTH, nsplit=FFN_SPLIT, ncast=len(cast_ws)),
        grid=(batch, nt),
        in_specs=[
            row(d), row(A_WIDTH), row(B_WIDTH), row(C_WIDTH),
            pl.BlockSpec((None, None, 6, d), lambda b, i: (layer, b, 0, 0)),
            whole(w_out), wl(prep["norm2_g"]), whole(w_gate), whole(w_up), whole(w_down),
            _const_spec((1, d), lambda b, i: (0, 0)),
            *cast_in_specs,
        ],
        out_specs=[row(d)] + cast_out_specs,
        out_shape=[jax.ShapeDtypeStruct((t, d), F32)] + cast_shapes,
        scratch_shapes=[pltpu.VMEM((tm, d), BF16), pltpu.VMEM((tm, d), BF16), pltpu.VMEM((tm, hidden), BF16)],
        compiler_params=pltpu.CompilerParams(
            dimension_semantics=("arbitrary", "arbitrary"), vmem_limit_bytes=VMEM_LIMIT),
        name="outproj_ffn",
    )(x, mix_a, mix_b, mix_c, mods, w_out, prep["norm2_g"], w_gate, w_up, w_down, final_g, *cast_ws)


def _prepare(norm1_g, w_in, mla_q_norm, mla_kv_norm, mla_w_uq, mla_w_ukv, w_out, norm2_g,
             ffn_w_gate, ffn_w_up, ffn_w_down):
    depth, d, _ = w_in.shape
    b_lat0 = 3 * A_WIDTH
    rope0 = b_lat0 + B_Q_RANK + B_KV_RANK
    c0 = rope0 + B_ROPE
    assert C_QK + B_ROPE <= C_QPAD and C_QK % LANES == B_NOPE
    assert c0 + 2 * C_QK + 2 * C_WIDTH == w_in.shape[2]
    uq = mla_w_uq.reshape(depth, B_Q_RANK, B_HEADS, B_NOPE + B_ROPE)
    uq = jnp.pad(uq, ((0, 0), (0, 0), (0, 0), (0, LANES - B_NOPE - B_ROPE)))
    ukv = mla_w_ukv.reshape(depth, B_KV_RANK, B_HEADS, B_NOPE + B_V)
    uk = jnp.pad(ukv[..., :B_NOPE], ((0, 0), (0, 0), (0, 0), (0, LANES - B_NOPE)))
    uv = ukv[..., B_NOPE:]
    return {
        "norm1_g": norm1_g[:, None, :],
        "w_in": jnp.swapaxes(w_in, 1, 2),
        "q_norm": mla_q_norm[:, None, :],
        "kv_norm": mla_kv_norm[:, None, :],
        "w_uq": uq.reshape(depth, B_Q_RANK, B_HEADS * LANES).astype(BF16),
        "w_uk": uk.reshape(depth, B_KV_RANK, B_HEADS * LANES).astype(BF16),
        "w_uv": uv.reshape(depth, B_KV_RANK, B_WIDTH).astype(BF16),
        "w_out": w_out,
        "norm2_g": norm2_g[:, None, :],
        "w_gate": ffn_w_gate,
        "w_up": ffn_w_up,
        "w_down": ffn_w_down,
    }


def kernel(x, c, positions, rel_bias, ada_w, ada_b, norm1_g, w_in, mla_q_norm, mla_kv_norm, mla_w_uq,
           mla_w_ukv, mix_gain, w_out, norm2_g, ffn_w_gate, ffn_w_up, ffn_w_down, final_norm):
    batch, seq, d = x.shape
    depth = w_in.shape[0]
    tm = ROW_TILE
    assert seq % tm == 0 and seq % C_CHUNK == 0

    prep = _prepare(norm1_g, w_in, mla_q_norm, mla_kv_norm, mla_w_uq, mla_w_ukv, w_out, norm2_g,
                    ffn_w_gate, ffn_w_up, ffn_w_down)
    mods = _mods(c, ada_w, ada_b).reshape(depth, batch, 6, d)
    tables = _rope_tables(positions)
    bias = _bias_tables(rel_bias)
    ret_consts = _retention_consts()
    final_g = final_norm[None, :]

    xf = x.reshape(batch * seq, d)
    for l in range(depth):
        a_qkv, bq, bk, bv, cqk, cv, cg, *cast0 = _inproj(xf, mods, l, prep, tables, batch, seq, tm,
                                                         cast_layer=0 if l == 0 else None)
        ffn_w = cast0 if l == 0 else ffn_w_next
        gain = mix_gain[l][None, :]
        mix_a = _dilated(a_qkv, bias, gain[:, :A_WIDTH], batch, seq)
        mix_b = _mla(bq, bk, bv, gain[:, A_WIDTH:A_WIDTH + B_WIDTH], batch, seq)
        mix_c = _retention(cqk, cv, cg, gain[:, A_WIDTH + B_WIDTH:], ret_consts, batch, seq)
        xf, *ffn_w_next = _outffn(xf, mix_a, mix_b, mix_c, mods, l, prep, ffn_w, final_g, batch, seq, tm,
                                  final=(l == depth - 1), cast_layer=l + 1 if l + 1 < depth else None)
    return xf.reshape(batch, seq, d)
```

```python
import functools

import numpy as np
import jax
import jax.numpy as jnp
from jax import lax
from jax.experimental import pallas as pl
from jax.experimental.pallas import tpu as pltpu

F32 = jnp.float32
BF16 = jnp.bfloat16

HEAD_DIM = 64
A_HEADS = 6
A_PATTERNS = ((128, 1), (512, 4), (2048, 16))
A_WIDTH = A_HEADS * HEAD_DIM
B_HEADS = 4
B_NOPE = 64
B_ROPE = 32
B_V = 64
B_Q_RANK = 256
B_KV_RANK = 128
B_WIDTH = B_HEADS * B_V
C_HEADS = 6
C_KEY = 32
C_VAL = 64
C_WIDTH = C_HEADS * C_VAL
C_QK = C_HEADS * C_KEY
C_CHUNK = 128
N_BUCKETS = 32
MAX_DISTANCE = 2048
ROPE_BASE = 10000.0
EPS = 1e-6

LANES = 128
WIN = 128
STEP = 4
NEG = -1e30
LOG2E = 1.4426950408889634
FFN_CAST = ("w_out", "w_gate", "w_up", "w_down")
VMEM_LIMIT = 56 * 1024 * 1024
MXU_WIDTH = 256
ROW_TILE = 512
MLA_BLOCK = 512
MODS_COL_TILE = 1536
ROPE_ROW_TILE = 2048
FFN_SPLIT = 2
INPROJ_SPLIT = 2

OFF_AQ = 0
OFF_BKV = OFF_AQ + A_WIDTH
OFF_AKV = OFF_BKV + B_KV_RANK
OFF_BQ = OFF_AKV + 2 * A_WIDTH
OFF_C = OFF_BQ + B_Q_RANK
C_QPAD = 2 * LANES
OFF_CV = OFF_C + 2 * C_QPAD
OFF_CG = OFF_CV + C_WIDTH
IN_COLS = OFF_CG + C_WIDTH
assert all(off % MXU_WIDTH == 0 for off in (OFF_AKV, OFF_BQ, OFF_C, OFF_CV, IN_COLS))


def _const_spec(shape, index_map):
    return pl.BlockSpec(shape, index_map, pipeline_mode=pl.Buffered(1))


def _silu(x):
    return x / (1.0 + jnp.exp(-x))


def _rms(x):
    return x * lax.rsqrt(jnp.mean(x * x, axis=-1, keepdims=True) + EPS)


def _mods_kernel(c_ref, w_ref, b_ref, o_ref):
    def split(a):
        hi = a.astype(BF16)
        return hi, (a - hi.astype(F32)).astype(BF16)

    c_hi, c_lo = split(_silu(c_ref[...]))
    w_hi, w_lo = split(w_ref[...])
    nb = c_hi.shape[0]
    both = jnp.dot(jnp.concatenate([c_hi, c_lo], axis=0), w_hi, preferred_element_type=F32)
    o_ref[...] = (both[:nb] + both[nb:] + jnp.dot(c_hi, w_lo, preferred_element_type=F32)) + b_ref[...]


def _mods(c, ada_w, ada_b):
    depth, d, n = ada_w.shape
    b = c.shape[0]
    tn = MODS_COL_TILE
    return pl.pallas_call(
        _mods_kernel,
        grid=(depth, n // tn),
        in_specs=[
            pl.BlockSpec((b, d), lambda l, j: (0, 0)),
            pl.BlockSpec((None, d, tn), lambda l, j: (l, 0, j)),
            pl.BlockSpec((None, 1, tn), lambda l, j: (l, 0, j)),
        ],
        out_specs=pl.BlockSpec((None, b, tn), lambda l, j: (l, 0, j)),
        out_shape=jax.ShapeDtypeStruct((depth, b, n), F32),
        compiler_params=pltpu.CompilerParams(vmem_limit_bytes=VMEM_LIMIT),
        name="adaln_mods",
    )(c, ada_w, ada_b.reshape(depth, 1, n))


def _rope_kernel(pos_ref, f_ref, s_ref, cos_ref, sin_ref):
    ang = pos_ref[...].astype(F32) * f_ref[...]
    cos_ref[...] = jnp.cos(ang)
    sin_ref[...] = jnp.sin(ang) * s_ref[...]


def _rope_tables(positions):
    t = positions.size
    half = C_KEY // 2
    assert B_ROPE == C_KEY and B_NOPE % C_KEY == 0
    inv_freq = (1.0 / (ROPE_BASE ** (np.arange(half, dtype=np.float32) / half))).astype(np.float32)
    ones = np.ones(half, np.float32)
    freqs = jnp.asarray(np.tile(np.concatenate([inv_freq, inv_freq]), LANES // C_KEY)[None, :])
    signs = jnp.asarray(np.tile(np.concatenate([-ones, ones]), LANES // C_KEY)[None, :])
    tr = ROPE_ROW_TILE
    out = jax.ShapeDtypeStruct((t, LANES), F32)
    row = pl.BlockSpec((tr, LANES), lambda i: (i, 0))
    return pl.pallas_call(
        _rope_kernel,
        grid=(t // tr,),
        in_specs=[
            pl.BlockSpec((tr, 1), lambda i: (i, 0)),
            pl.BlockSpec((1, LANES), lambda i: (0, 0)),
            pl.BlockSpec((1, LANES), lambda i: (0, 0)),
        ],
        out_specs=[row, row],
        out_shape=[out, out],
        name="rope_tables",
    )(positions.reshape(t, 1), freqs, signs)


def _t5_bucket(dist):
    max_exact = N_BUCKETS // 2
    safe = np.maximum(dist, 1).astype(np.float32)
    large = max_exact + (np.log(safe / max_exact) / np.log(MAX_DISTANCE / max_exact)
                         * (N_BUCKETS - max_exact)).astype(np.int32)
    large = np.minimum(large, N_BUCKETS - 1)
    return np.where(dist < max_exact, dist, large).astype(np.int32)


def _bias_kernel(bmap_ref, rb_ref, o_ref):
    h = pl.program_id(1)
    bm = bmap_ref[...]
    t = jnp.full(bm.shape, NEG, F32)
    for b in range(N_BUCKETS):
        t = jnp.where(bm == b, rb_ref[b, h] * LOG2E, t)
    o_ref[...] = t


def _bias_tables(rel_bias):
    qi = np.arange(WIN)[None, :]
    c = np.arange(2 * WIN)[:, None]
    j = qi - c + WIN
    maps = []
    for (w, d) in A_PATTERNS:
        assert w // d == WIN
        bucket = _t5_bucket(np.arange(WIN + 1, dtype=np.int32) * d)
        maps.append(np.where((j >= 0) & (j <= WIN), bucket[np.clip(j, 0, WIN)], -1).astype(np.int32))
    bmap = jnp.asarray(np.stack(maps))
    npat = len(A_PATTERNS)
    return pl.pallas_call(
        _bias_kernel,
        grid=(npat, A_HEADS),
        in_specs=[
            pl.BlockSpec((None, 2 * WIN, WIN), lambda p, h: (p, 0, 0)),
            pl.BlockSpec(memory_space=pltpu.SMEM),
        ],
        out_specs=pl.BlockSpec((None, 2 * WIN, WIN), lambda p, h: (p, 0, h)),
        out_shape=jax.ShapeDtypeStruct((npat, 2 * WIN, A_HEADS * WIN), F32),
        name="t5_bias_tables",
    )(bmap, rel_bias)


def _cast_plumbing(prep, cast_layer, steps, step_of):
    if cast_layer is None:
        return [], [], [], []
    ws, in_specs, out_specs = [prep[name] for name in FFN_CAST], [], []
    for w in ws:
        nslab = steps
        while w.shape[1] % (16 * nslab):
            assert nslab % 2 == 0
            nslab //= 2
        rep = steps // nslab
        block = (w.shape[1] // nslab, w.shape[2])
        in_specs.append(pl.BlockSpec((None,) + block, lambda *g, rep=rep: (cast_layer, step_of(*g) // rep, 0)))
        out_specs.append(pl.BlockSpec(block, lambda *g, rep=rep: (step_of(*g) // rep, 0)))
    return ws, in_specs, out_specs, [jax.ShapeDtypeStruct(w.shape[1:], BF16) for w in ws]


def _cast_slabs(cast_in, cast_out):
    for src, dst in zip(cast_in, cast_out):
        dst[...] = src[...].astype(BF16)


def _rot_half(x, x1_mask):
    w = x.shape[-1]
    half = B_ROPE // 2
    return jnp.where(x1_mask, pltpu.roll(x, w - half, 1), pltpu.roll(x, half, 1))


def _inproj_kernel(x_ref, mod_ref, g_ref, win_ref, qg_ref, kvg_ref, wuq_ref, wuk_ref, wuv_ref,
                   cc_ref, sc_ref, ks_ref, *refs, ncast):
    cast_in, refs = refs[:ncast], refs[ncast:]
    a_ref, bq_ref, bk_ref, bv_ref, cqk_ref, cv_ref, cg_ref = refs[:7]
    cast_out, w_ref = refs[7:7 + ncast], refs[7 + ncast]
    _cast_slabs(cast_in, cast_out)

    @pl.when((pl.program_id(0) == 0) & (pl.program_id(1) == 0))
    def _():
        d = win_ref.shape[1]
        bq0 = 3 * A_WIDTH
        bkv0 = bq0 + B_Q_RANK
        rope0 = bkv0 + B_KV_RANK
        c0 = rope0 + B_ROPE
        zeros = lambda n: jnp.zeros((n, d), F32)

        def place(dst, src, n):
            for k in range(0, n, LANES):
                w_ref[:, dst + k:dst + k + LANES] = win_ref[src + k:src + k + LANES, :].T.astype(BF16)

        place(OFF_AQ, 0, A_WIDTH)
        place(OFF_BKV, bkv0, B_KV_RANK)
        place(OFF_AKV, A_WIDTH, 2 * A_WIDTH)
        place(OFF_BQ, bq0, B_Q_RANK)
        q_rows = jnp.concatenate([win_ref[c0:c0 + C_QK, :], win_ref[rope0:c0, :],
                                  zeros(C_QPAD - C_QK - B_ROPE)], axis=0)
        w_ref[:, OFF_C:OFF_C + C_QPAD] = q_rows.T.astype(BF16)
        k_rows = jnp.concatenate([win_ref[c0 + C_QK:c0 + 2 * C_QK, :], zeros(C_QPAD - C_QK)], axis=0)
        w_ref[:, OFF_C + C_QPAD:OFF_CV] = k_rows.T.astype(BF16)
        place(OFF_CV, c0 + 2 * C_QK, 2 * C_WIDTH)

    tm = x_ref.shape[0] // INPROJ_SPLIT
    parts = [slice(k * tm, (k + 1) * tm) for k in range(INPROJ_SPLIT)]
    lane = lax.broadcasted_iota(jnp.int32, (tm, LANES), 1)
    b_x1 = (lane >= B_NOPE) & (lane < B_NOPE + B_ROPE // 2)
    b_rope = (lane >= B_NOPE) & (lane < B_NOPE + B_ROPE)
    c_x1 = (lane & (C_KEY // 2)) == 0

    def proj(h, lo, hi):
        return jnp.dot(h, w_ref[:, lo:hi], preferred_element_type=F32)

    hs = [(_rms(x_ref[r, :]) * g_ref[...] * (1.0 + mod_ref[1:2, :]) + mod_ref[0:1, :]).astype(BF16)
          for r in parts]
    lat = []
    for r, h in zip(parts, hs):
        aq_kv = proj(h, OFF_AQ, OFF_AKV)
        a_ref[r, 0:A_WIDTH] = aq_kv[:, :A_WIDTH] * (HEAD_DIM ** -0.5 * LOG2E)
        lat.append((proj(h, OFF_BQ, OFF_C), aq_kv[:, OFF_BKV:OFF_AKV], proj(h, OFF_C, OFF_CV)))
    normed = [((_rms(q_lat) * qg_ref[...]).astype(BF16), (_rms(kv_lat) * kvg_ref[...]).astype(BF16))
              for q_lat, kv_lat, _ in lat]
    for r, h in zip(parts, hs):
        a_ref[r, A_WIDTH:3 * A_WIDTH] = proj(h, OFF_AKV, OFF_BQ)
    ups = []
    for r, h, (qn, kvn) in zip(parts, hs, normed):
        ups.append((jnp.dot(qn, wuq_ref[...], preferred_element_type=F32),
                    jnp.dot(kvn, wuk_ref[...], preferred_element_type=F32)))
        bv_ref[r, :] = jnp.dot(kvn, wuv_ref[...], preferred_element_type=F32).astype(BF16)
        v_gate = proj(h, OFF_CV, IN_COLS)
        cv_ref[r, :] = v_gate[:, :C_WIDTH].astype(BF16)
        cg_ref[r, :] = _silu(v_gate[:, C_WIDTH:]).astype(BF16)
    b_scale = (B_NOPE + B_ROPE) ** -0.5 * LOG2E
    for r, (_, _, qk), (q, k_nope) in zip(parts, lat, ups):
        cos_c, sin_c = cc_ref[r, :], sc_ref[r, :]
        cos_b, sin_b = jnp.where(b_rope, cos_c, 1.0), jnp.where(b_rope, sin_c, 0.0)
        k_rope = jnp.where(lane >= B_NOPE, qk[:, LANES:2 * LANES], 0.0)
        k_pe = k_rope * cos_b + _rot_half(k_rope, b_x1) * sin_b
        for hh in range(B_HEADS):
            sl = slice(hh * LANES, (hh + 1) * LANES)
            qh = q[:, sl]
            bq_ref[r, sl] = ((qh * cos_b + _rot_half(qh, b_x1) * sin_b) * b_scale).astype(BF16)
            bk_ref[r, sl] = (k_nope[:, sl] + k_pe).astype(BF16)
        for t in range(2 * C_QPAD // LANES):
            sl = slice(t * LANES, (t + 1) * LANES)
            xt = qk[:, sl]
            cqk_ref[r, sl] = ((xt * cos_c + _rot_half(xt, c_x1) * sin_c) * ks_ref[:, sl]).astype(BF16)


def _inproj(x, mods, layer, prep, tables, batch, seq, tm, cast_layer):
    t, d = x.shape
    nt = seq // tm
    row = lambda w: pl.BlockSpec((tm, w), lambda b, i: (b * nt + i, 0))
    wl = lambda a: _const_spec((None,) + a.shape[1:], lambda b, i: (layer,) + (0,) * (a.ndim - 1))
    cos_c, sin_c = tables
    kscale = jnp.asarray(np.concatenate([np.ones(C_QPAD, np.float32),
                                         np.full(C_QPAD, C_KEY ** -0.5, np.float32)])[None, :])
    outs = [(3 * A_WIDTH, F32), (B_HEADS * LANES, BF16), (B_HEADS * LANES, BF16), (B_WIDTH, BF16),
            (2 * C_QPAD, BF16), (C_WIDTH, BF16), (C_WIDTH, BF16)]
    cast_ws, cast_in_specs, cast_out_specs, cast_shapes = _cast_plumbing(
        prep, cast_layer, batch * nt, lambda b, i: b * nt + i)
    return pl.pallas_call(
        functools.partial(_inproj_kernel, ncast=len(cast_ws)),
        grid=(batch, nt),
        in_specs=[
            row(d),
            pl.BlockSpec((None, None, 6, d), lambda b, i: (layer, b, 0, 0)),
            wl(prep["norm1_g"]), wl(prep["w_in"]), wl(prep["q_norm"]), wl(prep["kv_norm"]),
            wl(prep["w_uq"]), wl(prep["w_uk"]), wl(prep["w_uv"]),
            row(LANES), row(LANES),
            _const_spec((1, 2 * C_QPAD), lambda b, i: (0, 0)),
            *cast_in_specs,
        ],
        out_specs=[row(w) for w, _ in outs] + cast_out_specs,
        out_shape=[jax.ShapeDtypeStruct((t, w), dt) for w, dt in outs] + cast_shapes,
        scratch_shapes=[pltpu.VMEM((d, IN_COLS), BF16)],
        compiler_params=pltpu.CompilerParams(
            dimension_semantics=("arbitrary", "arbitrary"), vmem_limit_bytes=VMEM_LIMIT),
        name="inproj",
    )(x, mods, prep["norm1_g"], prep["w_in"], prep["q_norm"], prep["kv_norm"],
      prep["w_uq"], prep["w_uk"], prep["w_uv"], cos_c, sin_c, kscale, *cast_ws)


def _class_major_blocks(seq, level):
    n = seq // STEP
    perm = np.arange(seq)
    for _ in range(level):
        perm = perm.reshape(n, STEP).T.reshape(seq)
    blocks = perm.reshape(seq // WIN, WIN)
    stride = STEP ** level
    assert (blocks == blocks[:, :1] + stride * np.arange(WIN)).all()
    return [(int(b[0]), stride) for b in blocks]


def _dilated_kernel(q_ref, k_ref, v_ref, bias_ref, gain_ref, o_ref,
                    qf, kf, vf, qlo, qhi, kp, vt, *stage_refs, seq):
    npat = len(A_PATTERNS)
    nblk = seq // WIN
    s_sc, out_s, lse_s = (stage_refs[i * npat:(i + 1) * npat] for i in range(3))
    lane = lax.broadcasted_iota(jnp.int32, (WIN, LANES), 1)
    lo_lane = lane < HEAD_DIM
    lo_row = lax.broadcasted_iota(jnp.int32, (LANES, WIN), 0) < HEAD_DIM

    n = seq // STEP

    def regroup(p):
        for r in range(STEP):
            rows = slice(r * n, (r + 1) * n)
            src = rows if p == 0 else pl.ds(r, n, stride=STEP)
            level_below = (q_ref, k_ref, v_ref) if p <= 1 else (qf.at[p - 2], kf.at[p - 2], vf.at[p - 2])
            q, k, v = (f[src, :] for f in level_below)
            if 0 < p < npat - 1:
                qf[p - 1, rows, :], kf[p - 1, rows, :], vf[p - 1, rows, :] = q, k, v
            lo_q = lax.broadcasted_iota(jnp.int32, q.shape, 1) < HEAD_DIM
            qlo[p, rows, :] = jnp.where(lo_q, q, 0.0).astype(BF16)
            qhi[p, rows, :] = jnp.where(lo_q, 0.0, q).astype(BF16)
            vt[p, :, rows] = v.T.astype(BF16)
            kp[p, rows, :] = k.astype(BF16)

    def key_rows(p, blk):
        r0 = blk * WIN
        first = blk % (nblk // A_PATTERNS[p][1]) == 0
        return (r0 if first else r0 - WIN), r0 + WIN

    def logits(p):
        for blk in range(nblk):
            k0, k1 = key_rows(p, blk)
            r0 = blk * WIN
            q2 = jnp.concatenate([qlo[p, r0:r0 + WIN, :], qhi[p, r0:r0 + WIN, :]], axis=0)
            bias = bias_ref[p, 2 * WIN - (k1 - k0):, :]
            s_sc[p][blk, 0:k1 - k0, :] = lax.dot_general(
                kp[p, k0:k1, :], q2, (((1,), (1,)), ((), ())), preferred_element_type=F32) + bias

    def attend(p):
        for blk, (start, stride) in enumerate(_class_major_blocks(seq, p)):
            k0, k1 = key_rows(p, blk)
            s = s_sc[p][blk, 0:k1 - k0, :]
            m = jnp.max(s, axis=0, keepdims=True)
            e = jnp.exp2(s - m)
            l = jnp.sum(e, axis=0, keepdims=True)
            acc = jnp.dot(vt[p, :, k0:k1], e.astype(BF16), preferred_element_type=F32)
            out = acc * (1.0 / l)
            lse = m + jnp.log2(l)
            out = jnp.where(lo_row, out[:, :WIN], out[:, WIN:])
            lse = jnp.where(lo_row, lse[:, :WIN], lse[:, WIN:])
            dst = pl.ds(start, WIN, stride=stride) if stride > 1 else slice(start, start + WIN)
            out_s[p][dst, :] = out.T
            lse_s[p][dst, :] = lse.T

    for p in range(npat):
        regroup(p)
    for p in range(npat - 1, -1, -1):
        logits(p)
        attend(p)

    gain = gain_ref[...]
    for blk in range(nblk):
        sl = slice(blk * WIN, (blk + 1) * WIN)
        lses = [s[sl, :] for s in lse_s]
        mx = functools.reduce(jnp.maximum, lses)
        ws = [jnp.exp2(s - mx) for s in lses]
        o = sum(w * x[sl, :] for w, x in zip(ws, out_s)) / sum(ws)
        sq = o * o
        ms_lo = jnp.sum(jnp.where(lo_lane, sq, 0.0), axis=1, keepdims=True)
        ms_hi = jnp.sum(jnp.where(lo_lane, 0.0, sq), axis=1, keepdims=True)
        inv = lax.rsqrt(jnp.where(lo_lane, ms_lo, ms_hi) * (1.0 / HEAD_DIM) + EPS)
        o_ref[sl, :] = (o * inv * gain).astype(o_ref.dtype)


def _dilated(a_qkv, bias, gain_a, batch, seq):
    t = a_qkv.shape[0]
    pairs = A_WIDTH // LANES
    assert all(d == STEP ** p and seq % (d * WIN) == 0 for p, (_, d) in enumerate(A_PATTERNS))
    npat = bias.shape[0]
    col = lambda off: pl.BlockSpec((seq, LANES), lambda b, g: (b, off + g))
    scratch = ([pltpu.VMEM((npat - 2, seq, LANES), F32) for _ in range(3)]
               + [pltpu.VMEM((npat, seq, LANES), BF16) for _ in range(2)]
               + [pltpu.VMEM((npat, seq, LANES), BF16)]
               + [pltpu.VMEM((npat, LANES, seq), BF16)]
               + [pltpu.VMEM((seq // WIN, 2 * WIN, 2 * WIN), F32) for _ in range(npat)]
               + [pltpu.VMEM((seq, LANES), F32) for _ in range(2 * npat)])
    return pl.pallas_call(
        functools.partial(_dilated_kernel, seq=seq),
        grid=(batch, pairs),
        in_specs=[
            col(0), col(pairs), col(2 * pairs),
            pl.BlockSpec((npat, 2 * WIN, 2 * WIN), lambda b, g: (0, 0, g)),
            pl.BlockSpec((1, LANES), lambda b, g: (0, g)),
        ],
        out_specs=pl.BlockSpec((seq, LANES), lambda b, g: (b, g)),
        out_shape=jax.ShapeDtypeStruct((t, A_WIDTH), BF16),
        scratch_shapes=scratch,
        compiler_params=pltpu.CompilerParams(
            dimension_semantics=("arbitrary", "arbitrary"), vmem_limit_bytes=VMEM_LIMIT),
        name="dilated_attention",
    )(a_qkv, a_qkv, a_qkv, bias, gain_a)


def _mla_kernel(q_ref, k_ref, v_ref, gain_ref, o_ref, vt, s_sc, e_sc, *, seq, tq):
    th = tq // 2
    causal = lax.broadcasted_iota(jnp.int32, (th, th), 0) <= lax.broadcasted_iota(jnp.int32, (th, th), 1)
    gain = gain_ref[...]
    for c in range(0, seq, tq):
        vt[:, c:c + tq] = v_ref[c:c + tq, :].astype(F32).T.astype(BF16)

    units = [(i, hh) for i in range(seq // tq) for hh in range(2)]
    base = np.concatenate([[0], np.cumsum([i + 1 for i, _ in units])]).tolist()

    def logits(u):
        i, hh = units[u]
        hs = slice(hh * LANES, (hh + 1) * LANES)
        q = q_ref[i * tq:(i + 1) * tq, hs]
        qk = lambda keys, qs: lax.dot_general(k_ref[keys, hs], qs, (((1,), (1,)), ((), ())),
                                              preferred_element_type=F32)
        m = None
        for j in range(i):
            s = qk(slice(j * tq, (j + 1) * tq), q)
            s_sc[base[u] + j] = s
            bm = jnp.max(s, axis=0, keepdims=True)
            m = bm if m is None else jnp.maximum(m, bm)
        d0 = i * tq
        top = qk(slice(d0, d0 + th), q)
        top = jnp.concatenate([jnp.where(causal, top[:, :th], NEG), top[:, th:]], axis=1)
        bot = jnp.where(causal, qk(slice(d0 + th, d0 + tq), q_ref[d0 + th:d0 + tq, hs]), NEG)
        s_sc[base[u] + i, 0:th, :] = top
        s_sc[base[u] + i, th:tq, th:tq] = bot
        bm = jnp.max(top, axis=0, keepdims=True)
        bm = jnp.concatenate([bm[:, :th], jnp.maximum(bm[:, th:], jnp.max(bot, axis=0, keepdims=True))], axis=1)
        return bm if m is None else jnp.maximum(m, bm)

    def attend(u, m):
        i, hh = units[u]
        l = None
        for j in range(i):
            e = jnp.exp2(s_sc[base[u] + j] - m)
            ls = jnp.sum(e, axis=0, keepdims=True)
            l = ls if l is None else l + ls
            e_sc[(base[u] + j) * tq:(base[u] + j + 1) * tq, :] = e.astype(BF16)
        r0 = (base[u] + i) * tq
        e_top = jnp.exp2(s_sc[base[u] + i, 0:th, :] - m)
        e_bot = jnp.exp2(s_sc[base[u] + i, th:tq, th:tq] - m[:, th:])
        ls = jnp.sum(e_top, axis=0, keepdims=True)
        ls = jnp.concatenate([ls[:, :th], ls[:, th:] + jnp.sum(e_bot, axis=0, keepdims=True)], axis=1)
        l = ls if l is None else l + ls
        e_sc[r0:r0 + th, :] = e_top.astype(BF16)
        e_sc[r0 + th:r0 + tq, 0:th] = jnp.zeros((th, th), BF16)
        e_sc[r0 + th:r0 + tq, th:tq] = e_bot.astype(BF16)
        acc = jnp.dot(vt[hh * B_V:(hh + 1) * B_V, 0:(i + 1) * tq], e_sc[base[u] * tq:base[u + 1] * tq, :],
                      preferred_element_type=F32)
        o = acc / l
        return o * lax.rsqrt(jnp.mean(o * o, axis=0, keepdims=True) + EPS)

    m_next = logits(0)
    normed = []
    for u, (i, hh) in enumerate(units):
        m_cur = m_next
        if u + 1 < len(units):
            m_next = logits(u + 1)
        normed.append(attend(u, m_cur))
        if hh == 1:
            o_ref[i * tq:(i + 1) * tq, :] = (jnp.concatenate(normed, axis=0).T * gain).astype(o_ref.dtype)
            normed = []


def _mla(bq, bk, bv, gain_b, batch, seq):
    t = bq.shape[0]
    pairs = B_HEADS // 2
    tq = MLA_BLOCK
    assert seq % tq == 0
    nq = seq // tq
    nslots = 2 * (nq * (nq + 1) // 2)
    return pl.pallas_call(
        functools.partial(_mla_kernel, seq=seq, tq=tq),
        grid=(batch, pairs),
        in_specs=[
            pl.BlockSpec((seq, 2 * LANES), lambda b, g: (b, g)),
            pl.BlockSpec((seq, 2 * LANES), lambda b, g: (b, g)),
            pl.BlockSpec((seq, LANES), lambda b, g: (b, g)),
            pl.BlockSpec((1, LANES), lambda b, g: (0, g)),
        ],
        out_specs=pl.BlockSpec((seq, LANES), lambda b, g: (b, g)),
        out_shape=jax.ShapeDtypeStruct((t, B_WIDTH), BF16),
        scratch_shapes=[pltpu.VMEM((LANES, seq), BF16),
                        pltpu.VMEM((nslots, tq, tq), F32), pltpu.VMEM((nslots * tq, tq), BF16)],
        compiler_params=pltpu.CompilerParams(
            dimension_semantics=("arbitrary", "arbitrary"), vmem_limit_bytes=VMEM_LIMIT),
        name="latent_attention",
    )(bq, bk, bv, gain_b)


def _retention_consts():
    h = C_HEADS
    log_g = np.log(1.0 - 2.0 ** (-5.0 - np.arange(h))).astype(np.float32)
    i = np.arange(C_CHUNK, dtype=np.float32)
    rel = i[:, None] - i[None, :]
    decay = (np.exp(np.maximum(rel, 0.0)[None] * log_g[:, None, None]) * (rel >= 0)[None]).astype(np.float32)
    xi = np.exp((i + 1.0)[None, :] * log_g[:, None]).astype(np.float32)
    zeta = np.exp((C_CHUNK - 1.0 - i)[None, :] * log_g[:, None]).astype(np.float32)
    chunk_decay = np.exp(C_CHUNK * log_g).astype(np.float32)
    decay_all = np.concatenate(list(decay), axis=1)
    xi_mat = np.repeat(xi.T, C_VAL, axis=1)
    zeta_t = np.zeros((C_QPAD, C_CHUNK), np.float32)
    zeta_t[:C_QK] = np.repeat(zeta, C_KEY, axis=0)
    cd = np.repeat(chunk_decay, C_VAL)[None, :]
    bd = np.zeros((C_QPAD, C_WIDTH), np.float32)
    for hh in range(h):
        bd[hh * C_KEY:(hh + 1) * C_KEY, hh * C_VAL:(hh + 1) * C_VAL] = 1.0
    return tuple(jnp.asarray(a) for a in (decay_all, xi_mat, zeta_t, cd, bd))


def _retention_kernel(qk_ref, v_ref, g_ref, decay_ref, xi_ref, zeta_ref, cd_ref, bd_ref, gain_ref,
                      o_ref, *, seq):
    lane = lax.broadcasted_iota(jnp.int32, (C_CHUNK, LANES), 1)
    lo = lane < C_VAL
    gain = gain_ref[...]
    state = jnp.zeros((C_QPAD, C_WIDTH), F32)
    kshape, vshape = (C_QPAD, C_HEADS * C_CHUNK), (C_HEADS * C_CHUNK, C_WIDTH)
    k_on = (lax.broadcasted_iota(jnp.int32, kshape, 0) // C_KEY) == (lax.broadcasted_iota(jnp.int32, kshape, 1) // C_CHUNK)
    v_on = (lax.broadcasted_iota(jnp.int32, vshape, 0) // C_CHUNK) == (lax.broadcasted_iota(jnp.int32, vshape, 1) // C_VAL)

    for n in range(seq // C_CHUNK):
        rows = slice(n * C_CHUNK, (n + 1) * C_CHUNK)
        q = qk_ref[rows, 0:C_QPAD]
        k_t = qk_ref[rows, C_QPAD:2 * C_QPAD].astype(F32).T
        v = v_ref[rows, :]
        k_bd = jnp.where(k_on, jnp.concatenate([k_t] * C_HEADS, axis=1), 0.0).astype(BF16)
        s = jnp.dot(q, k_bd, preferred_element_type=F32) * decay_ref[...]
        v_bd = jnp.where(v_on, jnp.concatenate([v] * C_HEADS, axis=0), jnp.zeros((), BF16))
        o = (jnp.dot(s.astype(BF16), v_bd, preferred_element_type=F32)
             + jnp.dot(q, state.astype(BF16), preferred_element_type=F32) * xi_ref[...])
        upd = jnp.dot((k_t * zeta_ref[...]).astype(BF16), v, preferred_element_type=F32)
        state = state * cd_ref[...] + upd * bd_ref[...]
        for t in range(C_WIDTH // LANES):
            sl = slice(t * LANES, (t + 1) * LANES)
            x = o[:, sl]
            mu_lo = jnp.sum(jnp.where(lo, x, 0.0), axis=1, keepdims=True)
            mu_hi = jnp.sum(jnp.where(lo, 0.0, x), axis=1, keepdims=True)
            dlt = x - jnp.where(lo, mu_lo, mu_hi) * (1.0 / C_VAL)
            sq = dlt * dlt
            var_lo = jnp.sum(jnp.where(lo, sq, 0.0), axis=1, keepdims=True)
            var_hi = jnp.sum(jnp.where(lo, 0.0, sq), axis=1, keepdims=True)
            y = dlt * lax.rsqrt(jnp.where(lo, var_lo, var_hi) * (1.0 / C_VAL) + EPS)
            o_ref[rows, sl] = (y * g_ref[rows, sl].astype(F32) * gain[:, sl]).astype(o_ref.dtype)


def _retention(cqk, cv, cg, gain_c, consts, batch, seq):
    t = cqk.shape[0]
    tok = lambda width: pl.BlockSpec((seq, width), lambda b: (b, 0))
    full = lambda a: _const_spec(a.shape, lambda b: (0,) * a.ndim)
    return pl.pallas_call(
        functools.partial(_retention_kernel, seq=seq),
        grid=(batch,),
        in_specs=[tok(2 * C_QPAD), tok(C_WIDTH), tok(C_WIDTH)] + [full(a) for a in consts] + [full(gain_c)],
        out_specs=tok(C_WIDTH),
        out_shape=jax.ShapeDtypeStruct((t, C_WIDTH), BF16),
        compiler_params=pltpu.CompilerParams(
            dimension_semantics=("arbitrary",), vmem_limit_bytes=VMEM_LIMIT),
        name="retention",
    )(cqk, cv, cg, *consts, gain_c)


def _outffn_kernel(x_ref, ma_ref, mb_ref, mc_ref, mod_ref, wo_ref, g_ref, wg_ref, wu_ref, wd_ref, fg_ref,
                   *refs, final, fchunk, nsplit, ncast):
    cast_in, o_ref, cast_out = refs[:ncast], refs[ncast], refs[ncast + 1:2 * ncast + 1]
    mix_ref, h_ref, hid_ref = refs[2 * ncast + 1:]
    _cast_slabs(cast_in, cast_out)
    tm = x_ref.shape[0]
    parts = [slice(k * tm // nsplit, (k + 1) * tm // nsplit) for k in range(nsplit)]
    mix_ref[:, 0:A_WIDTH] = ma_ref[...]
    mix_ref[:, A_WIDTH:A_WIDTH + B_WIDTH] = mb_ref[...]
    mix_ref[:, A_WIDTH + B_WIDTH:] = mc_ref[...]
    for r in parts:
        att = jnp.dot(mix_ref[r, :], wo_ref[...], preferred_element_type=F32)
        o_ref[r, :] = x_ref[r, :] + mod_ref[2:3, :] * att
    for r in parts:
        h_ref[r, :] = (_rms(o_ref[r, :]) * g_ref[...] * (1.0 + mod_ref[4:5, :]) + mod_ref[3:4, :]).astype(BF16)
    hidden = wg_ref.shape[1]
    for r in parts:
        for j in range(hidden // fchunk):
            sl = slice(j * fchunk, (j + 1) * fchunk)
            gate = jnp.dot(h_ref[r, :], wg_ref[:, sl], preferred_element_type=F32)
            up = jnp.dot(h_ref[r, :], wu_ref[:, sl], preferred_element_type=F32)
            hid_ref[r, sl] = (_silu(gate) * up).astype(BF16)
    for r in parts:
        ffn = jnp.dot(hid_ref[r, :], wd_ref[...], preferred_element_type=F32)
        y = o_ref[r, :] + mod_ref[5:6, :] * ffn
        if final:
            y = _rms(y) * fg_ref[...]
        o_ref[r, :] = y


def _outffn(x, mix_a, mix_b, mix_c, mods, layer, prep, ffn_w, final_g, batch, seq, tm, final, cast_layer):
    t, d = x.shape
    nt = seq // tm
    w_out, w_gate, w_up, w_down = ffn_w
    hidden = w_gate.shape[1]
    whole = lambda a: _const_spec(a.shape, lambda b, i: (0,) * a.ndim)
    row = lambda w: pl.BlockSpec((tm, w), lambda b, i: (b * nt + i, 0))
    wl = lambda a: _const_spec((None,) + a.shape[1:], lambda b, i: (layer,) + (0,) * (a.ndim - 1))
    cast_ws, cast_in_specs, cast_out_specs, cast_shapes = _cast_plumbing(
        prep, cast_layer, batch * nt, lambda b, i: b * nt + i)
    return pl.pallas_call(
        functools.partial(_outffn_kernel, final=final, fchunk=MXU_WIDTH, nsplit=FFN_SPLIT, ncast=len(cast_ws)),
        grid=(batch, nt),
        in_specs=[
            row(d), row(A_WIDTH), row(B_WIDTH), row(C_WIDTH),
            pl.BlockSpec((None, None, 6, d), lambda b, i: (layer, b, 0, 0)),
            whole(w_out), wl(prep["norm2_g"]), whole(w_gate), whole(w_up), whole(w_down),
            _const_spec((1, d), lambda b, i: (0, 0)),
            *cast_in_specs,
        ],
        out_specs=[row(d)] + cast_out_specs,
        out_shape=[jax.ShapeDtypeStruct((t, d), F32)] + cast_shapes,
        scratch_shapes=[pltpu.VMEM((tm, d), BF16), pltpu.VMEM((tm, d), BF16), pltpu.VMEM((tm, hidden), BF16)],
        compiler_params=pltpu.CompilerParams(
            dimension_semantics=("arbitrary", "arbitrary"), vmem_limit_bytes=VMEM_LIMIT),
        name="outproj_ffn",
    )(x, mix_a, mix_b, mix_c, mods, w_out, prep["norm2_g"], w_gate, w_up, w_down, final_g, *cast_ws)


def _prepare(norm1_g, w_in, mla_q_norm, mla_kv_norm, mla_w_uq, mla_w_ukv, w_out, norm2_g,
             ffn_w_gate, ffn_w_up, ffn_w_down):
    depth, d, _ = w_in.shape
    b_lat0 = 3 * A_WIDTH
    rope0 = b_lat0 + B_Q_RANK + B_KV_RANK
    c0 = rope0 + B_ROPE
    assert C_QK + B_ROPE <= C_QPAD and C_QK % LANES == B_NOPE
    assert c0 + 2 * C_QK + 2 * C_WIDTH == w_in.shape[2]
    uq = mla_w_uq.reshape(depth, B_Q_RANK, B_HEADS, B_NOPE + B_ROPE)
    uq = jnp.pad(uq, ((0, 0), (0, 0), (0, 0), (0, LANES - B_NOPE - B_ROPE)))
    ukv = mla_w_ukv.reshape(depth, B_KV_RANK, B_HEADS, B_NOPE + B_V)
    uk = jnp.pad(ukv[..., :B_NOPE], ((0, 0), (0, 0), (0, 0), (0, LANES - B_NOPE)))
    uv = ukv[..., B_NOPE:]
    return {
        "norm1_g": norm1_g[:, None, :],
        "w_in": jnp.swapaxes(w_in, 1, 2),
        "q_norm": mla_q_norm[:, None, :],
        "kv_norm": mla_kv_norm[:, None, :],
        "w_uq": uq.reshape(depth, B_Q_RANK, B_HEADS * LANES).astype(BF16),
        "w_uk": uk.reshape(depth, B_KV_RANK, B_HEADS * LANES).astype(BF16),
        "w_uv": uv.reshape(depth, B_KV_RANK, B_WIDTH).astype(BF16),
        "w_out": w_out,
        "norm2_g": norm2_g[:, None, :],
        "w_gate": ffn_w_gate,
        "w_up": ffn_w_up,
        "w_down": ffn_w_down,
    }


def kernel(x, c, positions, rel_bias, ada_w, ada_b, norm1_g, w_in, mla_q_norm, mla_kv_norm, mla_w_uq,
           mla_w_ukv, mix_gain, w_out, norm2_g, ffn_w_gate, ffn_w_up, ffn_w_down, final_norm):
    batch, seq, d = x.shape
    depth = w_in.shape[0]
    tm = ROW_TILE
    assert seq % tm == 0 and seq % C_CHUNK == 0

    prep = _prepare(norm1_g, w_in, mla_q_norm, mla_kv_norm, mla_w_uq, mla_w_ukv, w_out, norm2_g,
                    ffn_w_gate, ffn_w_up, ffn_w_down)
    mods = _mods(c, ada_w, ada_b).reshape(depth, batch, 6, d)
    tables = _rope_tables(positions)
    bias = _bias_tables(rel_bias)
    ret_consts = _retention_consts()
    final_g = final_norm[None, :]

    xf = x.reshape(batch * seq, d)
    for l in range(depth):
        a_qkv, bq, bk, bv, cqk, cv, cg, *cast0 = _inproj(xf, mods, l, prep, tables, batch, seq, tm,
                                                         cast_layer=0 if l == 0 else None)
        ffn_w = cast0 if l == 0 else ffn_w_next
        gain = mix_gain[l][None, :]
        mix_a = _dilated(a_qkv, bias, gain[:, :A_WIDTH], batch, seq)
        mix_b = _mla(bq, bk, bv, gain[:, A_WIDTH:A_WIDTH + B_WIDTH], batch, seq)
        mix_c = _retention(cqk, cv, cg, gain[:, A_WIDTH + B_WIDTH:], ret_consts, batch, seq)
        xf, *ffn_w_next = _outffn(xf, mix_a, mix_b, mix_c, mods, l, prep, ffn_w, final_g, batch, seq, tm,
                                  final=(l == depth - 1), cast_layer=l + 1 if l + 1 < depth else None)
    return xf.reshape(batch, seq, d)
```

```python
import functools

import numpy as np
import jax
import jax.numpy as jnp
from jax import lax
from jax.experimental import pallas as pl
from jax.experimental.pallas import tpu as pltpu

F32 = jnp.float32
BF16 = jnp.bfloat16

HEAD_DIM = 64
A_HEADS = 6
A_PATTERNS = ((128, 1), (512, 4), (2048, 16))
A_WIDTH = A_HEADS * HEAD_DIM
B_HEADS = 4
B_NOPE = 64
B_ROPE = 32
B_V = 64
B_Q_RANK = 256
B_KV_RANK = 128
B_WIDTH = B_HEADS * B_V
C_HEADS = 6
C_KEY = 32
C_VAL = 64
C_WIDTH = C_HEADS * C_VAL
C_QK = C_HEADS * C_KEY
C_CHUNK = 128
N_BUCKETS = 32
MAX_DISTANCE = 2048
ROPE_BASE = 10000.0
EPS = 1e-6

LANES = 128
WIN = 128
STEP = 4
NEG = -1e30
LOG2E = 1.4426950408889634
FFN_CAST = ("w_out", "w_gate", "w_up", "w_down")
VMEM_LIMIT = 56 * 1024 * 1024
MXU_WIDTH = 256
ROW_TILE = 512
MLA_BLOCK = 512
MODS_COL_TILE = 1536
ROPE_ROW_TILE = 2048
FFN_SPLIT = 2
INPROJ_SPLIT = 2

OFF_AQ = 0
OFF_BKV = OFF_AQ + A_WIDTH
OFF_AKV = OFF_BKV + B_KV_RANK
OFF_BQ = OFF_AKV + 2 * A_WIDTH
OFF_C = OFF_BQ + B_Q_RANK
C_QPAD = 2 * LANES
OFF_CV = OFF_C + 2 * C_QPAD
OFF_CG = OFF_CV + C_WIDTH
IN_COLS = OFF_CG + C_WIDTH
assert all(off % MXU_WIDTH == 0 for off in (OFF_AKV, OFF_BQ, OFF_C, OFF_CV, IN_COLS))


def _const_spec(shape, index_map):
    return pl.BlockSpec(shape, index_map, pipeline_mode=pl.Buffered(1))


def _silu(x):
    return x / (1.0 + jnp.exp(-x))


def _rms(x):
    return x * lax.rsqrt(jnp.mean(x * x, axis=-1, keepdims=True) + EPS)


def _mods_kernel(c_ref, w_ref, b_ref, o_ref):
    def split(a):
        hi = a.astype(BF16)
        return hi, (a - hi.astype(F32)).astype(BF16)

    c_hi, c_lo = split(_silu(c_ref[...]))
    w_hi, w_lo = split(w_ref[...])
    nb = c_hi.shape[0]
    both = jnp.dot(jnp.concatenate([c_hi, c_lo], axis=0), w_hi, preferred_element_type=F32)
    o_ref[...] = (both[:nb] + both[nb:] + jnp.dot(c_hi, w_lo, preferred_element_type=F32)) + b_ref[...]


def _mods(c, ada_w, ada_b):
    depth, d, n = ada_w.shape
    b = c.shape[0]
    tn = MODS_COL_TILE
    return pl.pallas_call(
        _mods_kernel,
        grid=(depth, n // tn),
        in_specs=[
            pl.BlockSpec((b, d), lambda l, j: (0, 0)),
            pl.BlockSpec((None, d, tn), lambda l, j: (l, 0, j)),
            pl.BlockSpec((None, 1, tn), lambda l, j: (l, 0, j)),
        ],
        out_specs=pl.BlockSpec((None, b, tn), lambda l, j: (l, 0, j)),
        out_shape=jax.ShapeDtypeStruct((depth, b, n), F32),
        compiler_params=pltpu.CompilerParams(vmem_limit_bytes=VMEM_LIMIT),
        name="adaln_mods",
    )(c, ada_w, ada_b.reshape(depth, 1, n))


def _rope_kernel(pos_ref, f_ref, s_ref, cos_ref, sin_ref):
    ang = pos_ref[...].astype(F32) * f_ref[...]
    cos_ref[...] = jnp.cos(ang)
    sin_ref[...] = jnp.sin(ang) * s_ref[...]


def _rope_tables(positions):
    t = positions.size
    half = C_KEY // 2
    assert B_ROPE == C_KEY and B_NOPE % C_KEY == 0
    inv_freq = (1.0 / (ROPE_BASE ** (np.arange(half, dtype=np.float32) / half))).astype(np.float32)
    ones = np.ones(half, np.float32)
    freqs = jnp.asarray(np.tile(np.concatenate([inv_freq, inv_freq]), LANES // C_KEY)[None, :])
    signs = jnp.asarray(np.tile(np.concatenate([-ones, ones]), LANES // C_KEY)[None, :])
    tr = ROPE_ROW_TILE
    out = jax.ShapeDtypeStruct((t, LANES), F32)
    row = pl.BlockSpec((tr, LANES), lambda i: (i, 0))
    return pl.pallas_call(
        _rope_kernel,
        grid=(t // tr,),
        in_specs=[
            pl.BlockSpec((tr, 1), lambda i: (i, 0)),
            pl.BlockSpec((1, LANES), lambda i: (0, 0)),
            pl.BlockSpec((1, LANES), lambda i: (0, 0)),
        ],
        out_specs=[row, row],
        out_shape=[out, out],
        name="rope_tables",
    )(positions.reshape(t, 1), freqs, signs)


def _t5_bucket(dist):
    max_exact = N_BUCKETS // 2
    safe = np.maximum(dist, 1).astype(np.float32)
    large = max_exact + (np.log(safe / max_exact) / np.log(MAX_DISTANCE / max_exact)
                         * (N_BUCKETS - max_exact)).astype(np.int32)
    large = np.minimum(large, N_BUCKETS - 1)
    return np.where(dist < max_exact, dist, large).astype(np.int32)


def _bias_kernel(bmap_ref, rb_ref, o_ref):
    h = pl.program_id(1)
    bm = bmap_ref[...]
    t = jnp.full(bm.shape, NEG, F32)
    for b in range(N_BUCKETS):
        t = jnp.where(bm == b, rb_ref[b, h] * LOG2E, t)
    o_ref[...] = t


def _bias_tables(rel_bias):
    qi = np.arange(WIN)[None, :]
    c = np.arange(2 * WIN)[:, None]
    j = qi - c + WIN
    maps = []
    for (w, d) in A_PATTERNS:
        assert w // d == WIN
        bucket = _t5_bucket(np.arange(WIN + 1, dtype=np.int32) * d)
        maps.append(np.where((j >= 0) & (j <= WIN), bucket[np.clip(j, 0, WIN)], -1).astype(np.int32))
    bmap = jnp.asarray(np.stack(maps))
    npat = len(A_PATTERNS)
    return pl.pallas_call(
        _bias_kernel,
        grid=(npat, A_HEADS),
        in_specs=[
            pl.BlockSpec((None, 2 * WIN, WIN), lambda p, h: (p, 0, 0)),
            pl.BlockSpec(memory_space=pltpu.SMEM),
        ],
        out_specs=pl.BlockSpec((None, 2 * WIN, WIN), lambda p, h: (p, 0, h)),
        out_shape=jax.ShapeDtypeStruct((npat, 2 * WIN, A_HEADS * WIN), F32),
        name="t5_bias_tables",
    )(bmap, rel_bias)


def _cast_plumbing(prep, cast_layer, steps, step_of):
    if cast_layer is None:
        return [], [], [], []
    ws, in_specs, out_specs = [prep[name] for name in FFN_CAST], [], []
    for w in ws:
        nslab = steps
        while w.shape[1] % (16 * nslab):
            assert nslab % 2 == 0
            nslab //= 2
        rep = steps // nslab
        block = (w.shape[1] // nslab, w.shape[2])
        in_specs.append(pl.BlockSpec((None,) + block, lambda *g, rep=rep: (cast_layer, step_of(*g) // rep, 0)))
        out_specs.append(pl.BlockSpec(block, lambda *g, rep=rep: (step_of(*g) // rep, 0)))
    return ws, in_specs, out_specs, [jax.ShapeDtypeStruct(w.shape[1:], BF16) for w in ws]


def _cast_slabs(cast_in, cast_out):
    for src, dst in zip(cast_in, cast_out):
        dst[...] = src[...].astype(BF16)


def _rot_half(x, x1_mask):
    w = x.shape[-1]
    half = B_ROPE // 2
    return jnp.where(x1_mask, pltpu.roll(x, w - half, 1), pltpu.roll(x, half, 1))


def _inproj_kernel(x_ref, mod_ref, g_ref, win_ref, qg_ref, kvg_ref, wuq_ref, wuk_ref, wuv_ref,
                   cc_ref, sc_ref, ks_ref, *refs, ncast):
    cast_in, refs = refs[:ncast], refs[ncast:]
    a_ref, bq_ref, bk_ref, bv_ref, cqk_ref, cv_ref, cg_ref = refs[:7]
    cast_out, w_ref = refs[7:7 + ncast], refs[7 + ncast]
    _cast_slabs(cast_in, cast_out)

    @pl.when((pl.program_id(0) == 0) & (pl.program_id(1) == 0))
    def _():
        d = win_ref.shape[1]
        bq0 = 3 * A_WIDTH
        bkv0 = bq0 + B_Q_RANK
        rope0 = bkv0 + B_KV_RANK
        c0 = rope0 + B_ROPE
        zeros = lambda n: jnp.zeros((n, d), F32)

        def place(dst, src, n):
            for k in range(0, n, LANES):
                w_ref[:, dst + k:dst + k + LANES] = win_ref[src + k:src + k + LANES, :].T.astype(BF16)

        place(OFF_AQ, 0, A_WIDTH)
        place(OFF_BKV, bkv0, B_KV_RANK)
        place(OFF_AKV, A_WIDTH, 2 * A_WIDTH)
        place(OFF_BQ, bq0, B_Q_RANK)
        q_rows = jnp.concatenate([win_ref[c0:c0 + C_QK, :], win_ref[rope0:c0, :],
                                  zeros(C_QPAD - C_QK - B_ROPE)], axis=0)
        w_ref[:, OFF_C:OFF_C + C_QPAD] = q_rows.T.astype(BF16)
        k_rows = jnp.concatenate([win_ref[c0 + C_QK:c0 + 2 * C_QK, :], zeros(C_QPAD - C_QK)], axis=0)
        w_ref[:, OFF_C + C_QPAD:OFF_CV] = k_rows.T.astype(BF16)
        place(OFF_CV, c0 + 2 * C_QK, 2 * C_WIDTH)

    tm = x_ref.shape[0] // INPROJ_SPLIT
    parts = [slice(k * tm, (k + 1) * tm) for k in range(INPROJ_SPLIT)]
    lane = lax.broadcasted_iota(jnp.int32, (tm, LANES), 1)
    b_x1 = (lane >= B_NOPE) & (lane < B_NOPE + B_ROPE // 2)
    b_rope = (lane >= B_NOPE) & (lane < B_NOPE + B_ROPE)
    c_x1 = (lane & (C_KEY // 2)) == 0

    def proj(h, lo, hi):
        return jnp.dot(h, w_ref[:, lo:hi], preferred_element_type=F32)

    hs = [(_rms(x_ref[r, :]) * g_ref[...] * (1.0 + mod_ref[1:2, :]) + mod_ref[0:1, :]).astype(BF16)
          for r in parts]
    lat = []
    for r, h in zip(parts, hs):
        aq_kv = proj(h, OFF_AQ, OFF_AKV)
        a_ref[r, 0:A_WIDTH] = aq_kv[:, :A_WIDTH] * (HEAD_DIM ** -0.5 * LOG2E)
        lat.append((proj(h, OFF_BQ, OFF_C), aq_kv[:, OFF_BKV:OFF_AKV], proj(h, OFF_C, OFF_CV)))
    normed = [((_rms(q_lat) * qg_ref[...]).astype(BF16), (_rms(kv_lat) * kvg_ref[...]).astype(BF16))
              for q_lat, kv_lat, _ in lat]
    for r, h in zip(parts, hs):
        a_ref[r, A_WIDTH:3 * A_WIDTH] = proj(h, OFF_AKV, OFF_BQ)
    ups = []
    for r, h, (qn, kvn) in zip(parts, hs, normed):
        ups.append((jnp.dot(qn, wuq_ref[...], preferred_element_type=F32),
                    jnp.dot(kvn, wuk_ref[...], preferred_element_type=F32)))
        bv_ref[r, :] = jnp.dot(kvn, wuv_ref[...], preferred_element_type=F32).astype(BF16)
        v_gate = proj(h, OFF_CV, IN_COLS)
        cv_ref[r, :] = v_gate[:, :C_WIDTH].astype(BF16)
        cg_ref[r, :] = _silu(v_gate[:, C_WIDTH:]).astype(BF16)
    b_scale = (B_NOPE + B_ROPE) ** -0.5 * LOG2E
    for r, (_, _, qk), (q, k_nope) in zip(parts, lat, ups):
        cos_c, sin_c = cc_ref[r, :], sc_ref[r, :]
        cos_b, sin_b = jnp.where(b_rope, cos_c, 1.0), jnp.where(b_rope, sin_c, 0.0)
        k_rope = jnp.where(lane >= B_NOPE, qk[:, LANES:2 * LANES], 0.0)
        k_pe = k_rope * cos_b + _rot_half(k_rope, b_x1) * sin_b
        for hh in range(B_HEADS):
            sl = slice(hh * LANES, (hh + 1) * LANES)
            qh = q[:, sl]
            bq_ref[r, sl] = ((qh * cos_b + _rot_half(qh, b_x1) * sin_b) * b_scale).astype(BF16)
            bk_ref[r, sl] = (k_nope[:, sl] + k_pe).astype(BF16)
        for t in range(2 * C_QPAD // LANES):
            sl = slice(t * LANES, (t + 1) * LANES)
            xt = qk[:, sl]
            cqk_ref[r, sl] = ((xt * cos_c + _rot_half(xt, c_x1) * sin_c) * ks_ref[:, sl]).astype(BF16)


def _inproj(x, mods, layer, prep, tables, batch, seq, tm, cast_layer):
    t, d = x.shape
    nt = seq // tm
    row = lambda w: pl.BlockSpec((tm, w), lambda b, i: (b * nt + i, 0))
    wl = lambda a: _const_spec((None,) + a.shape[1:], lambda b, i: (layer,) + (0,) * (a.ndim - 1))
    cos_c, sin_c = tables
    kscale = jnp.asarray(np.concatenate([np.ones(C_QPAD, np.float32),
                                         np.full(C_QPAD, C_KEY ** -0.5, np.float32)])[None, :])
    outs = [(3 * A_WIDTH, F32), (B_HEADS * LANES, BF16), (B_HEADS * LANES, BF16), (B_WIDTH, BF16),
            (2 * C_QPAD, BF16), (C_WIDTH, BF16), (C_WIDTH, BF16)]
    cast_ws, cast_in_specs, cast_out_specs, cast_shapes = _cast_plumbing(
        prep, cast_layer, batch * nt, lambda b, i: b * nt + i)
    return pl.pallas_call(
        functools.partial(_inproj_kernel, ncast=len(cast_ws)),
        grid=(batch, nt),
        in_specs=[
            row(d),
            pl.BlockSpec((None, None, 6, d), lambda b, i: (layer, b, 0, 0)),
            wl(prep["norm1_g"]), wl(prep["w_in"]), wl(prep["q_norm"]), wl(prep["kv_norm"]),
            wl(prep["w_uq"]), wl(prep["w_uk"]), wl(prep["w_uv"]),
            row(LANES), row(LANES),
            _const_spec((1, 2 * C_QPAD), lambda b, i: (0, 0)),
            *cast_in_specs,
        ],
        out_specs=[row(w) for w, _ in outs] + cast_out_specs,
        out_shape=[jax.ShapeDtypeStruct((t, w), dt) for w, dt in outs] + cast_shapes,
        scratch_shapes=[pltpu.VMEM((d, IN_COLS), BF16)],
        compiler_params=pltpu.CompilerParams(
            dimension_semantics=("arbitrary", "arbitrary"), vmem_limit_bytes=VMEM_LIMIT),
        name="inproj",
    )(x, mods, prep["norm1_g"], prep["w_in"], prep["q_norm"], prep["kv_norm"],
      prep["w_uq"], prep["w_uk"], prep["w_uv"], cos_c, sin_c, kscale, *cast_ws)


def _class_major_blocks(seq, level):
    n = seq // STEP
    perm = np.arange(seq)
    for _ in range(level):
        perm = perm.reshape(n, STEP).T.reshape(seq)
    blocks = perm.reshape(seq // WIN, WIN)
    stride = STEP ** level
    assert (blocks == blocks[:, :1] + stride * np.arange(WIN)).all()
    return [(int(b[0]), stride) for b in blocks]


def _dilated_kernel(q_ref, k_ref, v_ref, bias_ref, gain_ref, o_ref,
                    qf, kf, vf, qlo, qhi, kp, vt, *stage_refs, seq):
    npat = len(A_PATTERNS)
    nblk = seq // WIN
    s_sc, out_s, lse_s = (stage_refs[i * npat:(i + 1) * npat] for i in range(3))
    lane = lax.broadcasted_iota(jnp.int32, (WIN, LANES), 1)
    lo_lane = lane < HEAD_DIM
    lo_row = lax.broadcasted_iota(jnp.int32, (LANES, WIN), 0) < HEAD_DIM

    n = seq // STEP

    def regroup(p):
        for r in range(STEP):
            rows = slice(r * n, (r + 1) * n)
            src = rows if p == 0 else pl.ds(r, n, stride=STEP)
            level_below = (q_ref, k_ref, v_ref) if p <= 1 else (qf.at[p - 2], kf.at[p - 2], vf.at[p - 2])
            q, k, v = (f[src, :] for f in level_below)
            if 0 < p < npat - 1:
                qf[p - 1, rows, :], kf[p - 1, rows, :], vf[p - 1, rows, :] = q, k, v
            lo_q = lax.broadcasted_iota(jnp.int32, q.shape, 1) < HEAD_DIM
            qlo[p, rows, :] = jnp.where(lo_q, q, 0.0).astype(BF16)
            qhi[p, rows, :] = jnp.where(lo_q, 0.0, q).astype(BF16)
            vt[p, :, rows] = v.T.astype(BF16)
            kp[p, rows, :] = k.astype(BF16)

    def key_rows(p, blk):
        r0 = blk * WIN
        first = blk % (nblk // A_PATTERNS[p][1]) == 0
        return (r0 if first else r0 - WIN), r0 + WIN

    def logits(p):
        for blk in range(nblk):
            k0, k1 = key_rows(p, blk)
            r0 = blk * WIN
            q2 = jnp.concatenate([qlo[p, r0:r0 + WIN, :], qhi[p, r0:r0 + WIN, :]], axis=0)
            bias = bias_ref[p, 2 * WIN - (k1 - k0):, :]
            s_sc[p][blk, 0:k1 - k0, :] = lax.dot_general(
                kp[p, k0:k1, :], q2, (((1,), (1,)), ((), ())), preferred_element_type=F32) + bias

    def attend(p):
        for blk, (start, stride) in enumerate(_class_major_blocks(seq, p)):
            k0, k1 = key_rows(p, blk)
            s = s_sc[p][blk, 0:k1 - k0, :]
            m = jnp.max(s, axis=0, keepdims=True)
            e = jnp.exp2(s - m)
            l = jnp.sum(e, axis=0, keepdims=True)
            acc = jnp.dot(vt[p, :, k0:k1], e.astype(BF16), preferred_element_type=F32)
            out = acc * (1.0 / l)
            lse = m + jnp.log2(l)
            out = jnp.where(lo_row, out[:, :WIN], out[:, WIN:])
            lse = jnp.where(lo_row, lse[:, :WIN], lse[:, WIN:])
            dst = pl.ds(start, WIN, stride=stride) if stride > 1 else slice(start, start + WIN)
            out_s[p][dst, :] = out.T
            lse_s[p][dst, :] = lse.T

    for p in range(npat):
        regroup(p)
    for p in range(npat - 1, -1, -1):
        logits(p)
        attend(p)

    gain = gain_ref[...]
    for blk in range(nblk):
        sl = slice(blk * WIN, (blk + 1) * WIN)
        lses = [s[sl, :] for s in lse_s]
        mx = functools.reduce(jnp.maximum, lses)
        ws = [jnp.exp2(s - mx) for s in lses]
        o = sum(w * x[sl, :] for w, x in zip(ws, out_s)) / sum(ws)
        sq = o * o
        ms_lo = jnp.sum(jnp.where(lo_lane, sq, 0.0), axis=1, keepdims=True)
        ms_hi = jnp.sum(jnp.where(lo_lane, 0.0, sq), axis=1, keepdims=True)
        inv = lax.rsqrt(jnp.where(lo_lane, ms_lo, ms_hi) * (1.0 / HEAD_DIM) + EPS)
        o_ref[sl, :] = (o * inv * gain).astype(o_ref.dtype)


def _dilated(a_qkv, bias, gain_a, batch, seq):
    t = a_qkv.shape[0]
    pairs = A_WIDTH // LANES
    assert all(d == STEP ** p and seq % (d * WIN) == 0 for p, (_, d) in enumerate(A_PATTERNS))
    npat = bias.shape[0]
    col = lambda off: pl.BlockSpec((seq, LANES), lambda b, g: (b, off + g))
    scratch = ([pltpu.VMEM((npat - 2, seq, LANES), F32) for _ in range(3)]
               + [pltpu.VMEM((npat, seq, LANES), BF16) for _ in range(2)]
               + [pltpu.VMEM((npat, seq, LANES), BF16)]
               + [pltpu.VMEM((npat, LANES, seq), BF16)]
               + [pltpu.VMEM((seq // WIN, 2 * WIN, 2 * WIN), F32) for _ in range(npat)]
               + [pltpu.VMEM((seq, LANES), F32) for _ in range(2 * npat)])
    return pl.pallas_call(
        functools.partial(_dilated_kernel, seq=seq),
        grid=(batch, pairs),
        in_specs=[
            col(0), col(pairs), col(2 * pairs),
            pl.BlockSpec((npat, 2 * WIN, 2 * WIN), lambda b, g: (0, 0, g)),
            pl.BlockSpec((1, LANES), lambda b, g: (0, g)),
        ],
        out_specs=pl.BlockSpec((seq, LANES), lambda b, g: (b, g)),
        out_shape=jax.ShapeDtypeStruct((t, A_WIDTH), BF16),
        scratch_shapes=scratch,
        compiler_params=pltpu.CompilerParams(
            dimension_semantics=("arbitrary", "arbitrary"), vmem_limit_bytes=VMEM_LIMIT),
        name="dilated_attention",
    )(a_qkv, a_qkv, a_qkv, bias, gain_a)


def _mla_kernel(q_ref, k_ref, v_ref, gain_ref, o_ref, vt, s_sc, e_sc, *, seq, tq):
    th = tq // 2
    causal = lax.broadcasted_iota(jnp.int32, (th, th), 0) <= lax.broadcasted_iota(jnp.int32, (th, th), 1)
    gain = gain_ref[...]
    for c in range(0, seq, tq):
        vt[:, c:c + tq] = v_ref[c:c + tq, :].astype(F32).T.astype(BF16)

    units = [(i, hh) for i in range(seq // tq) for hh in range(2)]
    base = np.concatenate([[0], np.cumsum([i + 1 for i, _ in units])]).tolist()

    def logits(u):
        i, hh = units[u]
        hs = slice(hh * LANES, (hh + 1) * LANES)
        q = q_ref[i * tq:(i + 1) * tq, hs]
        qk = lambda keys, qs: lax.dot_general(k_ref[keys, hs], qs, (((1,), (1,)), ((), ())),
                                              preferred_element_type=F32)
        m = None
        for j in range(i):
            s = qk(slice(j * tq, (j + 1) * tq), q)
            s_sc[base[u] + j] = s
            bm = jnp.max(s, axis=0, keepdims=True)
            m = bm if m is None else jnp.maximum(m, bm)
        d0 = i * tq
        top = qk(slice(d0, d0 + th), q)
        top = jnp.concatenate([jnp.where(causal, top[:, :th], NEG), top[:, th:]], axis=1)
        bot = jnp.where(causal, qk(slice(d0 + th, d0 + tq), q_ref[d0 + th:d0 + tq, hs]), NEG)
        s_sc[base[u] + i, 0:th, :] = top
        s_sc[base[u] + i, th:tq, th:tq] = bot
        bm = jnp.max(top, axis=0, keepdims=True)
        bm = jnp.concatenate([bm[:, :th], jnp.maximum(bm[:, th:], jnp.max(bot, axis=0, keepdims=True))], axis=1)
        return bm if m is None else jnp.maximum(m, bm)

    def attend(u, m):
        i, hh = units[u]
        l = None
        for j in range(i):
            e = jnp.exp2(s_sc[base[u] + j] - m)
            ls = jnp.sum(e, axis=0, keepdims=True)
            l = ls if l is None else l + ls
            e_sc[(base[u] + j) * tq:(base[u] + j + 1) * tq, :] = e.astype(BF16)
        r0 = (base[u] + i) * tq
        e_top = jnp.exp2(s_sc[base[u] + i, 0:th, :] - m)
        e_bot = jnp.exp2(s_sc[base[u] + i, th:tq, th:tq] - m[:, th:])
        ls = jnp.sum(e_top, axis=0, keepdims=True)
        ls = jnp.concatenate([ls[:, :th], ls[:, th:] + jnp.sum(e_bot, axis=0, keepdims=True)], axis=1)
        l = ls if l is None else l + ls
        e_sc[r0:r0 + th, :] = e_top.astype(BF16)
        e_sc[r0 + th:r0 + tq, 0:th] = jnp.zeros((th, th), BF16)
        e_sc[r0 + th:r0 + tq, th:tq] = e_bot.astype(BF16)
        acc = jnp.dot(vt[hh * B_V:(hh + 1) * B_V, 0:(i + 1) * tq], e_sc[base[u] * tq:base[u + 1] * tq, :],
                      preferred_element_type=F32)
        o = acc / l
        return o * lax.rsqrt(jnp.mean(o * o, axis=0, keepdims=True) + EPS)

    m_next = logits(0)
    normed = []
    for u, (i, hh) in enumerate(units):
        m_cur = m_next
        if u + 1 < len(units):
            m_next = logits(u + 1)
        normed.append(attend(u, m_cur))
        if hh == 1:
            o_ref[i * tq:(i + 1) * tq, :] = (jnp.concatenate(normed, axis=0).T * gain).astype(o_ref.dtype)
            normed = []


def _mla(bq, bk, bv, gain_b, batch, seq):
    t = bq.shape[0]
    pairs = B_HEADS // 2
    tq = MLA_BLOCK
    assert seq % tq == 0
    nq = seq // tq
    nslots = 2 * (nq * (nq + 1) // 2)
    return pl.pallas_call(
        functools.partial(_mla_kernel, seq=seq, tq=tq),
        grid=(batch, pairs),
        in_specs=[
            pl.BlockSpec((seq, 2 * LANES), lambda b, g: (b, g)),
            pl.BlockSpec((seq, 2 * LANES), lambda b, g: (b, g)),
            pl.BlockSpec((seq, LANES), lambda b, g: (b, g)),
            pl.BlockSpec((1, LANES), lambda b, g: (0, g)),
        ],
        out_specs=pl.BlockSpec((seq, LANES), lambda b, g: (b, g)),
        out_shape=jax.ShapeDtypeStruct((t, B_WIDTH), BF16),
        scratch_shapes=[pltpu.VMEM((LANES, seq), BF16),
                        pltpu.VMEM((nslots, tq, tq), F32), pltpu.VMEM((nslots * tq, tq), BF16)],
        compiler_params=pltpu.CompilerParams(
            dimension_semantics=("arbitrary", "arbitrary"), vmem_limit_bytes=VMEM_LIMIT),
        name="latent_attention",
    )(bq, bk, bv, gain_b)


def _retention_consts():
    h = C_HEADS
    log_g = np.log(1.0 - 2.0 ** (-5.0 - np.arange(h))).astype(np.float32)
    i = np.arange(C_CHUNK, dtype=np.float32)
    rel = i[:, None] - i[None, :]
    decay = (np.exp(np.maximum(rel, 0.0)[None] * log_g[:, None, None]) * (rel >= 0)[None]).astype(np.float32)
    xi = np.exp((i + 1.0)[None, :] * log_g[:, None]).astype(np.float32)
    zeta = np.exp((C_CHUNK - 1.0 - i)[None, :] * log_g[:, None]).astype(np.float32)
    chunk_decay = np.exp(C_CHUNK * log_g).astype(np.float32)
    decay_all = np.concatenate(list(decay), axis=1)
    xi_mat = np.repeat(xi.T, C_VAL, axis=1)
    zeta_t = np.zeros((C_QPAD, C_CHUNK), np.float32)
    zeta_t[:C_QK] = np.repeat(zeta, C_KEY, axis=0)
    cd = np.repeat(chunk_decay, C_VAL)[None, :]
    bd = np.zeros((C_QPAD, C_WIDTH), np.float32)
    for hh in range(h):
        bd[hh * C_KEY:(hh + 1) * C_KEY, hh * C_VAL:(hh + 1) * C_VAL] = 1.0
    return tuple(jnp.asarray(a) for a in (decay_all, xi_mat, zeta_t, cd, bd))


def _retention_kernel(qk_ref, v_ref, g_ref, decay_ref, xi_ref, zeta_ref, cd_ref, bd_ref, gain_ref,
                      o_ref, *, seq):
    lane = lax.broadcasted_iota(jnp.int32, (C_CHUNK, LANES), 1)
    lo = lane < C_VAL
    gain = gain_ref[...]
    state = jnp.zeros((C_QPAD, C_WIDTH), F32)
    kshape, vshape = (C_QPAD, C_HEADS * C_CHUNK), (C_HEADS * C_CHUNK, C_WIDTH)
    k_on = (lax.broadcasted_iota(jnp.int32, kshape, 0) // C_KEY) == (lax.broadcasted_iota(jnp.int32, kshape, 1) // C_CHUNK)
    v_on = (lax.broadcasted_iota(jnp.int32, vshape, 0) // C_CHUNK) == (lax.broadcasted_iota(jnp.int32, vshape, 1) // C_VAL)

    for n in range(seq // C_CHUNK):
        rows = slice(n * C_CHUNK, (n + 1) * C_CHUNK)
        q = qk_ref[rows, 0:C_QPAD]
        k_t = qk_ref[rows, C_QPAD:2 * C_QPAD].astype(F32).T
        v = v_ref[rows, :]
        k_bd = jnp.where(k_on, jnp.concatenate([k_t] * C_HEADS, axis=1), 0.0).astype(BF16)
        s = jnp.dot(q, k_bd, preferred_element_type=F32) * decay_ref[...]
        inter = jnp.dot(q, state.astype(BF16), preferred_element_type=F32) * xi_ref[...]
        upd = jnp.dot((k_t * zeta_ref[...]).astype(BF16), v, preferred_element_type=F32)
        state = state * cd_ref[...] + upd * bd_ref[...]
        v_bd = jnp.where(v_on, jnp.concatenate([v] * C_HEADS, axis=0), jnp.zeros((), BF16))
        o = jnp.dot(s.astype(BF16), v_bd, preferred_element_type=F32) + inter
        for t in range(C_WIDTH // LANES):
            sl = slice(t * LANES, (t + 1) * LANES)
            x = o[:, sl]
            mu_lo = jnp.sum(jnp.where(lo, x, 0.0), axis=1, keepdims=True)
            mu_hi = jnp.sum(jnp.where(lo, 0.0, x), axis=1, keepdims=True)
            dlt = x - jnp.where(lo, mu_lo, mu_hi) * (1.0 / C_VAL)
            sq = dlt * dlt
            var_lo = jnp.sum(jnp.where(lo, sq, 0.0), axis=1, keepdims=True)
            var_hi = jnp.sum(jnp.where(lo, 0.0, sq), axis=1, keepdims=True)
            y = dlt * lax.rsqrt(jnp.where(lo, var_lo, var_hi) * (1.0 / C_VAL) + EPS)
            o_ref[rows, sl] = (y * g_ref[rows, sl].astype(F32) * gain[:, sl]).astype(o_ref.dtype)


def _retention(cqk, cv, cg, gain_c, consts, batch, seq):
    t = cqk.shape[0]
    tok = lambda width: pl.BlockSpec((seq, width), lambda b: (b, 0))
    full = lambda a: _const_spec(a.shape, lambda b: (0,) * a.ndim)
    return pl.pallas_call(
        functools.partial(_retention_kernel, seq=seq),
        grid=(batch,),
        in_specs=[tok(2 * C_QPAD), tok(C_WIDTH), tok(C_WIDTH)] + [full(a) for a in consts] + [full(gain_c)],
        out_specs=tok(C_WIDTH),
        out_shape=jax.ShapeDtypeStruct((t, C_WIDTH), BF16),
        compiler_params=pltpu.CompilerParams(
            dimension_semantics=("arbitrary",), vmem_limit_bytes=VMEM_LIMIT),
        name="retention",
    )(cqk, cv, cg, *consts, gain_c)


def _outffn_kernel(x_ref, ma_ref, mb_ref, mc_ref, mod_ref, wo_ref, g_ref, wg_ref, wu_ref, wd_ref, fg_ref,
                   *refs, final, fchunk, nsplit, ncast):
    cast_in, o_ref, cast_out = refs[:ncast], refs[ncast], refs[ncast + 1:2 * ncast + 1]
    mix_ref, h_ref, hid_ref = refs[2 * ncast + 1:]
    _cast_slabs(cast_in, cast_out)
    tm = x_ref.shape[0]
    parts = [slice(k * tm // nsplit, (k + 1) * tm // nsplit) for k in range(nsplit)]
    mix_ref[:, 0:A_WIDTH] = ma_ref[...]
    mix_ref[:, A_WIDTH:A_WIDTH + B_WIDTH] = mb_ref[...]
    mix_ref[:, A_WIDTH + B_WIDTH:] = mc_ref[...]
    for r in parts:
        att = jnp.dot(mix_ref[r, :], wo_ref[...], preferred_element_type=F32)
        o_ref[r, :] = x_ref[r, :] + mod_ref[2:3, :] * att
    for r in parts:
        h_ref[r, :] = (_rms(o_ref[r, :]) * g_ref[...] * (1.0 + mod_ref[4:5, :]) + mod_ref[3:4, :]).astype(BF16)
    hidden = wg_ref.shape[1]
    for r in parts:
        for j in range(hidden // fchunk):
            sl = slice(j * fchunk, (j + 1) * fchunk)
            gate = jnp.dot(h_ref[r, :], wg_ref[:, sl], preferred_element_type=F32)
            up = jnp.dot(h_ref[r, :], wu_ref[:, sl], preferred_element_type=F32)
            hid_ref[r, sl] = (_silu(gate) * up).astype(BF16)
    for r in parts:
        ffn = jnp.dot(hid_ref[r, :], wd_ref[...], preferred_element_type=F32)
        y = o_ref[r, :] + mod_ref[5:6, :] * ffn
        if final:
            y = _rms(y) * fg_ref[...]
        o_ref[r, :] = y


def _outffn(x, mix_a, mix_b, mix_c, mods, layer, prep, ffn_w, final_g, batch, seq, tm, final, cast_layer):
    t, d = x.shape
    nt = seq // tm
    w_out, w_gate, w_up, w_down = ffn_w
    hidden = w_gate.shape[1]
    whole = lambda a: _const_spec(a.shape, lambda b, i: (0,) * a.ndim)
    row = lambda w: pl.BlockSpec((tm, w), lambda b, i: (b * nt + i, 0))
    wl = lambda a: _const_spec((None,) + a.shape[1:], lambda b, i: (layer,) + (0,) * (a.ndim - 1))
    cast_ws, cast_in_specs, cast_out_specs, cast_shapes = _cast_plumbing(
        prep, cast_layer, batch * nt, lambda b, i: b * nt + i)
    return pl.pallas_call(
        functools.partial(_outffn_kernel, final=final, fchunk=MXU_WIDTH, nsplit=FFN_SPLIT, ncast=len(cast_ws)),
        grid=(batch, nt),
        in_specs=[
            row(d), row(A_WIDTH), row(B_WIDTH), row(C_WIDTH),
            pl.BlockSpec((None, None, 6, d), lambda b, i: (layer, b, 0, 0)),
            whole(w_out), wl(prep["norm2_g"]), whole(w_gate), whole(w_up), whole(w_down),
            _const_spec((1, d), lambda b, i: (0, 0)),
            *cast_in_specs,
        ],
        out_specs=[row(d)] + cast_out_specs,
        out_shape=[jax.ShapeDtypeStruct((t, d), F32)] + cast_shapes,
        scratch_shapes=[pltpu.VMEM((tm, d), BF16), pltpu.VMEM((tm, d), BF16), pltpu.VMEM((tm, hidden), BF16)],
        compiler_params=pltpu.CompilerParams(
            dimension_semantics=("arbitrary", "arbitrary"), vmem_limit_bytes=VMEM_LIMIT),
        name="outproj_ffn",
    )(x, mix_a, mix_b, mix_c, mods, w_out, prep["norm2_g"], w_gate, w_up, w_down, final_g, *cast_ws)


def _prepare(norm1_g, w_in, mla_q_norm, mla_kv_norm, mla_w_uq, mla_w_ukv, w_out, norm2_g,
             ffn_w_gate, ffn_w_up, ffn_w_down):
    depth, d, _ = w_in.shape
    b_lat0 = 3 * A_WIDTH
    rope0 = b_lat0 + B_Q_RANK + B_KV_RANK
    c0 = rope0 + B_ROPE
    assert C_QK + B_ROPE <= C_QPAD and C_QK % LANES == B_NOPE
    assert c0 + 2 * C_QK + 2 * C_WIDTH == w_in.shape[2]
    uq = mla_w_uq.reshape(depth, B_Q_RANK, B_HEADS, B_NOPE + B_ROPE)
    uq = jnp.pad(uq, ((0, 0), (0, 0), (0, 0), (0, LANES - B_NOPE - B_ROPE)))
    ukv = mla_w_ukv.reshape(depth, B_KV_RANK, B_HEADS, B_NOPE + B_V)
    uk = jnp.pad(ukv[..., :B_NOPE], ((0, 0), (0, 0), (0, 0), (0, LANES - B_NOPE)))
    uv = ukv[..., B_NOPE:]
    return {
        "norm1_g": norm1_g[:, None, :],
        "w_in": jnp.swapaxes(w_in, 1, 2),
        "q_norm": mla_q_norm[:, None, :],
        "kv_norm": mla_kv_norm[:, None, :],
        "w_uq": uq.reshape(depth, B_Q_RANK, B_HEADS * LANES).astype(BF16),
        "w_uk": uk.reshape(depth, B_KV_RANK, B_HEADS * LANES).astype(BF16),
        "w_uv": uv.reshape(depth, B_KV_RANK, B_WIDTH).astype(BF16),
        "w_out": w_out,
        "norm2_g": norm2_g[:, None, :],
        "w_gate": ffn_w_gate,
        "w_up": ffn_w_up,
        "w_down": ffn_w_down,
    }


def kernel(x, c, positions, rel_bias, ada_w, ada_b, norm1_g, w_in, mla_q_norm, mla_kv_norm, mla_w_uq,
           mla_w_ukv, mix_gain, w_out, norm2_g, ffn_w_gate, ffn_w_up, ffn_w_down, final_norm):
    batch, seq, d = x.shape
    depth = w_in.shape[0]
    tm = ROW_TILE
    assert seq % tm == 0 and seq % C_CHUNK == 0

    prep = _prepare(norm1_g, w_in, mla_q_norm, mla_kv_norm, mla_w_uq, mla_w_ukv, w_out, norm2_g,
                    ffn_w_gate, ffn_w_up, ffn_w_down)
    mods = _mods(c, ada_w, ada_b).reshape(depth, batch, 6, d)
    tables = _rope_tables(positions)
    bias = _bias_tables(rel_bias)
    ret_consts = _retention_consts()
    final_g = final_norm[None, :]

    xf = x.reshape(batch * seq, d)
    for l in range(depth):
        a_qkv, bq, bk, bv, cqk, cv, cg, *cast0 = _inproj(xf, mods, l, prep, tables, batch, seq, tm,
                                                         cast_layer=0 if l == 0 else None)
        ffn_w = cast0 if l == 0 else ffn_w_next
        gain = mix_gain[l][None, :]
        mix_a = _dilated(a_qkv, bias, gain[:, :A_WIDTH], batch, seq)
        mix_b = _mla(bq, bk, bv, gain[:, A_WIDTH:A_WIDTH + B_WIDTH], batch, seq)
        mix_c = _retention(cqk, cv, cg, gain[:, A_WIDTH + B_WIDTH:], ret_consts, batch, seq)
        xf, *ffn_w_next = _outffn(xf, mix_a, mix_b, mix_c, mods, l, prep, ffn_w, final_g, batch, seq, tm,
                                  final=(l == depth - 1), cast_layer=l + 1 if l + 1 < depth else None)
    return xf.reshape(batch, seq, d)
```

```python
import functools

import numpy as np
import jax
import jax.numpy as jnp
from jax import lax
from jax.experimental import pallas as pl
from jax.experimental.pallas import tpu as pltpu

F32 = jnp.float32
BF16 = jnp.bfloat16

HEAD_DIM = 64
A_HEADS = 6
A_PATTERNS = ((128, 1), (512, 4), (2048, 16))
A_WIDTH = A_HEADS * HEAD_DIM
B_HEADS = 4
B_NOPE = 64
B_ROPE = 32
B_V = 64
B_Q_RANK = 256
B_KV_RANK = 128
B_WIDTH = B_HEADS * B_V
C_HEADS = 6
C_KEY = 32
C_VAL = 64
C_WIDTH = C_HEADS * C_VAL
C_QK = C_HEADS * C_KEY
C_CHUNK = 128
N_BUCKETS = 32
MAX_DISTANCE = 2048
ROPE_BASE = 10000.0
EPS = 1e-6

LANES = 128
WIN = 128
STEP = 4
NEG = -1e30
LOG2E = 1.4426950408889634
FFN_CAST = ("w_out", "w_gate", "w_up", "w_down")
VMEM_LIMIT = 56 * 1024 * 1024
MXU_WIDTH = 256
ROW_TILE = 512
MLA_BLOCK = 512
MODS_COL_TILE = 1536
ROPE_ROW_TILE = 2048
FFN_SPLIT = 2
INPROJ_SPLIT = 2

OFF_AQ = 0
OFF_BKV = OFF_AQ + A_WIDTH
OFF_AKV = OFF_BKV + B_KV_RANK
OFF_BQ = OFF_AKV + 2 * A_WIDTH
OFF_C = OFF_BQ + B_Q_RANK
C_QPAD = 2 * LANES
OFF_CV = OFF_C + 2 * C_QPAD
OFF_CG = OFF_CV + C_WIDTH
IN_COLS = OFF_CG + C_WIDTH
assert all(off % MXU_WIDTH == 0 for off in (OFF_AKV, OFF_BQ, OFF_C, OFF_CV, IN_COLS))


def _const_spec(shape, index_map):
    return pl.BlockSpec(shape, index_map, pipeline_mode=pl.Buffered(1))


def _silu(x):
    return x / (1.0 + jnp.exp(-x))


def _rms(x):
    return x * lax.rsqrt(jnp.mean(x * x, axis=-1, keepdims=True) + EPS)


def _mods_kernel(c_ref, w_ref, b_ref, o_ref):
    def split(a):
        hi = a.astype(BF16)
        return hi, (a - hi.astype(F32)).astype(BF16)

    c_hi, c_lo = split(_silu(c_ref[...]))
    w_hi, w_lo = split(w_ref[...])
    nb = c_hi.shape[0]
    both = jnp.dot(jnp.concatenate([c_hi, c_lo], axis=0), w_hi, preferred_element_type=F32)
    o_ref[...] = (both[:nb] + both[nb:] + jnp.dot(c_hi, w_lo, preferred_element_type=F32)) + b_ref[...]


def _mods(c, ada_w, ada_b):
    depth, d, n = ada_w.shape
    b = c.shape[0]
    tn = MODS_COL_TILE
    return pl.pallas_call(
        _mods_kernel,
        grid=(depth, n // tn),
        in_specs=[
            pl.BlockSpec((b, d), lambda l, j: (0, 0)),
            pl.BlockSpec((None, d, tn), lambda l, j: (l, 0, j)),
            pl.BlockSpec((None, 1, tn), lambda l, j: (l, 0, j)),
        ],
        out_specs=pl.BlockSpec((None, b, tn), lambda l, j: (l, 0, j)),
        out_shape=jax.ShapeDtypeStruct((depth, b, n), F32),
        compiler_params=pltpu.CompilerParams(vmem_limit_bytes=VMEM_LIMIT),
        name="adaln_mods",
    )(c, ada_w, ada_b.reshape(depth, 1, n))


def _rope_kernel(pos_ref, f_ref, s_ref, cos_ref, sin_ref):
    ang = pos_ref[...].astype(F32) * f_ref[...]
    cos_ref[...] = jnp.cos(ang)
    sin_ref[...] = jnp.sin(ang) * s_ref[...]


def _rope_tables(positions):
    t = positions.size
    half = C_KEY // 2
    assert B_ROPE == C_KEY and B_NOPE % C_KEY == 0
    inv_freq = (1.0 / (ROPE_BASE ** (np.arange(half, dtype=np.float32) / half))).astype(np.float32)
    ones = np.ones(half, np.float32)
    freqs = jnp.asarray(np.tile(np.concatenate([inv_freq, inv_freq]), LANES // C_KEY)[None, :])
    signs = jnp.asarray(np.tile(np.concatenate([-ones, ones]), LANES // C_KEY)[None, :])
    tr = ROPE_ROW_TILE
    out = jax.ShapeDtypeStruct((t, LANES), F32)
    row = pl.BlockSpec((tr, LANES), lambda i: (i, 0))
    return pl.pallas_call(
        _rope_kernel,
        grid=(t // tr,),
        in_specs=[
            pl.BlockSpec((tr, 1), lambda i: (i, 0)),
            pl.BlockSpec((1, LANES), lambda i: (0, 0)),
            pl.BlockSpec((1, LANES), lambda i: (0, 0)),
        ],
        out_specs=[row, row],
        out_shape=[out, out],
        name="rope_tables",
    )(positions.reshape(t, 1), freqs, signs)


def _t5_bucket(dist):
    max_exact = N_BUCKETS // 2
    safe = np.maximum(dist, 1).astype(np.float32)
    large = max_exact + (np.log(safe / max_exact) / np.log(MAX_DISTANCE / max_exact)
                         * (N_BUCKETS - max_exact)).astype(np.int32)
    large = np.minimum(large, N_BUCKETS - 1)
    return np.where(dist < max_exact, dist, large).astype(np.int32)


def _bias_kernel(bmap_ref, rb_ref, o_ref):
    h = pl.program_id(1)
    bm = bmap_ref[...]
    t = jnp.full(bm.shape, NEG, F32)
    for b in range(N_BUCKETS):
        t = jnp.where(bm == b, rb_ref[b, h] * LOG2E, t)
    o_ref[...] = t


def _bias_tables(rel_bias):
    qi = np.arange(WIN)[None, :]
    c = np.arange(2 * WIN)[:, None]
    j = qi - c + WIN
    maps = []
    for (w, d) in A_PATTERNS:
        assert w // d == WIN
        bucket = _t5_bucket(np.arange(WIN + 1, dtype=np.int32) * d)
        maps.append(np.where((j >= 0) & (j <= WIN), bucket[np.clip(j, 0, WIN)], -1).astype(np.int32))
    bmap = jnp.asarray(np.stack(maps))
    npat = len(A_PATTERNS)
    return pl.pallas_call(
        _bias_kernel,
        grid=(npat, A_HEADS),
        in_specs=[
            pl.BlockSpec((None, 2 * WIN, WIN), lambda p, h: (p, 0, 0)),
            pl.BlockSpec(memory_space=pltpu.SMEM),
        ],
        out_specs=pl.BlockSpec((None, 2 * WIN, WIN), lambda p, h: (p, 0, h)),
        out_shape=jax.ShapeDtypeStruct((npat, 2 * WIN, A_HEADS * WIN), F32),
        name="t5_bias_tables",
    )(bmap, rel_bias)


def _cast_plumbing(prep, cast_layer, steps, step_of):
    if cast_layer is None:
        return [], [], [], []
    ws, in_specs, out_specs = [prep[name] for name in FFN_CAST], [], []
    for w in ws:
        nslab = steps
        while w.shape[1] % (16 * nslab):
            assert nslab % 2 == 0
            nslab //= 2
        rep = steps // nslab
        block = (w.shape[1] // nslab, w.shape[2])
        in_specs.append(pl.BlockSpec((None,) + block, lambda *g, rep=rep: (cast_layer, step_of(*g) // rep, 0)))
        out_specs.append(pl.BlockSpec(block, lambda *g, rep=rep: (step_of(*g) // rep, 0)))
    return ws, in_specs, out_specs, [jax.ShapeDtypeStruct(w.shape[1:], BF16) for w in ws]


def _cast_slabs(cast_in, cast_out):
    for src, dst in zip(cast_in, cast_out):
        dst[...] = src[...].astype(BF16)


def _rot_half(x, x1_mask):
    w = x.shape[-1]
    half = B_ROPE // 2
    return jnp.where(x1_mask, pltpu.roll(x, w - half, 1), pltpu.roll(x, half, 1))


def _inproj_kernel(x_ref, mod_ref, g_ref, win_ref, qg_ref, kvg_ref, wuq_ref, wuk_ref, wuv_ref,
                   cc_ref, sc_ref, ks_ref, *refs, ncast):
    cast_in, refs = refs[:ncast], refs[ncast:]
    a_ref, bq_ref, bk_ref, bv_ref, cqk_ref, cv_ref, cg_ref = refs[:7]
    cast_out, w_ref = refs[7:7 + ncast], refs[7 + ncast]
    _cast_slabs(cast_in, cast_out)

    @pl.when((pl.program_id(0) == 0) & (pl.program_id(1) == 0))
    def _():
        d = win_ref.shape[1]
        bq0 = 3 * A_WIDTH
        bkv0 = bq0 + B_Q_RANK
        rope0 = bkv0 + B_KV_RANK
        c0 = rope0 + B_ROPE
        zeros = lambda n: jnp.zeros((n, d), F32)

        def place(dst, src, n):
            for k in range(0, n, LANES):
                w_ref[:, dst + k:dst + k + LANES] = win_ref[src + k:src + k + LANES, :].T.astype(BF16)

        place(OFF_AQ, 0, A_WIDTH)
        place(OFF_BKV, bkv0, B_KV_RANK)
        place(OFF_AKV, A_WIDTH, 2 * A_WIDTH)
        place(OFF_BQ, bq0, B_Q_RANK)
        q_rows = jnp.concatenate([win_ref[c0:c0 + C_QK, :], win_ref[rope0:c0, :],
                                  zeros(C_QPAD - C_QK - B_ROPE)], axis=0)
        w_ref[:, OFF_C:OFF_C + C_QPAD] = q_rows.T.astype(BF16)
        k_rows = jnp.concatenate([win_ref[c0 + C_QK:c0 + 2 * C_QK, :], zeros(C_QPAD - C_QK)], axis=0)
        w_ref[:, OFF_C + C_QPAD:OFF_CV] = k_rows.T.astype(BF16)
        place(OFF_CV, c0 + 2 * C_QK, 2 * C_WIDTH)

    tm = x_ref.shape[0] // INPROJ_SPLIT
    parts = [slice(k * tm, (k + 1) * tm) for k in range(INPROJ_SPLIT)]
    lane = lax.broadcasted_iota(jnp.int32, (tm, LANES), 1)
    b_x1 = (lane >= B_NOPE) & (lane < B_NOPE + B_ROPE // 2)
    b_rope = (lane >= B_NOPE) & (lane < B_NOPE + B_ROPE)
    c_x1 = (lane & (C_KEY // 2)) == 0

    def proj(h, lo, hi):
        return jnp.dot(h, w_ref[:, lo:hi], preferred_element_type=F32)

    hs = [(_rms(x_ref[r, :]) * g_ref[...] * (1.0 + mod_ref[1:2, :]) + mod_ref[0:1, :]).astype(BF16)
          for r in parts]
    lat = []
    for r, h in zip(parts, hs):
        aq_kv = proj(h, OFF_AQ, OFF_AKV)
        a_ref[r, 0:A_WIDTH] = aq_kv[:, :A_WIDTH] * (HEAD_DIM ** -0.5 * LOG2E)
        lat.append((proj(h, OFF_BQ, OFF_C), aq_kv[:, OFF_BKV:OFF_AKV], proj(h, OFF_C, OFF_CV)))
    normed = [((_rms(q_lat) * qg_ref[...]).astype(BF16), (_rms(kv_lat) * kvg_ref[...]).astype(BF16))
              for q_lat, kv_lat, _ in lat]
    for r, h in zip(parts, hs):
        a_ref[r, A_WIDTH:3 * A_WIDTH] = proj(h, OFF_AKV, OFF_BQ)
    ups = []
    for r, h, (qn, kvn) in zip(parts, hs, normed):
        ups.append((jnp.dot(qn, wuq_ref[...], preferred_element_type=F32),
                    jnp.dot(kvn, wuk_ref[...], preferred_element_type=F32)))
        bv_ref[r, :] = jnp.dot(kvn, wuv_ref[...], preferred_element_type=F32).astype(BF16)
        v_gate = proj(h, OFF_CV, IN_COLS)
        cv_ref[r, :] = v_gate[:, :C_WIDTH].astype(BF16)
        cg_ref[r, :] = _silu(v_gate[:, C_WIDTH:]).astype(BF16)
    b_scale = (B_NOPE + B_ROPE) ** -0.5 * LOG2E
    for r, (_, _, qk), (q, k_nope) in zip(parts, lat, ups):
        cos_c, sin_c = cc_ref[r, :], sc_ref[r, :]
        cos_b, sin_b = jnp.where(b_rope, cos_c, 1.0), jnp.where(b_rope, sin_c, 0.0)
        k_rope = jnp.where(lane >= B_NOPE, qk[:, LANES:2 * LANES], 0.0)
        k_pe = k_rope * cos_b + _rot_half(k_rope, b_x1) * sin_b
        for hh in range(B_HEADS):
            sl = slice(hh * LANES, (hh + 1) * LANES)
            qh = q[:, sl]
            bq_ref[r, sl] = ((qh * cos_b + _rot_half(qh, b_x1) * sin_b) * b_scale).astype(BF16)
            bk_ref[r, sl] = (k_nope[:, sl] + k_pe).astype(BF16)
        for t in range(2 * C_QPAD // LANES):
            sl = slice(t * LANES, (t + 1) * LANES)
            xt = qk[:, sl]
            cqk_ref[r, sl] = ((xt * cos_c + _rot_half(xt, c_x1) * sin_c) * ks_ref[:, sl]).astype(BF16)


def _inproj(x, mods, layer, prep, tables, batch, seq, tm, cast_layer):
    t, d = x.shape
    nt = seq // tm
    row = lambda w: pl.BlockSpec((tm, w), lambda b, i: (b * nt + i, 0))
    wl = lambda a: _const_spec((None,) + a.shape[1:], lambda b, i: (layer,) + (0,) * (a.ndim - 1))
    cos_c, sin_c = tables
    kscale = jnp.asarray(np.concatenate([np.ones(C_QPAD, np.float32),
                                         np.full(C_QPAD, C_KEY ** -0.5, np.float32)])[None, :])
    outs = [(3 * A_WIDTH, F32), (B_HEADS * LANES, BF16), (B_HEADS * LANES, BF16), (B_WIDTH, BF16),
            (2 * C_QPAD, BF16), (C_WIDTH, BF16), (C_WIDTH, BF16)]
    cast_ws, cast_in_specs, cast_out_specs, cast_shapes = _cast_plumbing(
        prep, cast_layer, batch * nt, lambda b, i: b * nt + i)
    return pl.pallas_call(
        functools.partial(_inproj_kernel, ncast=len(cast_ws)),
        grid=(batch, nt),
        in_specs=[
            row(d),
            pl.BlockSpec((None, None, 6, d), lambda b, i: (layer, b, 0, 0)),
            wl(prep["norm1_g"]), wl(prep["w_in"]), wl(prep["q_norm"]), wl(prep["kv_norm"]),
            wl(prep["w_uq"]), wl(prep["w_uk"]), wl(prep["w_uv"]),
            row(LANES), row(LANES),
            _const_spec((1, 2 * C_QPAD), lambda b, i: (0, 0)),
            *cast_in_specs,
        ],
        out_specs=[row(w) for w, _ in outs] + cast_out_specs,
        out_shape=[jax.ShapeDtypeStruct((t, w), dt) for w, dt in outs] + cast_shapes,
        scratch_shapes=[pltpu.VMEM((d, IN_COLS), BF16)],
        compiler_params=pltpu.CompilerParams(
            dimension_semantics=("arbitrary", "arbitrary"), vmem_limit_bytes=VMEM_LIMIT),
        name="inproj",
    )(x, mods, prep["norm1_g"], prep["w_in"], prep["q_norm"], prep["kv_norm"],
      prep["w_uq"], prep["w_uk"], prep["w_uv"], cos_c, sin_c, kscale, *cast_ws)


def _class_major_blocks(seq, level):
    n = seq // STEP
    perm = np.arange(seq)
    for _ in range(level):
        perm = perm.reshape(n, STEP).T.reshape(seq)
    blocks = perm.reshape(seq // WIN, WIN)
    stride = STEP ** level
    assert (blocks == blocks[:, :1] + stride * np.arange(WIN)).all()
    return [(int(b[0]), stride) for b in blocks]


def _dilated_kernel(q_ref, k_ref, v_ref, bias_ref, gain_ref, o_ref,
                    qf, kf, vf, qlo, qhi, kp, vt, *stage_refs, seq):
    npat = len(A_PATTERNS)
    nblk = seq // WIN
    s_sc, out_s, lse_s = (stage_refs[i * npat:(i + 1) * npat] for i in range(3))
    lane = lax.broadcasted_iota(jnp.int32, (WIN, LANES), 1)
    lo_lane = lane < HEAD_DIM
    lo_row = lax.broadcasted_iota(jnp.int32, (LANES, WIN), 0) < HEAD_DIM

    n = seq // STEP

    def regroup(p):
        for r in range(STEP):
            rows = slice(r * n, (r + 1) * n)
            src = rows if p == 0 else pl.ds(r, n, stride=STEP)
            level_below = (q_ref, k_ref, v_ref) if p <= 1 else (qf.at[p - 2], kf.at[p - 2], vf.at[p - 2])
            q, k, v = (f[src, :] for f in level_below)
            if 0 < p < npat - 1:
                qf[p - 1, rows, :], kf[p - 1, rows, :], vf[p - 1, rows, :] = q, k, v
            lo_q = lax.broadcasted_iota(jnp.int32, q.shape, 1) < HEAD_DIM
            qlo[p, rows, :] = jnp.where(lo_q, q, 0.0).astype(BF16)
            qhi[p, rows, :] = jnp.where(lo_q, 0.0, q).astype(BF16)
            vt[p, :, rows] = v.T.astype(BF16)
            kp[p, rows, :] = k.astype(BF16)

    def key_rows(p, blk):
        r0 = blk * WIN
        first = blk % (nblk // A_PATTERNS[p][1]) == 0
        return (r0 if first else r0 - WIN), r0 + WIN

    def logits(p):
        for blk in range(nblk):
            k0, k1 = key_rows(p, blk)
            r0 = blk * WIN
            q2 = jnp.concatenate([qlo[p, r0:r0 + WIN, :], qhi[p, r0:r0 + WIN, :]], axis=0)
            bias = bias_ref[p, 2 * WIN - (k1 - k0):, :]
            s_sc[p][blk, 0:k1 - k0, :] = lax.dot_general(
                kp[p, k0:k1, :], q2, (((1,), (1,)), ((), ())), preferred_element_type=F32) + bias

    def attend(p):
        for blk, (start, stride) in enumerate(_class_major_blocks(seq, p)):
            k0, k1 = key_rows(p, blk)
            s = s_sc[p][blk, 0:k1 - k0, :]
            m = jnp.max(s, axis=0, keepdims=True)
            e = jnp.exp2(s - m)
            l = jnp.sum(e, axis=0, keepdims=True)
            acc = jnp.dot(vt[p, :, k0:k1], e.astype(BF16), preferred_element_type=F32)
            out = acc * (1.0 / l)
            lse = m + jnp.log2(l)
            out = jnp.where(lo_row, out[:, :WIN], out[:, WIN:])
            lse = jnp.where(lo_row, lse[:, :WIN], lse[:, WIN:])
            dst = pl.ds(start, WIN, stride=stride) if stride > 1 else slice(start, start + WIN)
            out_s[p][dst, :] = out.T
            lse_s[p][dst, :] = lse.T

    for p in range(npat):
        regroup(p)
    for p in range(npat - 1, -1, -1):
        logits(p)
        attend(p)

    gain = gain_ref[...]
    for blk in range(nblk):
        sl = slice(blk * WIN, (blk + 1) * WIN)
        lses = [s[sl, :] for s in lse_s]
        mx = functools.reduce(jnp.maximum, lses)
        ws = [jnp.exp2(s - mx) for s in lses]
        o = sum(w * x[sl, :] for w, x in zip(ws, out_s)) / sum(ws)
        sq = o * o
        ms_lo = jnp.sum(jnp.where(lo_lane, sq, 0.0), axis=1, keepdims=True)
        ms_hi = jnp.sum(jnp.where(lo_lane, 0.0, sq), axis=1, keepdims=True)
        inv = lax.rsqrt(jnp.where(lo_lane, ms_lo, ms_hi) * (1.0 / HEAD_DIM) + EPS)
        o_ref[sl, :] = (o * inv * gain).astype(o_ref.dtype)


def _dilated(a_qkv, bias, gain_a, batch, seq):
    t = a_qkv.shape[0]
    pairs = A_WIDTH // LANES
    assert all(d == STEP ** p and seq % (d * WIN) == 0 for p, (_, d) in enumerate(A_PATTERNS))
    npat = bias.shape[0]
    col = lambda off: pl.BlockSpec((seq, LANES), lambda b, g: (b, off + g))
    scratch = ([pltpu.VMEM((npat - 2, seq, LANES), F32) for _ in range(3)]
               + [pltpu.VMEM((npat, seq, LANES), BF16) for _ in range(2)]
               + [pltpu.VMEM((npat, seq, LANES), BF16)]
               + [pltpu.VMEM((npat, LANES, seq), BF16)]
               + [pltpu.VMEM((seq // WIN, 2 * WIN, 2 * WIN), F32) for _ in range(npat)]
               + [pltpu.VMEM((seq, LANES), F32) for _ in range(2 * npat)])
    return pl.pallas_call(
        functools.partial(_dilated_kernel, seq=seq),
        grid=(batch, pairs),
        in_specs=[
            col(0), col(pairs), col(2 * pairs),
            pl.BlockSpec((npat, 2 * WIN, 2 * WIN), lambda b, g: (0, 0, g)),
            pl.BlockSpec((1, LANES), lambda b, g: (0, g)),
        ],
        out_specs=pl.BlockSpec((seq, LANES), lambda b, g: (b, g)),
        out_shape=jax.ShapeDtypeStruct((t, A_WIDTH), BF16),
        scratch_shapes=scratch,
        compiler_params=pltpu.CompilerParams(
            dimension_semantics=("arbitrary", "arbitrary"), vmem_limit_bytes=VMEM_LIMIT),
        name="dilated_attention",
    )(a_qkv, a_qkv, a_qkv, bias, gain_a)


def _mla_kernel(q_ref, k_ref, v_ref, gain_ref, o_ref, vt, s_sc, e_sc, *, seq, tq):
    th = tq // 2
    causal = lax.broadcasted_iota(jnp.int32, (th, th), 0) <= lax.broadcasted_iota(jnp.int32, (th, th), 1)
    gain = gain_ref[...]
    for c in range(0, seq, tq):
        vt[:, c:c + tq] = v_ref[c:c + tq, :].astype(F32).T.astype(BF16)

    units = [(i, hh) for i in range(seq // tq) for hh in range(2)]
    base = np.concatenate([[0], np.cumsum([i + 1 for i, _ in units])]).tolist()

    def logits(u):
        i, hh = units[u]
        hs = slice(hh * LANES, (hh + 1) * LANES)
        q = q_ref[i * tq:(i + 1) * tq, hs]
        qk = lambda keys, qs: lax.dot_general(k_ref[keys, hs], qs, (((1,), (1,)), ((), ())),
                                              preferred_element_type=F32)
        m = None
        for j in range(i):
            s = qk(slice(j * tq, (j + 1) * tq), q)
            s_sc[base[u] + j] = s
            bm = jnp.max(s, axis=0, keepdims=True)
            m = bm if m is None else jnp.maximum(m, bm)
        d0 = i * tq
        top = qk(slice(d0, d0 + th), q)
        top = jnp.concatenate([jnp.where(causal, top[:, :th], NEG), top[:, th:]], axis=1)
        bot = jnp.where(causal, qk(slice(d0 + th, d0 + tq), q_ref[d0 + th:d0 + tq, hs]), NEG)
        s_sc[base[u] + i, 0:th, :] = top
        s_sc[base[u] + i, th:tq, th:tq] = bot
        bm = jnp.max(top, axis=0, keepdims=True)
        bm = jnp.concatenate([bm[:, :th], jnp.maximum(bm[:, th:], jnp.max(bot, axis=0, keepdims=True))], axis=1)
        return bm if m is None else jnp.maximum(m, bm)

    def attend(u, m):
        i, hh = units[u]
        l = None
        for j in range(i):
            e = jnp.exp2(s_sc[base[u] + j] - m)
            ls = jnp.sum(e, axis=0, keepdims=True)
            l = ls if l is None else l + ls
            e_sc[(base[u] + j) * tq:(base[u] + j + 1) * tq, :] = e.astype(BF16)
        r0 = (base[u] + i) * tq
        e_top = jnp.exp2(s_sc[base[u] + i, 0:th, :] - m)
        e_bot = jnp.exp2(s_sc[base[u] + i, th:tq, th:tq] - m[:, th:])
        ls = jnp.sum(e_top, axis=0, keepdims=True)
        ls = jnp.concatenate([ls[:, :th], ls[:, th:] + jnp.sum(e_bot, axis=0, keepdims=True)], axis=1)
        l = ls if l is None else l + ls
        e_sc[r0:r0 + th, :] = e_top.astype(BF16)
        e_sc[r0 + th:r0 + tq, 0:th] = jnp.zeros((th, th), BF16)
        e_sc[r0 + th:r0 + tq, th:tq] = e_bot.astype(BF16)
        acc = jnp.dot(vt[hh * B_V:(hh + 1) * B_V, 0:(i + 1) * tq], e_sc[base[u] * tq:base[u + 1] * tq, :],
                      preferred_element_type=F32)
        o = acc / l
        return o * lax.rsqrt(jnp.mean(o * o, axis=0, keepdims=True) + EPS)

    m_next = logits(0)
    normed = []
    for u, (i, hh) in enumerate(units):
        m_cur = m_next
        if u + 1 < len(units):
            m_next = logits(u + 1)
        normed.append(attend(u, m_cur))
        if hh == 1:
            o_ref[i * tq:(i + 1) * tq, :] = (jnp.concatenate(normed, axis=0).T * gain).astype(o_ref.dtype)
            normed = []


def _mla(bq, bk, bv, gain_b, batch, seq):
    t = bq.shape[0]
    pairs = B_HEADS // 2
    tq = MLA_BLOCK
    assert seq % tq == 0
    nq = seq // tq
    nslots = 2 * (nq * (nq + 1) // 2)
    return pl.pallas_call(
        functools.partial(_mla_kernel, seq=seq, tq=tq),
        grid=(batch, pairs),
        in_specs=[
            pl.BlockSpec((seq, 2 * LANES), lambda b, g: (b, g)),
            pl.BlockSpec((seq, 2 * LANES), lambda b, g: (b, g)),
            pl.BlockSpec((seq, LANES), lambda b, g: (b, g)),
            pl.BlockSpec((1, LANES), lambda b, g: (0, g)),
        ],
        out_specs=pl.BlockSpec((seq, LANES), lambda b, g: (b, g)),
        out_shape=jax.ShapeDtypeStruct((t, B_WIDTH), BF16),
        scratch_shapes=[pltpu.VMEM((LANES, seq), BF16),
                        pltpu.VMEM((nslots, tq, tq), F32), pltpu.VMEM((nslots * tq, tq), BF16)],
        compiler_params=pltpu.CompilerParams(
            dimension_semantics=("arbitrary", "arbitrary"), vmem_limit_bytes=VMEM_LIMIT),
        name="latent_attention",
    )(bq, bk, bv, gain_b)


def _retention_consts():
    h = C_HEADS
    log_g = np.log(1.0 - 2.0 ** (-5.0 - np.arange(h))).astype(np.float32)
    i = np.arange(C_CHUNK, dtype=np.float32)
    rel = i[:, None] - i[None, :]
    decay = (np.exp(np.maximum(rel, 0.0)[None] * log_g[:, None, None]) * (rel >= 0)[None]).astype(np.float32)
    xi = np.exp((i + 1.0)[None, :] * log_g[:, None]).astype(np.float32)
    zeta = np.exp((C_CHUNK - 1.0 - i)[None, :] * log_g[:, None]).astype(np.float32)
    chunk_decay = np.exp(C_CHUNK * log_g).astype(np.float32)
    decay_all = np.concatenate(list(decay), axis=1)
    xi_mat = np.repeat(xi.T, C_VAL, axis=1)
    zeta_t = np.zeros((C_QPAD, C_CHUNK), np.float32)
    zeta_t[:C_QK] = np.repeat(zeta, C_KEY, axis=0)
    cd = np.repeat(chunk_decay, C_VAL)[None, :]
    bd = np.zeros((C_QPAD, C_WIDTH), np.float32)
    for hh in range(h):
        bd[hh * C_KEY:(hh + 1) * C_KEY, hh * C_VAL:(hh + 1) * C_VAL] = 1.0
    return tuple(jnp.asarray(a) for a in (decay_all, xi_mat, zeta_t, cd, bd))


def _retention_kernel(qk_ref, v_ref, g_ref, decay_ref, xi_ref, zeta_ref, cd_ref, bd_ref, gain_ref,
                      o_ref, *, seq):
    gain = gain_ref[...]
    state = jnp.zeros((C_QPAD, C_WIDTH), F32)
    kshape, vshape = (C_QPAD, C_HEADS * C_CHUNK), (C_HEADS * C_CHUNK, C_WIDTH)
    k_on = (lax.broadcasted_iota(jnp.int32, kshape, 0) // C_KEY) == (lax.broadcasted_iota(jnp.int32, kshape, 1) // C_CHUNK)
    v_on = (lax.broadcasted_iota(jnp.int32, vshape, 0) // C_CHUNK) == (lax.broadcasted_iota(jnp.int32, vshape, 1) // C_VAL)

    for n in range(seq // C_CHUNK):
        rows = slice(n * C_CHUNK, (n + 1) * C_CHUNK)
        q = qk_ref[rows, 0:C_QPAD]
        k_t = qk_ref[rows, C_QPAD:2 * C_QPAD].astype(F32).T
        v = v_ref[rows, :]
        k_bd = jnp.where(k_on, jnp.concatenate([k_t] * C_HEADS, axis=1), 0.0).astype(BF16)
        s = jnp.dot(q, k_bd, preferred_element_type=F32) * decay_ref[...]
        inter = jnp.dot(q, state.astype(BF16), preferred_element_type=F32) * xi_ref[...]
        upd = jnp.dot((k_t * zeta_ref[...]).astype(BF16), v, preferred_element_type=F32)
        state = state * cd_ref[...] + upd * bd_ref[...]
        v_bd = jnp.where(v_on, jnp.concatenate([v] * C_HEADS, axis=0), jnp.zeros((), BF16))
        o = jnp.dot(s.astype(BF16), v_bd, preferred_element_type=F32) + inter
        o_t = o.T
        normed = []
        for hh in range(C_HEADS):
            x = o_t[hh * C_VAL:(hh + 1) * C_VAL, :]
            dlt = x - jnp.mean(x, axis=0, keepdims=True)
            normed.append(dlt * lax.rsqrt(jnp.mean(dlt * dlt, axis=0, keepdims=True) + EPS))
        y = jnp.concatenate(normed, axis=0).T
        o_ref[rows, :] = (y * g_ref[rows, :].astype(F32) * gain).astype(o_ref.dtype)


def _retention(cqk, cv, cg, gain_c, consts, batch, seq):
    t = cqk.shape[0]
    tok = lambda width: pl.BlockSpec((seq, width), lambda b: (b, 0))
    full = lambda a: _const_spec(a.shape, lambda b: (0,) * a.ndim)
    return pl.pallas_call(
        functools.partial(_retention_kernel, seq=seq),
        grid=(batch,),
        in_specs=[tok(2 * C_QPAD), tok(C_WIDTH), tok(C_WIDTH)] + [full(a) for a in consts] + [full(gain_c)],
        out_specs=tok(C_WIDTH),
        out_shape=jax.ShapeDtypeStruct((t, C_WIDTH), BF16),
        compiler_params=pltpu.CompilerParams(
            dimension_semantics=("arbitrary",), vmem_limit_bytes=VMEM_LIMIT),
        name="retention",
    )(cqk, cv, cg, *consts, gain_c)


def _outffn_kernel(x_ref, ma_ref, mb_ref, mc_ref, mod_ref, wo_ref, g_ref, wg_ref, wu_ref, wd_ref, fg_ref,
                   *refs, final, fchunk, nsplit, ncast):
    cast_in, o_ref, cast_out = refs[:ncast], refs[ncast], refs[ncast + 1:2 * ncast + 1]
    mix_ref, h_ref, hid_ref = refs[2 * ncast + 1:]
    _cast_slabs(cast_in, cast_out)
    tm = x_ref.shape[0]
    parts = [slice(k * tm // nsplit, (k + 1) * tm // nsplit) for k in range(nsplit)]
    mix_ref[:, 0:A_WIDTH] = ma_ref[...]
    mix_ref[:, A_WIDTH:A_WIDTH + B_WIDTH] = mb_ref[...]
    mix_ref[:, A_WIDTH + B_WIDTH:] = mc_ref[...]
    for r in parts:
        att = jnp.dot(mix_ref[r, :], wo_ref[...], preferred_element_type=F32)
        o_ref[r, :] = x_ref[r, :] + mod_ref[2:3, :] * att
    for r in parts:
        h_ref[r, :] = (_rms(o_ref[r, :]) * g_ref[...] * (1.0 + mod_ref[4:5, :]) + mod_ref[3:4, :]).astype(BF16)
    hidden = wg_ref.shape[1]
    for r in parts:
        for j in range(hidden // fchunk):
            sl = slice(j * fchunk, (j + 1) * fchunk)
            gate = jnp.dot(h_ref[r, :], wg_ref[:, sl], preferred_element_type=F32)
            up = jnp.dot(h_ref[r, :], wu_ref[:, sl], preferred_element_type=F32)
            hid_ref[r, sl] = (_silu(gate) * up).astype(BF16)
    for r in parts:
        ffn = jnp.dot(hid_ref[r, :], wd_ref[...], preferred_element_type=F32)
        y = o_ref[r, :] + mod_ref[5:6, :] * ffn
        if final:
            y = _rms(y) * fg_ref[...]
        o_ref[r, :] = y


def _outffn(x, mix_a, mix_b, mix_c, mods, layer, prep, ffn_w, final_g, batch, seq, tm, final, cast_layer):
    t, d = x.shape
    nt = seq // tm
    w_out, w_gate, w_up, w_down = ffn_w
    hidden = w_gate.shape[1]
    whole = lambda a: _const_spec(a.shape, lambda b, i: (0,) * a.ndim)
    row = lambda w: pl.BlockSpec((tm, w), lambda b, i: (b * nt + i, 0))
    wl = lambda a: _const_spec((None,) + a.shape[1:], lambda b, i: (layer,) + (0,) * (a.ndim - 1))
    cast_ws, cast_in_specs, cast_out_specs, cast_shapes = _cast_plumbing(
        prep, cast_layer, batch * nt, lambda b, i: b * nt + i)
    return pl.pallas_call(
        functools.partial(_outffn_kernel, final=final, fchunk=MXU_WIDTH, nsplit=FFN_SPLIT, ncast=len(cast_ws)),
        grid=(batch, nt),
        in_specs=[
            row(d), row(A_WIDTH), row(B_WIDTH), row(C_WIDTH),
            pl.BlockSpec((None, None, 6, d), lambda b, i: (layer, b, 0, 0)),
            whole(w_out), wl(prep["norm2_g"]), whole(w_gate), whole(w_up), whole(w_down),
            _const_spec((1, d), lambda b, i: (0, 0)),
            *cast_in_specs,
        ],
        out_specs=[row(d)] + cast_out_specs,
        out_shape=[jax.ShapeDtypeStruct((t, d), F32)] + cast_shapes,
        scratch_shapes=[pltpu.VMEM((tm, d), BF16), pltpu.VMEM((tm, d), BF16), pltpu.VMEM((tm, hidden), BF16)],
        compiler_params=pltpu.CompilerParams(
            dimension_semantics=("arbitrary", "arbitrary"), vmem_limit_bytes=VMEM_LIMIT),
        name="outproj_ffn",
    )(x, mix_a, mix_b, mix_c, mods, w_out, prep["norm2_g"], w_gate, w_up, w_down, final_g, *cast_ws)


def _prepare(norm1_g, w_in, mla_q_norm, mla_kv_norm, mla_w_uq, mla_w_ukv, w_out, norm2_g,
             ffn_w_gate, ffn_w_up, ffn_w_down):
    depth, d, _ = w_in.shape
    b_lat0 = 3 * A_WIDTH
    rope0 = b_lat0 + B_Q_RANK + B_KV_RANK
    c0 = rope0 + B_ROPE
    assert C_QK + B_ROPE <= C_QPAD and C_QK % LANES == B_NOPE
    assert c0 + 2 * C_QK + 2 * C_WIDTH == w_in.shape[2]
    uq = mla_w_uq.reshape(depth, B_Q_RANK, B_HEADS, B_NOPE + B_ROPE)
    uq = jnp.pad(uq, ((0, 0), (0, 0), (0, 0), (0, LANES - B_NOPE - B_ROPE)))
    ukv = mla_w_ukv.reshape(depth, B_KV_RANK, B_HEADS, B_NOPE + B_V)
    uk = jnp.pad(ukv[..., :B_NOPE], ((0, 0), (0, 0), (0, 0), (0, LANES - B_NOPE)))
    uv = ukv[..., B_NOPE:]
    return {
        "norm1_g": norm1_g[:, None, :],
        "w_in": jnp.swapaxes(w_in, 1, 2),
        "q_norm": mla_q_norm[:, None, :],
        "kv_norm": mla_kv_norm[:, None, :],
        "w_uq": uq.reshape(depth, B_Q_RANK, B_HEADS * LANES).astype(BF16),
        "w_uk": uk.reshape(depth, B_KV_RANK, B_HEADS * LANES).astype(BF16),
        "w_uv": uv.reshape(depth, B_KV_RANK, B_WIDTH).astype(BF16),
        "w_out": w_out,
        "norm2_g": norm2_g[:, None, :],
        "w_gate": ffn_w_gate,
        "w_up": ffn_w_up,
        "w_down": ffn_w_down,
    }


def kernel(x, c, positions, rel_bias, ada_w, ada_b, norm1_g, w_in, mla_q_norm, mla_kv_norm, mla_w_uq,
           mla_w_ukv, mix_gain, w_out, norm2_g, ffn_w_gate, ffn_w_up, ffn_w_down, final_norm):
    batch, seq, d = x.shape
    depth = w_in.shape[0]
    tm = ROW_TILE
    assert seq % tm == 0 and seq % C_CHUNK == 0

    prep = _prepare(norm1_g, w_in, mla_q_norm, mla_kv_norm, mla_w_uq, mla_w_ukv, w_out, norm2_g,
                    ffn_w_gate, ffn_w_up, ffn_w_down)
    mods = _mods(c, ada_w, ada_b).reshape(depth, batch, 6, d)
    tables = _rope_tables(positions)
    bias = _bias_tables(rel_bias)
    ret_consts = _retention_consts()
    final_g = final_norm[None, :]

    xf = x.reshape(batch * seq, d)
    for l in range(depth):
        a_qkv, bq, bk, bv, cqk, cv, cg, *cast0 = _inproj(xf, mods, l, prep, tables, batch, seq, tm,
                                                         cast_layer=0 if l == 0 else None)
        ffn_w = cast0 if l == 0 else ffn_w_next
        gain = mix_gain[l][None, :]
        mix_a = _dilated(a_qkv, bias, gain[:, :A_WIDTH], batch, seq)
        mix_b = _mla(bq, bk, bv, gain[:, A_WIDTH:A_WIDTH + B_WIDTH], batch, seq)
        mix_c = _retention(cqk, cv, cg, gain[:, A_WIDTH + B_WIDTH:], ret_consts, batch, seq)
        xf, *ffn_w_next = _outffn(xf, mix_a, mix_b, mix_c, mods, l, prep, ffn_w, final_g, batch, seq, tm,
                                  final=(l == depth - 1), cast_layer=l + 1 if l + 1 < depth else None)
    return xf.reshape(batch, seq, d)
```

```python
import functools

import numpy as np
import jax
import jax.numpy as jnp
from jax import lax
from jax.experimental import pallas as pl
from jax.experimental.pallas import tpu as pltpu

F32 = jnp.float32
BF16 = jnp.bfloat16

HEAD_DIM = 64
A_HEADS = 6
A_PATTERNS = ((128, 1), (512, 4), (2048, 16))
A_WIDTH = A_HEADS * HEAD_DIM
B_HEADS = 4
B_NOPE = 64
B_ROPE = 32
B_V = 64
B_Q_RANK = 256
B_KV_RANK = 128
B_WIDTH = B_HEADS * B_V
C_HEADS = 6
C_KEY = 32
C_VAL = 64
C_WIDTH = C_HEADS * C_VAL
C_QK = C_HEADS * C_KEY
C_CHUNK = 128
N_BUCKETS = 32
MAX_DISTANCE = 2048
ROPE_BASE = 10000.0
EPS = 1e-6

LANES = 128
WIN = 128
STEP = 4
NEG = -1e30
LOG2E = 1.4426950408889634
FFN_CAST = ("w_out", "w_gate", "w_up", "w_down")
VMEM_LIMIT = 56 * 1024 * 1024
MXU_WIDTH = 256
ROW_TILE = 512
MLA_BLOCK = 512
MODS_COL_TILE = 1536
ROPE_ROW_TILE = 2048
FFN_SPLIT = 2
INPROJ_SPLIT = 2

OFF_AQ = 0
OFF_BKV = OFF_AQ + A_WIDTH
OFF_AKV = OFF_BKV + B_KV_RANK
OFF_BQ = OFF_AKV + 2 * A_WIDTH
OFF_C = OFF_BQ + B_Q_RANK
C_QPAD = 2 * LANES
OFF_CV = OFF_C + 2 * C_QPAD
OFF_CG = OFF_CV + C_WIDTH
IN_COLS = OFF_CG + C_WIDTH
assert all(off % MXU_WIDTH == 0 for off in (OFF_AKV, OFF_BQ, OFF_C, OFF_CV, IN_COLS))


def _const_spec(shape, index_map):
    return pl.BlockSpec(shape, index_map, pipeline_mode=pl.Buffered(1))


def _silu(x):
    return x / (1.0 + jnp.exp(-x))


def _rms(x):
    return x * lax.rsqrt(jnp.mean(x * x, axis=-1, keepdims=True) + EPS)


def _mods_kernel(c_ref, w_ref, b_ref, o_ref):
    def split(a):
        hi = a.astype(BF16)
        return hi, (a - hi.astype(F32)).astype(BF16)

    c_hi, c_lo = split(_silu(c_ref[...]))
    w_hi, w_lo = split(w_ref[...])
    nb = c_hi.shape[0]
    both = jnp.dot(jnp.concatenate([c_hi, c_lo], axis=0), w_hi, preferred_element_type=F32)
    o_ref[...] = (both[:nb] + both[nb:] + jnp.dot(c_hi, w_lo, preferred_element_type=F32)) + b_ref[...]


def _mods(c, ada_w, ada_b):
    depth, d, n = ada_w.shape
    b = c.shape[0]
    tn = MODS_COL_TILE
    return pl.pallas_call(
        _mods_kernel,
        grid=(depth, n // tn),
        in_specs=[
            pl.BlockSpec((b, d), lambda l, j: (0, 0)),
            pl.BlockSpec((None, d, tn), lambda l, j: (l, 0, j)),
            pl.BlockSpec((None, 1, tn), lambda l, j: (l, 0, j)),
        ],
        out_specs=pl.BlockSpec((None, b, tn), lambda l, j: (l, 0, j)),
        out_shape=jax.ShapeDtypeStruct((depth, b, n), F32),
        compiler_params=pltpu.CompilerParams(vmem_limit_bytes=VMEM_LIMIT),
        name="adaln_mods",
    )(c, ada_w, ada_b.reshape(depth, 1, n))


def _rope_kernel(pos_ref, f_ref, s_ref, cos_ref, sin_ref):
    ang = pos_ref[...].astype(F32) * f_ref[...]
    cos_ref[...] = jnp.cos(ang)
    sin_ref[...] = jnp.sin(ang) * s_ref[...]


def _rope_tables(positions):
    t = positions.size
    half = C_KEY // 2
    assert B_ROPE == C_KEY and B_NOPE % C_KEY == 0
    inv_freq = (1.0 / (ROPE_BASE ** (np.arange(half, dtype=np.float32) / half))).astype(np.float32)
    ones = np.ones(half, np.float32)
    freqs = jnp.asarray(np.tile(np.concatenate([inv_freq, inv_freq]), LANES // C_KEY)[None, :])
    signs = jnp.asarray(np.tile(np.concatenate([-ones, ones]), LANES // C_KEY)[None, :])
    tr = ROPE_ROW_TILE
    out = jax.ShapeDtypeStruct((t, LANES), F32)
    row = pl.BlockSpec((tr, LANES), lambda i: (i, 0))
    return pl.pallas_call(
        _rope_kernel,
        grid=(t // tr,),
        in_specs=[
            pl.BlockSpec((tr, 1), lambda i: (i, 0)),
            pl.BlockSpec((1, LANES), lambda i: (0, 0)),
            pl.BlockSpec((1, LANES), lambda i: (0, 0)),
        ],
        out_specs=[row, row],
        out_shape=[out, out],
        name="rope_tables",
    )(positions.reshape(t, 1), freqs, signs)


def _t5_bucket(dist):
    max_exact = N_BUCKETS // 2
    safe = np.maximum(dist, 1).astype(np.float32)
    large = max_exact + (np.log(safe / max_exact) / np.log(MAX_DISTANCE / max_exact)
                         * (N_BUCKETS - max_exact)).astype(np.int32)
    large = np.minimum(large, N_BUCKETS - 1)
    return np.where(dist < max_exact, dist, large).astype(np.int32)


def _bias_kernel(bmap_ref, rb_ref, o_ref):
    bm = bmap_ref[...]
    hits = [bm == b for b in range(N_BUCKETS)]
    for h in range(A_HEADS):
        t = jnp.full(bm.shape, NEG, F32)
        for b in range(N_BUCKETS):
            t = jnp.where(hits[b], rb_ref[b, h] * LOG2E, t)
        o_ref[:, h * WIN:(h + 1) * WIN] = t


def _bias_tables(rel_bias):
    qi = np.arange(WIN)[None, :]
    c = np.arange(2 * WIN)[:, None]
    j = qi - c + WIN
    maps = []
    for (w, d) in A_PATTERNS:
        assert w // d == WIN
        bucket = _t5_bucket(np.arange(WIN + 1, dtype=np.int32) * d)
        maps.append(np.where((j >= 0) & (j <= WIN), bucket[np.clip(j, 0, WIN)], -1).astype(np.int32))
    bmap = jnp.asarray(np.stack(maps))
    npat = len(A_PATTERNS)
    return pl.pallas_call(
        _bias_kernel,
        grid=(npat,),
        in_specs=[
            pl.BlockSpec((None, 2 * WIN, WIN), lambda p: (p, 0, 0)),
            pl.BlockSpec(memory_space=pltpu.SMEM),
        ],
        out_specs=pl.BlockSpec((None, 2 * WIN, A_HEADS * WIN), lambda p: (p, 0, 0)),
        out_shape=jax.ShapeDtypeStruct((npat, 2 * WIN, A_HEADS * WIN), F32),
        name="t5_bias_tables",
    )(bmap, rel_bias)


def _cast_plumbing(prep, cast_layer, steps, step_of):
    if cast_layer is None:
        return [], [], [], []
    ws, in_specs, out_specs = [prep[name] for name in FFN_CAST], [], []
    for w in ws:
        nslab = steps
        while w.shape[1] % (16 * nslab):
            assert nslab % 2 == 0
            nslab //= 2
        rep = steps // nslab
        block = (w.shape[1] // nslab, w.shape[2])
        in_specs.append(pl.BlockSpec((None,) + block, lambda *g, rep=rep: (cast_layer, step_of(*g) // rep, 0)))
        out_specs.append(pl.BlockSpec(block, lambda *g, rep=rep: (step_of(*g) // rep, 0)))
    return ws, in_specs, out_specs, [jax.ShapeDtypeStruct(w.shape[1:], BF16) for w in ws]


def _cast_slabs(cast_in, cast_out):
    for src, dst in zip(cast_in, cast_out):
        dst[...] = src[...].astype(BF16)


def _rot_half(x, x1_mask):
    w = x.shape[-1]
    half = B_ROPE // 2
    return jnp.where(x1_mask, pltpu.roll(x, w - half, 1), pltpu.roll(x, half, 1))


def _inproj_kernel(x_ref, mod_ref, g_ref, win_ref, qg_ref, kvg_ref, wuq_ref, wuk_ref, wuv_ref,
                   cc_ref, sc_ref, ks_ref, *refs, ncast):
    cast_in, refs = refs[:ncast], refs[ncast:]
    a_ref, bq_ref, bk_ref, bv_ref, cqk_ref, cv_ref, cg_ref = refs[:7]
    cast_out, w_ref = refs[7:7 + ncast], refs[7 + ncast]
    _cast_slabs(cast_in, cast_out)

    @pl.when((pl.program_id(0) == 0) & (pl.program_id(1) == 0))
    def _():
        d = win_ref.shape[1]
        bq0 = 3 * A_WIDTH
        bkv0 = bq0 + B_Q_RANK
        rope0 = bkv0 + B_KV_RANK
        c0 = rope0 + B_ROPE
        zeros = lambda n: jnp.zeros((n, d), F32)

        def place(dst, src, n):
            for k in range(0, n, LANES):
                w_ref[:, dst + k:dst + k + LANES] = win_ref[src + k:src + k + LANES, :].T.astype(BF16)

        place(OFF_AQ, 0, A_WIDTH)
        place(OFF_BKV, bkv0, B_KV_RANK)
        place(OFF_AKV, A_WIDTH, 2 * A_WIDTH)
        place(OFF_BQ, bq0, B_Q_RANK)
        q_rows = jnp.concatenate([win_ref[c0:c0 + C_QK, :], win_ref[rope0:c0, :],
                                  zeros(C_QPAD - C_QK - B_ROPE)], axis=0)
        w_ref[:, OFF_C:OFF_C + C_QPAD] = q_rows.T.astype(BF16)
        k_rows = jnp.concatenate([win_ref[c0 + C_QK:c0 + 2 * C_QK, :], zeros(C_QPAD - C_QK)], axis=0)
        w_ref[:, OFF_C + C_QPAD:OFF_CV] = k_rows.T.astype(BF16)
        place(OFF_CV, c0 + 2 * C_QK, 2 * C_WIDTH)

    tm = x_ref.shape[0] // INPROJ_SPLIT
    parts = [slice(k * tm, (k + 1) * tm) for k in range(INPROJ_SPLIT)]
    lane = lax.broadcasted_iota(jnp.int32, (tm, LANES), 1)
    b_x1 = (lane >= B_NOPE) & (lane < B_NOPE + B_ROPE // 2)
    b_rope = (lane >= B_NOPE) & (lane < B_NOPE + B_ROPE)
    c_x1 = (lane & (C_KEY // 2)) == 0

    def proj(h, lo, hi):
        return jnp.dot(h, w_ref[:, lo:hi], preferred_element_type=F32)

    hs = [(_rms(x_ref[r, :]) * g_ref[...] * (1.0 + mod_ref[1:2, :]) + mod_ref[0:1, :]).astype(BF16)
          for r in parts]
    lat = []
    for r, h in zip(parts, hs):
        aq_kv = proj(h, OFF_AQ, OFF_AKV)
        a_ref[r, 0:A_WIDTH] = aq_kv[:, :A_WIDTH] * (HEAD_DIM ** -0.5 * LOG2E)
        lat.append((proj(h, OFF_BQ, OFF_C), aq_kv[:, OFF_BKV:OFF_AKV], proj(h, OFF_C, OFF_CV)))
    normed = [((_rms(q_lat) * qg_ref[...]).astype(BF16), (_rms(kv_lat) * kvg_ref[...]).astype(BF16))
              for q_lat, kv_lat, _ in lat]
    for r, h in zip(parts, hs):
        a_ref[r, A_WIDTH:3 * A_WIDTH] = proj(h, OFF_AKV, OFF_BQ)
    ups = []
    for r, h, (qn, kvn) in zip(parts, hs, normed):
        ups.append((jnp.dot(qn, wuq_ref[...], preferred_element_type=F32),
                    jnp.dot(kvn, wuk_ref[...], preferred_element_type=F32)))
        bv_ref[r, :] = jnp.dot(kvn, wuv_ref[...], preferred_element_type=F32).astype(BF16)
        v_gate = proj(h, OFF_CV, IN_COLS)
        cv_ref[r, :] = v_gate[:, :C_WIDTH].astype(BF16)
        cg_ref[r, :] = _silu(v_gate[:, C_WIDTH:]).astype(BF16)
    b_scale = (B_NOPE + B_ROPE) ** -0.5 * LOG2E
    for r, (_, _, qk), (q, k_nope) in zip(parts, lat, ups):
        cos_c, sin_c = cc_ref[r, :], sc_ref[r, :]
        cos_b, sin_b = jnp.where(b_rope, cos_c, 1.0), jnp.where(b_rope, sin_c, 0.0)
        k_rope = jnp.where(lane >= B_NOPE, qk[:, LANES:2 * LANES], 0.0)
        k_pe = k_rope * cos_b + _rot_half(k_rope, b_x1) * sin_b
        for hh in range(B_HEADS):
            sl = slice(hh * LANES, (hh + 1) * LANES)
            qh = q[:, sl]
            bq_ref[r, sl] = ((qh * cos_b + _rot_half(qh, b_x1) * sin_b) * b_scale).astype(BF16)
            bk_ref[r, sl] = (k_nope[:, sl] + k_pe).astype(BF16)
        for t in range(2 * C_QPAD // LANES):
            sl = slice(t * LANES, (t + 1) * LANES)
            xt = qk[:, sl]
            cqk_ref[r, sl] = ((xt * cos_c + _rot_half(xt, c_x1) * sin_c) * ks_ref[:, sl]).astype(BF16)


def _inproj(x, mods, layer, prep, tables, batch, seq, tm, cast_layer):
    t, d = x.shape
    nt = seq // tm
    row = lambda w: pl.BlockSpec((tm, w), lambda b, i: (b * nt + i, 0))
    wl = lambda a: _const_spec((None,) + a.shape[1:], lambda b, i: (layer,) + (0,) * (a.ndim - 1))
    cos_c, sin_c = tables
    kscale = jnp.asarray(np.concatenate([np.ones(C_QPAD, np.float32),
                                         np.full(C_QPAD, C_KEY ** -0.5, np.float32)])[None, :])
    outs = [(3 * A_WIDTH, F32), (B_HEADS * LANES, BF16), (B_HEADS * LANES, BF16), (B_WIDTH, BF16),
            (2 * C_QPAD, BF16), (C_WIDTH, BF16), (C_WIDTH, BF16)]
    cast_ws, cast_in_specs, cast_out_specs, cast_shapes = _cast_plumbing(
        prep, cast_layer, batch * nt, lambda b, i: b * nt + i)
    return pl.pallas_call(
        functools.partial(_inproj_kernel, ncast=len(cast_ws)),
        grid=(batch, nt),
        in_specs=[
            row(d),
            pl.BlockSpec((None, None, 6, d), lambda b, i: (layer, b, 0, 0)),
            wl(prep["norm1_g"]), wl(prep["w_in"]), wl(prep["q_norm"]), wl(prep["kv_norm"]),
            wl(prep["w_uq"]), wl(prep["w_uk"]), wl(prep["w_uv"]),
            row(LANES), row(LANES),
            _const_spec((1, 2 * C_QPAD), lambda b, i: (0, 0)),
            *cast_in_specs,
        ],
        out_specs=[row(w) for w, _ in outs] + cast_out_specs,
        out_shape=[jax.ShapeDtypeStruct((t, w), dt) for w, dt in outs] + cast_shapes,
        scratch_shapes=[pltpu.VMEM((d, IN_COLS), BF16)],
        compiler_params=pltpu.CompilerParams(
            dimension_semantics=("arbitrary", "arbitrary"), vmem_limit_bytes=VMEM_LIMIT),
        name="inproj",
    )(x, mods, prep["norm1_g"], prep["w_in"], prep["q_norm"], prep["kv_norm"],
      prep["w_uq"], prep["w_uk"], prep["w_uv"], cos_c, sin_c, kscale, *cast_ws)


def _class_major_blocks(seq, level):
    n = seq // STEP
    perm = np.arange(seq)
    for _ in range(level):
        perm = perm.reshape(n, STEP).T.reshape(seq)
    blocks = perm.reshape(seq // WIN, WIN)
    stride = STEP ** level
    assert (blocks == blocks[:, :1] + stride * np.arange(WIN)).all()
    return [(int(b[0]), stride) for b in blocks]


def _dilated_kernel(q_ref, k_ref, v_ref, bias_ref, gain_ref, o_ref,
                    qf, kf, vf, qlo, qhi, kp, vt, *stage_refs, seq):
    npat = len(A_PATTERNS)
    nblk = seq // WIN
    s_sc, out_s, lse_s = (stage_refs[i * npat:(i + 1) * npat] for i in range(3))
    lane = lax.broadcasted_iota(jnp.int32, (WIN, LANES), 1)
    lo_lane = lane < HEAD_DIM
    lo_row = lax.broadcasted_iota(jnp.int32, (LANES, WIN), 0) < HEAD_DIM

    n = seq // STEP

    def regroup(p):
        for r in range(STEP):
            rows = slice(r * n, (r + 1) * n)
            src = rows if p == 0 else pl.ds(r, n, stride=STEP)
            level_below = (q_ref, k_ref, v_ref) if p <= 1 else (qf.at[p - 2], kf.at[p - 2], vf.at[p - 2])
            q, k, v = (f[src, :] for f in level_below)
            if 0 < p < npat - 1:
                qf[p - 1, rows, :], kf[p - 1, rows, :], vf[p - 1, rows, :] = q, k, v
            lo_q = lax.broadcasted_iota(jnp.int32, q.shape, 1) < HEAD_DIM
            qlo[p, rows, :] = jnp.where(lo_q, q, 0.0).astype(BF16)
            qhi[p, rows, :] = jnp.where(lo_q, 0.0, q).astype(BF16)
            vt[p, :, rows] = v.T.astype(BF16)
            kp[p, rows, :] = k.astype(BF16)

    def key_rows(p, blk):
        r0 = blk * WIN
        first = blk % (nblk // A_PATTERNS[p][1]) == 0
        return (r0 if first else r0 - WIN), r0 + WIN

    def logits(p):
        for blk in range(nblk):
            k0, k1 = key_rows(p, blk)
            r0 = blk * WIN
            q2 = jnp.concatenate([qlo[p, r0:r0 + WIN, :], qhi[p, r0:r0 + WIN, :]], axis=0)
            bias = bias_ref[p, 2 * WIN - (k1 - k0):, :]
            s_sc[p][blk, 0:k1 - k0, :] = lax.dot_general(
                kp[p, k0:k1, :], q2, (((1,), (1,)), ((), ())), preferred_element_type=F32) + bias

    def attend(p):
        for blk, (start, stride) in enumerate(_class_major_blocks(seq, p)):
            k0, k1 = key_rows(p, blk)
            s = s_sc[p][blk, 0:k1 - k0, :]
            m = jnp.max(s, axis=0, keepdims=True)
            e = jnp.exp2(s - m)
            l = jnp.sum(e, axis=0, keepdims=True)
            acc = jnp.dot(vt[p, :, k0:k1], e.astype(BF16), preferred_element_type=F32)
            out = acc * (1.0 / l)
            lse = m + jnp.log2(l)
            out = jnp.where(lo_row, out[:, :WIN], out[:, WIN:])
            lse = jnp.where(lo_row, lse[:, :WIN], lse[:, WIN:])
            dst = pl.ds(start, WIN, stride=stride) if stride > 1 else slice(start, start + WIN)
            out_s[p][dst, :] = out.T
            lse_s[p][dst, :] = lse.T

    for p in range(npat):
        regroup(p)
    for p in range(npat - 1, -1, -1):
        logits(p)
        attend(p)

    gain = gain_ref[...]
    for blk in range(nblk):
        sl = slice(blk * WIN, (blk + 1) * WIN)
        lses = [s[sl, :] for s in lse_s]
        mx = functools.reduce(jnp.maximum, lses)
        ws = [jnp.exp2(s - mx) for s in lses]
        o = sum(w * x[sl, :] for w, x in zip(ws, out_s)) / sum(ws)
        sq = o * o
        ms_lo = jnp.sum(jnp.where(lo_lane, sq, 0.0), axis=1, keepdims=True)
        ms_hi = jnp.sum(jnp.where(lo_lane, 0.0, sq), axis=1, keepdims=True)
        inv = lax.rsqrt(jnp.where(lo_lane, ms_lo, ms_hi) * (1.0 / HEAD_DIM) + EPS)
        o_ref[sl, :] = (o * inv * gain).astype(o_ref.dtype)


def _dilated(a_qkv, bias, gain_a, batch, seq):
    t = a_qkv.shape[0]
    pairs = A_WIDTH // LANES
    assert all(d == STEP ** p and seq % (d * WIN) == 0 for p, (_, d) in enumerate(A_PATTERNS))
    npat = bias.shape[0]
    col = lambda off: pl.BlockSpec((seq, LANES), lambda b, g: (b, off + g))
    scratch = ([pltpu.VMEM((npat - 2, seq, LANES), F32) for _ in range(3)]
               + [pltpu.VMEM((npat, seq, LANES), BF16) for _ in range(2)]
               + [pltpu.VMEM((npat, seq, LANES), BF16)]
               + [pltpu.VMEM((npat, LANES, seq), BF16)]
               + [pltpu.VMEM((seq // WIN, 2 * WIN, 2 * WIN), F32) for _ in range(npat)]
               + [pltpu.VMEM((seq, LANES), F32) for _ in range(2 * npat)])
    return pl.pallas_call(
        functools.partial(_dilated_kernel, seq=seq),
        grid=(batch, pairs),
        in_specs=[
            col(0), col(pairs), col(2 * pairs),
            pl.BlockSpec((npat, 2 * WIN, 2 * WIN), lambda b, g: (0, 0, g)),
            pl.BlockSpec((1, LANES), lambda b, g: (0, g)),
        ],
        out_specs=pl.BlockSpec((seq, LANES), lambda b, g: (b, g)),
        out_shape=jax.ShapeDtypeStruct((t, A_WIDTH), BF16),
        scratch_shapes=scratch,
        compiler_params=pltpu.CompilerParams(
            dimension_semantics=("arbitrary", "arbitrary"), vmem_limit_bytes=VMEM_LIMIT),
        name="dilated_attention",
    )(a_qkv, a_qkv, a_qkv, bias, gain_a)


def _mla_kernel(q_ref, k_ref, v_ref, gain_ref, o_ref, vt, s_sc, e_sc, *, seq, tq):
    th = tq // 2
    causal = lax.broadcasted_iota(jnp.int32, (th, th), 0) <= lax.broadcasted_iota(jnp.int32, (th, th), 1)
    gain = gain_ref[...]
    for c in range(0, seq, tq):
        vt[:, c:c + tq] = v_ref[c:c + tq, :].astype(F32).T.astype(BF16)

    units = [(i, hh) for i in range(seq // tq) for hh in range(2)]
    base = np.concatenate([[0], np.cumsum([i + 1 for i, _ in units])]).tolist()

    def logits(u):
        i, hh = units[u]
        hs = slice(hh * LANES, (hh + 1) * LANES)
        q = q_ref[i * tq:(i + 1) * tq, hs]
        qk = lambda keys, qs: lax.dot_general(k_ref[keys, hs], qs, (((1,), (1,)), ((), ())),
                                              preferred_element_type=F32)
        m = None
        for j in range(i):
            s = qk(slice(j * tq, (j + 1) * tq), q)
            s_sc[base[u] + j] = s
            bm = jnp.max(s, axis=0, keepdims=True)
            m = bm if m is None else jnp.maximum(m, bm)
        d0 = i * tq
        top = qk(slice(d0, d0 + th), q)
        top = jnp.concatenate([jnp.where(causal, top[:, :th], NEG), top[:, th:]], axis=1)
        bot = jnp.where(causal, qk(slice(d0 + th, d0 + tq), q_ref[d0 + th:d0 + tq, hs]), NEG)
        s_sc[base[u] + i, 0:th, :] = top
        s_sc[base[u] + i, th:tq, th:tq] = bot
        bm = jnp.max(top, axis=0, keepdims=True)
        bm = jnp.concatenate([bm[:, :th], jnp.maximum(bm[:, th:], jnp.max(bot, axis=0, keepdims=True))], axis=1)
        return bm if m is None else jnp.maximum(m, bm)

    def attend(u, m):
        i, hh = units[u]
        l = None
        for j in range(i):
            e = jnp.exp2(s_sc[base[u] + j] - m)
            ls = jnp.sum(e, axis=0, keepdims=True)
            l = ls if l is None else l + ls
            e_sc[(base[u] + j) * tq:(base[u] + j + 1) * tq, :] = e.astype(BF16)
        r0 = (base[u] + i) * tq
        e_top = jnp.exp2(s_sc[base[u] + i, 0:th, :] - m)
        e_bot = jnp.exp2(s_sc[base[u] + i, th:tq, th:tq] - m[:, th:])
        ls = jnp.sum(e_top, axis=0, keepdims=True)
        ls = jnp.concatenate([ls[:, :th], ls[:, th:] + jnp.sum(e_bot, axis=0, keepdims=True)], axis=1)
        l = ls if l is None else l + ls
        e_sc[r0:r0 + th, :] = e_top.astype(BF16)
        e_sc[r0 + th:r0 + tq, 0:th] = jnp.zeros((th, th), BF16)
        e_sc[r0 + th:r0 + tq, th:tq] = e_bot.astype(BF16)
        acc = jnp.dot(vt[hh * B_V:(hh + 1) * B_V, 0:(i + 1) * tq], e_sc[base[u] * tq:base[u + 1] * tq, :],
                      preferred_element_type=F32)
        o = acc / l
        return o * lax.rsqrt(jnp.mean(o * o, axis=0, keepdims=True) + EPS)

    m_next = logits(0)
    normed = []
    for u, (i, hh) in enumerate(units):
        m_cur = m_next
        if u + 1 < len(units):
            m_next = logits(u + 1)
        normed.append(attend(u, m_cur))
        if hh == 1:
            o_ref[i * tq:(i + 1) * tq, :] = (jnp.concatenate(normed, axis=0).T * gain).astype(o_ref.dtype)
            normed = []


def _mla(bq, bk, bv, gain_b, batch, seq):
    t = bq.shape[0]
    pairs = B_HEADS // 2
    tq = MLA_BLOCK
    assert seq % tq == 0
    nq = seq // tq
    nslots = 2 * (nq * (nq + 1) // 2)
    return pl.pallas_call(
        functools.partial(_mla_kernel, seq=seq, tq=tq),
        grid=(batch, pairs),
        in_specs=[
            pl.BlockSpec((seq, 2 * LANES), lambda b, g: (b, g)),
            pl.BlockSpec((seq, 2 * LANES), lambda b, g: (b, g)),
            pl.BlockSpec((seq, LANES), lambda b, g: (b, g)),
            pl.BlockSpec((1, LANES), lambda b, g: (0, g)),
        ],
        out_specs=pl.BlockSpec((seq, LANES), lambda b, g: (b, g)),
        out_shape=jax.ShapeDtypeStruct((t, B_WIDTH), BF16),
        scratch_shapes=[pltpu.VMEM((LANES, seq), BF16),
                        pltpu.VMEM((nslots, tq, tq), F32), pltpu.VMEM((nslots * tq, tq), BF16)],
        compiler_params=pltpu.CompilerParams(
            dimension_semantics=("arbitrary", "arbitrary"), vmem_limit_bytes=VMEM_LIMIT),
        name="latent_attention",
    )(bq, bk, bv, gain_b)


def _retention_consts():
    h = C_HEADS
    log_g = np.log(1.0 - 2.0 ** (-5.0 - np.arange(h))).astype(np.float32)
    i = np.arange(C_CHUNK, dtype=np.float32)
    rel = i[:, None] - i[None, :]
    decay = (np.exp(np.maximum(rel, 0.0)[None] * log_g[:, None, None]) * (rel >= 0)[None]).astype(np.float32)
    xi = np.exp((i + 1.0)[None, :] * log_g[:, None]).astype(np.float32)
    zeta = np.exp((C_CHUNK - 1.0 - i)[None, :] * log_g[:, None]).astype(np.float32)
    chunk_decay = np.exp(C_CHUNK * log_g).astype(np.float32)
    decay_all = np.concatenate(list(decay), axis=1)
    xi_mat = np.repeat(xi.T, C_VAL, axis=1)
    zeta_t = np.zeros((C_QPAD, C_CHUNK), np.float32)
    zeta_t[:C_QK] = np.repeat(zeta, C_KEY, axis=0)
    cd = np.repeat(chunk_decay, C_VAL)[None, :]
    bd = np.zeros((C_QPAD, C_WIDTH), np.float32)
    for hh in range(h):
        bd[hh * C_KEY:(hh + 1) * C_KEY, hh * C_VAL:(hh + 1) * C_VAL] = 1.0
    return tuple(jnp.asarray(a) for a in (decay_all, xi_mat, zeta_t, cd, bd))


def _retention_kernel(qk_ref, v_ref, g_ref, decay_ref, xi_ref, zeta_ref, cd_ref, bd_ref, gain_ref,
                      o_ref, *, seq):
    gain = gain_ref[...]
    state = jnp.zeros((C_QPAD, C_WIDTH), F32)
    kshape, vshape = (C_QPAD, C_HEADS * C_CHUNK), (C_HEADS * C_CHUNK, C_WIDTH)
    k_on = (lax.broadcasted_iota(jnp.int32, kshape, 0) // C_KEY) == (lax.broadcasted_iota(jnp.int32, kshape, 1) // C_CHUNK)
    v_on = (lax.broadcasted_iota(jnp.int32, vshape, 0) // C_CHUNK) == (lax.broadcasted_iota(jnp.int32, vshape, 1) // C_VAL)

    for n in range(seq // C_CHUNK):
        rows = slice(n * C_CHUNK, (n + 1) * C_CHUNK)
        q = qk_ref[rows, 0:C_QPAD]
        k_t = qk_ref[rows, C_QPAD:2 * C_QPAD].astype(F32).T
        v = v_ref[rows, :]
        k_bd = jnp.where(k_on, jnp.concatenate([k_t] * C_HEADS, axis=1), 0.0).astype(BF16)
        s = jnp.dot(q, k_bd, preferred_element_type=F32) * decay_ref[...]
        inter = jnp.dot(q, state.astype(BF16), preferred_element_type=F32) * xi_ref[...]
        upd = jnp.dot((k_t * zeta_ref[...]).astype(BF16), v, preferred_element_type=F32)
        state = state * cd_ref[...] + upd * bd_ref[...]
        v_bd = jnp.where(v_on, jnp.concatenate([v] * C_HEADS, axis=0), jnp.zeros((), BF16))
        o = jnp.dot(s.astype(BF16), v_bd, preferred_element_type=F32) + inter
        o_t = o.T
        normed = []
        for hh in range(C_HEADS):
            x = o_t[hh * C_VAL:(hh + 1) * C_VAL, :]
            dlt = x - jnp.mean(x, axis=0, keepdims=True)
            normed.append(dlt * lax.rsqrt(jnp.mean(dlt * dlt, axis=0, keepdims=True) + EPS))
        y = jnp.concatenate(normed, axis=0).T
        o_ref[rows, :] = (y * g_ref[rows, :].astype(F32) * gain).astype(o_ref.dtype)


def _retention(cqk, cv, cg, gain_c, consts, batch, seq):
    t = cqk.shape[0]
    tok = lambda width: pl.BlockSpec((seq, width), lambda b: (b, 0))
    full = lambda a: _const_spec(a.shape, lambda b: (0,) * a.ndim)
    return pl.pallas_call(
        functools.partial(_retention_kernel, seq=seq),
        grid=(batch,),
        in_specs=[tok(2 * C_QPAD), tok(C_WIDTH), tok(C_WIDTH)] + [full(a) for a in consts] + [full(gain_c)],
        out_specs=tok(C_WIDTH),
        out_shape=jax.ShapeDtypeStruct((t, C_WIDTH), BF16),
        compiler_params=pltpu.CompilerParams(
            dimension_semantics=("arbitrary",), vmem_limit_bytes=VMEM_LIMIT),
        name="retention",
    )(cqk, cv, cg, *consts, gain_c)


def _outffn_kernel(x_ref, ma_ref, mb_ref, mc_ref, mod_ref, wo_ref, g_ref, wg_ref, wu_ref, wd_ref, fg_ref,
                   *refs, final, fchunk, nsplit, ncast):
    cast_in, o_ref, cast_out = refs[:ncast], refs[ncast], refs[ncast + 1:2 * ncast + 1]
    mix_ref, h_ref, hid_ref = refs[2 * ncast + 1:]
    _cast_slabs(cast_in, cast_out)
    tm = x_ref.shape[0]
    parts = [slice(k * tm // nsplit, (k + 1) * tm // nsplit) for k in range(nsplit)]
    mix_ref[:, 0:A_WIDTH] = ma_ref[...]
    mix_ref[:, A_WIDTH:A_WIDTH + B_WIDTH] = mb_ref[...]
    mix_ref[:, A_WIDTH + B_WIDTH:] = mc_ref[...]
    for r in parts:
        att = jnp.dot(mix_ref[r, :], wo_ref[...], preferred_element_type=F32)
        o_ref[r, :] = x_ref[r, :] + mod_ref[2:3, :] * att
    for r in parts:
        h_ref[r, :] = (_rms(o_ref[r, :]) * g_ref[...] * (1.0 + mod_ref[4:5, :]) + mod_ref[3:4, :]).astype(BF16)
    hidden = wg_ref.shape[1]
    for r in parts:
        for j in range(hidden // fchunk):
            sl = slice(j * fchunk, (j + 1) * fchunk)
            gate = jnp.dot(h_ref[r, :], wg_ref[:, sl], preferred_element_type=F32)
            up = jnp.dot(h_ref[r, :], wu_ref[:, sl], preferred_element_type=F32)
            hid_ref[r, sl] = (_silu(gate) * up).astype(BF16)
    for r in parts:
        ffn = jnp.dot(hid_ref[r, :], wd_ref[...], preferred_element_type=F32)
        y = o_ref[r, :] + mod_ref[5:6, :] * ffn
        if final:
            y = _rms(y) * fg_ref[...]
        o_ref[r, :] = y


def _outffn(x, mix_a, mix_b, mix_c, mods, layer, prep, ffn_w, final_g, batch, seq, tm, final, cast_layer):
    t, d = x.shape
    nt = seq // tm
    w_out, w_gate, w_up, w_down = ffn_w
    hidden = w_gate.shape[1]
    whole = lambda a: _const_spec(a.shape, lambda b, i: (0,) * a.ndim)
    row = lambda w: pl.BlockSpec((tm, w), lambda b, i: (b * nt + i, 0))
    wl = lambda a: _const_spec((None,) + a.shape[1:], lambda b, i: (layer,) + (0,) * (a.ndim - 1))
    cast_ws, cast_in_specs, cast_out_specs, cast_shapes = _cast_plumbing(
        prep, cast_layer, batch * nt, lambda b, i: b * nt + i)
    return pl.pallas_call(
        functools.partial(_outffn_kernel, final=final, fchunk=MXU_WIDTH, nsplit=FFN_SPLIT, ncast=len(cast_ws)),
        grid=(batch, nt),
        in_specs=[
            row(d), row(A_WIDTH), row(B_WIDTH), row(C_WIDTH),
            pl.BlockSpec((None, None, 6, d), lambda b, i: (layer, b, 0, 0)),
            whole(w_out), wl(prep["norm2_g"]), whole(w_gate), whole(w_up), whole(w_down),
            _const_spec((1, d), lambda b, i: (0, 0)),
            *cast_in_specs,
        ],
        out_specs=[row(d)] + cast_out_specs,
        out_shape=[jax.ShapeDtypeStruct((t, d), F32)] + cast_shapes,
        scratch_shapes=[pltpu.VMEM((tm, d), BF16), pltpu.VMEM((tm, d), BF16), pltpu.VMEM((tm, hidden), BF16)],
        compiler_params=pltpu.CompilerParams(
            dimension_semantics=("arbitrary", "arbitrary"), vmem_limit_bytes=VMEM_LIMIT),
        name="outproj_ffn",
    )(x, mix_a, mix_b, mix_c, mods, w_out, prep["norm2_g"], w_gate, w_up, w_down, final_g, *cast_ws)


def _prepare(norm1_g, w_in, mla_q_norm, mla_kv_norm, mla_w_uq, mla_w_ukv, w_out, norm2_g,
             ffn_w_gate, ffn_w_up, ffn_w_down):
    depth, d, _ = w_in.shape
    b_lat0 = 3 * A_WIDTH
    rope0 = b_lat0 + B_Q_RANK + B_KV_RANK
    c0 = rope0 + B_ROPE
    assert C_QK + B_ROPE <= C_QPAD and C_QK % LANES == B_NOPE
    assert c0 + 2 * C_QK + 2 * C_WIDTH == w_in.shape[2]
    uq = mla_w_uq.reshape(depth, B_Q_RANK, B_HEADS, B_NOPE + B_ROPE)
    uq = jnp.pad(uq, ((0, 0), (0, 0), (0, 0), (0, LANES - B_NOPE - B_ROPE)))
    ukv = mla_w_ukv.reshape(depth, B_KV_RANK, B_HEADS, B_NOPE + B_V)
    uk = jnp.pad(ukv[..., :B_NOPE], ((0, 0), (0, 0), (0, 0), (0, LANES - B_NOPE)))
    uv = ukv[..., B_NOPE:]
    return {
        "norm1_g": norm1_g[:, None, :],
        "w_in": jnp.swapaxes(w_in, 1, 2),
        "q_norm": mla_q_norm[:, None, :],
        "kv_norm": mla_kv_norm[:, None, :],
        "w_uq": uq.reshape(depth, B_Q_RANK, B_HEADS * LANES).astype(BF16),
        "w_uk": uk.reshape(depth, B_KV_RANK, B_HEADS * LANES).astype(BF16),
        "w_uv": uv.reshape(depth, B_KV_RANK, B_WIDTH).astype(BF16),
        "w_out": w_out,
        "norm2_g": norm2_g[:, None, :],
        "w_gate": ffn_w_gate,
        "w_up": ffn_w_up,
        "w_down": ffn_w_down,
    }


def kernel(x, c, positions, rel_bias, ada_w, ada_b, norm1_g, w_in, mla_q_norm, mla_kv_norm, mla_w_uq,
           mla_w_ukv, mix_gain, w_out, norm2_g, ffn_w_gate, ffn_w_up, ffn_w_down, final_norm):
    batch, seq, d = x.shape
    depth = w_in.shape[0]
    tm = ROW_TILE
    assert seq % tm == 0 and seq % C_CHUNK == 0

    prep = _prepare(norm1_g, w_in, mla_q_norm, mla_kv_norm, mla_w_uq, mla_w_ukv, w_out, norm2_g,
                    ffn_w_gate, ffn_w_up, ffn_w_down)
    mods = _mods(c, ada_w, ada_b).reshape(depth, batch, 6, d)
    tables = _rope_tables(positions)
    bias = _bias_tables(rel_bias)
    ret_consts = _retention_consts()
    final_g = final_norm[None, :]

    xf = x.reshape(batch * seq, d)
    for l in range(depth):
        a_qkv, bq, bk, bv, cqk, cv, cg, *cast0 = _inproj(xf, mods, l, prep, tables, batch, seq, tm,
                                                         cast_layer=0 if l == 0 else None)
        ffn_w = cast0 if l == 0 else ffn_w_next
        gain = mix_gain[l][None, :]
        mix_a = _dilated(a_qkv, bias, gain[:, :A_WIDTH], batch, seq)
        mix_b = _mla(bq, bk, bv, gain[:, A_WIDTH:A_WIDTH + B_WIDTH], batch, seq)
        mix_c = _retention(cqk, cv, cg, gain[:, A_WIDTH + B_WIDTH:], ret_consts, batch, seq)
        xf, *ffn_w_next = _outffn(xf, mix_a, mix_b, mix_c, mods, l, prep, ffn_w, final_g, batch, seq, tm,
                                  final=(l == depth - 1), cast_layer=l + 1 if l + 1 < depth else None)
    return xf.reshape(batch, seq, d)
```
